```python
import math
import jax, jax.numpy as jnp
from jax import lax
import numpy as np

D_MODEL = 2048
BATCH = 2
SEQ = 16384
DEPTH = 2

MIX_WIDTH = D_MODEL
GROUP_WIDTH = MIX_WIDTH // 4

POOL_WINDOWS = (2, 4, 8, 16)
POOL_GROUP = GROUP_WIDTH // len(POOL_WINDOWS)

ATT_HEAD_DIM = 64
ATT_HEADS = GROUP_WIDTH // ATT_HEAD_DIM
DILATION_PAIRS = ((128, 1), (512, 4), (2048, 16))
REL_BUCKETS = 32
REL_MAX_DIST = 1024

CONV_WIDTH = 31

DN_HEAD_DIM = 128
DN_HEADS = GROUP_WIDTH // DN_HEAD_DIM
DN_CONV_WIDTH = 4
DN_CONV_PAD = (2, 1)
DN_CHUNK = 64

FFN_HIDDEN = 256 * ((8 * D_MODEL // 3 + 255) // 256)
FFN_CONV_WIDTH = 3

EPS = 1e-6
NEG_INF = -1e30

PROJ_SPLITS = (GROUP_WIDTH, GROUP_WIDTH, GROUP_WIDTH, GROUP_WIDTH, 2 * GROUP_WIDTH,
               3 * GROUP_WIDTH, GROUP_WIDTH, 2 * DN_HEADS, 2 * DN_HEADS)
PROJ_WIDTH = 10 * GROUP_WIDTH + 4 * DN_HEADS

kernel_name = 'hybrid_parallel_group_encoder'


def rms_norm(x, g):
    xf = x.astype(jnp.float32)
    y = xf * lax.rsqrt(jnp.mean(xf * xf, axis=-1, keepdims=True) + EPS)
    return (y * g.astype(jnp.float32)).astype(x.dtype)


def layer_norm(x, g, b):
    xf = x.astype(jnp.float32)
    mu = jnp.mean(xf, axis=-1, keepdims=True)
    xc = xf - mu
    var = jnp.mean(xc * xc, axis=-1, keepdims=True)
    return (xc * lax.rsqrt(var + EPS) * g.astype(jnp.float32) + b.astype(jnp.float32)).astype(x.dtype)


def l2_norm(x):
    xf = x.astype(jnp.float32)
    return xf * lax.rsqrt(jnp.sum(xf * xf, axis=-1, keepdims=True) + EPS)


def depthwise_conv(x, w, pad):
    return lax.conv_general_dilated(x, w[:, None, :].astype(x.dtype), (1,), [pad],
                                    dimension_numbers=('NWC', 'WIO', 'NWC'),
                                    feature_group_count=x.shape[-1])


def pool_mixer(u, w_pool, pool_scale):
    B, S, _ = u.shape
    ug = u.reshape(B, S, len(POOL_WINDOWS), POOL_GROUP).astype(jnp.float32)
    cs = jnp.pad(jnp.cumsum(ug, axis=1), ((0, 0), (1, 0), (0, 0), (0, 0)))
    t = jnp.arange(S)
    means = []
    for gi, win in enumerate(POOL_WINDOWS):
        lo = jnp.clip(t - win // 2, 0, S)
        hi = jnp.clip(t + win // 2, 0, S)
        csg = cs[:, :, gi]
        total = jnp.take(csg, hi, axis=1) - jnp.take(csg, lo, axis=1)
        cnt = (hi - lo).astype(jnp.float32)[None, :, None]
        means.append(total / cnt)
    pooled = (jnp.stack(means, axis=2) - ug).astype(u.dtype)
    y = jnp.einsum('bsgc,gcd->bsgd', pooled, w_pool)
    return y.reshape(B, S, GROUP_WIDTH) * pool_scale


def t5_bucket(rel):
    nb = REL_BUCKETS // 2
    max_exact = nb // 2
    n = jnp.abs(rel)
    nf = jnp.maximum(n, 1).astype(jnp.float32)
    large = max_exact + (jnp.log(nf / max_exact) / math.log(REL_MAX_DIST / max_exact)
                         * (nb - max_exact)).astype(jnp.int32)
    large = jnp.minimum(large, nb - 1)
    return jnp.where(rel > 0, nb, 0) + jnp.where(n < max_exact, n, large)


def dilated_branch(q, k, v, rel_bias, dil, radius):
    B, S, H, Dh = q.shape
    L = S // dil
    nblk = -(-L // radius)
    Lp = nblk * radius

    def split(t):
        t = t.reshape(B, L, dil, H, Dh).transpose(0, 2, 1, 3, 4)
        return jnp.pad(t, ((0, 0), (0, 0), (0, Lp - L), (0, 0), (0, 0)))

    def neighbours(t):
        t = jnp.pad(t, ((0, 0), (0, 0), (radius, radius), (0, 0), (0, 0)))
        t = t.reshape(B, dil, nblk + 2, radius, H, Dh)
        return jnp.concatenate([t[:, :, :-2], t[:, :, 1:-1], t[:, :, 2:]], axis=3)

    qb = split(q).reshape(B, dil, nblk, radius, H, Dh)
    kb = neighbours(split(k))
    vb = neighbours(split(v))

    i = jnp.arange(radius)[:, None]
    j = jnp.arange(3 * radius)[None, :]
    off = j - radius - i
    key_idx = jnp.arange(nblk)[:, None, None] * radius + j - radius
    valid = (jnp.abs(off) <= radius)[None] & (key_idx >= 0) & (key_idx < L)
    bias = rel_bias.astype(jnp.float32)[t5_bucket(off * dil)].transpose(2, 0, 1)

    s = jnp.einsum('bdnqhc,bdnkhc->bdnhqk', qb, kb, preferred_element_type=jnp.float32)
    s = jnp.where(valid[None, None, :, None], s + bias[None, None, None], NEG_INF)
    lse = jax.nn.logsumexp(s, axis=-1)
    p = jnp.exp(s - lse[..., None]).astype(v.dtype)
    o = jnp.einsum('bdnhqk,bdnkhc->bdnqhc', p, vb)
    o = o.reshape(B, dil, Lp, H, Dh)[:, :, :L].transpose(0, 2, 1, 3, 4).reshape(B, S, H, Dh)
    lse = lse.transpose(0, 1, 2, 4, 3).reshape(B, dil, Lp, H)[:, :, :L]
    lse = lse.transpose(0, 2, 1, 3).reshape(B, S, H)
    return o, lse


def dilated_attention_mixer(a_q, a_k, a_v, q_g, k_g, rel_bias):
    B, S, _ = a_q.shape
    shape = (B, S, ATT_HEADS, ATT_HEAD_DIM)
    q = rms_norm(a_q.reshape(shape), q_g) * (ATT_HEAD_DIM ** -0.5)
    k = rms_norm(a_k.reshape(shape), k_g)
    v = a_v.reshape(shape)
    outs, lses = [], []
    for win, dil in DILATION_PAIRS:
        o, l = dilated_branch(q, k, v, rel_bias, dil, win // (2 * dil))
        outs.append(o)
        lses.append(l)
    wts = jax.nn.softmax(jnp.stack(lses, axis=0), axis=0)
    o = jnp.einsum('rbsh,rbshc->bshc', wts, jnp.stack(outs, axis=0).astype(jnp.float32))
    return o.astype(a_v.dtype).reshape(B, S, GROUP_WIDTH)


def conformer_conv_mixer(u, dw_w, dw_b, ln_g, ln_b, pw):
    val, gate = jnp.split(u, 2, axis=-1)
    h = val * jax.nn.sigmoid(gate)
    h = depthwise_conv(h, dw_w, (CONV_WIDTH // 2, CONV_WIDTH // 2)) + dw_b
    h = jax.nn.silu(layer_norm(h, ln_g, ln_b))
    return h @ pw


def gated_delta_rule_chunked(q, k, v, g, beta):
    f32 = jnp.float32
    N, S, H, Dk = q.shape
    Dv = v.shape[-1]
    C = DN_CHUNK
    nc = S // C

    def chunks(t):
        return t.astype(f32).reshape(N, nc, C, H, -1).transpose(0, 3, 1, 2, 4)

    q, k, v = chunks(q), chunks(k), chunks(v)
    g = chunks(g[..., None])[..., 0]
    beta = chunks(beta[..., None])[..., 0]
    gc = jnp.cumsum(g, axis=-1)
    idx = jnp.arange(C)
    incl = idx[:, None] >= idx[None, :]
    strict = idx[:, None] > idx[None, :]
    decay = jnp.exp(jnp.where(incl, gc[..., :, None] - gc[..., None, :], NEG_INF))
    kb = k * beta[..., None]
    a = jnp.einsum('nhcik,nhcjk->nhcij', kb, k) * jnp.where(strict, decay, 0.0)
    eye = jnp.eye(C, dtype=f32)
    t_inv = lax.linalg.triangular_solve(eye + a, jnp.broadcast_to(eye, a.shape),
                                        left_side=True, lower=True)
    u = jnp.einsum('nhcij,nhcjv->nhciv', t_inv, v * beta[..., None])
    w = jnp.einsum('nhcij,nhcjk->nhcik', t_inv, kb * jnp.exp(gc)[..., None])
    attn = jnp.einsum('nhcik,nhcjk->nhcij', q, k) * decay
    q_dec = q * jnp.exp(gc)[..., None]
    k_dec = k * jnp.exp(gc[..., -1:] - gc)[..., None]
    g_last = jnp.exp(gc[..., -1])

    def step(state, inp):
        u_c, w_c, q_c, k_c, a_c, gl_c = inp
        v_new = u_c - jnp.einsum('nhck,nhkv->nhcv', w_c, state)
        o_c = jnp.einsum('nhck,nhkv->nhcv', q_c, state) + jnp.einsum('nhij,nhjv->nhiv', a_c, v_new)
        state = state * gl_c[..., None, None] + jnp.einsum('nhck,nhcv->nhkv', k_c, v_new)
        return state, o_c

    xs = tuple(jnp.moveaxis(t, 2, 0) for t in (u, w, q_dec, k_dec, attn, g_last))
    _, o = lax.scan(step, jnp.zeros((N, H, Dk, Dv), f32), xs)
    return o.transpose(1, 0, 3, 2, 4).reshape(N, S, H, Dv)


def gated_deltanet_mixer(d_qkv, d_z, d_a, d_b, conv_w, a_log, dt_bias, norm_g):
    B, S, _ = d_qkv.shape
    shape = (B, S, DN_HEADS, DN_HEAD_DIM)
    qkv = jax.nn.silu(depthwise_conv(d_qkv, conv_w, DN_CONV_PAD))
    q, k, v = jnp.split(qkv, 3, axis=-1)
    q = l2_norm(q.reshape(shape)) * (DN_HEAD_DIM ** -0.5)
    k = l2_norm(k.reshape(shape))
    v = v.reshape(shape)
    a = d_a.reshape(B, S, 2, DN_HEADS).astype(jnp.float32)
    b = d_b.reshape(B, S, 2, DN_HEADS).astype(jnp.float32)
    g = -jnp.exp(a_log.astype(jnp.float32)) * jax.nn.softplus(a + dt_bias.astype(jnp.float32))
    beta = jax.nn.sigmoid(b)
    flip = lambda t: jnp.flip(t, axis=1)
    o = gated_delta_rule_chunked(
        jnp.concatenate([q, flip(q)], axis=0),
        jnp.concatenate([k, flip(k)], axis=0),
        jnp.concatenate([v, flip(v)], axis=0),
        jnp.concatenate([g[:, :, 0], flip(g[:, :, 1])], axis=0),
        jnp.concatenate([beta[:, :, 0], flip(beta[:, :, 1])], axis=0))
    o = (o[:B] + flip(o[B:])).astype(d_qkv.dtype)
    o = rms_norm(o, norm_g) * jax.nn.silu(d_z.reshape(shape))
    return o.reshape(B, S, GROUP_WIDTH)


def conv_glu_ffn(h, w_gate, dw_w, dw_b, w_up, w_down):
    gate = depthwise_conv(h @ w_gate, dw_w, (FFN_CONV_WIDTH // 2, FFN_CONV_WIDTH // 2)) + dw_b
    return (jax.nn.silu(gate) * (h @ w_up)) @ w_down


def hybrid_layer(x, rel_bias, norm1_g, w_in, w_pool, pool_scale, att_q_g, att_k_g,
                 conv_dw_w, conv_dw_b, conv_ln_g, conv_ln_b, conv_pw,
                 dn_conv_w, dn_a_log, dn_dt_bias, dn_norm_g, w_out,
                 norm2_g, ffn_w_gate, ffn_dw_w, ffn_dw_b, ffn_w_up, ffn_w_down):
    h = rms_norm(x, norm1_g)
    p = h @ w_in
    bounds = np.cumsum(PROJ_SPLITS)[:-1].tolist()
    u_pool, a_q, a_k, a_v, u_conv, d_qkv, d_z, d_a, d_b = jnp.split(p, bounds, axis=-1)
    y = jnp.concatenate([
        pool_mixer(u_pool, w_pool, pool_scale),
        dilated_attention_mixer(a_q, a_k, a_v, att_q_g, att_k_g, rel_bias),
        conformer_conv_mixer(u_conv, conv_dw_w, conv_dw_b, conv_ln_g, conv_ln_b, conv_pw),
        gated_deltanet_mixer(d_qkv, d_z, d_a, d_b, dn_conv_w, dn_a_log, dn_dt_bias, dn_norm_g),
    ], axis=-1)
    x = x + y @ w_out
    x = x + conv_glu_ffn(rms_norm(x, norm2_g), ffn_w_gate, ffn_dw_w, ffn_dw_b, ffn_w_up, ffn_w_down)
    return x


def setup_inputs(seed: int = 0) -> dict:
    key = jax.random.key(seed)
    ks = jax.random.split(key, 24)
    f32 = jnp.float32
    L, G = DEPTH, GROUP_WIDTH

    def nrm(k, shape, scale):
        return jax.random.normal(k, shape, f32) * scale

    dt = jnp.exp(jax.random.uniform(ks[13], (L, 2, DN_HEADS), f32, math.log(1e-3), math.log(1e-1)))
    return {
        'x': nrm(ks[0], (BATCH, SEQ, D_MODEL), 1.0),
        'rel_bias': nrm(ks[1], (REL_BUCKETS, ATT_HEADS), 0.5),
        'norm1_g': 1.0 + nrm(ks[2], (L, D_MODEL), 0.02),
        'w_in': nrm(ks[3], (L, D_MODEL, PROJ_WIDTH), D_MODEL ** -0.5),
        'w_pool': nrm(ks[4], (L, len(POOL_WINDOWS), POOL_GROUP, POOL_GROUP), POOL_GROUP ** -0.5),
        'pool_scale': 1.0 + nrm(ks[5], (L, G), 0.02),
        'att_q_g': 1.0 + nrm(ks[6], (L, ATT_HEAD_DIM), 0.02),
        'att_k_g': 1.0 + nrm(ks[7], (L, ATT_HEAD_DIM), 0.02),
        'conv_dw_w': nrm(ks[8], (L, CONV_WIDTH, G), CONV_WIDTH ** -0.5),
        'conv_dw_b': nrm(ks[9], (L, G), 0.02),
        'conv_ln_g': 1.0 + nrm(ks[10], (L, G), 0.02),
        'conv_ln_b': nrm(ks[11], (L, G), 0.02),
        'conv_pw': nrm(ks[12], (L, G, G), G ** -0.5),
        'dn_conv_w': nrm(ks[14], (L, DN_CONV_WIDTH, 3 * G), DN_CONV_WIDTH ** -0.5),
        'dn_a_log': jnp.log(jax.random.uniform(ks[15], (L, 2, DN_HEADS), f32, 1.0, 16.0)),
        'dn_dt_bias': dt + jnp.log(-jnp.expm1(-dt)),
        'dn_norm_g': 1.0 + nrm(ks[16], (L, DN_HEAD_DIM), 0.02),
        'w_out': nrm(ks[17], (L, MIX_WIDTH, D_MODEL), MIX_WIDTH ** -0.5),
        'norm2_g': 1.0 + nrm(ks[18], (L, D_MODEL), 0.02),
        'ffn_w_gate': nrm(ks[19], (L, D_MODEL, FFN_HIDDEN), D_MODEL ** -0.5),
        'ffn_dw_w': nrm(ks[20], (L, FFN_CONV_WIDTH, FFN_HIDDEN), FFN_CONV_WIDTH ** -0.5),
        'ffn_dw_b': nrm(ks[21], (L, FFN_HIDDEN), 0.02),
        'ffn_w_up': nrm(ks[22], (L, D_MODEL, FFN_HIDDEN), D_MODEL ** -0.5),
        'ffn_w_down': nrm(ks[23], (L, FFN_HIDDEN, D_MODEL), FFN_HIDDEN ** -0.5),
    }


def reference(x, rel_bias, norm1_g, w_in, w_pool, pool_scale, att_q_g, att_k_g,
              conv_dw_w, conv_dw_b, conv_ln_g, conv_ln_b, conv_pw,
              dn_conv_w, dn_a_log, dn_dt_bias, dn_norm_g, w_out,
              norm2_g, ffn_w_gate, ffn_dw_w, ffn_dw_b, ffn_w_up, ffn_w_down):
    for l in range(DEPTH):
        x = hybrid_layer(x, rel_bias, norm1_g[l], w_in[l], w_pool[l], pool_scale[l],
                         att_q_g[l], att_k_g[l], conv_dw_w[l], conv_dw_b[l], conv_ln_g[l],
                         conv_ln_b[l], conv_pw[l], dn_conv_w[l], dn_a_log[l], dn_dt_bias[l],
                         dn_norm_g[l], w_out[l], norm2_g[l], ffn_w_gate[l], ffn_dw_w[l],
                         ffn_dw_b[l], ffn_w_up[l], ffn_w_down[l])
    return x
```

```python
import functools
import math

import jax
import jax.numpy as jnp
import numpy as np
from jax import lax
from jax.experimental import pallas as pl
from jax.experimental.pallas import tpu as pltpu

F32 = jnp.float32
BF16 = jnp.bfloat16

D_MODEL = 2048
GROUP = 512
POOL_WINDOWS = (2, 4, 8, 16)
ATT_HEAD_DIM = 64
ATT_HEADS = 8
ATT_RADIUS = 64
ATT_DILATIONS = (1, 4, 16)
REL_BUCKETS = 32
REL_MAX_DIST = 1024
CONV_WIDTH = 31
DN_HEAD_DIM = 128
DN_HEADS = 4
DN_CHUNK = 64
FFN_HIDDEN = 5632
EPS = 1e-6
NEG_INF = -1e30

PROJ_MAIN = 10 * GROUP
GATE_LANES = 128

VMEM_LIMIT_BYTES = 56 * 1024 * 1024


def _cparams(*sem):
    return pltpu.CompilerParams(dimension_semantics=sem, vmem_limit_bytes=VMEM_LIMIT_BYTES)


def _sigmoid(x):
    return 1.0 / (1.0 + jnp.exp(-x))


def _silu(x):
    return x * _sigmoid(x)


def _dot(a, b):
    return jnp.dot(a, b, preferred_element_type=F32)


def _dot_nt(a, b):
    return lax.dot_general(a, b, (((1,), (1,)), ((), ())), preferred_element_type=F32)


def _dot_tn(a, b):
    return lax.dot_general(a, b, (((0,), (0,)), ((), ())), preferred_element_type=F32)


def _inproj_kernel(x_ref, g_ref, w_ref, wg_ref, p_ref, pg_ref, h_scr):
    @pl.when(pl.program_id(1) == 0)
    def _():
        x = x_ref[...]
        ms = jnp.mean(x * x, axis=-1, keepdims=True)
        h = (x * lax.rsqrt(ms + EPS) * g_ref[...]).astype(BF16)
        h_scr[...] = h
        pg_ref[...] = _dot(h, wg_ref[...])

    p_ref[...] = _dot(h_scr[...], w_ref[...])


def _in_proj(x2, norm_g, w_main, w_gate, tm=1024, tn=1024):
    T = x2.shape[0]
    return pl.pallas_call(
        _inproj_kernel,
        grid=(T // tm, PROJ_MAIN // tn),
        in_specs=[
            pl.BlockSpec((tm, D_MODEL), lambda i, j: (i, 0)),
            pl.BlockSpec((1, D_MODEL), lambda i, j: (0, 0)),
            pl.BlockSpec((D_MODEL, tn), lambda i, j: (0, j)),
            pl.BlockSpec((D_MODEL, GATE_LANES), lambda i, j: (0, 0)),
        ],
        out_specs=[
            pl.BlockSpec((tm, tn), lambda i, j: (i, j)),
            pl.BlockSpec((tm, GATE_LANES), lambda i, j: (i, 0)),
        ],
        out_shape=[
            jax.ShapeDtypeStruct((T, PROJ_MAIN), F32),
            jax.ShapeDtypeStruct((T, GATE_LANES), F32),
        ],
        scratch_shapes=[pltpu.VMEM((tm, D_MODEL), BF16)],
        compiler_params=_cparams("parallel", "arbitrary"),
        name="in_proj",
    )(x2, norm_g, w_main, w_gate)


def _halo_specs(tm, halo, width, col_block, n_tiles):
    r = tm // halo
    last = n_tiles * r - 1
    prev_spec = pl.BlockSpec((halo, width), lambda i, *_: (jnp.maximum(i * r - 1, 0), col_block))
    next_spec = pl.BlockSpec((halo, width), lambda i, *_: (jnp.minimum((i + 1) * r, last), col_block))
    return prev_spec, next_spec


def _seq_edges(i, tiles_per_seq):
    k = lax.rem(i, tiles_per_seq)
    return k == 0, k == tiles_per_seq - 1, k


def _pool_kernel(u_ref, up_ref, un_ref, w_ref, sc_ref, y_ref, ext_scr, *, tm, seq_len):
    i = pl.program_id(0)
    first, last, k = _seq_edges(i, seq_len // tm)
    ext_scr[0:8, :] = jnp.where(first, jnp.zeros_like(up_ref[...]), up_ref[...])
    ext_scr[8:8 + tm, :] = u_ref[...]
    ext_scr[8 + tm:16 + tm, :] = jnp.where(last, jnp.zeros_like(un_ref[...]), un_ref[...])
    t = k * tm + lax.broadcasted_iota(jnp.int32, (tm, 1), 0)
    for gi, win in enumerate(POOL_WINDOWS):
        half = win // 2
        lanes = slice(gi * 128, (gi + 1) * 128)
        s = ext_scr[pl.ds(8 - half, tm), lanes]
        for kk in range(1, win):
            s = s + ext_scr[pl.ds(8 - half + kk, tm), lanes]
        cnt = (jnp.minimum(t + half, seq_len) - jnp.maximum(t - half, 0)).astype(F32)
        pooled = s / cnt - ext_scr[pl.ds(8, tm), lanes]
        y = _dot(pooled.astype(BF16), w_ref[gi]) * sc_ref[:, lanes]
        y_ref[:, lanes] = y.astype(BF16)


def _pool_mixer(p, w_pool, pool_scale, seq_len, tm=1024):
    T = p.shape[0]
    n_tiles = T // tm
    prev_spec, next_spec = _halo_specs(tm, 8, GROUP, 0, n_tiles)
    return pl.pallas_call(
        functools.partial(_pool_kernel, tm=tm, seq_len=seq_len),
        grid=(n_tiles,),
        in_specs=[
            pl.BlockSpec((tm, GROUP), lambda i: (i, 0)),
            prev_spec,
            next_spec,
            pl.BlockSpec((4, 128, 128), lambda i: (0, 0, 0)),
            pl.BlockSpec((1, GROUP), lambda i: (0, 0)),
        ],
        out_specs=pl.BlockSpec((tm, GROUP), lambda i: (i, 0)),
        out_shape=jax.ShapeDtypeStruct((T, GROUP), BF16),
        scratch_shapes=[pltpu.VMEM((tm + 16, GROUP), F32)],
        compiler_params=_cparams("parallel"),
        name="pool_mixer",
    )(p, p, p, w_pool, pool_scale)


def _conformer_kernel(v_ref, vp_ref, vn_ref, g_ref, gp_ref, gn_ref, dw_ref, db_ref, lg_ref, lb_ref,
                      pw_ref, y_ref, ext_scr, *, tm, seq_len):
    i = pl.program_id(0)
    first, last, _ = _seq_edges(i, seq_len // tm)
    hp = vp_ref[...] * _sigmoid(gp_ref[...])
    hn = vn_ref[...] * _sigmoid(gn_ref[...])
    ext_scr[0:16, :] = jnp.where(first, jnp.zeros_like(hp), hp)
    ext_scr[16:16 + tm, :] = v_ref[...] * _sigmoid(g_ref[...])
    ext_scr[16 + tm:32 + tm, :] = jnp.where(last, jnp.zeros_like(hn), hn)
    pad = CONV_WIDTH // 2
    acc = ext_scr[pl.ds(16 - pad, tm), :] * dw_ref[0:1, :]
    for kk in range(1, CONV_WIDTH):
        acc = acc + ext_scr[pl.ds(16 - pad + kk, tm), :] * dw_ref[kk:kk + 1, :]
    acc = acc + db_ref[...]
    mu = jnp.mean(acc, axis=-1, keepdims=True)
    xc = acc - mu
    var = jnp.mean(xc * xc, axis=-1, keepdims=True)
    h = _silu(xc * lax.rsqrt(var + EPS) * lg_ref[...] + lb_ref[...])
    y_ref[...] = _dot(h.astype(BF16), pw_ref[...]).astype(BF16)


def _conformer_mixer(p, dw_w, dw_b, ln_g, ln_b, pw, seq_len, tm=512):
    T = p.shape[0]
    n_tiles = T // tm
    vprev, vnext = _halo_specs(tm, 16, GROUP, 4, n_tiles)
    gprev, gnext = _halo_specs(tm, 16, GROUP, 5, n_tiles)
    const = lambda shape: pl.BlockSpec(shape, lambda i: (0,) * len(shape))
    return pl.pallas_call(
        functools.partial(_conformer_kernel, tm=tm, seq_len=seq_len),
        grid=(n_tiles,),
        in_specs=[
            pl.BlockSpec((tm, GROUP), lambda i: (i, 4)), vprev, vnext,
            pl.BlockSpec((tm, GROUP), lambda i: (i, 5)), gprev, gnext,
            const((CONV_WIDTH, GROUP)), const((1, GROUP)), const((1, GROUP)), const((1, GROUP)),
            const((GROUP, GROUP)),
        ],
        out_specs=pl.BlockSpec((tm, GROUP), lambda i: (i, 0)),
        out_shape=jax.ShapeDtypeStruct((T, GROUP), BF16),
        scratch_shapes=[pltpu.VMEM((tm + 32, GROUP), F32)],
        compiler_params=_cparams("parallel"),
        name="conformer_mixer",
    )(p, p, p, p, p, p, dw_w, dw_b, ln_g, ln_b, pw)


ATT_TILE = 1024


def _t5_bucket_table():
    nb = REL_BUCKETS // 2
    max_exact = nb // 2
    i = np.arange(ATT_RADIUS)[:, None]
    j = np.arange(3 * ATT_RADIUS)[None, :]
    off = j - ATT_RADIUS - i
    tables = []
    for dil in ATT_DILATIONS:
        rel = off * dil
        n = np.abs(rel)
        nf = np.maximum(n, 1).astype(np.float32)
        large = max_exact + (np.log(nf / np.float32(max_exact)) / np.float32(math.log(REL_MAX_DIST / max_exact))
                             * np.float32(nb - max_exact)).astype(np.int32)
        large = np.minimum(large, nb - 1)
        bucket = np.where(rel > 0, nb, 0) + np.where(n < max_exact, n, large)
        tables.append(np.where(np.abs(off) <= ATT_RADIUS, bucket, -1))
    return np.stack(tables).astype(np.int32)


def _att_kernel(rb_ref, bkt_ref, qg_ref, kg_ref, mseg_ref, q_ref, kp_ref, kc_ref, kn_ref,
                vp_ref, vc_ref, vn_ref, y_ref, bias_scr, qbuf, kbuf, vbuf, acc_scr, m_scr, l_scr,
                *, seq_len):
    tile = ATT_TILE
    rad = ATT_RADIUS
    hp = pl.program_id(0)
    i = pl.program_id(1)

    @pl.when(i == 0)
    def _():
        for di in range(len(ATT_DILATIONS)):
            bkt = bkt_ref[di]
            for hh in range(2):
                b = jnp.zeros(bkt.shape, F32)
                for bb in range(REL_BUCKETS):
                    b = jnp.where(bkt == bb, rb_ref[bb, 2 * hp + hh], b)
                bias_scr[di, hh * rad:(hh + 1) * rad, :] = jnp.where(bkt < 0, NEG_INF, b)

    first, last, _ = _seq_edges(i, seq_len // tile)

    def rms(x, g):
        ms = _dot((x * x).astype(BF16), mseg_ref[...])
        return x * lax.rsqrt(ms + EPS) * g

    qbuf[...] = rms(q_ref[...], qg_ref[...]) * (ATT_HEAD_DIM ** -0.5)
    kbuf[0:tile, :] = rms(kp_ref[...], kg_ref[...])
    kbuf[tile:2 * tile, :] = rms(kc_ref[...], kg_ref[...])
    kbuf[2 * tile:3 * tile, :] = rms(kn_ref[...], kg_ref[...])
    vbuf[0:tile, :] = vp_ref[...]
    vbuf[tile:2 * tile, :] = vc_ref[...]
    vbuf[2 * tile:3 * tile, :] = vn_ref[...]

    head0 = lax.broadcasted_iota(jnp.int32, (1, 128), 1) < ATT_HEAD_DIM
    col = lax.broadcasted_iota(jnp.int32, (1, 3 * rad), 1)

    def rows(start, size, dil):
        return pl.ds(start, size) if dil == 1 else pl.ds(start, size, stride=dil)

    for di, dil in enumerate(ATT_DILATIONS):
        span = rad * dil
        for r in range(dil):
            for m in range(tile // span):
                qstart = r + span * m
                kstart = tile + qstart - span
                qb = qbuf[rows(qstart, rad, dil), :]
                kb = kbuf[rows(kstart, 3 * rad, dil), :].astype(BF16)
                vb = vbuf[rows(kstart, 3 * rad, dil), :].astype(BF16)
                q2 = jnp.concatenate([jnp.where(head0, qb, 0.0), jnp.where(head0, 0.0, qb)], axis=0)
                s = _dot_nt(q2.astype(BF16), kb) + bias_scr[di]
                n_prev = max(0, -(-(tile - kstart) // dil))
                n_upto = min(3 * rad, -(-(2 * tile - kstart) // dil))
                if n_prev > 0:
                    s = jnp.where(col < jnp.where(first, n_prev, 0), NEG_INF, s)
                if n_upto < 3 * rad:
                    s = jnp.where(col >= jnp.where(last, n_upto, 3 * rad), NEG_INF, s)
                mx = jnp.max(s, axis=-1, keepdims=True)
                pe = jnp.exp(s - mx)
                ls = jnp.sum(pe, axis=-1, keepdims=True)
                o = _dot(pe.astype(BF16), vb)
                dst = rows(qstart, rad, dil)
                acc_scr[di, dst, :] = jnp.where(head0, o[0:rad], o[rad:2 * rad])
                m_scr[di, dst, :] = jnp.where(head0, mx[0:rad], mx[rad:2 * rad])
                l_scr[di, dst, :] = jnp.where(head0, ls[0:rad], ls[rad:2 * rad])

    m_all = jnp.maximum(jnp.maximum(m_scr[0], m_scr[1]), m_scr[2])
    num = jnp.zeros((tile, 128), F32)
    den = jnp.zeros((tile, 128), F32)
    for di in range(len(ATT_DILATIONS)):
        e = jnp.exp(m_scr[di] - m_all)
        num = num + acc_scr[di] * e
        den = den + l_scr[di] * e
    y_ref[...] = (num / den).astype(BF16)


def _attention_mixer(p, q_g, k_g, rel_bias, seq_len):
    T = p.shape[0]
    tile = ATT_TILE
    n_tiles = T // tile
    rad = ATT_RADIUS
    seg = np.kron(np.eye(2), np.full((ATT_HEAD_DIM, ATT_HEAD_DIM), 1.0 / ATT_HEAD_DIM))
    mseg = jnp.asarray(seg, BF16)
    bkt = jnp.asarray(_t5_bucket_table())
    qg2 = jnp.tile(q_g, 2)[None]
    kg2 = jnp.tile(k_g, 2)[None]

    def blk(col0, shift):
        return pl.BlockSpec((tile, 128),
                            lambda hp, i: (jnp.clip(i + shift, 0, n_tiles - 1), col0 * 4 + hp))

    const = lambda shape: pl.BlockSpec(shape, lambda hp, i: (0,) * len(shape))
    return pl.pallas_call(
        functools.partial(_att_kernel, seq_len=seq_len),
        grid=(4, n_tiles),
        in_specs=[
            pl.BlockSpec(memory_space=pltpu.SMEM),
            const((3, rad, 3 * rad)), const((1, 128)), const((1, 128)), const((128, 128)),
            blk(1, 0),
            blk(2, -1), blk(2, 0), blk(2, 1),
            blk(3, -1), blk(3, 0), blk(3, 1),
        ],
        out_specs=pl.BlockSpec((tile, 128), lambda hp, i: (i, hp)),
        out_shape=jax.ShapeDtypeStruct((T, GROUP), BF16),
        scratch_shapes=[
            pltpu.VMEM((3, 2 * rad, 3 * rad), F32),
            pltpu.VMEM((tile, 128), F32),
            pltpu.VMEM((3 * tile, 128), F32),
            pltpu.VMEM((3 * tile, 128), F32),
            pltpu.VMEM((3, tile, 128), F32),
            pltpu.VMEM((3, tile, 128), F32),
            pltpu.VMEM((3, tile, 128), F32),
        ],
        compiler_params=_cparams("arbitrary", "arbitrary"),
        name="dilated_attention",
    )(rel_bias, bkt, qg2, kg2, mseg, p, p, p, p, p, p, p)


def _softplus(x):
    return jnp.maximum(x, 0.0) + jnp.log1p(jnp.exp(-jnp.abs(x)))


def _dn_prep_kernel(x_ref, xp_ref, xn_ref, w_ref, o_ref, ext_scr, *, tm, seq_len):
    i = pl.program_id(0)
    first, last, _ = _seq_edges(i, seq_len // tm)
    ext_scr[0:8, :] = jnp.where(first, jnp.zeros_like(xp_ref[...]), xp_ref[...])
    ext_scr[8:8 + tm, :] = x_ref[...]
    ext_scr[8 + tm:16 + tm, :] = jnp.where(last, jnp.zeros_like(xn_ref[...]), xn_ref[...])
    for cb in range(3 * DN_HEADS):
        lanes = slice(cb * 128, (cb + 1) * 128)
        acc = ext_scr[pl.ds(6, tm), lanes] * w_ref[0:1, lanes]
        for kk in range(1, 4):
            acc = acc + ext_scr[pl.ds(6 + kk, tm), lanes] * w_ref[kk:kk + 1, lanes]
        y = _silu(acc)
        if cb < 2 * DN_HEADS:
            y = y * lax.rsqrt(jnp.sum(y * y, axis=-1, keepdims=True) + EPS)
        if cb < DN_HEADS:
            y = y * (DN_HEAD_DIM ** -0.5)
        o_ref[:, lanes] = y


def _dn_prep(p, conv_w, seq_len, tm=512):
    T = p.shape[0]
    n_tiles = T // tm
    width = 3 * GROUP
    prev_spec, next_spec = _halo_specs(tm, 8, width, 2, n_tiles)
    return pl.pallas_call(
        functools.partial(_dn_prep_kernel, tm=tm, seq_len=seq_len),
        grid=(n_tiles,),
        in_specs=[pl.BlockSpec((tm, width), lambda i: (i, 2)), prev_spec, next_spec,
                  pl.BlockSpec((4, width), lambda i: (0, 0))],
        out_specs=pl.BlockSpec((tm, width), lambda i: (i, 0)),
        out_shape=jax.ShapeDtypeStruct((T, width), F32),
        scratch_shapes=[pltpu.VMEM((tm + 16, width), F32)],
        compiler_params=_cparams("parallel"),
        name="deltanet_prep",
    )(p, p, p, conv_w)


DN_GROUP = 4
DN_ROWS = DN_HEADS * DN_CHUNK


def _dn_scan_kernel(xf_ref, xb_ref, gf_ref, gb_ref, rf_ref, rb_ref, alc_ref, dtc_ref, alr_ref, dtr_ref,
                    of_ref, ob_ref, s_scr):
    C = DN_CHUNK
    H = DN_HEADS
    R = DN_GROUP * C

    @pl.when(pl.program_id(1) == 0)
    def _():
        s_scr[...] = jnp.zeros_like(s_scr)

    row_in_chunk = lax.rem(lax.broadcasted_iota(jnp.int32, (R, 1), 0), C)
    lane_in_chunk = lax.rem(lax.broadcasted_iota(jnp.int32, (1, DN_ROWS), 1), C)
    rid = lax.broadcasted_iota(jnp.int32, (DN_ROWS, DN_ROWS), 0)
    cid = lax.broadcasted_iota(jnp.int32, (DN_ROWS, DN_ROWS), 1)
    same_head = (rid // C) == (cid // C)
    eye = (rid == cid).astype(F32)

    def stack_heads(x, col0):
        return jnp.concatenate([x[:, col0 + h * 128:col0 + (h + 1) * 128] for h in range(H)], axis=0)

    def stack_cols(x, lane0, rows=None):
        parts = []
        for h in range(H):
            c = x[:, lane0 + h:lane0 + h + 1]
            if rows is not None:
                c = c[rows:rows + 1, :]
            parts.append(jnp.broadcast_to(c, (C, 128)))
        return jnp.concatenate(parts, axis=0)

    for d, (x_ref, gc_ref, gr_ref, o_ref) in enumerate(((xf_ref, gf_ref, rf_ref, of_ref),
                                                        (xb_ref, gb_ref, rb_ref, ob_ref))):
        reverse = d == 1
        gcol = gc_ref[...]
        g = -jnp.exp(alc_ref[...]) * _softplus(gcol + dtc_ref[...])
        beta = _sigmoid(gcol)
        gcum = g
        for sh in (1, 2, 4, 8, 16, 32):
            if reverse:
                gcum = gcum + jnp.where(row_in_chunk < C - sh, pltpu.roll(gcum, R - sh, 0), 0.0)
            else:
                gcum = gcum + jnp.where(row_in_chunk >= sh, pltpu.roll(gcum, sh, 0), 0.0)
        grow_all = -jnp.exp(alr_ref[...])[None] * _softplus(gr_ref[...] + dtr_ref[...][None])
        for sh in (1, 2, 4, 8, 16, 32):
            if reverse:
                grow_all = grow_all + jnp.where(lane_in_chunk < C - sh,
                                                pltpu.roll(grow_all, DN_ROWS - sh, 2), 0.0)
            else:
                grow_all = grow_all + jnp.where(lane_in_chunk >= sh, pltpu.roll(grow_all, sh, 2), 0.0)

        tri = (rid <= cid) if reverse else (rid >= cid)
        incl = same_head & tri
        strict = incl & (rid != cid)
        last_row = 0 if reverse else C - 1

        order = range(DN_GROUP - 1, -1, -1) if reverse else range(DN_GROUP)
        for ci in order:
            rows = slice(ci * C, (ci + 1) * C)
            x = x_ref[rows, :]
            q_st = stack_heads(x, 0)
            k_st = stack_heads(x, GROUP)
            v_st = stack_heads(x, 2 * GROUP)
            gc_c = gcum[rows, :]
            beta_st = stack_cols(beta[rows, :], 8 + d * H)
            gcol_st = stack_cols(gc_c, d * H)
            glast_st = stack_cols(gc_c, d * H, rows=last_row)
            grow = grow_all[ci, d:d + 1, :]

            decay = jnp.where(incl, jnp.exp(jnp.concatenate([gcol_st, gcol_st], axis=1) - grow), 0.0)
            kb_st = k_st * beta_st
            kk = _dot_nt(jnp.concatenate([kb_st, q_st], axis=0).astype(BF16), k_st.astype(BF16))
            a = jnp.where(strict, kk[0:DN_ROWS] * decay, 0.0)
            attn = kk[DN_ROWS:2 * DN_ROWS] * decay

            pinv = eye - a
            apow = _dot(a.astype(BF16), a.astype(BF16))
            for _ in range(4):
                both = _dot(jnp.concatenate([pinv, apow], axis=0).astype(BF16), apow.astype(BF16))
                pinv = pinv + both[0:DN_ROWS]
                apow = both[DN_ROWS:2 * DN_ROWS]
            pinv = pinv + _dot(pinv.astype(BF16), apow.astype(BF16))

            eg = jnp.exp(gcol_st)
            rhs = jnp.concatenate([v_st * beta_st, kb_st * eg], axis=1).astype(BF16)
            uw = _dot(pinv.astype(BF16), rhs)
            u_st = uw[:, 0:128]
            w_st = uw[:, 128:256]
            qdec_st = q_st * eg
            kdec_st = k_st * jnp.exp(glast_st - gcol_st)
            gl_st = jnp.exp(glast_st)

            vnew_parts = []
            qs_parts = []
            for h in range(H):
                hr = slice(h * C, (h + 1) * C)
                s_h = s_scr[d * H + h].astype(BF16)
                wq = _dot(jnp.concatenate([w_st[hr], qdec_st[hr]], axis=0).astype(BF16), s_h)
                vnew_parts.append(u_st[hr] - wq[0:C])
                qs_parts.append(wq[C:2 * C])
            vnew_st = jnp.concatenate(vnew_parts, axis=0)
            o_st = jnp.concatenate(qs_parts, axis=0) + _dot(attn.astype(BF16), vnew_st.astype(BF16))
            for h in range(H):
                hr = slice(h * C, (h + 1) * C)
                s_scr[d * H + h] = (s_scr[d * H + h] * gl_st[h * C:h * C + 1, :]
                                    + _dot_tn(kdec_st[hr].astype(BF16), vnew_parts[h].astype(BF16)))
                o_ref[rows, h * 128:(h + 1) * 128] = o_st[hr]


def _dn_scan(qkvn, pg, a_log, dt_bias, batch, seq_len):
    T = qkvn.shape[0]
    C, H, G = DN_CHUNK, DN_HEADS, DN_GROUP
    R = G * C
    nc = seq_len // C
    ncg = nc // G
    ab = pg[:, 0:4 * H].reshape(batch, nc, C, 2, 2, H)
    ab_row = jnp.transpose(ab, (0, 1, 3, 4, 5, 2)).reshape(batch * nc, 4, H * C)
    ab_row = jnp.pad(ab_row, ((0, 0), (0, 4), (0, 0)))
    pad_lanes = lambda v: jnp.pad(v.reshape(1, 2 * H), ((0, 0), (0, GATE_LANES - 2 * H)))
    row_param = lambda v: jnp.pad(jnp.repeat(v, C, axis=1), ((0, 6), (0, 0)))

    fwd = lambda b, c: (b * ncg + c, 0)
    bwd = lambda b, c: (b * ncg + ncg - 1 - c, 0)
    fwd3 = lambda b, c: (b * ncg + c, 0, 0)
    bwd3 = lambda b, c: (b * ncg + ncg - 1 - c, 0, 0)
    const = lambda shape: pl.BlockSpec(shape, lambda b, c: (0,) * len(shape))
    return pl.pallas_call(
        _dn_scan_kernel,
        grid=(batch, ncg),
        in_specs=[
            pl.BlockSpec((R, 3 * GROUP), fwd), pl.BlockSpec((R, 3 * GROUP), bwd),
            pl.BlockSpec((R, GATE_LANES), fwd), pl.BlockSpec((R, GATE_LANES), bwd),
            pl.BlockSpec((G, 8, H * C), fwd3), pl.BlockSpec((G, 8, H * C), bwd3),
            const((1, GATE_LANES)), const((1, GATE_LANES)), const((8, H * C)), const((8, H * C)),
        ],
        out_specs=[pl.BlockSpec((R, GROUP), fwd), pl.BlockSpec((R, GROUP), bwd)],
        out_shape=[jax.ShapeDtypeStruct((T, GROUP), F32), jax.ShapeDtypeStruct((T, GROUP), F32)],
        scratch_shapes=[pltpu.VMEM((2 * H, DN_HEAD_DIM, DN_HEAD_DIM), F32)],
        compiler_params=_cparams("arbitrary", "arbitrary"),
        name="deltanet_scan",
    )(qkvn, qkvn, pg, pg, ab_row, ab_row, pad_lanes(a_log), pad_lanes(dt_bias),
      row_param(a_log), row_param(dt_bias))


def _dn_post_kernel(of_ref, ob_ref, z_ref, g_ref, y_ref):
    for h in range(DN_HEADS):
        lanes = slice(h * 128, (h + 1) * 128)
        o = of_ref[:, lanes] + ob_ref[:, lanes]
        o = o * lax.rsqrt(jnp.mean(o * o, axis=-1, keepdims=True) + EPS) * g_ref[...]
        y_ref[:, lanes] = (o * _silu(z_ref[:, lanes])).astype(BF16)


def _dn_post(o_f, o_b, p, norm_g, tm=1024):
    T = p.shape[0]
    row = lambda cb: pl.BlockSpec((tm, GROUP), lambda i: (i, cb))
    return pl.pallas_call(
        _dn_post_kernel,
        grid=(T // tm,),
        in_specs=[row(0), row(0), row(9), pl.BlockSpec((1, DN_HEAD_DIM), lambda i: (0, 0))],
        out_specs=row(0),
        out_shape=jax.ShapeDtypeStruct((T, GROUP), BF16),
        compiler_params=_cparams("parallel"),
        name="deltanet_post",
    )(o_f, o_b, p, norm_g)


def _deltanet_mixer(p, pg, conv_w, a_log, dt_bias, norm_g, batch, seq_len):
    qkvn = _dn_prep(p, conv_w, seq_len)
    o_f, o_b = _dn_scan(qkvn, pg, a_log, dt_bias, batch, seq_len)
    return _dn_post(o_f, o_b, p, norm_g[None])


def _outproj_kernel(x_ref, ya_ref, yb_ref, yc_ref, yd_ref, w_ref, g_ref, xn_ref, hn_ref):
    acc = x_ref[...]
    for gi, y_ref in enumerate((ya_ref, yb_ref, yc_ref, yd_ref)):
        acc = acc + _dot(y_ref[...], w_ref[gi * GROUP:(gi + 1) * GROUP, :])
    xn_ref[...] = acc
    ms = jnp.mean(acc * acc, axis=-1, keepdims=True)
    hn_ref[...] = (acc * lax.rsqrt(ms + EPS) * g_ref[...]).astype(BF16)


def _out_proj(x2, ys, w_out, norm2_g, tm=512):
    T = x2.shape[0]
    row = lambda width: pl.BlockSpec((tm, width), lambda i: (i, 0))
    return pl.pallas_call(
        _outproj_kernel,
        grid=(T // tm,),
        in_specs=[row(D_MODEL), row(GROUP), row(GROUP), row(GROUP), row(GROUP),
                  pl.BlockSpec((D_MODEL, D_MODEL), lambda i: (0, 0)),
                  pl.BlockSpec((1, D_MODEL), lambda i: (0, 0))],
        out_specs=[row(D_MODEL), row(D_MODEL)],
        out_shape=[jax.ShapeDtypeStruct((T, D_MODEL), F32),
                   jax.ShapeDtypeStruct((T, D_MODEL), BF16)],
        compiler_params=_cparams("parallel"),
        name="out_proj",
    )(x2, *ys, w_out, norm2_g)


def _ffn_kernel(hn_ref, hp_ref, hx_ref, xn_ref, wg_ref, wu_ref, wd_ref, dw_ref, db_ref, o_ref,
                hext_scr, g_scr, *, tm, seq_len):
    i = pl.program_id(0)

    @pl.when(pl.program_id(1) == 0)
    def _():
        first, last, _ = _seq_edges(i, seq_len // tm)
        hext_scr[0:16, :] = jnp.where(first, jnp.zeros_like(hp_ref[...]), hp_ref[...])
        hext_scr[16:16 + tm, :] = hn_ref[...]
        hext_scr[16 + tm:32 + tm, :] = jnp.where(last, jnp.zeros_like(hx_ref[...]), hx_ref[...])
        o_ref[...] = xn_ref[...]

    g_scr[...] = _dot(hext_scr[...], wg_ref[...])
    gate = (g_scr[pl.ds(15, tm), :] * dw_ref[0:1, :] + g_scr[pl.ds(16, tm), :] * dw_ref[1:2, :]
            + g_scr[pl.ds(17, tm), :] * dw_ref[2:3, :] + db_ref[...])
    up = _dot(hn_ref[...], wu_ref[...])
    a = (_silu(gate) * up).astype(BF16)
    o_ref[...] += _dot(a, wd_ref[...])


def _ffn(hn, xn, w_gate, dw_w, dw_b, w_up, w_down, seq_len, tm=1024, th=512):
    T = hn.shape[0]
    n_tiles = T // tm
    hprev, hnext = _halo_specs(tm, 16, D_MODEL, 0, n_tiles)
    return pl.pallas_call(
        functools.partial(_ffn_kernel, tm=tm, seq_len=seq_len),
        grid=(n_tiles, FFN_HIDDEN // th),
        in_specs=[
            pl.BlockSpec((tm, D_MODEL), lambda i, j: (i, 0)),
            hprev, hnext,
            pl.BlockSpec((tm, D_MODEL), lambda i, j: (i, 0), pipeline_mode=pl.Buffered(1)),
            pl.BlockSpec((D_MODEL, th), lambda i, j: (0, j)),
            pl.BlockSpec((D_MODEL, th), lambda i, j: (0, j)),
            pl.BlockSpec((th, D_MODEL), lambda i, j: (j, 0)),
            pl.BlockSpec((3, th), lambda i, j: (0, j)),
            pl.BlockSpec((1, th), lambda i, j: (0, j)),
        ],
        out_specs=pl.BlockSpec((tm, D_MODEL), lambda i, j: (i, 0)),
        out_shape=jax.ShapeDtypeStruct((T, D_MODEL), F32),
        scratch_shapes=[pltpu.VMEM((tm + 32, D_MODEL), BF16), pltpu.VMEM((tm + 32, th), F32)],
        compiler_params=_cparams("parallel", "arbitrary"),
        name="conv_glu_ffn",
    )(hn, hn, hn, xn, w_gate, w_up, w_down, dw_w, dw_b)


def _layer(x2, l, batch, seq_len, rel_bias, norm1_g, w_in, w_pool, pool_scale, att_q_g, att_k_g,
           conv_dw_w, conv_dw_b, conv_ln_g, conv_ln_b, conv_pw, dn_conv_w, dn_a_log, dn_dt_bias,
           dn_norm_g, w_out, norm2_g, ffn_w_gate, ffn_dw_w, ffn_dw_b, ffn_w_up, ffn_w_down):
    w_main = w_in[l][:, :PROJ_MAIN].astype(BF16)
    w_gates = jnp.pad(w_in[l][:, PROJ_MAIN:], ((0, 0), (0, GATE_LANES - 4 * DN_HEADS))).astype(BF16)
    p, pg = _in_proj(x2, norm1_g[l][None], w_main, w_gates)
    ya = _pool_mixer(p, w_pool[l].astype(BF16), pool_scale[l][None], seq_len)
    yb = _attention_mixer(p, att_q_g[l], att_k_g[l], rel_bias, seq_len)
    yc = _conformer_mixer(p, conv_dw_w[l], conv_dw_b[l][None], conv_ln_g[l][None], conv_ln_b[l][None],
                          conv_pw[l].astype(BF16), seq_len)
    yd = _deltanet_mixer(p, pg, dn_conv_w[l], dn_a_log[l], dn_dt_bias[l], dn_norm_g[l], batch, seq_len)
    xn, hn = _out_proj(x2, (ya, yb, yc, yd), w_out[l].astype(BF16), norm2_g[l][None])
    return _ffn(hn, xn, ffn_w_gate[l].astype(BF16), ffn_dw_w[l], ffn_dw_b[l][None],
                ffn_w_up[l].astype(BF16), ffn_w_down[l].astype(BF16), seq_len)


def kernel(x, rel_bias, norm1_g, w_in, w_pool, pool_scale, att_q_g, att_k_g, conv_dw_w, conv_dw_b,
           conv_ln_g, conv_ln_b, conv_pw, dn_conv_w, dn_a_log, dn_dt_bias, dn_norm_g, w_out, norm2_g,
           ffn_w_gate, ffn_dw_w, ffn_dw_b, ffn_w_up, ffn_w_down):
    batch, seq_len, _ = x.shape
    x2 = x.reshape(batch * seq_len, D_MODEL)
    for l in range(norm1_g.shape[0]):
        x2 = _layer(x2, l, batch, seq_len, rel_bias, norm1_g, w_in, w_pool, pool_scale, att_q_g,
                    att_k_g, conv_dw_w, conv_dw_b, conv_ln_g, conv_ln_b, conv_pw, dn_conv_w, dn_a_log,
                    dn_dt_bias, dn_norm_g, w_out, norm2_g, ffn_w_gate, ffn_dw_w, ffn_dw_b, ffn_w_up,
                    ffn_w_down)
    return x2.reshape(batch, seq_len, D_MODEL)
```

```python
import functools
import math

import jax
import jax.numpy as jnp
import numpy as np
from jax import lax
from jax.experimental import pallas as pl
from jax.experimental.pallas import tpu as pltpu

F32 = jnp.float32
BF16 = jnp.bfloat16

D_MODEL = 2048
GROUP = 512
POOL_WINDOWS = (2, 4, 8, 16)
ATT_HEAD_DIM = 64
ATT_HEADS = 8
ATT_RADIUS = 64
ATT_DILATIONS = (1, 4, 16)
REL_BUCKETS = 32
REL_MAX_DIST = 1024
CONV_WIDTH = 31
DN_HEAD_DIM = 128
DN_HEADS = 4
DN_CHUNK = 64
FFN_HIDDEN = 5632
EPS = 1e-6
NEG_INF = -1e30

PROJ_MAIN = 10 * GROUP
GATE_LANES = 128

VMEM_LIMIT_BYTES = 56 * 1024 * 1024


def _cparams(*sem):
    return pltpu.CompilerParams(dimension_semantics=sem, vmem_limit_bytes=VMEM_LIMIT_BYTES)


def _sigmoid(x):
    return 1.0 / (1.0 + jnp.exp(-x))


def _silu(x):
    return x * _sigmoid(x)


def _dot(a, b):
    return jnp.dot(a, b, preferred_element_type=F32)


def _dot_nt(a, b):
    return lax.dot_general(a, b, (((1,), (1,)), ((), ())), preferred_element_type=F32)


def _dot_tn(a, b):
    return lax.dot_general(a, b, (((0,), (0,)), ((), ())), preferred_element_type=F32)


def _inproj_kernel(x_ref, g_ref, w_ref, wg_ref, p_ref, pg_ref, h_scr):
    @pl.when(pl.program_id(1) == 0)
    def _():
        x = x_ref[...]
        ms = jnp.mean(x * x, axis=-1, keepdims=True)
        h = (x * lax.rsqrt(ms + EPS) * g_ref[...]).astype(BF16)
        h_scr[...] = h
        pg_ref[...] = _dot(h, wg_ref[...])

    p_ref[...] = _dot(h_scr[...], w_ref[...])


def _in_proj(x2, norm_g, w_main, w_gate, tm=1024, tn=1024):
    T = x2.shape[0]
    return pl.pallas_call(
        _inproj_kernel,
        grid=(T // tm, PROJ_MAIN // tn),
        in_specs=[
            pl.BlockSpec((tm, D_MODEL), lambda i, j: (i, 0)),
            pl.BlockSpec((1, D_MODEL), lambda i, j: (0, 0)),
            pl.BlockSpec((D_MODEL, tn), lambda i, j: (0, j)),
            pl.BlockSpec((D_MODEL, GATE_LANES), lambda i, j: (0, 0)),
        ],
        out_specs=[
            pl.BlockSpec((tm, tn), lambda i, j: (i, j)),
            pl.BlockSpec((tm, GATE_LANES), lambda i, j: (i, 0)),
        ],
        out_shape=[
            jax.ShapeDtypeStruct((T, PROJ_MAIN), F32),
            jax.ShapeDtypeStruct((T, GATE_LANES), F32),
        ],
        scratch_shapes=[pltpu.VMEM((tm, D_MODEL), BF16)],
        compiler_params=_cparams("parallel", "arbitrary"),
        name="in_proj",
    )(x2, norm_g, w_main, w_gate)


def _halo_specs(tm, halo, width, col_block, n_tiles):
    r = tm // halo
    last = n_tiles * r - 1
    prev_spec = pl.BlockSpec((halo, width), lambda i, *_: (jnp.maximum(i * r - 1, 0), col_block))
    next_spec = pl.BlockSpec((halo, width), lambda i, *_: (jnp.minimum((i + 1) * r, last), col_block))
    return prev_spec, next_spec


def _seq_edges(i, tiles_per_seq):
    k = lax.rem(i, tiles_per_seq)
    return k == 0, k == tiles_per_seq - 1, k


def _pool_kernel(u_ref, up_ref, un_ref, w_ref, sc_ref, y_ref, ext_scr, *, tm, seq_len):
    i = pl.program_id(0)
    first, last, k = _seq_edges(i, seq_len // tm)
    ext_scr[0:8, :] = jnp.where(first, jnp.zeros_like(up_ref[...]), up_ref[...])
    ext_scr[8:8 + tm, :] = u_ref[...]
    ext_scr[8 + tm:16 + tm, :] = jnp.where(last, jnp.zeros_like(un_ref[...]), un_ref[...])
    t = k * tm + lax.broadcasted_iota(jnp.int32, (tm, 1), 0)
    for gi, win in enumerate(POOL_WINDOWS):
        half = win // 2
        lanes = slice(gi * 128, (gi + 1) * 128)
        s = ext_scr[pl.ds(8 - half, tm), lanes]
        for kk in range(1, win):
            s = s + ext_scr[pl.ds(8 - half + kk, tm), lanes]
        cnt = (jnp.minimum(t + half, seq_len) - jnp.maximum(t - half, 0)).astype(F32)
        pooled = s / cnt - ext_scr[pl.ds(8, tm), lanes]
        y = _dot(pooled.astype(BF16), w_ref[gi]) * sc_ref[:, lanes]
        y_ref[:, lanes] = y.astype(BF16)


def _pool_mixer(p, w_pool, pool_scale, seq_len, tm=1024):
    T = p.shape[0]
    n_tiles = T // tm
    prev_spec, next_spec = _halo_specs(tm, 8, GROUP, 0, n_tiles)
    return pl.pallas_call(
        functools.partial(_pool_kernel, tm=tm, seq_len=seq_len),
        grid=(n_tiles,),
        in_specs=[
            pl.BlockSpec((tm, GROUP), lambda i: (i, 0)),
            prev_spec,
            next_spec,
            pl.BlockSpec((4, 128, 128), lambda i: (0, 0, 0)),
            pl.BlockSpec((1, GROUP), lambda i: (0, 0)),
        ],
        out_specs=pl.BlockSpec((tm, GROUP), lambda i: (i, 0)),
        out_shape=jax.ShapeDtypeStruct((T, GROUP), BF16),
        scratch_shapes=[pltpu.VMEM((tm + 16, GROUP), F32)],
        compiler_params=_cparams("parallel"),
        name="pool_mixer",
    )(p, p, p, w_pool, pool_scale)


def _conformer_kernel(v_ref, vp_ref, vn_ref, g_ref, gp_ref, gn_ref, dw_ref, db_ref, lg_ref, lb_ref,
                      pw_ref, y_ref, ext_scr, *, tm, seq_len):
    i = pl.program_id(0)
    first, last, _ = _seq_edges(i, seq_len // tm)
    hp = vp_ref[...] * _sigmoid(gp_ref[...])
    hn = vn_ref[...] * _sigmoid(gn_ref[...])
    ext_scr[0:16, :] = jnp.where(first, jnp.zeros_like(hp), hp)
    ext_scr[16:16 + tm, :] = v_ref[...] * _sigmoid(g_ref[...])
    ext_scr[16 + tm:32 + tm, :] = jnp.where(last, jnp.zeros_like(hn), hn)
    pad = CONV_WIDTH // 2
    acc = ext_scr[pl.ds(16 - pad, tm), :] * dw_ref[0:1, :]
    for kk in range(1, CONV_WIDTH):
        acc = acc + ext_scr[pl.ds(16 - pad + kk, tm), :] * dw_ref[kk:kk + 1, :]
    acc = acc + db_ref[...]
    mu = jnp.mean(acc, axis=-1, keepdims=True)
    xc = acc - mu
    var = jnp.mean(xc * xc, axis=-1, keepdims=True)
    h = _silu(xc * lax.rsqrt(var + EPS) * lg_ref[...] + lb_ref[...])
    y_ref[...] = _dot(h.astype(BF16), pw_ref[...]).astype(BF16)


def _conformer_mixer(p, dw_w, dw_b, ln_g, ln_b, pw, seq_len, tm=512):
    T = p.shape[0]
    n_tiles = T // tm
    vprev, vnext = _halo_specs(tm, 16, GROUP, 4, n_tiles)
    gprev, gnext = _halo_specs(tm, 16, GROUP, 5, n_tiles)
    const = lambda shape: pl.BlockSpec(shape, lambda i: (0,) * len(shape))
    return pl.pallas_call(
        functools.partial(_conformer_kernel, tm=tm, seq_len=seq_len),
        grid=(n_tiles,),
        in_specs=[
            pl.BlockSpec((tm, GROUP), lambda i: (i, 4)), vprev, vnext,
            pl.BlockSpec((tm, GROUP), lambda i: (i, 5)), gprev, gnext,
            const((CONV_WIDTH, GROUP)), const((1, GROUP)), const((1, GROUP)), const((1, GROUP)),
            const((GROUP, GROUP)),
        ],
        out_specs=pl.BlockSpec((tm, GROUP), lambda i: (i, 0)),
        out_shape=jax.ShapeDtypeStruct((T, GROUP), BF16),
        scratch_shapes=[pltpu.VMEM((tm + 32, GROUP), F32)],
        compiler_params=_cparams("parallel"),
        name="conformer_mixer",
    )(p, p, p, p, p, p, dw_w, dw_b, ln_g, ln_b, pw)


ATT_TILE = 1024
ATT_BLOCK_GROUP = 8


def _t5_bucket_table():
    nb = REL_BUCKETS // 2
    max_exact = nb // 2
    i = np.arange(ATT_RADIUS)[:, None]
    j = np.arange(3 * ATT_RADIUS)[None, :]
    off = j - ATT_RADIUS - i
    tables = []
    for dil in ATT_DILATIONS:
        rel = off * dil
        n = np.abs(rel)
        nf = np.maximum(n, 1).astype(np.float32)
        large = max_exact + (np.log(nf / np.float32(max_exact)) / np.float32(math.log(REL_MAX_DIST / max_exact))
                             * np.float32(nb - max_exact)).astype(np.int32)
        large = np.minimum(large, nb - 1)
        bucket = np.where(rel > 0, nb, 0) + np.where(n < max_exact, n, large)
        tables.append(np.where(np.abs(off) <= ATT_RADIUS, bucket, -1))
    return np.stack(tables).astype(np.int32)


def _att_kernel(rb_ref, bkt_ref, qg_ref, kg_ref, mseg_ref, q_ref, kp_ref, kc_ref, kn_ref,
                vp_ref, vc_ref, vn_ref, y_ref, bias_scr, qbuf, kbuf, vbuf, acc_scr, m_scr, l_scr,
                *, seq_len):
    tile = ATT_TILE
    rad = ATT_RADIUS
    hp = pl.program_id(0)
    i = pl.program_id(1)

    @pl.when(i == 0)
    def _():
        for di in range(len(ATT_DILATIONS)):
            bkt = bkt_ref[di]
            for hh in range(2):
                b = jnp.zeros(bkt.shape, F32)
                for bb in range(REL_BUCKETS):
                    b = jnp.where(bkt == bb, rb_ref[bb, 2 * hp + hh], b)
                bias_scr[di, hh * rad:(hh + 1) * rad, :] = jnp.where(bkt < 0, NEG_INF, b)

    first, last, _ = _seq_edges(i, seq_len // tile)

    def rms(x, g):
        ms = _dot((x * x).astype(BF16), mseg_ref[...])
        return x * lax.rsqrt(ms + EPS) * g

    qbuf[...] = rms(q_ref[...], qg_ref[...]) * (ATT_HEAD_DIM ** -0.5)
    kbuf[0:tile, :] = rms(kp_ref[...], kg_ref[...])
    kbuf[tile:2 * tile, :] = rms(kc_ref[...], kg_ref[...])
    kbuf[2 * tile:3 * tile, :] = rms(kn_ref[...], kg_ref[...])
    vbuf[0:tile, :] = vp_ref[...]
    vbuf[tile:2 * tile, :] = vc_ref[...]
    vbuf[2 * tile:3 * tile, :] = vn_ref[...]

    head0 = lax.broadcasted_iota(jnp.int32, (1, 128), 1) < ATT_HEAD_DIM
    col = lax.broadcasted_iota(jnp.int32, (1, 3 * rad), 1)

    def rows(start, size, dil):
        return pl.ds(start, size) if dil == 1 else pl.ds(start, size, stride=dil)

    blocks = [(di, dil, r + rad * dil * m)
              for di, dil in enumerate(ATT_DILATIONS) for r in range(dil) for m in range(tile // (rad * dil))]
    for g0 in range(0, len(blocks), ATT_BLOCK_GROUP):
        group = blocks[g0:g0 + ATT_BLOCK_GROUP]
        scores = []
        for di, dil, qstart in group:
            kstart = tile + qstart - rad * dil
            qb = qbuf[rows(qstart, rad, dil), :]
            kb = kbuf[rows(kstart, 3 * rad, dil), :].astype(BF16)
            q2 = jnp.concatenate([jnp.where(head0, qb, 0.0), jnp.where(head0, 0.0, qb)], axis=0)
            scores.append(_dot_nt(q2.astype(BF16), kb))
        probs = []
        for (di, dil, qstart), s in zip(group, scores):
            kstart = tile + qstart - rad * dil
            s = s + bias_scr[di]
            n_prev = max(0, -(-(tile - kstart) // dil))
            n_upto = min(3 * rad, -(-(2 * tile - kstart) // dil))
            if n_prev > 0:
                s = jnp.where(col < jnp.where(first, n_prev, 0), NEG_INF, s)
            if n_upto < 3 * rad:
                s = jnp.where(col >= jnp.where(last, n_upto, 3 * rad), NEG_INF, s)
            mx = jnp.max(s, axis=-1, keepdims=True)
            pe = jnp.exp(s - mx)
            probs.append((mx, jnp.sum(pe, axis=-1, keepdims=True), pe.astype(BF16)))
        outs = []
        for (di, dil, qstart), (mx, ls, pe) in zip(group, probs):
            kstart = tile + qstart - rad * dil
            outs.append(_dot(pe, vbuf[rows(kstart, 3 * rad, dil), :].astype(BF16)))
        for (di, dil, qstart), (mx, ls, pe), o in zip(group, probs, outs):
            dst = rows(qstart, rad, dil)
            acc_scr[di, dst, :] = jnp.where(head0, o[0:rad], o[rad:2 * rad])
            m_scr[di, dst, :] = jnp.where(head0, mx[0:rad], mx[rad:2 * rad])
            l_scr[di, dst, :] = jnp.where(head0, ls[0:rad], ls[rad:2 * rad])

    m_all = jnp.maximum(jnp.maximum(m_scr[0], m_scr[1]), m_scr[2])
    num = jnp.zeros((tile, 128), F32)
    den = jnp.zeros((tile, 128), F32)
    for di in range(len(ATT_DILATIONS)):
        e = jnp.exp(m_scr[di] - m_all)
        num = num + acc_scr[di] * e
        den = den + l_scr[di] * e
    y_ref[...] = (num / den).astype(BF16)


def _attention_mixer(p, q_g, k_g, rel_bias, seq_len):
    T = p.shape[0]
    tile = ATT_TILE
    n_tiles = T // tile
    rad = ATT_RADIUS
    seg = np.kron(np.eye(2), np.full((ATT_HEAD_DIM, ATT_HEAD_DIM), 1.0 / ATT_HEAD_DIM))
    mseg = jnp.asarray(seg, BF16)
    bkt = jnp.asarray(_t5_bucket_table())
    qg2 = jnp.tile(q_g, 2)[None]
    kg2 = jnp.tile(k_g, 2)[None]

    def blk(col0, shift):
        return pl.BlockSpec((tile, 128),
                            lambda hp, i: (jnp.clip(i + shift, 0, n_tiles - 1), col0 * 4 + hp))

    const = lambda shape: pl.BlockSpec(shape, lambda hp, i: (0,) * len(shape))
    return pl.pallas_call(
        functools.partial(_att_kernel, seq_len=seq_len),
        grid=(4, n_tiles),
        in_specs=[
            pl.BlockSpec(memory_space=pltpu.SMEM),
            const((3, rad, 3 * rad)), const((1, 128)), const((1, 128)), const((128, 128)),
            blk(1, 0),
            blk(2, -1), blk(2, 0), blk(2, 1),
            blk(3, -1), blk(3, 0), blk(3, 1),
        ],
        out_specs=pl.BlockSpec((tile, 128), lambda hp, i: (i, hp)),
        out_shape=jax.ShapeDtypeStruct((T, GROUP), BF16),
        scratch_shapes=[
            pltpu.VMEM((3, 2 * rad, 3 * rad), F32),
            pltpu.VMEM((tile, 128), F32),
            pltpu.VMEM((3 * tile, 128), F32),
            pltpu.VMEM((3 * tile, 128), F32),
            pltpu.VMEM((3, tile, 128), F32),
            pltpu.VMEM((3, tile, 128), F32),
            pltpu.VMEM((3, tile, 128), F32),
        ],
        compiler_params=_cparams("arbitrary", "arbitrary"),
        name="dilated_attention",
    )(rel_bias, bkt, qg2, kg2, mseg, p, p, p, p, p, p, p)


def _softplus(x):
    return jnp.maximum(x, 0.0) + jnp.log1p(jnp.exp(-jnp.abs(x)))


def _dn_prep_kernel(x_ref, xp_ref, xn_ref, w_ref, o_ref, ext_scr, *, tm, seq_len):
    i = pl.program_id(0)
    first, last, _ = _seq_edges(i, seq_len // tm)
    ext_scr[0:8, :] = jnp.where(first, jnp.zeros_like(xp_ref[...]), xp_ref[...])
    ext_scr[8:8 + tm, :] = x_ref[...]
    ext_scr[8 + tm:16 + tm, :] = jnp.where(last, jnp.zeros_like(xn_ref[...]), xn_ref[...])
    for cb in range(3 * DN_HEADS):
        lanes = slice(cb * 128, (cb + 1) * 128)
        acc = ext_scr[pl.ds(6, tm), lanes] * w_ref[0:1, lanes]
        for kk in range(1, 4):
            acc = acc + ext_scr[pl.ds(6 + kk, tm), lanes] * w_ref[kk:kk + 1, lanes]
        y = _silu(acc)
        if cb < 2 * DN_HEADS:
            y = y * lax.rsqrt(jnp.sum(y * y, axis=-1, keepdims=True) + EPS)
        if cb < DN_HEADS:
            y = y * (DN_HEAD_DIM ** -0.5)
        o_ref[:, lanes] = y


def _dn_prep(p, conv_w, seq_len, tm=512):
    T = p.shape[0]
    n_tiles = T // tm
    width = 3 * GROUP
    prev_spec, next_spec = _halo_specs(tm, 8, width, 2, n_tiles)
    return pl.pallas_call(
        functools.partial(_dn_prep_kernel, tm=tm, seq_len=seq_len),
        grid=(n_tiles,),
        in_specs=[pl.BlockSpec((tm, width), lambda i: (i, 2)), prev_spec, next_spec,
                  pl.BlockSpec((4, width), lambda i: (0, 0))],
        out_specs=pl.BlockSpec((tm, width), lambda i: (i, 0)),
        out_shape=jax.ShapeDtypeStruct((T, width), F32),
        scratch_shapes=[pltpu.VMEM((tm + 16, width), F32)],
        compiler_params=_cparams("parallel"),
        name="deltanet_prep",
    )(p, p, p, conv_w)


DN_GROUP = 4
DN_ROWS = DN_HEADS * DN_CHUNK


def _dn_scan_kernel(xf_ref, xb_ref, gf_ref, gb_ref, rf_ref, rb_ref, alc_ref, dtc_ref, alr_ref, dtr_ref,
                    of_ref, ob_ref, s_scr):
    C = DN_CHUNK
    H = DN_HEADS
    R = DN_GROUP * C

    @pl.when(pl.program_id(1) == 0)
    def _():
        s_scr[...] = jnp.zeros_like(s_scr)

    row_in_chunk = lax.rem(lax.broadcasted_iota(jnp.int32, (R, 1), 0), C)
    lane_in_chunk = lax.rem(lax.broadcasted_iota(jnp.int32, (1, DN_ROWS), 1), C)
    rid = lax.broadcasted_iota(jnp.int32, (DN_ROWS, DN_ROWS), 0)
    cid = lax.broadcasted_iota(jnp.int32, (DN_ROWS, DN_ROWS), 1)
    same_head = (rid // C) == (cid // C)
    eye = (rid == cid).astype(F32)

    def stack_heads(x, col0):
        return jnp.concatenate([x[:, col0 + h * 128:col0 + (h + 1) * 128] for h in range(H)], axis=0)

    def stack_cols(x, lane0, rows=None):
        parts = []
        for h in range(H):
            c = x[:, lane0 + h:lane0 + h + 1]
            if rows is not None:
                c = c[rows:rows + 1, :]
            parts.append(jnp.broadcast_to(c, (C, 128)))
        return jnp.concatenate(parts, axis=0)

    refs = ((xf_ref, gf_ref, rf_ref, of_ref), (xb_ref, gb_ref, rb_ref, ob_ref))
    gates = []
    for d, (x_ref, gc_ref, gr_ref, o_ref) in enumerate(refs):
        reverse = d == 1
        gcol = gc_ref[...]
        g = -jnp.exp(alc_ref[...]) * _softplus(gcol + dtc_ref[...])
        beta = _sigmoid(gcol)
        gcum = g
        for sh in (1, 2, 4, 8, 16, 32):
            if reverse:
                gcum = gcum + jnp.where(row_in_chunk < C - sh, pltpu.roll(gcum, R - sh, 0), 0.0)
            else:
                gcum = gcum + jnp.where(row_in_chunk >= sh, pltpu.roll(gcum, sh, 0), 0.0)
        grow_all = -jnp.exp(alr_ref[...])[None] * _softplus(gr_ref[...] + dtr_ref[...][None])
        for sh in (1, 2, 4, 8, 16, 32):
            if reverse:
                grow_all = grow_all + jnp.where(lane_in_chunk < C - sh,
                                                pltpu.roll(grow_all, DN_ROWS - sh, 2), 0.0)
            else:
                grow_all = grow_all + jnp.where(lane_in_chunk >= sh, pltpu.roll(grow_all, sh, 2), 0.0)

        gates.append((gcum, beta, grow_all))

    units = []
    for step in range(DN_GROUP):
        for d, (x_ref, _, _, o_ref) in enumerate(refs):
            reverse = d == 1
            ci = DN_GROUP - 1 - step if reverse else step
            gcum, beta, grow_all = gates[d]
            rows = slice(ci * C, (ci + 1) * C)
            x = x_ref[rows, :]
            u = dict(d=d, rows=rows, o_ref=o_ref)
            u["q"] = stack_heads(x, 0)
            u["k"] = stack_heads(x, GROUP)
            v_st = stack_heads(x, 2 * GROUP)
            gc_c = gcum[rows, :]
            beta_st = stack_cols(beta[rows, :], 8 + d * H)
            gcol_st = stack_cols(gc_c, d * H)
            glast_st = stack_cols(gc_c, d * H, rows=0 if reverse else C - 1)
            grow = grow_all[ci, d:d + 1, :]
            tri = (rid <= cid) if reverse else (rid >= cid)
            u["incl"] = same_head & tri
            u["decay"] = jnp.where(u["incl"],
                                   jnp.exp(jnp.concatenate([gcol_st, gcol_st], axis=1) - grow), 0.0)
            kb_st = u["k"] * beta_st
            eg = jnp.exp(gcol_st)
            u["kq"] = jnp.concatenate([kb_st, u["q"]], axis=0).astype(BF16)
            u["rhs"] = jnp.concatenate([v_st * beta_st, kb_st * eg], axis=1).astype(BF16)
            u["qdec"] = u["q"] * eg
            u["kdec"] = (u["k"] * jnp.exp(glast_st - gcol_st)).astype(BF16)
            u["gl"] = jnp.exp(glast_st)
            units.append(u)

    for u in units:
        u["kk"] = _dot_nt(u["kq"], u["k"].astype(BF16))
    for u in units:
        strict = u["incl"] & (rid != cid)
        a = jnp.where(strict, u["kk"][0:DN_ROWS] * u["decay"], 0.0)
        u["attn"] = (u["kk"][DN_ROWS:2 * DN_ROWS] * u["decay"]).astype(BF16)
        u["pinv"] = eye - a
        u["a"] = a.astype(BF16)
    for u in units:
        u["apow"] = _dot(u["a"], u["a"])
    for _ in range(4):
        for u in units:
            ap = u["apow"].astype(BF16)
            u["both"] = _dot(jnp.concatenate([u["pinv"].astype(BF16), ap], axis=0), ap)
        for u in units:
            u["pinv"] = u["pinv"] + u["both"][0:DN_ROWS]
            u["apow"] = u["both"][DN_ROWS:2 * DN_ROWS]
    for u in units:
        u["last"] = _dot(u["pinv"].astype(BF16), u["apow"].astype(BF16))
    for u in units:
        u["uw"] = _dot((u["pinv"] + u["last"]).astype(BF16), u["rhs"])

    for step in range(DN_GROUP):
        pair = units[2 * step:2 * step + 2]
        for u in pair:
            d = u["d"]
            u["wq"] = []
            for h in range(H):
                hr = slice(h * C, (h + 1) * C)
                lhs = jnp.concatenate([u["uw"][hr, 128:256], u["qdec"][hr]], axis=0).astype(BF16)
                u["wq"].append(_dot(lhs, s_scr[d * H + h].astype(BF16)))
        for u in pair:
            u["vnew"] = [u["uw"][h * C:(h + 1) * C, 0:128] - u["wq"][h][0:C] for h in range(H)]
            vnew_st = jnp.concatenate(u["vnew"], axis=0).astype(BF16)
            u["o"] = jnp.concatenate([w[C:2 * C] for w in u["wq"]], axis=0) + _dot(u["attn"], vnew_st)
        for u in pair:
            d = u["d"]
            for h in range(H):
                hr = slice(h * C, (h + 1) * C)
                s_scr[d * H + h] = (s_scr[d * H + h] * u["gl"][h * C:h * C + 1, :]
                                    + _dot_tn(u["kdec"][hr], u["vnew"][h].astype(BF16)))
                u["o_ref"][u["rows"], h * 128:(h + 1) * 128] = u["o"][hr]


def _dn_scan(qkvn, pg, a_log, dt_bias, batch, seq_len):
    T = qkvn.shape[0]
    C, H, G = DN_CHUNK, DN_HEADS, DN_GROUP
    R = G * C
    nc = seq_len // C
    ncg = nc // G
    ab = pg[:, 0:4 * H].reshape(batch, nc, C, 2, 2, H)
    ab_row = jnp.transpose(ab, (0, 1, 3, 4, 5, 2)).reshape(batch * nc, 4, H * C)
    ab_row = jnp.pad(ab_row, ((0, 0), (0, 4), (0, 0)))
    pad_lanes = lambda v: jnp.pad(v.reshape(1, 2 * H), ((0, 0), (0, GATE_LANES - 2 * H)))
    row_param = lambda v: jnp.pad(jnp.repeat(v, C, axis=1), ((0, 6), (0, 0)))

    fwd = lambda b, c: (b * ncg + c, 0)
    bwd = lambda b, c: (b * ncg + ncg - 1 - c, 0)
    fwd3 = lambda b, c: (b * ncg + c, 0, 0)
    bwd3 = lambda b, c: (b * ncg + ncg - 1 - c, 0, 0)
    const = lambda shape: pl.BlockSpec(shape, lambda b, c: (0,) * len(shape))
    return pl.pallas_call(
        _dn_scan_kernel,
        grid=(batch, ncg),
        in_specs=[
            pl.BlockSpec((R, 3 * GROUP), fwd), pl.BlockSpec((R, 3 * GROUP), bwd),
            pl.BlockSpec((R, GATE_LANES), fwd), pl.BlockSpec((R, GATE_LANES), bwd),
            pl.BlockSpec((G, 8, H * C), fwd3), pl.BlockSpec((G, 8, H * C), bwd3),
            const((1, GATE_LANES)), const((1, GATE_LANES)), const((8, H * C)), const((8, H * C)),
        ],
        out_specs=[pl.BlockSpec((R, GROUP), fwd), pl.BlockSpec((R, GROUP), bwd)],
        out_shape=[jax.ShapeDtypeStruct((T, GROUP), F32), jax.ShapeDtypeStruct((T, GROUP), F32)],
        scratch_shapes=[pltpu.VMEM((2 * H, DN_HEAD_DIM, DN_HEAD_DIM), F32)],
        compiler_params=_cparams("arbitrary", "arbitrary"),
        name="deltanet_scan",
    )(qkvn, qkvn, pg, pg, ab_row, ab_row, pad_lanes(a_log), pad_lanes(dt_bias),
      row_param(a_log), row_param(dt_bias))


def _dn_post_kernel(of_ref, ob_ref, z_ref, g_ref, y_ref):
    for h in range(DN_HEADS):
        lanes = slice(h * 128, (h + 1) * 128)
        o = of_ref[:, lanes] + ob_ref[:, lanes]
        o = o * lax.rsqrt(jnp.mean(o * o, axis=-1, keepdims=True) + EPS) * g_ref[...]
        y_ref[:, lanes] = (o * _silu(z_ref[:, lanes])).astype(BF16)


def _dn_post(o_f, o_b, p, norm_g, tm=1024):
    T = p.shape[0]
    row = lambda cb: pl.BlockSpec((tm, GROUP), lambda i: (i, cb))
    return pl.pallas_call(
        _dn_post_kernel,
        grid=(T // tm,),
        in_specs=[row(0), row(0), row(9), pl.BlockSpec((1, DN_HEAD_DIM), lambda i: (0, 0))],
        out_specs=row(0),
        out_shape=jax.ShapeDtypeStruct((T, GROUP), BF16),
        compiler_params=_cparams("parallel"),
        name="deltanet_post",
    )(o_f, o_b, p, norm_g)


def _deltanet_mixer(p, pg, conv_w, a_log, dt_bias, norm_g, batch, seq_len):
    qkvn = _dn_prep(p, conv_w, seq_len)
    o_f, o_b = _dn_scan(qkvn, pg, a_log, dt_bias, batch, seq_len)
    return _dn_post(o_f, o_b, p, norm_g[None])


def _outproj_kernel(x_ref, ya_ref, yb_ref, yc_ref, yd_ref, w_ref, g_ref, xn_ref, hn_ref):
    acc = x_ref[...]
    for gi, y_ref in enumerate((ya_ref, yb_ref, yc_ref, yd_ref)):
        acc = acc + _dot(y_ref[...], w_ref[gi * GROUP:(gi + 1) * GROUP, :])
    xn_ref[...] = acc
    ms = jnp.mean(acc * acc, axis=-1, keepdims=True)
    hn_ref[...] = (acc * lax.rsqrt(ms + EPS) * g_ref[...]).astype(BF16)


def _out_proj(x2, ys, w_out, norm2_g, tm=512):
    T = x2.shape[0]
    row = lambda width: pl.BlockSpec((tm, width), lambda i: (i, 0))
    return pl.pallas_call(
        _outproj_kernel,
        grid=(T // tm,),
        in_specs=[row(D_MODEL), row(GROUP), row(GROUP), row(GROUP), row(GROUP),
                  pl.BlockSpec((D_MODEL, D_MODEL), lambda i: (0, 0)),
                  pl.BlockSpec((1, D_MODEL), lambda i: (0, 0))],
        out_specs=[row(D_MODEL), row(D_MODEL)],
        out_shape=[jax.ShapeDtypeStruct((T, D_MODEL), F32),
                   jax.ShapeDtypeStruct((T, D_MODEL), BF16)],
        compiler_params=_cparams("parallel"),
        name="out_proj",
    )(x2, *ys, w_out, norm2_g)


def _ffn_kernel(hn_ref, hp_ref, hx_ref, xn_ref, wg_ref, wu_ref, wd_ref, dw_ref, db_ref, o_ref,
                hext_scr, g_scr, *, tm, seq_len):
    i = pl.program_id(0)

    @pl.when(pl.program_id(1) == 0)
    def _():
        first, last, _ = _seq_edges(i, seq_len // tm)
        hext_scr[0:16, :] = jnp.where(first, jnp.zeros_like(hp_ref[...]), hp_ref[...])
        hext_scr[16:16 + tm, :] = hn_ref[...]
        hext_scr[16 + tm:32 + tm, :] = jnp.where(last, jnp.zeros_like(hx_ref[...]), hx_ref[...])
        o_ref[...] = xn_ref[...]

    g_scr[...] = _dot(hext_scr[...], wg_ref[...])
    gate = (g_scr[pl.ds(15, tm), :] * dw_ref[0:1, :] + g_scr[pl.ds(16, tm), :] * dw_ref[1:2, :]
            + g_scr[pl.ds(17, tm), :] * dw_ref[2:3, :] + db_ref[...])
    up = _dot(hn_ref[...], wu_ref[...])
    a = (_silu(gate) * up).astype(BF16)
    o_ref[...] += _dot(a, wd_ref[...])


def _ffn(hn, xn, w_gate, dw_w, dw_b, w_up, w_down, seq_len, tm=1024, th=512):
    T = hn.shape[0]
    n_tiles = T // tm
    hprev, hnext = _halo_specs(tm, 16, D_MODEL, 0, n_tiles)
    return pl.pallas_call(
        functools.partial(_ffn_kernel, tm=tm, seq_len=seq_len),
        grid=(n_tiles, FFN_HIDDEN // th),
        in_specs=[
            pl.BlockSpec((tm, D_MODEL), lambda i, j: (i, 0)),
            hprev, hnext,
            pl.BlockSpec((tm, D_MODEL), lambda i, j: (i, 0), pipeline_mode=pl.Buffered(1)),
            pl.BlockSpec((D_MODEL, th), lambda i, j: (0, j)),
            pl.BlockSpec((D_MODEL, th), lambda i, j: (0, j)),
            pl.BlockSpec((th, D_MODEL), lambda i, j: (j, 0)),
            pl.BlockSpec((3, th), lambda i, j: (0, j)),
            pl.BlockSpec((1, th), lambda i, j: (0, j)),
        ],
        out_specs=pl.BlockSpec((tm, D_MODEL), lambda i, j: (i, 0)),
        out_shape=jax.ShapeDtypeStruct((T, D_MODEL), F32),
        scratch_shapes=[pltpu.VMEM((tm + 32, D_MODEL), BF16), pltpu.VMEM((tm + 32, th), F32)],
        compiler_params=_cparams("parallel", "arbitrary"),
        name="conv_glu_ffn",
    )(hn, hn, hn, xn, w_gate, w_up, w_down, dw_w, dw_b)


def _layer(x2, l, batch, seq_len, rel_bias, norm1_g, w_in, w_pool, pool_scale, att_q_g, att_k_g,
           conv_dw_w, conv_dw_b, conv_ln_g, conv_ln_b, conv_pw, dn_conv_w, dn_a_log, dn_dt_bias,
           dn_norm_g, w_out, norm2_g, ffn_w_gate, ffn_dw_w, ffn_dw_b, ffn_w_up, ffn_w_down):
    w_main = w_in[l][:, :PROJ_MAIN].astype(BF16)
    w_gates = jnp.pad(w_in[l][:, PROJ_MAIN:], ((0, 0), (0, GATE_LANES - 4 * DN_HEADS))).astype(BF16)
    p, pg = _in_proj(x2, norm1_g[l][None], w_main, w_gates)
    ya = _pool_mixer(p, w_pool[l].astype(BF16), pool_scale[l][None], seq_len)
    yb = _attention_mixer(p, att_q_g[l], att_k_g[l], rel_bias, seq_len)
    yc = _conformer_mixer(p, conv_dw_w[l], conv_dw_b[l][None], conv_ln_g[l][None], conv_ln_b[l][None],
                          conv_pw[l].astype(BF16), seq_len)
    yd = _deltanet_mixer(p, pg, dn_conv_w[l], dn_a_log[l], dn_dt_bias[l], dn_norm_g[l], batch, seq_len)
    xn, hn = _out_proj(x2, (ya, yb, yc, yd), w_out[l].astype(BF16), norm2_g[l][None])
    return _ffn(hn, xn, ffn_w_gate[l].astype(BF16), ffn_dw_w[l], ffn_dw_b[l][None],
                ffn_w_up[l].astype(BF16), ffn_w_down[l].astype(BF16), seq_len)


def kernel(x, rel_bias, norm1_g, w_in, w_pool, pool_scale, att_q_g, att_k_g, conv_dw_w, conv_dw_b,
           conv_ln_g, conv_ln_b, conv_pw, dn_conv_w, dn_a_log, dn_dt_bias, dn_norm_g, w_out, norm2_g,
           ffn_w_gate, ffn_dw_w, ffn_dw_b, ffn_w_up, ffn_w_down):
    batch, seq_len, _ = x.shape
    x2 = x.reshape(batch * seq_len, D_MODEL)
    for l in range(norm1_g.shape[0]):
        x2 = _layer(x2, l, batch, seq_len, rel_bias, norm1_g, w_in, w_pool, pool_scale, att_q_g,
                    att_k_g, conv_dw_w, conv_dw_b, conv_ln_g, conv_ln_b, conv_pw, dn_conv_w, dn_a_log,
                    dn_dt_bias, dn_norm_g, w_out, norm2_g, ffn_w_gate, ffn_dw_w, ffn_dw_b, ffn_w_up,
                    ffn_w_down)
    return x2.reshape(batch, seq_len, D_MODEL)
```

```python
import functools
import math

import jax
import jax.numpy as jnp
import numpy as np
from jax import lax
from jax.experimental import pallas as pl
from jax.experimental.pallas import tpu as pltpu

F32 = jnp.float32
BF16 = jnp.bfloat16

D_MODEL = 2048
GROUP = 512
POOL_WINDOWS = (2, 4, 8, 16)
ATT_HEAD_DIM = 64
ATT_HEADS = 8
ATT_RADIUS = 64
ATT_DILATIONS = (1, 4, 16)
REL_BUCKETS = 32
REL_MAX_DIST = 1024
CONV_WIDTH = 31
DN_HEAD_DIM = 128
DN_HEADS = 4
DN_CHUNK = 64
FFN_HIDDEN = 5632
EPS = 1e-6
NEG_INF = -1e30

PROJ_MAIN = 10 * GROUP
GATE_LANES = 128

VMEM_LIMIT_BYTES = 56 * 1024 * 1024


def _cparams(*sem):
    return pltpu.CompilerParams(dimension_semantics=sem, vmem_limit_bytes=VMEM_LIMIT_BYTES)


def _sigmoid(x):
    return 1.0 / (1.0 + jnp.exp(-x))


def _silu(x):
    return x * _sigmoid(x)


def _dot(a, b):
    return jnp.dot(a, b, preferred_element_type=F32)


def _dot_nt(a, b):
    return lax.dot_general(a, b, (((1,), (1,)), ((), ())), preferred_element_type=F32)


def _dot_tn(a, b):
    return lax.dot_general(a, b, (((0,), (0,)), ((), ())), preferred_element_type=F32)


def _inproj_kernel(x_ref, g_ref, w_ref, wg_ref, p_ref, pg_ref):
    x = x_ref[...]
    ms = jnp.mean(x * x, axis=-1, keepdims=True)
    h = (x * lax.rsqrt(ms + EPS) * g_ref[...]).astype(BF16)
    pg_ref[...] = _dot(h, wg_ref[...])
    p_ref[...] = _dot(h, w_ref[...])


def _in_proj(x2, norm_g, w_main, w_gate, tm=256):
    T = x2.shape[0]
    resident = lambda shape: pl.BlockSpec(shape, lambda i: (0, 0), pipeline_mode=pl.Buffered(1))
    return pl.pallas_call(
        _inproj_kernel,
        grid=(T // tm,),
        in_specs=[
            pl.BlockSpec((tm, D_MODEL), lambda i: (i, 0)),
            resident((1, D_MODEL)),
            resident((D_MODEL, PROJ_MAIN)),
            resident((D_MODEL, GATE_LANES)),
        ],
        out_specs=[
            pl.BlockSpec((tm, PROJ_MAIN), lambda i: (i, 0)),
            pl.BlockSpec((tm, GATE_LANES), lambda i: (i, 0)),
        ],
        out_shape=[
            jax.ShapeDtypeStruct((T, PROJ_MAIN), F32),
            jax.ShapeDtypeStruct((T, GATE_LANES), F32),
        ],
        compiler_params=_cparams("parallel"),
        name="in_proj",
    )(x2, norm_g, w_main, w_gate)


def _halo_specs(tm, halo, width, col_block, n_tiles):
    r = tm // halo
    last = n_tiles * r - 1
    prev_spec = pl.BlockSpec((halo, width), lambda i, *_: (jnp.maximum(i * r - 1, 0), col_block))
    next_spec = pl.BlockSpec((halo, width), lambda i, *_: (jnp.minimum((i + 1) * r, last), col_block))
    return prev_spec, next_spec


def _seq_edges(i, tiles_per_seq):
    k = lax.rem(i, tiles_per_seq)
    return k == 0, k == tiles_per_seq - 1, k


def _pool_kernel(u_ref, up_ref, un_ref, w_ref, sc_ref, y_ref, ext_scr, *, tm, seq_len):
    i = pl.program_id(0)
    first, last, k = _seq_edges(i, seq_len // tm)
    ext_scr[0:8, :] = jnp.where(first, jnp.zeros_like(up_ref[...]), up_ref[...])
    ext_scr[8:8 + tm, :] = u_ref[...]
    ext_scr[8 + tm:16 + tm, :] = jnp.where(last, jnp.zeros_like(un_ref[...]), un_ref[...])
    t = k * tm + lax.broadcasted_iota(jnp.int32, (tm, 1), 0)
    for gi, win in enumerate(POOL_WINDOWS):
        half = win // 2
        lanes = slice(gi * 128, (gi + 1) * 128)
        s = ext_scr[pl.ds(8 - half, tm), lanes]
        for kk in range(1, win):
            s = s + ext_scr[pl.ds(8 - half + kk, tm), lanes]
        cnt = (jnp.minimum(t + half, seq_len) - jnp.maximum(t - half, 0)).astype(F32)
        pooled = s / cnt - ext_scr[pl.ds(8, tm), lanes]
        y = _dot(pooled.astype(BF16), w_ref[gi]) * sc_ref[:, lanes]
        y_ref[:, lanes] = y.astype(BF16)


def _pool_mixer(p, w_pool, pool_scale, seq_len, tm=1024):
    T = p.shape[0]
    n_tiles = T // tm
    prev_spec, next_spec = _halo_specs(tm, 8, GROUP, 0, n_tiles)
    return pl.pallas_call(
        functools.partial(_pool_kernel, tm=tm, seq_len=seq_len),
        grid=(n_tiles,),
        in_specs=[
            pl.BlockSpec((tm, GROUP), lambda i: (i, 0)),
            prev_spec,
            next_spec,
            pl.BlockSpec((4, 128, 128), lambda i: (0, 0, 0)),
            pl.BlockSpec((1, GROUP), lambda i: (0, 0)),
        ],
        out_specs=pl.BlockSpec((tm, GROUP), lambda i: (i, 0)),
        out_shape=jax.ShapeDtypeStruct((T, GROUP), BF16),
        scratch_shapes=[pltpu.VMEM((tm + 16, GROUP), F32)],
        compiler_params=_cparams("parallel"),
        name="pool_mixer",
    )(p, p, p, w_pool, pool_scale)


def _conformer_kernel(v_ref, vp_ref, vn_ref, g_ref, gp_ref, gn_ref, dw_ref, db_ref, lg_ref, lb_ref,
                      pw_ref, y_ref, ext_scr, shift_scr, *, tm, seq_len):
    i = pl.program_id(0)
    first, last, _ = _seq_edges(i, seq_len // tm)
    hp = vp_ref[...] * _sigmoid(gp_ref[...])
    hn = vn_ref[...] * _sigmoid(gn_ref[...])
    ext_scr[0:16, :] = jnp.where(first, jnp.zeros_like(hp), hp)
    ext_scr[16:16 + tm, :] = v_ref[...] * _sigmoid(g_ref[...])
    ext_scr[16 + tm:32 + tm, :] = jnp.where(last, jnp.zeros_like(hn), hn)
    base = 16 - CONV_WIDTH // 2
    acc = db_ref[...]
    for b in range(8):
        taps = [(a, 8 * a + b - base) for a in range(5) if 0 <= 8 * a + b - base < CONV_WIDTH]
        rows = tm + 8 * taps[-1][0]
        shift_scr[b, 0:rows, :] = ext_scr[pl.ds(b, rows), :]
        for a, kk in taps:
            acc = acc + shift_scr[b, 8 * a:8 * a + tm, :] * dw_ref[kk:kk + 1, :]
    mu = jnp.mean(acc, axis=-1, keepdims=True)
    xc = acc - mu
    var = jnp.mean(xc * xc, axis=-1, keepdims=True)
    h = _silu(xc * lax.rsqrt(var + EPS) * lg_ref[...] + lb_ref[...])
    y_ref[...] = _dot(h.astype(BF16), pw_ref[...]).astype(BF16)


def _conformer_mixer(p, dw_w, dw_b, ln_g, ln_b, pw, seq_len, tm=512):
    T = p.shape[0]
    n_tiles = T // tm
    vprev, vnext = _halo_specs(tm, 16, GROUP, 4, n_tiles)
    gprev, gnext = _halo_specs(tm, 16, GROUP, 5, n_tiles)
    const = lambda shape: pl.BlockSpec(shape, lambda i: (0,) * len(shape))
    return pl.pallas_call(
        functools.partial(_conformer_kernel, tm=tm, seq_len=seq_len),
        grid=(n_tiles,),
        in_specs=[
            pl.BlockSpec((tm, GROUP), lambda i: (i, 4)), vprev, vnext,
            pl.BlockSpec((tm, GROUP), lambda i: (i, 5)), gprev, gnext,
            const((CONV_WIDTH, GROUP)), const((1, GROUP)), const((1, GROUP)), const((1, GROUP)),
            const((GROUP, GROUP)),
        ],
        out_specs=pl.BlockSpec((tm, GROUP), lambda i: (i, 0)),
        out_shape=jax.ShapeDtypeStruct((T, GROUP), BF16),
        scratch_shapes=[pltpu.VMEM((tm + 32, GROUP), F32), pltpu.VMEM((8, tm + 32, GROUP), F32)],
        compiler_params=_cparams("parallel"),
        name="conformer_mixer",
    )(p, p, p, p, p, p, dw_w, dw_b, ln_g, ln_b, pw)


ATT_TILE = 1024
ATT_BLOCK_GROUP = 8


def _t5_bucket_table():
    nb = REL_BUCKETS // 2
    max_exact = nb // 2
    i = np.arange(ATT_RADIUS)[:, None]
    j = np.arange(3 * ATT_RADIUS)[None, :]
    off = j - ATT_RADIUS - i
    tables = []
    for dil in ATT_DILATIONS:
        rel = off * dil
        n = np.abs(rel)
        nf = np.maximum(n, 1).astype(np.float32)
        large = max_exact + (np.log(nf / np.float32(max_exact)) / np.float32(math.log(REL_MAX_DIST / max_exact))
                             * np.float32(nb - max_exact)).astype(np.int32)
        large = np.minimum(large, nb - 1)
        bucket = np.where(rel > 0, nb, 0) + np.where(n < max_exact, n, large)
        tables.append(np.where(np.abs(off) <= ATT_RADIUS, bucket, -1))
    return np.stack(tables).astype(np.int32)


def _att_kernel(rb_ref, bkt_ref, qg_ref, kg_ref, mseg_ref, q_ref, kp_ref, kc_ref, kn_ref,
                vp_ref, vc_ref, vn_ref, y_ref, bias_scr, qbuf, kbuf, vbuf, acc_scr, m_scr, l_scr,
                *, seq_len):
    tile = ATT_TILE
    rad = ATT_RADIUS
    hp = pl.program_id(0)
    i = pl.program_id(1)

    @pl.when(i == 0)
    def _():
        for di in range(len(ATT_DILATIONS)):
            bkt = bkt_ref[di]
            for hh in range(2):
                b = jnp.zeros(bkt.shape, F32)
                for bb in range(REL_BUCKETS):
                    b = jnp.where(bkt == bb, rb_ref[bb, 2 * hp + hh], b)
                bias_scr[di, hh * rad:(hh + 1) * rad, :] = jnp.where(bkt < 0, NEG_INF, b)

    first, last, _ = _seq_edges(i, seq_len // tile)

    def rms(x, g):
        ms = _dot((x * x).astype(BF16), mseg_ref[...])
        return x * lax.rsqrt(ms + EPS) * g

    qbuf[...] = rms(q_ref[...], qg_ref[...]) * (ATT_HEAD_DIM ** -0.5)
    kbuf[0:tile, :] = rms(kp_ref[...], kg_ref[...])
    kbuf[tile:2 * tile, :] = rms(kc_ref[...], kg_ref[...])
    kbuf[2 * tile:3 * tile, :] = rms(kn_ref[...], kg_ref[...])
    vbuf[0:tile, :] = vp_ref[...]
    vbuf[tile:2 * tile, :] = vc_ref[...]
    vbuf[2 * tile:3 * tile, :] = vn_ref[...]

    head0 = lax.broadcasted_iota(jnp.int32, (1, 128), 1) < ATT_HEAD_DIM
    col = lax.broadcasted_iota(jnp.int32, (1, 3 * rad), 1)

    def rows(start, size, dil):
        return pl.ds(start, size) if dil == 1 else pl.ds(start, size, stride=dil)

    blocks = [(di, dil, r + rad * dil * m)
              for di, dil in enumerate(ATT_DILATIONS) for r in range(dil) for m in range(tile // (rad * dil))]
    for g0 in range(0, len(blocks), ATT_BLOCK_GROUP):
        group = blocks[g0:g0 + ATT_BLOCK_GROUP]
        scores = []
        for di, dil, qstart in group:
            kstart = tile + qstart - rad * dil
            qb = qbuf[rows(qstart, rad, dil), :]
            kb = kbuf[rows(kstart, 3 * rad, dil), :].astype(BF16)
            q2 = jnp.concatenate([jnp.where(head0, qb, 0.0), jnp.where(head0, 0.0, qb)], axis=0)
            scores.append(_dot_nt(q2.astype(BF16), kb))
        probs = []
        for (di, dil, qstart), s in zip(group, scores):
            kstart = tile + qstart - rad * dil
            s = s + bias_scr[di]
            n_prev = max(0, -(-(tile - kstart) // dil))
            n_upto = min(3 * rad, -(-(2 * tile - kstart) // dil))
            if n_prev > 0:
                s = jnp.where(col < jnp.where(first, n_prev, 0), NEG_INF, s)
            if n_upto < 3 * rad:
                s = jnp.where(col >= jnp.where(last, n_upto, 3 * rad), NEG_INF, s)
            mx = jnp.max(s, axis=-1, keepdims=True)
            pe = jnp.exp(s - mx)
            probs.append((mx, jnp.sum(pe, axis=-1, keepdims=True), pe.astype(BF16)))
        outs = []
        for (di, dil, qstart), (mx, ls, pe) in zip(group, probs):
            kstart = tile + qstart - rad * dil
            outs.append(_dot(pe, vbuf[rows(kstart, 3 * rad, dil), :].astype(BF16)))
        for (di, dil, qstart), (mx, ls, pe), o in zip(group, probs, outs):
            dst = rows(qstart, rad, dil)
            acc_scr[di, dst, :] = jnp.where(head0, o[0:rad], o[rad:2 * rad])
            m_scr[di, dst, :] = jnp.where(head0, mx[0:rad], mx[rad:2 * rad])
            l_scr[di, dst, :] = jnp.where(head0, ls[0:rad], ls[rad:2 * rad])

    m_all = jnp.maximum(jnp.maximum(m_scr[0], m_scr[1]), m_scr[2])
    num = jnp.zeros((tile, 128), F32)
    den = jnp.zeros((tile, 128), F32)
    for di in range(len(ATT_DILATIONS)):
        e = jnp.exp(m_scr[di] - m_all)
        num = num + acc_scr[di] * e
        den = den + l_scr[di] * e
    y_ref[...] = (num / den).astype(BF16)


def _attention_mixer(p, q_g, k_g, rel_bias, seq_len):
    T = p.shape[0]
    tile = ATT_TILE
    n_tiles = T // tile
    rad = ATT_RADIUS
    seg = np.kron(np.eye(2), np.full((ATT_HEAD_DIM, ATT_HEAD_DIM), 1.0 / ATT_HEAD_DIM))
    mseg = jnp.asarray(seg, BF16)
    bkt = jnp.asarray(_t5_bucket_table())
    qg2 = jnp.tile(q_g, 2)[None]
    kg2 = jnp.tile(k_g, 2)[None]

    def blk(col0, shift):
        return pl.BlockSpec((tile, 128),
                            lambda hp, i: (jnp.clip(i + shift, 0, n_tiles - 1), col0 * 4 + hp))

    const = lambda shape: pl.BlockSpec(shape, lambda hp, i: (0,) * len(shape))
    return pl.pallas_call(
        functools.partial(_att_kernel, seq_len=seq_len),
        grid=(4, n_tiles),
        in_specs=[
            pl.BlockSpec(memory_space=pltpu.SMEM),
            const((3, rad, 3 * rad)), const((1, 128)), const((1, 128)), const((128, 128)),
            blk(1, 0),
            blk(2, -1), blk(2, 0), blk(2, 1),
            blk(3, -1), blk(3, 0), blk(3, 1),
        ],
        out_specs=pl.BlockSpec((tile, 128), lambda hp, i: (i, hp)),
        out_shape=jax.ShapeDtypeStruct((T, GROUP), BF16),
        scratch_shapes=[
            pltpu.VMEM((3, 2 * rad, 3 * rad), F32),
            pltpu.VMEM((tile, 128), F32),
            pltpu.VMEM((3 * tile, 128), F32),
            pltpu.VMEM((3 * tile, 128), F32),
            pltpu.VMEM((3, tile, 128), F32),
            pltpu.VMEM((3, tile, 128), F32),
            pltpu.VMEM((3, tile, 128), F32),
        ],
        compiler_params=_cparams("arbitrary", "arbitrary"),
        name="dilated_attention",
    )(rel_bias, bkt, qg2, kg2, mseg, p, p, p, p, p, p, p)


def _softplus(x):
    return jnp.maximum(x, 0.0) + jnp.log1p(jnp.exp(-jnp.abs(x)))


def _dn_prep_kernel(x_ref, xp_ref, xn_ref, w_ref, o_ref, ext_scr, *, tm, seq_len):
    i = pl.program_id(0)
    first, last, _ = _seq_edges(i, seq_len // tm)
    ext_scr[0:8, :] = jnp.where(first, jnp.zeros_like(xp_ref[...]), xp_ref[...])
    ext_scr[8:8 + tm, :] = x_ref[...]
    ext_scr[8 + tm:16 + tm, :] = jnp.where(last, jnp.zeros_like(xn_ref[...]), xn_ref[...])
    for cb in range(3 * DN_HEADS):
        lanes = slice(cb * 128, (cb + 1) * 128)
        acc = ext_scr[pl.ds(6, tm), lanes] * w_ref[0:1, lanes]
        for kk in range(1, 4):
            acc = acc + ext_scr[pl.ds(6 + kk, tm), lanes] * w_ref[kk:kk + 1, lanes]
        y = _silu(acc)
        if cb < 2 * DN_HEADS:
            y = y * lax.rsqrt(jnp.sum(y * y, axis=-1, keepdims=True) + EPS)
        if cb < DN_HEADS:
            y = y * (DN_HEAD_DIM ** -0.5)
        o_ref[:, lanes] = y


def _dn_prep(p, conv_w, seq_len, tm=512):
    T = p.shape[0]
    n_tiles = T // tm
    width = 3 * GROUP
    prev_spec, next_spec = _halo_specs(tm, 8, width, 2, n_tiles)
    return pl.pallas_call(
        functools.partial(_dn_prep_kernel, tm=tm, seq_len=seq_len),
        grid=(n_tiles,),
        in_specs=[pl.BlockSpec((tm, width), lambda i: (i, 2)), prev_spec, next_spec,
                  pl.BlockSpec((4, width), lambda i: (0, 0))],
        out_specs=pl.BlockSpec((tm, width), lambda i: (i, 0)),
        out_shape=jax.ShapeDtypeStruct((T, width), F32),
        scratch_shapes=[pltpu.VMEM((tm + 16, width), F32)],
        compiler_params=_cparams("parallel"),
        name="deltanet_prep",
    )(p, p, p, conv_w)


DN_GROUP = 4
DN_ROWS = DN_HEADS * DN_CHUNK


def _dn_scan_kernel(xf_ref, xb_ref, gf_ref, gb_ref, rf_ref, rb_ref, alc_ref, dtc_ref, alr_ref, dtr_ref,
                    of_ref, ob_ref, s_scr):
    C = DN_CHUNK
    H = DN_HEADS
    R = DN_GROUP * C

    @pl.when(pl.program_id(1) == 0)
    def _():
        s_scr[...] = jnp.zeros_like(s_scr)

    row_in_chunk = lax.rem(lax.broadcasted_iota(jnp.int32, (R, 1), 0), C)
    lane_in_chunk = lax.rem(lax.broadcasted_iota(jnp.int32, (1, DN_ROWS), 1), C)
    rid = lax.broadcasted_iota(jnp.int32, (DN_ROWS, DN_ROWS), 0)
    cid = lax.broadcasted_iota(jnp.int32, (DN_ROWS, DN_ROWS), 1)
    same_head = (rid // C) == (cid // C)
    eye = (rid == cid).astype(F32)

    def stack_heads(x, col0):
        return jnp.concatenate([x[:, col0 + h * 128:col0 + (h + 1) * 128] for h in range(H)], axis=0)

    def stack_cols(x, lane0, rows=None):
        parts = []
        for h in range(H):
            c = x[:, lane0 + h:lane0 + h + 1]
            if rows is not None:
                c = c[rows:rows + 1, :]
            parts.append(jnp.broadcast_to(c, (C, 128)))
        return jnp.concatenate(parts, axis=0)

    refs = ((xf_ref, gf_ref, rf_ref, of_ref), (xb_ref, gb_ref, rb_ref, ob_ref))
    gates = []
    for d, (x_ref, gc_ref, gr_ref, o_ref) in enumerate(refs):
        reverse = d == 1
        gcol = gc_ref[...]
        g = -jnp.exp(alc_ref[...]) * _softplus(gcol + dtc_ref[...])
        beta = _sigmoid(gcol)
        gcum = g
        for sh in (1, 2, 4, 8, 16, 32):
            if reverse:
                gcum = gcum + jnp.where(row_in_chunk < C - sh, pltpu.roll(gcum, R - sh, 0), 0.0)
            else:
                gcum = gcum + jnp.where(row_in_chunk >= sh, pltpu.roll(gcum, sh, 0), 0.0)
        grow_all = -jnp.exp(alr_ref[...])[None] * _softplus(gr_ref[...] + dtr_ref[...][None])
        for sh in (1, 2, 4, 8, 16, 32):
            if reverse:
                grow_all = grow_all + jnp.where(lane_in_chunk < C - sh,
                                                pltpu.roll(grow_all, DN_ROWS - sh, 2), 0.0)
            else:
                grow_all = grow_all + jnp.where(lane_in_chunk >= sh, pltpu.roll(grow_all, sh, 2), 0.0)

        gates.append((gcum, beta, grow_all))

    units = []
    for step in range(DN_GROUP):
        for d, (x_ref, _, _, o_ref) in enumerate(refs):
            reverse = d == 1
            ci = DN_GROUP - 1 - step if reverse else step
            gcum, beta, grow_all = gates[d]
            rows = slice(ci * C, (ci + 1) * C)
            x = x_ref[rows, :]
            u = dict(d=d, rows=rows, o_ref=o_ref)
            u["q"] = stack_heads(x, 0)
            u["k"] = stack_heads(x, GROUP)
            v_st = stack_heads(x, 2 * GROUP)
            gc_c = gcum[rows, :]
            beta_st = stack_cols(beta[rows, :], 8 + d * H)
            gcol_st = stack_cols(gc_c, d * H)
            glast_st = stack_cols(gc_c, d * H, rows=0 if reverse else C - 1)
            grow = grow_all[ci, d:d + 1, :]
            tri = (rid <= cid) if reverse else (rid >= cid)
            u["incl"] = same_head & tri
            u["decay"] = jnp.where(u["incl"],
                                   jnp.exp(jnp.concatenate([gcol_st, gcol_st], axis=1) - grow), 0.0)
            kb_st = u["k"] * beta_st
            eg = jnp.exp(gcol_st)
            u["kq"] = jnp.concatenate([kb_st, u["q"]], axis=0).astype(BF16)
            u["rhs"] = jnp.concatenate([v_st * beta_st, kb_st * eg], axis=1).astype(BF16)
            u["qdec"] = u["q"] * eg
            u["kdec"] = (u["k"] * jnp.exp(glast_st - gcol_st)).astype(BF16)
            u["gl"] = jnp.exp(glast_st)
            units.append(u)

    for u in units:
        u["kk"] = _dot_nt(u["kq"], u["k"].astype(BF16))
    for u in units:
        strict = u["incl"] & (rid != cid)
        a = jnp.where(strict, u["kk"][0:DN_ROWS] * u["decay"], 0.0)
        u["attn"] = (u["kk"][DN_ROWS:2 * DN_ROWS] * u["decay"]).astype(BF16)
        u["pinv"] = eye - a
        u["a"] = a.astype(BF16)
    for u in units:
        u["apow"] = _dot(u["a"], u["a"])
    for _ in range(4):
        for u in units:
            ap = u["apow"].astype(BF16)
            u["both"] = _dot(jnp.concatenate([u["pinv"].astype(BF16), ap], axis=0), ap)
        for u in units:
            u["pinv"] = u["pinv"] + u["both"][0:DN_ROWS]
            u["apow"] = u["both"][DN_ROWS:2 * DN_ROWS]
    for u in units:
        u["last"] = _dot(u["pinv"].astype(BF16), u["apow"].astype(BF16))
    for u in units:
        u["uw"] = _dot((u["pinv"] + u["last"]).astype(BF16), u["rhs"])

    for step in range(DN_GROUP):
        pair = units[2 * step:2 * step + 2]
        for u in pair:
            d = u["d"]
            u["wq"] = []
            for h in range(H):
                hr = slice(h * C, (h + 1) * C)
                lhs = jnp.concatenate([u["uw"][hr, 128:256], u["qdec"][hr]], axis=0).astype(BF16)
                u["wq"].append(_dot(lhs, s_scr[d * H + h].astype(BF16)))
        for u in pair:
            u["vnew"] = [u["uw"][h * C:(h + 1) * C, 0:128] - u["wq"][h][0:C] for h in range(H)]
            vnew_st = jnp.concatenate(u["vnew"], axis=0).astype(BF16)
            u["o"] = jnp.concatenate([w[C:2 * C] for w in u["wq"]], axis=0) + _dot(u["attn"], vnew_st)
        for u in pair:
            d = u["d"]
            for h in range(H):
                hr = slice(h * C, (h + 1) * C)
                s_scr[d * H + h] = (s_scr[d * H + h] * u["gl"][h * C:h * C + 1, :]
                                    + _dot_tn(u["kdec"][hr], u["vnew"][h].astype(BF16)))
                u["o_ref"][u["rows"], h * 128:(h + 1) * 128] = u["o"][hr]


def _dn_scan(qkvn, pg, a_log, dt_bias, batch, seq_len):
    T = qkvn.shape[0]
    C, H, G = DN_CHUNK, DN_HEADS, DN_GROUP
    R = G * C
    nc = seq_len // C
    ncg = nc // G
    ab = pg[:, 0:4 * H].reshape(batch, nc, C, 2, 2, H)
    ab_row = jnp.transpose(ab, (0, 1, 3, 4, 5, 2)).reshape(batch * nc, 4, H * C)
    ab_row = jnp.pad(ab_row, ((0, 0), (0, 4), (0, 0)))
    pad_lanes = lambda v: jnp.pad(v.reshape(1, 2 * H), ((0, 0), (0, GATE_LANES - 2 * H)))
    row_param = lambda v: jnp.pad(jnp.repeat(v, C, axis=1), ((0, 6), (0, 0)))

    fwd = lambda b, c: (b * ncg + c, 0)
    bwd = lambda b, c: (b * ncg + ncg - 1 - c, 0)
    fwd3 = lambda b, c: (b * ncg + c, 0, 0)
    bwd3 = lambda b, c: (b * ncg + ncg - 1 - c, 0, 0)
    const = lambda shape: pl.BlockSpec(shape, lambda b, c: (0,) * len(shape))
    return pl.pallas_call(
        _dn_scan_kernel,
        grid=(batch, ncg),
        in_specs=[
            pl.BlockSpec((R, 3 * GROUP), fwd), pl.BlockSpec((R, 3 * GROUP), bwd),
            pl.BlockSpec((R, GATE_LANES), fwd), pl.BlockSpec((R, GATE_LANES), bwd),
            pl.BlockSpec((G, 8, H * C), fwd3), pl.BlockSpec((G, 8, H * C), bwd3),
            const((1, GATE_LANES)), const((1, GATE_LANES)), const((8, H * C)), const((8, H * C)),
        ],
        out_specs=[pl.BlockSpec((R, GROUP), fwd), pl.BlockSpec((R, GROUP), bwd)],
        out_shape=[jax.ShapeDtypeStruct((T, GROUP), F32), jax.ShapeDtypeStruct((T, GROUP), F32)],
        scratch_shapes=[pltpu.VMEM((2 * H, DN_HEAD_DIM, DN_HEAD_DIM), F32)],
        compiler_params=_cparams("arbitrary", "arbitrary"),
        name="deltanet_scan",
    )(qkvn, qkvn, pg, pg, ab_row, ab_row, pad_lanes(a_log), pad_lanes(dt_bias),
      row_param(a_log), row_param(dt_bias))


def _dn_post_kernel(of_ref, ob_ref, z_ref, g_ref, y_ref):
    for h in range(DN_HEADS):
        lanes = slice(h * 128, (h + 1) * 128)
        o = of_ref[:, lanes] + ob_ref[:, lanes]
        o = o * lax.rsqrt(jnp.mean(o * o, axis=-1, keepdims=True) + EPS) * g_ref[...]
        y_ref[:, lanes] = (o * _silu(z_ref[:, lanes])).astype(BF16)


def _dn_post(o_f, o_b, p, norm_g, tm=1024):
    T = p.shape[0]
    row = lambda cb: pl.BlockSpec((tm, GROUP), lambda i: (i, cb))
    return pl.pallas_call(
        _dn_post_kernel,
        grid=(T // tm,),
        in_specs=[row(0), row(0), row(9), pl.BlockSpec((1, DN_HEAD_DIM), lambda i: (0, 0))],
        out_specs=row(0),
        out_shape=jax.ShapeDtypeStruct((T, GROUP), BF16),
        compiler_params=_cparams("parallel"),
        name="deltanet_post",
    )(o_f, o_b, p, norm_g)


def _deltanet_mixer(p, pg, conv_w, a_log, dt_bias, norm_g, batch, seq_len):
    qkvn = _dn_prep(p, conv_w, seq_len)
    o_f, o_b = _dn_scan(qkvn, pg, a_log, dt_bias, batch, seq_len)
    return _dn_post(o_f, o_b, p, norm_g[None])


def _outproj_kernel(x_ref, ya_ref, yb_ref, yc_ref, yd_ref, w_ref, g_ref, xn_ref, hn_ref):
    acc = x_ref[...]
    for gi, y_ref in enumerate((ya_ref, yb_ref, yc_ref, yd_ref)):
        acc = acc + _dot(y_ref[...], w_ref[gi * GROUP:(gi + 1) * GROUP, :])
    xn_ref[...] = acc
    ms = jnp.mean(acc * acc, axis=-1, keepdims=True)
    hn_ref[...] = (acc * lax.rsqrt(ms + EPS) * g_ref[...]).astype(BF16)


def _out_proj(x2, ys, w_out, norm2_g, tm=512):
    T = x2.shape[0]
    row = lambda width: pl.BlockSpec((tm, width), lambda i: (i, 0))
    return pl.pallas_call(
        _outproj_kernel,
        grid=(T // tm,),
        in_specs=[row(D_MODEL), row(GROUP), row(GROUP), row(GROUP), row(GROUP),
                  pl.BlockSpec((D_MODEL, D_MODEL), lambda i: (0, 0)),
                  pl.BlockSpec((1, D_MODEL), lambda i: (0, 0))],
        out_specs=[row(D_MODEL), row(D_MODEL)],
        out_shape=[jax.ShapeDtypeStruct((T, D_MODEL), F32),
                   jax.ShapeDtypeStruct((T, D_MODEL), BF16)],
        compiler_params=_cparams("parallel"),
        name="out_proj",
    )(x2, *ys, w_out, norm2_g)


def _ffn_kernel(hn_ref, hp_ref, hx_ref, xn_ref, wg_ref, wu_ref, wd_ref, dw_ref, db_ref, o_ref,
                hext_scr, g_scr, a_scr, *, tm, seq_len, n_chunks):
    i = pl.program_id(0)
    j = pl.program_id(1)
    slot = lax.rem(j, 2)

    def hidden_chunk():
        g_scr[...] = _dot(hext_scr[...], wg_ref[...])
        up = _dot(hn_ref[...], wu_ref[...])
        gate = (g_scr[pl.ds(15, tm), :] * dw_ref[0:1, :] + g_scr[pl.ds(16, tm), :] * dw_ref[1:2, :]
                + g_scr[pl.ds(17, tm), :] * dw_ref[2:3, :] + db_ref[...])
        return (_silu(gate) * up).astype(BF16)

    def down_prev():
        o_ref[...] += _dot(a_scr[1 - slot], wd_ref[...])

    @pl.when(j == 0)
    def _():
        first, last, _ = _seq_edges(i, seq_len // tm)
        hext_scr[0:16, :] = jnp.where(first, jnp.zeros_like(hp_ref[...]), hp_ref[...])
        hext_scr[16:16 + tm, :] = hn_ref[...]
        hext_scr[16 + tm:32 + tm, :] = jnp.where(last, jnp.zeros_like(hx_ref[...]), hx_ref[...])
        o_ref[...] = xn_ref[...]
        a_scr[slot] = hidden_chunk()

    @pl.when((j > 0) & (j < n_chunks))
    def _():
        a_new = hidden_chunk()
        down_prev()
        a_scr[slot] = a_new

    @pl.when(j == n_chunks)
    def _():
        down_prev()


def _ffn(hn, xn, w_gate, dw_w, dw_b, w_up, w_down, seq_len, tm=1024, th=512):
    T = hn.shape[0]
    n_tiles = T // tm
    n_chunks = FFN_HIDDEN // th
    hprev, hnext = _halo_specs(tm, 16, D_MODEL, 0, n_tiles)
    cur = lambda i, j: (0, jnp.minimum(j, n_chunks - 1))
    return pl.pallas_call(
        functools.partial(_ffn_kernel, tm=tm, seq_len=seq_len, n_chunks=n_chunks),
        grid=(n_tiles, n_chunks + 1),
        in_specs=[
            pl.BlockSpec((tm, D_MODEL), lambda i, j: (i, 0)),
            hprev, hnext,
            pl.BlockSpec((tm, D_MODEL), lambda i, j: (i, 0), pipeline_mode=pl.Buffered(1)),
            pl.BlockSpec((D_MODEL, th), cur),
            pl.BlockSpec((D_MODEL, th), cur),
            pl.BlockSpec((th, D_MODEL), lambda i, j: (jnp.maximum(j - 1, 0), 0)),
            pl.BlockSpec((3, th), cur),
            pl.BlockSpec((1, th), cur),
        ],
        out_specs=pl.BlockSpec((tm, D_MODEL), lambda i, j: (i, 0)),
        out_shape=jax.ShapeDtypeStruct((T, D_MODEL), F32),
        scratch_shapes=[pltpu.VMEM((tm + 32, D_MODEL), BF16), pltpu.VMEM((tm + 32, th), F32),
                        pltpu.VMEM((2, tm, th), BF16)],
        compiler_params=_cparams("parallel", "arbitrary"),
        name="conv_glu_ffn",
    )(hn, hn, hn, xn, w_gate, w_up, w_down, dw_w, dw_b)


def _layer(x2, l, batch, seq_len, rel_bias, norm1_g, w_in, w_pool, pool_scale, att_q_g, att_k_g,
           conv_dw_w, conv_dw_b, conv_ln_g, conv_ln_b, conv_pw, dn_conv_w, dn_a_log, dn_dt_bias,
           dn_norm_g, w_out, norm2_g, ffn_w_gate, ffn_dw_w, ffn_dw_b, ffn_w_up, ffn_w_down):
    w_main = w_in[l][:, :PROJ_MAIN].astype(BF16)
    w_gates = jnp.pad(w_in[l][:, PROJ_MAIN:], ((0, 0), (0, GATE_LANES - 4 * DN_HEADS))).astype(BF16)
    p, pg = _in_proj(x2, norm1_g[l][None], w_main, w_gates)
    ya = _pool_mixer(p, w_pool[l].astype(BF16), pool_scale[l][None], seq_len)
    yb = _attention_mixer(p, att_q_g[l], att_k_g[l], rel_bias, seq_len)
    yc = _conformer_mixer(p, conv_dw_w[l], conv_dw_b[l][None], conv_ln_g[l][None], conv_ln_b[l][None],
                          conv_pw[l].astype(BF16), seq_len)
    yd = _deltanet_mixer(p, pg, dn_conv_w[l], dn_a_log[l], dn_dt_bias[l], dn_norm_g[l], batch, seq_len)
    xn, hn = _out_proj(x2, (ya, yb, yc, yd), w_out[l].astype(BF16), norm2_g[l][None])
    return _ffn(hn, xn, ffn_w_gate[l].astype(BF16), ffn_dw_w[l], ffn_dw_b[l][None],
                ffn_w_up[l].astype(BF16), ffn_w_down[l].astype(BF16), seq_len)


def kernel(x, rel_bias, norm1_g, w_in, w_pool, pool_scale, att_q_g, att_k_g, conv_dw_w, conv_dw_b,
           conv_ln_g, conv_ln_b, conv_pw, dn_conv_w, dn_a_log, dn_dt_bias, dn_norm_g, w_out, norm2_g,
           ffn_w_gate, ffn_dw_w, ffn_dw_b, ffn_w_up, ffn_w_down):
    batch, seq_len, _ = x.shape
    x2 = x.reshape(batch * seq_len, D_MODEL)
    for l in range(norm1_g.shape[0]):
        x2 = _layer(x2, l, batch, seq_len, rel_bias, norm1_g, w_in, w_pool, pool_scale, att_q_g,
                    att_k_g, conv_dw_w, conv_dw_b, conv_ln_g, conv_ln_b, conv_pw, dn_conv_w, dn_a_log,
                    dn_dt_bias, dn_norm_g, w_out, norm2_g, ffn_w_gate, ffn_dw_w, ffn_dw_b, ffn_w_up,
                    ffn_w_down)
    return x2.reshape(batch, seq_len, D_MODEL)
```

```python
import functools
import math

import jax
import jax.numpy as jnp
import numpy as np
from jax import lax
from jax.experimental import pallas as pl
from jax.experimental.pallas import tpu as pltpu

F32 = jnp.float32
BF16 = jnp.bfloat16

D_MODEL = 2048
GROUP = 512
POOL_WINDOWS = (2, 4, 8, 16)
ATT_HEAD_DIM = 64
ATT_HEADS = 8
ATT_RADIUS = 64
ATT_DILATIONS = (1, 4, 16)
REL_BUCKETS = 32
REL_MAX_DIST = 1024
CONV_WIDTH = 31
DN_HEAD_DIM = 128
DN_HEADS = 4
DN_CHUNK = 64
FFN_HIDDEN = 5632
EPS = 1e-6
NEG_INF = -1e30

PROJ_MAIN = 10 * GROUP
GATE_LANES = 128

VMEM_LIMIT_BYTES = 56 * 1024 * 1024


def _cparams(*sem):
    return pltpu.CompilerParams(dimension_semantics=sem, vmem_limit_bytes=VMEM_LIMIT_BYTES)


def _sigmoid(x):
    return 1.0 / (1.0 + jnp.exp(-x))


def _silu(x):
    return x * _sigmoid(x)


def _dot(a, b):
    return jnp.dot(a, b, preferred_element_type=F32)


def _dot_nt(a, b):
    return lax.dot_general(a, b, (((1,), (1,)), ((), ())), preferred_element_type=F32)


def _dot_tn(a, b):
    return lax.dot_general(a, b, (((0,), (0,)), ((), ())), preferred_element_type=F32)


def _inproj_kernel(x_ref, g_ref, w_ref, wg_ref, p_ref, pg_ref):
    x = x_ref[...]
    ms = jnp.mean(x * x, axis=-1, keepdims=True)
    h = (x * lax.rsqrt(ms + EPS) * g_ref[...]).astype(BF16)
    pg_ref[...] = _dot(h, wg_ref[...])
    p_ref[...] = _dot(h, w_ref[...])


def _in_proj(x2, norm_g, w_main, w_gate, tm=256):
    T = x2.shape[0]
    resident = lambda shape: pl.BlockSpec(shape, lambda i: (0, 0), pipeline_mode=pl.Buffered(1))
    return pl.pallas_call(
        _inproj_kernel,
        grid=(T // tm,),
        in_specs=[
            pl.BlockSpec((tm, D_MODEL), lambda i: (i, 0)),
            resident((1, D_MODEL)),
            resident((D_MODEL, PROJ_MAIN)),
            resident((D_MODEL, GATE_LANES)),
        ],
        out_specs=[
            pl.BlockSpec((tm, PROJ_MAIN), lambda i: (i, 0)),
            pl.BlockSpec((tm, GATE_LANES), lambda i: (i, 0)),
        ],
        out_shape=[
            jax.ShapeDtypeStruct((T, PROJ_MAIN), F32),
            jax.ShapeDtypeStruct((T, GATE_LANES), F32),
        ],
        compiler_params=_cparams("parallel"),
        name="in_proj",
    )(x2, norm_g, w_main, w_gate)


def _halo_specs(tm, halo, width, col_block, n_tiles):
    r = tm // halo
    last = n_tiles * r - 1
    prev_spec = pl.BlockSpec((halo, width), lambda i, *_: (jnp.maximum(i * r - 1, 0), col_block))
    next_spec = pl.BlockSpec((halo, width), lambda i, *_: (jnp.minimum((i + 1) * r, last), col_block))
    return prev_spec, next_spec


def _seq_edges(i, tiles_per_seq):
    k = lax.rem(i, tiles_per_seq)
    return k == 0, k == tiles_per_seq - 1, k


def _pool_kernel(u_ref, up_ref, un_ref, w_ref, sc_ref, y_ref, ext_scr, *, tm, seq_len):
    i = pl.program_id(0)
    first, last, k = _seq_edges(i, seq_len // tm)
    ext_scr[0:8, :] = jnp.where(first, jnp.zeros_like(up_ref[...]), up_ref[...])
    ext_scr[8:8 + tm, :] = u_ref[...]
    ext_scr[8 + tm:16 + tm, :] = jnp.where(last, jnp.zeros_like(un_ref[...]), un_ref[...])
    t = k * tm + lax.broadcasted_iota(jnp.int32, (tm, 1), 0)
    for gi, win in enumerate(POOL_WINDOWS):
        half = win // 2
        lanes = slice(gi * 128, (gi + 1) * 128)
        s = ext_scr[pl.ds(8 - half, tm), lanes]
        for kk in range(1, win):
            s = s + ext_scr[pl.ds(8 - half + kk, tm), lanes]
        cnt = (jnp.minimum(t + half, seq_len) - jnp.maximum(t - half, 0)).astype(F32)
        pooled = s / cnt - ext_scr[pl.ds(8, tm), lanes]
        y = _dot(pooled.astype(BF16), w_ref[gi]) * sc_ref[:, lanes]
        y_ref[:, lanes] = y.astype(BF16)


def _pool_mixer(p, w_pool, pool_scale, seq_len, tm=1024):
    T = p.shape[0]
    n_tiles = T // tm
    prev_spec, next_spec = _halo_specs(tm, 8, GROUP, 0, n_tiles)
    return pl.pallas_call(
        functools.partial(_pool_kernel, tm=tm, seq_len=seq_len),
        grid=(n_tiles,),
        in_specs=[
            pl.BlockSpec((tm, GROUP), lambda i: (i, 0)),
            prev_spec,
            next_spec,
            pl.BlockSpec((4, 128, 128), lambda i: (0, 0, 0)),
            pl.BlockSpec((1, GROUP), lambda i: (0, 0)),
        ],
        out_specs=pl.BlockSpec((tm, GROUP), lambda i: (i, 0)),
        out_shape=jax.ShapeDtypeStruct((T, GROUP), BF16),
        scratch_shapes=[pltpu.VMEM((tm + 16, GROUP), F32)],
        compiler_params=_cparams("parallel"),
        name="pool_mixer",
    )(p, p, p, w_pool, pool_scale)


def _conformer_kernel(v_ref, vp_ref, vn_ref, g_ref, gp_ref, gn_ref, dw_ref, db_ref, lg_ref, lb_ref,
                      pw_ref, y_ref, ext_scr, shift_scr, *, tm, seq_len):
    i = pl.program_id(0)
    first, last, _ = _seq_edges(i, seq_len // tm)
    hp = vp_ref[...] * _sigmoid(gp_ref[...])
    hn = vn_ref[...] * _sigmoid(gn_ref[...])
    ext_scr[0:16, :] = jnp.where(first, jnp.zeros_like(hp), hp)
    ext_scr[16:16 + tm, :] = v_ref[...] * _sigmoid(g_ref[...])
    ext_scr[16 + tm:32 + tm, :] = jnp.where(last, jnp.zeros_like(hn), hn)
    base = 16 - CONV_WIDTH // 2
    acc = db_ref[...]
    for b in range(8):
        taps = [(a, 8 * a + b - base) for a in range(5) if 0 <= 8 * a + b - base < CONV_WIDTH]
        rows = tm + 8 * taps[-1][0]
        shift_scr[b, 0:rows, :] = ext_scr[pl.ds(b, rows), :]
        for a, kk in taps:
            acc = acc + shift_scr[b, 8 * a:8 * a + tm, :] * dw_ref[kk:kk + 1, :]
    mu = jnp.mean(acc, axis=-1, keepdims=True)
    xc = acc - mu
    var = jnp.mean(xc * xc, axis=-1, keepdims=True)
    h = _silu(xc * lax.rsqrt(var + EPS) * lg_ref[...] + lb_ref[...])
    y_ref[...] = _dot(h.astype(BF16), pw_ref[...]).astype(BF16)


def _conformer_mixer(p, dw_w, dw_b, ln_g, ln_b, pw, seq_len, tm=512):
    T = p.shape[0]
    n_tiles = T // tm
    vprev, vnext = _halo_specs(tm, 16, GROUP, 4, n_tiles)
    gprev, gnext = _halo_specs(tm, 16, GROUP, 5, n_tiles)
    const = lambda shape: pl.BlockSpec(shape, lambda i: (0,) * len(shape))
    return pl.pallas_call(
        functools.partial(_conformer_kernel, tm=tm, seq_len=seq_len),
        grid=(n_tiles,),
        in_specs=[
            pl.BlockSpec((tm, GROUP), lambda i: (i, 4)), vprev, vnext,
            pl.BlockSpec((tm, GROUP), lambda i: (i, 5)), gprev, gnext,
            const((CONV_WIDTH, GROUP)), const((1, GROUP)), const((1, GROUP)), const((1, GROUP)),
            const((GROUP, GROUP)),
        ],
        out_specs=pl.BlockSpec((tm, GROUP), lambda i: (i, 0)),
        out_shape=jax.ShapeDtypeStruct((T, GROUP), BF16),
        scratch_shapes=[pltpu.VMEM((tm + 32, GROUP), F32), pltpu.VMEM((8, tm + 32, GROUP), F32)],
        compiler_params=_cparams("parallel"),
        name="conformer_mixer",
    )(p, p, p, p, p, p, dw_w, dw_b, ln_g, ln_b, pw)


ATT_TILE = 1024
ATT_BLOCK_GROUP = 8


def _t5_bucket_table():
    nb = REL_BUCKETS // 2
    max_exact = nb // 2
    i = np.arange(ATT_RADIUS)[:, None]
    j = np.arange(3 * ATT_RADIUS)[None, :]
    off = j - ATT_RADIUS - i
    tables = []
    for dil in ATT_DILATIONS:
        rel = off * dil
        n = np.abs(rel)
        nf = np.maximum(n, 1).astype(np.float32)
        large = max_exact + (np.log(nf / np.float32(max_exact)) / np.float32(math.log(REL_MAX_DIST / max_exact))
                             * np.float32(nb - max_exact)).astype(np.int32)
        large = np.minimum(large, nb - 1)
        bucket = np.where(rel > 0, nb, 0) + np.where(n < max_exact, n, large)
        tables.append(np.where(np.abs(off) <= ATT_RADIUS, bucket, -1))
    return np.stack(tables).astype(np.int32)


def _att_kernel(rb_ref, bkt_ref, qg_ref, kg_ref, mseg_ref, q_ref, kp_ref, kc_ref, kn_ref,
                vp_ref, vc_ref, vn_ref, y_ref, bias_scr, qbuf, kbuf, vbuf, acc_scr, m_scr, l_scr,
                *, seq_len):
    tile = ATT_TILE
    rad = ATT_RADIUS
    hp = pl.program_id(0)
    i = pl.program_id(1)

    @pl.when(i == 0)
    def _():
        for di in range(len(ATT_DILATIONS)):
            bkt = bkt_ref[di]
            for hh in range(2):
                b = jnp.zeros(bkt.shape, F32)
                for bb in range(REL_BUCKETS):
                    b = jnp.where(bkt == bb, rb_ref[bb, 2 * hp + hh], b)
                bias_scr[di, hh * rad:(hh + 1) * rad, :] = jnp.where(bkt < 0, NEG_INF, b)

    first, last, _ = _seq_edges(i, seq_len // tile)

    def rms(x, g):
        ms = _dot((x * x).astype(BF16), mseg_ref[...])
        return x * lax.rsqrt(ms + EPS) * g

    qbuf[...] = rms(q_ref[...], qg_ref[...]) * (ATT_HEAD_DIM ** -0.5)
    kbuf[0:tile, :] = rms(kp_ref[...], kg_ref[...])
    kbuf[tile:2 * tile, :] = rms(kc_ref[...], kg_ref[...])
    kbuf[2 * tile:3 * tile, :] = rms(kn_ref[...], kg_ref[...])
    vbuf[0:tile, :] = vp_ref[...]
    vbuf[tile:2 * tile, :] = vc_ref[...]
    vbuf[2 * tile:3 * tile, :] = vn_ref[...]

    head0 = lax.broadcasted_iota(jnp.int32, (1, 128), 1) < ATT_HEAD_DIM
    col = lax.broadcasted_iota(jnp.int32, (1, 3 * rad), 1)

    def rows(start, size, dil):
        return pl.ds(start, size) if dil == 1 else pl.ds(start, size, stride=dil)

    blocks = [(di, dil, r + rad * dil * m)
              for di, dil in enumerate(ATT_DILATIONS) for r in range(dil) for m in range(tile // (rad * dil))]
    for g0 in range(0, len(blocks), ATT_BLOCK_GROUP):
        group = blocks[g0:g0 + ATT_BLOCK_GROUP]
        scores = []
        for di, dil, qstart in group:
            kstart = tile + qstart - rad * dil
            qb = qbuf[rows(qstart, rad, dil), :]
            kb = kbuf[rows(kstart, 3 * rad, dil), :].astype(BF16)
            q2 = jnp.concatenate([jnp.where(head0, qb, 0.0), jnp.where(head0, 0.0, qb)], axis=0)
            scores.append(_dot_nt(q2.astype(BF16), kb))
        probs = []
        for (di, dil, qstart), s in zip(group, scores):
            kstart = tile + qstart - rad * dil
            s = s + bias_scr[di]
            n_prev = max(0, -(-(tile - kstart) // dil))
            n_upto = min(3 * rad, -(-(2 * tile - kstart) // dil))
            if n_prev > 0:
                s = jnp.where(col < jnp.where(first, n_prev, 0), NEG_INF, s)
            if n_upto < 3 * rad:
                s = jnp.where(col >= jnp.where(last, n_upto, 3 * rad), NEG_INF, s)
            mx = jnp.max(s, axis=-1, keepdims=True)
            pe = jnp.exp(s - mx)
            probs.append((mx, jnp.sum(pe, axis=-1, keepdims=True), pe.astype(BF16)))
        outs = []
        for (di, dil, qstart), (mx, ls, pe) in zip(group, probs):
            kstart = tile + qstart - rad * dil
            outs.append(_dot(pe, vbuf[rows(kstart, 3 * rad, dil), :].astype(BF16)))
        for (di, dil, qstart), (mx, ls, pe), o in zip(group, probs, outs):
            dst = rows(qstart, rad, dil)
            acc_scr[di, dst, :] = jnp.where(head0, o[0:rad], o[rad:2 * rad])
            m_scr[di, dst, :] = jnp.where(head0, mx[0:rad], mx[rad:2 * rad])
            l_scr[di, dst, :] = jnp.where(head0, ls[0:rad], ls[rad:2 * rad])

    m_all = jnp.maximum(jnp.maximum(m_scr[0], m_scr[1]), m_scr[2])
    num = jnp.zeros((tile, 128), F32)
    den = jnp.zeros((tile, 128), F32)
    for di in range(len(ATT_DILATIONS)):
        e = jnp.exp(m_scr[di] - m_all)
        num = num + acc_scr[di] * e
        den = den + l_scr[di] * e
    y_ref[...] = (num / den).astype(BF16)


def _attention_mixer(p, q_g, k_g, rel_bias, seq_len):
    T = p.shape[0]
    tile = ATT_TILE
    n_tiles = T // tile
    rad = ATT_RADIUS
    seg = np.kron(np.eye(2), np.full((ATT_HEAD_DIM, ATT_HEAD_DIM), 1.0 / ATT_HEAD_DIM))
    mseg = jnp.asarray(seg, BF16)
    bkt = jnp.asarray(_t5_bucket_table())
    qg2 = jnp.tile(q_g, 2)[None]
    kg2 = jnp.tile(k_g, 2)[None]

    def blk(col0, shift):
        return pl.BlockSpec((tile, 128),
                            lambda hp, i: (jnp.clip(i + shift, 0, n_tiles - 1), col0 * 4 + hp))

    const = lambda shape: pl.BlockSpec(shape, lambda hp, i: (0,) * len(shape))
    return pl.pallas_call(
        functools.partial(_att_kernel, seq_len=seq_len),
        grid=(4, n_tiles),
        in_specs=[
            pl.BlockSpec(memory_space=pltpu.SMEM),
            const((3, rad, 3 * rad)), const((1, 128)), const((1, 128)), const((128, 128)),
            blk(1, 0),
            blk(2, -1), blk(2, 0), blk(2, 1),
            blk(3, -1), blk(3, 0), blk(3, 1),
        ],
        out_specs=pl.BlockSpec((tile, 128), lambda hp, i: (i, hp)),
        out_shape=jax.ShapeDtypeStruct((T, GROUP), BF16),
        scratch_shapes=[
            pltpu.VMEM((3, 2 * rad, 3 * rad), F32),
            pltpu.VMEM((tile, 128), F32),
            pltpu.VMEM((3 * tile, 128), F32),
            pltpu.VMEM((3 * tile, 128), F32),
            pltpu.VMEM((3, tile, 128), F32),
            pltpu.VMEM((3, tile, 128), F32),
            pltpu.VMEM((3, tile, 128), F32),
        ],
        compiler_params=_cparams("arbitrary", "arbitrary"),
        name="dilated_attention",
    )(rel_bias, bkt, qg2, kg2, mseg, p, p, p, p, p, p, p)


def _softplus(x):
    return jnp.maximum(x, 0.0) + jnp.log1p(jnp.exp(-jnp.abs(x)))


def _dn_prep_kernel(x_ref, xp_ref, xn_ref, w_ref, o_ref, ext_scr, *, tm, seq_len):
    i = pl.program_id(0)
    first, last, _ = _seq_edges(i, seq_len // tm)
    ext_scr[0:8, :] = jnp.where(first, jnp.zeros_like(xp_ref[...]), xp_ref[...])
    ext_scr[8:8 + tm, :] = x_ref[...]
    ext_scr[8 + tm:16 + tm, :] = jnp.where(last, jnp.zeros_like(xn_ref[...]), xn_ref[...])
    for cb in range(3 * DN_HEADS):
        lanes = slice(cb * 128, (cb + 1) * 128)
        acc = ext_scr[pl.ds(6, tm), lanes] * w_ref[0:1, lanes]
        for kk in range(1, 4):
            acc = acc + ext_scr[pl.ds(6 + kk, tm), lanes] * w_ref[kk:kk + 1, lanes]
        y = _silu(acc)
        if cb < 2 * DN_HEADS:
            y = y * lax.rsqrt(jnp.sum(y * y, axis=-1, keepdims=True) + EPS)
        if cb < DN_HEADS:
            y = y * (DN_HEAD_DIM ** -0.5)
        o_ref[:, lanes] = y


def _dn_prep(p, conv_w, seq_len, tm=512):
    T = p.shape[0]
    n_tiles = T // tm
    width = 3 * GROUP
    prev_spec, next_spec = _halo_specs(tm, 8, width, 2, n_tiles)
    return pl.pallas_call(
        functools.partial(_dn_prep_kernel, tm=tm, seq_len=seq_len),
        grid=(n_tiles,),
        in_specs=[pl.BlockSpec((tm, width), lambda i: (i, 2)), prev_spec, next_spec,
                  pl.BlockSpec((4, width), lambda i: (0, 0))],
        out_specs=pl.BlockSpec((tm, width), lambda i: (i, 0)),
        out_shape=jax.ShapeDtypeStruct((T, width), F32),
        scratch_shapes=[pltpu.VMEM((tm + 16, width), F32)],
        compiler_params=_cparams("parallel"),
        name="deltanet_prep",
    )(p, p, p, conv_w)


DN_GROUP = 4
DN_ROWS = DN_HEADS * DN_CHUNK


def _dn_scan_kernel(xf_ref, xb_ref, gf_ref, gb_ref, rf_ref, rb_ref, alc_ref, dtc_ref, alr_ref, dtr_ref,
                    of_ref, ob_ref, s_scr):
    C = DN_CHUNK
    H = DN_HEADS
    R = DN_GROUP * C

    @pl.when(pl.program_id(1) == 0)
    def _():
        s_scr[...] = jnp.zeros_like(s_scr)

    row_in_chunk = lax.rem(lax.broadcasted_iota(jnp.int32, (R, 1), 0), C)
    lane_in_chunk = lax.rem(lax.broadcasted_iota(jnp.int32, (1, DN_ROWS), 1), C)
    rid = lax.broadcasted_iota(jnp.int32, (DN_ROWS, DN_ROWS), 0)
    cid = lax.broadcasted_iota(jnp.int32, (DN_ROWS, DN_ROWS), 1)
    same_head = (rid // C) == (cid // C)
    eye = (rid == cid).astype(F32)

    def stack_heads(x, col0):
        return jnp.concatenate([x[:, col0 + h * 128:col0 + (h + 1) * 128] for h in range(H)], axis=0)

    def stack_cols(x, lane0, rows=None):
        parts = []
        for h in range(H):
            c = x[:, lane0 + h:lane0 + h + 1]
            if rows is not None:
                c = c[rows:rows + 1, :]
            parts.append(jnp.broadcast_to(c, (C, 128)))
        return jnp.concatenate(parts, axis=0)

    refs = ((xf_ref, gf_ref, rf_ref, of_ref), (xb_ref, gb_ref, rb_ref, ob_ref))
    gates = []
    for d, (x_ref, gc_ref, gr_ref, o_ref) in enumerate(refs):
        reverse = d == 1
        gcol = gc_ref[...]
        g = -jnp.exp(alc_ref[...]) * _softplus(gcol + dtc_ref[...])
        beta = _sigmoid(gcol)
        gcum = g
        for sh in (1, 2, 4, 8, 16, 32):
            if reverse:
                gcum = gcum + jnp.where(row_in_chunk < C - sh, pltpu.roll(gcum, R - sh, 0), 0.0)
            else:
                gcum = gcum + jnp.where(row_in_chunk >= sh, pltpu.roll(gcum, sh, 0), 0.0)
        grow_all = -jnp.exp(alr_ref[...])[None] * _softplus(gr_ref[...] + dtr_ref[...][None])
        for sh in (1, 2, 4, 8, 16, 32):
            if reverse:
                grow_all = grow_all + jnp.where(lane_in_chunk < C - sh,
                                                pltpu.roll(grow_all, DN_ROWS - sh, 2), 0.0)
            else:
                grow_all = grow_all + jnp.where(lane_in_chunk >= sh, pltpu.roll(grow_all, sh, 2), 0.0)

        gates.append((gcum, beta, grow_all))

    units = []
    for step in range(DN_GROUP):
        for d, (x_ref, _, _, o_ref) in enumerate(refs):
            reverse = d == 1
            ci = DN_GROUP - 1 - step if reverse else step
            gcum, beta, grow_all = gates[d]
            rows = slice(ci * C, (ci + 1) * C)
            x = x_ref[rows, :]
            u = dict(d=d, rows=rows, o_ref=o_ref)
            u["q"] = stack_heads(x, 0)
            u["k"] = stack_heads(x, GROUP)
            v_st = stack_heads(x, 2 * GROUP)
            gc_c = gcum[rows, :]
            beta_st = stack_cols(beta[rows, :], 8 + d * H)
            gcol_st = stack_cols(gc_c, d * H)
            glast_st = stack_cols(gc_c, d * H, rows=0 if reverse else C - 1)
            grow = grow_all[ci, d:d + 1, :]
            tri = (rid <= cid) if reverse else (rid >= cid)
            u["incl"] = same_head & tri
            u["decay"] = jnp.where(u["incl"],
                                   jnp.exp(jnp.concatenate([gcol_st, gcol_st], axis=1) - grow), 0.0)
            kb_st = u["k"] * beta_st
            eg = jnp.exp(gcol_st)
            u["kq"] = jnp.concatenate([kb_st, u["q"]], axis=0).astype(BF16)
            u["rhs"] = jnp.concatenate([v_st * beta_st, kb_st * eg], axis=1).astype(BF16)
            u["qdec"] = u["q"] * eg
            u["kdec"] = (u["k"] * jnp.exp(glast_st - gcol_st)).astype(BF16)
            u["gl"] = jnp.exp(glast_st)
            units.append(u)

    for u in units:
        u["kk"] = _dot_nt(u["kq"], u["k"].astype(BF16))
    for u in units:
        strict = u["incl"] & (rid != cid)
        a = jnp.where(strict, u["kk"][0:DN_ROWS] * u["decay"], 0.0)
        u["attn"] = (u["kk"][DN_ROWS:2 * DN_ROWS] * u["decay"]).astype(BF16)
        u["pinv"] = eye - a
        u["a"] = a.astype(BF16)
    for u in units:
        u["apow"] = _dot(u["a"], u["a"])
    for _ in range(4):
        for u in units:
            ap = u["apow"].astype(BF16)
            u["both"] = _dot(jnp.concatenate([u["pinv"].astype(BF16), ap], axis=0), ap)
        for u in units:
            u["pinv"] = u["pinv"] + u["both"][0:DN_ROWS]
            u["apow"] = u["both"][DN_ROWS:2 * DN_ROWS]
    for u in units:
        u["last"] = _dot(u["pinv"].astype(BF16), u["apow"].astype(BF16))
    for u in units:
        u["uw"] = _dot((u["pinv"] + u["last"]).astype(BF16), u["rhs"])

    for step in range(DN_GROUP):
        pair = units[2 * step:2 * step + 2]
        for u in pair:
            d = u["d"]
            u["wq"] = []
            for h in range(H):
                hr = slice(h * C, (h + 1) * C)
                lhs = jnp.concatenate([u["uw"][hr, 128:256], u["qdec"][hr]], axis=0).astype(BF16)
                u["wq"].append(_dot(lhs, s_scr[d * H + h].astype(BF16)))
        for u in pair:
            u["vnew"] = [u["uw"][h * C:(h + 1) * C, 0:128] - u["wq"][h][0:C] for h in range(H)]
            vnew_st = jnp.concatenate(u["vnew"], axis=0).astype(BF16)
            u["o"] = jnp.concatenate([w[C:2 * C] for w in u["wq"]], axis=0) + _dot(u["attn"], vnew_st)
        for u in pair:
            d = u["d"]
            for h in range(H):
                hr = slice(h * C, (h + 1) * C)
                s_scr[d * H + h] = (s_scr[d * H + h] * u["gl"][h * C:h * C + 1, :]
                                    + _dot_tn(u["kdec"][hr], u["vnew"][h].astype(BF16)))
                u["o_ref"][u["rows"], h * 128:(h + 1) * 128] = u["o"][hr]


def _dn_scan(qkvn, pg, a_log, dt_bias, batch, seq_len):
    T = qkvn.shape[0]
    C, H, G = DN_CHUNK, DN_HEADS, DN_GROUP
    R = G * C
    nc = seq_len // C
    ncg = nc // G
    ab = pg[:, 0:4 * H].reshape(batch, nc, C, 2, 2, H)
    ab_row = jnp.transpose(ab, (0, 1, 3, 4, 5, 2)).reshape(batch * nc, 4, H * C)
    ab_row = jnp.pad(ab_row, ((0, 0), (0, 4), (0, 0)))
    pad_lanes = lambda v: jnp.pad(v.reshape(1, 2 * H), ((0, 0), (0, GATE_LANES - 2 * H)))
    row_param = lambda v: jnp.pad(jnp.repeat(v, C, axis=1), ((0, 6), (0, 0)))

    fwd = lambda b, c: (b * ncg + c, 0)
    bwd = lambda b, c: (b * ncg + ncg - 1 - c, 0)
    fwd3 = lambda b, c: (b * ncg + c, 0, 0)
    bwd3 = lambda b, c: (b * ncg + ncg - 1 - c, 0, 0)
    const = lambda shape: pl.BlockSpec(shape, lambda b, c: (0,) * len(shape))
    return pl.pallas_call(
        _dn_scan_kernel,
        grid=(batch, ncg),
        in_specs=[
            pl.BlockSpec((R, 3 * GROUP), fwd), pl.BlockSpec((R, 3 * GROUP), bwd),
            pl.BlockSpec((R, GATE_LANES), fwd), pl.BlockSpec((R, GATE_LANES), bwd),
            pl.BlockSpec((G, 8, H * C), fwd3), pl.BlockSpec((G, 8, H * C), bwd3),
            const((1, GATE_LANES)), const((1, GATE_LANES)), const((8, H * C)), const((8, H * C)),
        ],
        out_specs=[pl.BlockSpec((R, GROUP), fwd), pl.BlockSpec((R, GROUP), bwd)],
        out_shape=[jax.ShapeDtypeStruct((T, GROUP), F32), jax.ShapeDtypeStruct((T, GROUP), F32)],
        scratch_shapes=[pltpu.VMEM((2 * H, DN_HEAD_DIM, DN_HEAD_DIM), F32)],
        compiler_params=_cparams("arbitrary", "arbitrary"),
        name="deltanet_scan",
    )(qkvn, qkvn, pg, pg, ab_row, ab_row, pad_lanes(a_log), pad_lanes(dt_bias),
      row_param(a_log), row_param(dt_bias))


def _dn_post_kernel(of_ref, ob_ref, z_ref, g_ref, y_ref):
    for h in range(DN_HEADS):
        lanes = slice(h * 128, (h + 1) * 128)
        o = of_ref[:, lanes] + ob_ref[:, lanes]
        o = o * lax.rsqrt(jnp.mean(o * o, axis=-1, keepdims=True) + EPS) * g_ref[...]
        y_ref[:, lanes] = (o * _silu(z_ref[:, lanes])).astype(BF16)


def _dn_post(o_f, o_b, p, norm_g, tm=1024):
    T = p.shape[0]
    row = lambda cb: pl.BlockSpec((tm, GROUP), lambda i: (i, cb))
    return pl.pallas_call(
        _dn_post_kernel,
        grid=(T // tm,),
        in_specs=[row(0), row(0), row(9), pl.BlockSpec((1, DN_HEAD_DIM), lambda i: (0, 0))],
        out_specs=row(0),
        out_shape=jax.ShapeDtypeStruct((T, GROUP), BF16),
        compiler_params=_cparams("parallel"),
        name="deltanet_post",
    )(o_f, o_b, p, norm_g)


def _deltanet_mixer(p, pg, conv_w, a_log, dt_bias, norm_g, batch, seq_len):
    qkvn = _dn_prep(p, conv_w, seq_len)
    o_f, o_b = _dn_scan(qkvn, pg, a_log, dt_bias, batch, seq_len)
    return _dn_post(o_f, o_b, p, norm_g[None])


def _outproj_kernel(x_ref, ya_ref, yb_ref, yc_ref, yd_ref, w_ref, g_ref, xn_ref, hn_ref):
    acc = x_ref[...]
    for gi, y_ref in enumerate((ya_ref, yb_ref, yc_ref, yd_ref)):
        acc = acc + _dot(y_ref[...], w_ref[gi * GROUP:(gi + 1) * GROUP, :])
    xn_ref[...] = acc
    ms = jnp.mean(acc * acc, axis=-1, keepdims=True)
    hn_ref[...] = (acc * lax.rsqrt(ms + EPS) * g_ref[...]).astype(BF16)


def _out_proj(x2, ys, w_out, norm2_g, tm=512):
    T = x2.shape[0]
    row = lambda width: pl.BlockSpec((tm, width), lambda i: (i, 0))
    return pl.pallas_call(
        _outproj_kernel,
        grid=(T // tm,),
        in_specs=[row(D_MODEL), row(GROUP), row(GROUP), row(GROUP), row(GROUP),
                  pl.BlockSpec((D_MODEL, D_MODEL), lambda i: (0, 0)),
                  pl.BlockSpec((1, D_MODEL), lambda i: (0, 0))],
        out_specs=[row(D_MODEL), row(D_MODEL)],
        out_shape=[jax.ShapeDtypeStruct((T, D_MODEL), F32),
                   jax.ShapeDtypeStruct((T, D_MODEL), BF16)],
        compiler_params=_cparams("parallel"),
        name="out_proj",
    )(x2, *ys, w_out, norm2_g)


FFN_HALF = FFN_HIDDEN // 2
FFN_COL_CHUNK = 256


def _ffn_hidden_kernel(hn_ref, hp_ref, hx_ref, wg_ref, wu_ref, dw_ref, db_ref, a_ref, hext_scr, g_scr,
                       *, tm, seq_len):
    first, last, _ = _seq_edges(pl.program_id(1), seq_len // tm)
    hext_scr[0:16, :] = jnp.where(first, jnp.zeros_like(hp_ref[...]), hp_ref[...])
    hext_scr[16:16 + tm, :] = hn_ref[...]
    hext_scr[16 + tm:32 + tm, :] = jnp.where(last, jnp.zeros_like(hx_ref[...]), hx_ref[...])
    for c in range(FFN_HALF // FFN_COL_CHUNK):
        cols = slice(c * FFN_COL_CHUNK, (c + 1) * FFN_COL_CHUNK)
        g = g_scr.at[c % 2]
        g[...] = _dot(hext_scr[...], wg_ref[:, cols])
        up = _dot(hn_ref[...], wu_ref[:, cols])
        gate = (g[pl.ds(15, tm), :] * dw_ref[0:1, cols] + g[pl.ds(16, tm), :] * dw_ref[1:2, cols]
                + g[pl.ds(17, tm), :] * dw_ref[2:3, cols] + db_ref[:, cols])
        a_ref[:, cols] = (_silu(gate) * up).astype(BF16)


def _ffn_down_kernel(a_ref, xn_ref, wd_ref, o_ref):
    o_ref[...] = xn_ref[...] + _dot(a_ref[...], wd_ref[...])


def _ffn(hn, xn, w_gate, dw_w, dw_b, w_up, w_down, seq_len, tm_hidden=512, tm_down=256):
    T = hn.shape[0]
    n_tiles = T // tm_hidden
    r = tm_hidden // 16
    half_cols = lambda rows: pl.BlockSpec((rows, FFN_HALF), lambda h, i: (0, h),
                                          pipeline_mode=pl.Buffered(1))
    act = pl.pallas_call(
        functools.partial(_ffn_hidden_kernel, tm=tm_hidden, seq_len=seq_len),
        grid=(2, n_tiles),
        in_specs=[
            pl.BlockSpec((tm_hidden, D_MODEL), lambda h, i: (i, 0)),
            pl.BlockSpec((16, D_MODEL), lambda h, i: (jnp.maximum(i * r - 1, 0), 0)),
            pl.BlockSpec((16, D_MODEL), lambda h, i: (jnp.minimum((i + 1) * r, n_tiles * r - 1), 0)),
            half_cols(D_MODEL), half_cols(D_MODEL), half_cols(3), half_cols(1),
        ],
        out_specs=pl.BlockSpec((tm_hidden, FFN_HALF), lambda h, i: (i, h)),
        out_shape=jax.ShapeDtypeStruct((T, FFN_HIDDEN), BF16),
        scratch_shapes=[pltpu.VMEM((tm_hidden + 32, D_MODEL), BF16),
                        pltpu.VMEM((2, tm_hidden + 32, FFN_COL_CHUNK), F32)],
        compiler_params=_cparams("arbitrary", "arbitrary"),
        name="ffn_hidden",
    )(hn, hn, hn, w_gate, w_up, dw_w, dw_b)
    return pl.pallas_call(
        _ffn_down_kernel,
        grid=(T // tm_down,),
        in_specs=[
            pl.BlockSpec((tm_down, FFN_HIDDEN), lambda i: (i, 0)),
            pl.BlockSpec((tm_down, D_MODEL), lambda i: (i, 0)),
            pl.BlockSpec((FFN_HIDDEN, D_MODEL), lambda i: (0, 0), pipeline_mode=pl.Buffered(1)),
        ],
        out_specs=pl.BlockSpec((tm_down, D_MODEL), lambda i: (i, 0)),
        out_shape=jax.ShapeDtypeStruct((T, D_MODEL), F32),
        compiler_params=_cparams("parallel"),
        name="ffn_down",
    )(act, xn, w_down)


def _layer(x2, l, batch, seq_len, rel_bias, norm1_g, w_in, w_pool, pool_scale, att_q_g, att_k_g,
           conv_dw_w, conv_dw_b, conv_ln_g, conv_ln_b, conv_pw, dn_conv_w, dn_a_log, dn_dt_bias,
           dn_norm_g, w_out, norm2_g, ffn_w_gate, ffn_dw_w, ffn_dw_b, ffn_w_up, ffn_w_down):
    w_main = w_in[l][:, :PROJ_MAIN].astype(BF16)
    w_gates = jnp.pad(w_in[l][:, PROJ_MAIN:], ((0, 0), (0, GATE_LANES - 4 * DN_HEADS))).astype(BF16)
    p, pg = _in_proj(x2, norm1_g[l][None], w_main, w_gates)
    ya = _pool_mixer(p, w_pool[l].astype(BF16), pool_scale[l][None], seq_len)
    yb = _attention_mixer(p, att_q_g[l], att_k_g[l], rel_bias, seq_len)
    yc = _conformer_mixer(p, conv_dw_w[l], conv_dw_b[l][None], conv_ln_g[l][None], conv_ln_b[l][None],
                          conv_pw[l].astype(BF16), seq_len)
    yd = _deltanet_mixer(p, pg, dn_conv_w[l], dn_a_log[l], dn_dt_bias[l], dn_norm_g[l], batch, seq_len)
    xn, hn = _out_proj(x2, (ya, yb, yc, yd), w_out[l].astype(BF16), norm2_g[l][None])
    return _ffn(hn, xn, ffn_w_gate[l].astype(BF16), ffn_dw_w[l], ffn_dw_b[l][None],
                ffn_w_up[l].astype(BF16), ffn_w_down[l].astype(BF16), seq_len)


def kernel(x, rel_bias, norm1_g, w_in, w_pool, pool_scale, att_q_g, att_k_g, conv_dw_w, conv_dw_b,
           conv_ln_g, conv_ln_b, conv_pw, dn_conv_w, dn_a_log, dn_dt_bias, dn_norm_g, w_out, norm2_g,
           ffn_w_gate, ffn_dw_w, ffn_dw_b, ffn_w_up, ffn_w_down):
    batch, seq_len, _ = x.shape
    x2 = x.reshape(batch * seq_len, D_MODEL)
    for l in range(norm1_g.shape[0]):
        x2 = _layer(x2, l, batch, seq_len, rel_bias, norm1_g, w_in, w_pool, pool_scale, att_q_g,
                    att_k_g, conv_dw_w, conv_dw_b, conv_ln_g, conv_ln_b, conv_pw, dn_conv_w, dn_a_log,
                    dn_dt_bias, dn_norm_g, w_out, norm2_g, ffn_w_gate, ffn_dw_w, ffn_dw_b, ffn_w_up,
                    ffn_w_down)
    return x2.reshape(batch, seq_len, D_MODEL)
```

```python
import functools
import math

import jax
import jax.numpy as jnp
import numpy as np
from jax import lax
from jax.experimental import pallas as pl
from jax.experimental.pallas import tpu as pltpu

F32 = jnp.float32
BF16 = jnp.bfloat16

D_MODEL = 2048
GROUP = 512
POOL_WINDOWS = (2, 4, 8, 16)
ATT_HEAD_DIM = 64
ATT_HEADS = 8
ATT_RADIUS = 64
ATT_DILATIONS = (1, 4, 16)
REL_BUCKETS = 32
REL_MAX_DIST = 1024
CONV_WIDTH = 31
DN_HEAD_DIM = 128
DN_HEADS = 4
DN_CHUNK = 64
FFN_HIDDEN = 5632
EPS = 1e-6
NEG_INF = -1e30

PROJ_MAIN = 10 * GROUP
GATE_LANES = 128

VMEM_LIMIT_BYTES = 56 * 1024 * 1024


def _cparams(*sem):
    return pltpu.CompilerParams(dimension_semantics=sem, vmem_limit_bytes=VMEM_LIMIT_BYTES)


def _sigmoid(x):
    return 1.0 / (1.0 + jnp.exp(-x))


def _silu(x):
    return x * _sigmoid(x)


def _dot(a, b):
    return jnp.dot(a, b, preferred_element_type=F32)


def _dot_nt(a, b):
    return lax.dot_general(a, b, (((1,), (1,)), ((), ())), preferred_element_type=F32)


def _dot_tn(a, b):
    return lax.dot_general(a, b, (((0,), (0,)), ((), ())), preferred_element_type=F32)


def _cast_kernel(w_ref, o_ref):
    o_ref[...] = w_ref[...].astype(BF16)


def _weight_bf16(w_stack, layer, cols=None, tr=512):
    _, rows, width = w_stack.shape
    cols = width if cols is None else cols
    tc = cols // 2 if cols >= 4096 else cols
    return pl.pallas_call(
        _cast_kernel,
        grid=(rows // tr, cols // tc),
        in_specs=[pl.BlockSpec((None, tr, tc), lambda i, j: (layer, i, j))],
        out_specs=pl.BlockSpec((tr, tc), lambda i, j: (i, j)),
        out_shape=jax.ShapeDtypeStruct((rows, cols), BF16),
        compiler_params=_cparams("parallel", "parallel"),
        name="weight_bf16",
    )(w_stack)


def _inproj_kernel(x_ref, g_ref, w_ref, wg_ref, p_ref, pg_ref):
    x = x_ref[...]
    ms = jnp.mean(x * x, axis=-1, keepdims=True)
    h = (x * lax.rsqrt(ms + EPS) * g_ref[...]).astype(BF16)
    pg_ref[...] = _dot(h, wg_ref[...])
    p_ref[...] = _dot(h, w_ref[...])


def _in_proj(x2, norm_g, w_main, w_gate, tm=256):
    T = x2.shape[0]
    resident = lambda shape: pl.BlockSpec(shape, lambda i: (0, 0), pipeline_mode=pl.Buffered(1))
    return pl.pallas_call(
        _inproj_kernel,
        grid=(T // tm,),
        in_specs=[
            pl.BlockSpec((tm, D_MODEL), lambda i: (i, 0)),
            resident((1, D_MODEL)),
            resident((D_MODEL, PROJ_MAIN)),
            resident((D_MODEL, GATE_LANES)),
        ],
        out_specs=[
            pl.BlockSpec((tm, PROJ_MAIN), lambda i: (i, 0)),
            pl.BlockSpec((tm, GATE_LANES), lambda i: (i, 0)),
        ],
        out_shape=[
            jax.ShapeDtypeStruct((T, PROJ_MAIN), F32),
            jax.ShapeDtypeStruct((T, GATE_LANES), F32),
        ],
        compiler_params=_cparams("parallel"),
        name="in_proj",
    )(x2, norm_g, w_main, w_gate)


def _halo_specs(tm, halo, width, col_block, n_tiles):
    r = tm // halo
    last = n_tiles * r - 1
    prev_spec = pl.BlockSpec((halo, width), lambda i, *_: (jnp.maximum(i * r - 1, 0), col_block))
    next_spec = pl.BlockSpec((halo, width), lambda i, *_: (jnp.minimum((i + 1) * r, last), col_block))
    return prev_spec, next_spec


def _seq_edges(i, tiles_per_seq):
    k = lax.rem(i, tiles_per_seq)
    return k == 0, k == tiles_per_seq - 1, k


def _pool_kernel(u_ref, up_ref, un_ref, w_ref, sc_ref, y_ref, ext_scr, *, tm, seq_len):
    i = pl.program_id(0)
    first, last, k = _seq_edges(i, seq_len // tm)
    ext_scr[0:8, :] = jnp.where(first, jnp.zeros_like(up_ref[...]), up_ref[...])
    ext_scr[8:8 + tm, :] = u_ref[...]
    ext_scr[8 + tm:16 + tm, :] = jnp.where(last, jnp.zeros_like(un_ref[...]), un_ref[...])
    t = k * tm + lax.broadcasted_iota(jnp.int32, (tm, 1), 0)
    for gi, win in enumerate(POOL_WINDOWS):
        half = win // 2
        lanes = slice(gi * 128, (gi + 1) * 128)
        s = ext_scr[pl.ds(8 - half, tm), lanes]
        for kk in range(1, win):
            s = s + ext_scr[pl.ds(8 - half + kk, tm), lanes]
        cnt = (jnp.minimum(t + half, seq_len) - jnp.maximum(t - half, 0)).astype(F32)
        pooled = s / cnt - ext_scr[pl.ds(8, tm), lanes]
        y = _dot(pooled.astype(BF16), w_ref[gi]) * sc_ref[:, lanes]
        y_ref[:, lanes] = y.astype(BF16)


def _pool_mixer(p, w_pool, pool_scale, seq_len, tm=1024):
    T = p.shape[0]
    n_tiles = T // tm
    prev_spec, next_spec = _halo_specs(tm, 8, GROUP, 0, n_tiles)
    return pl.pallas_call(
        functools.partial(_pool_kernel, tm=tm, seq_len=seq_len),
        grid=(n_tiles,),
        in_specs=[
            pl.BlockSpec((tm, GROUP), lambda i: (i, 0)),
            prev_spec,
            next_spec,
            pl.BlockSpec((4, 128, 128), lambda i: (0, 0, 0)),
            pl.BlockSpec((1, GROUP), lambda i: (0, 0)),
        ],
        out_specs=pl.BlockSpec((tm, GROUP), lambda i: (i, 0)),
        out_shape=jax.ShapeDtypeStruct((T, GROUP), BF16),
        scratch_shapes=[pltpu.VMEM((tm + 16, GROUP), F32)],
        compiler_params=_cparams("parallel"),
        name="pool_mixer",
    )(p, p, p, w_pool, pool_scale)


def _conformer_kernel(v_ref, vp_ref, vn_ref, g_ref, gp_ref, gn_ref, dw_ref, db_ref, lg_ref, lb_ref,
                      pw_ref, y_ref, ext_scr, shift_scr, *, tm, seq_len):
    i = pl.program_id(0)
    first, last, _ = _seq_edges(i, seq_len // tm)
    hp = vp_ref[...] * _sigmoid(gp_ref[...])
    hn = vn_ref[...] * _sigmoid(gn_ref[...])
    ext_scr[0:16, :] = jnp.where(first, jnp.zeros_like(hp), hp)
    ext_scr[16:16 + tm, :] = v_ref[...] * _sigmoid(g_ref[...])
    ext_scr[16 + tm:32 + tm, :] = jnp.where(last, jnp.zeros_like(hn), hn)
    base = 16 - CONV_WIDTH // 2
    acc = db_ref[...]
    for b in range(8):
        taps = [(a, 8 * a + b - base) for a in range(5) if 0 <= 8 * a + b - base < CONV_WIDTH]
        rows = tm + 8 * taps[-1][0]
        shift_scr[b, 0:rows, :] = ext_scr[pl.ds(b, rows), :]
        for a, kk in taps:
            acc = acc + shift_scr[b, 8 * a:8 * a + tm, :] * dw_ref[kk:kk + 1, :]
    mu = jnp.mean(acc, axis=-1, keepdims=True)
    xc = acc - mu
    var = jnp.mean(xc * xc, axis=-1, keepdims=True)
    h = _silu(xc * lax.rsqrt(var + EPS) * lg_ref[...] + lb_ref[...])
    y_ref[...] = _dot(h.astype(BF16), pw_ref[...]).astype(BF16)


def _conformer_mixer(p, dw_w, dw_b, ln_g, ln_b, pw, seq_len, tm=512):
    T = p.shape[0]
    n_tiles = T // tm
    vprev, vnext = _halo_specs(tm, 16, GROUP, 4, n_tiles)
    gprev, gnext = _halo_specs(tm, 16, GROUP, 5, n_tiles)
    const = lambda shape: pl.BlockSpec(shape, lambda i: (0,) * len(shape))
    return pl.pallas_call(
        functools.partial(_conformer_kernel, tm=tm, seq_len=seq_len),
        grid=(n_tiles,),
        in_specs=[
            pl.BlockSpec((tm, GROUP), lambda i: (i, 4)), vprev, vnext,
            pl.BlockSpec((tm, GROUP), lambda i: (i, 5)), gprev, gnext,
            const((CONV_WIDTH, GROUP)), const((1, GROUP)), const((1, GROUP)), const((1, GROUP)),
            const((GROUP, GROUP)),
        ],
        out_specs=pl.BlockSpec((tm, GROUP), lambda i: (i, 0)),
        out_shape=jax.ShapeDtypeStruct((T, GROUP), BF16),
        scratch_shapes=[pltpu.VMEM((tm + 32, GROUP), F32), pltpu.VMEM((8, tm + 32, GROUP), F32)],
        compiler_params=_cparams("parallel"),
        name="conformer_mixer",
    )(p, p, p, p, p, p, dw_w, dw_b, ln_g, ln_b, pw)


ATT_TILE = 1024
ATT_BLOCK_GROUP = 8


def _t5_bucket_table():
    nb = REL_BUCKETS // 2
    max_exact = nb // 2
    i = np.arange(ATT_RADIUS)[:, None]
    j = np.arange(3 * ATT_RADIUS)[None, :]
    off = j - ATT_RADIUS - i
    tables = []
    for dil in ATT_DILATIONS:
        rel = off * dil
        n = np.abs(rel)
        nf = np.maximum(n, 1).astype(np.float32)
        large = max_exact + (np.log(nf / np.float32(max_exact)) / np.float32(math.log(REL_MAX_DIST / max_exact))
                             * np.float32(nb - max_exact)).astype(np.int32)
        large = np.minimum(large, nb - 1)
        bucket = np.where(rel > 0, nb, 0) + np.where(n < max_exact, n, large)
        tables.append(np.where(np.abs(off) <= ATT_RADIUS, bucket, -1))
    return np.stack(tables).astype(np.int32)


def _att_kernel(rb_ref, bkt_ref, qg_ref, kg_ref, mseg_ref, q_ref, kc_ref, kn_ref,
                vp_ref, vc_ref, vn_ref, y_ref, bias_scr, qbuf, kbuf, vbuf, acc_scr, m_scr, l_scr,
                *, seq_len):
    tile = ATT_TILE
    rad = ATT_RADIUS
    hp = pl.program_id(0)
    i = pl.program_id(1)

    def rms(x, g):
        ms = _dot((x * x).astype(BF16), mseg_ref[...])
        return x * lax.rsqrt(ms + EPS) * g

    @pl.when(i == 0)
    def _():
        for di in range(len(ATT_DILATIONS)):
            bkt = bkt_ref[di]
            for hh in range(2):
                b = jnp.zeros(bkt.shape, F32)
                for bb in range(REL_BUCKETS):
                    b = jnp.where(bkt == bb, rb_ref[bb, 2 * hp + hh], b)
                bias_scr[di, hh * rad:(hh + 1) * rad, :] = jnp.where(bkt < 0, NEG_INF, b)
        kbuf[0:tile, :] = jnp.zeros((tile, 128), F32)
        kbuf[tile:2 * tile, :] = rms(kc_ref[...], kg_ref[...])

    @pl.when(i > 0)
    def _():
        kbuf[0:tile, :] = kbuf[tile:2 * tile, :]
        kbuf[tile:2 * tile, :] = kbuf[2 * tile:3 * tile, :]

    first, last, _ = _seq_edges(i, seq_len // tile)

    qbuf[...] = rms(q_ref[...], qg_ref[...]) * (ATT_HEAD_DIM ** -0.5)
    kbuf[2 * tile:3 * tile, :] = rms(kn_ref[...], kg_ref[...])
    vbuf[0:tile, :] = vp_ref[...]
    vbuf[tile:2 * tile, :] = vc_ref[...]
    vbuf[2 * tile:3 * tile, :] = vn_ref[...]

    head0 = lax.broadcasted_iota(jnp.int32, (1, 128), 1) < ATT_HEAD_DIM
    col = lax.broadcasted_iota(jnp.int32, (1, 3 * rad), 1)

    def rows(start, size, dil):
        return pl.ds(start, size) if dil == 1 else pl.ds(start, size, stride=dil)

    blocks = [(di, dil, r + rad * dil * m)
              for di, dil in enumerate(ATT_DILATIONS) for r in range(dil) for m in range(tile // (rad * dil))]
    for g0 in range(0, len(blocks), ATT_BLOCK_GROUP):
        group = blocks[g0:g0 + ATT_BLOCK_GROUP]
        scores = []
        for di, dil, qstart in group:
            kstart = tile + qstart - rad * dil
            qb = qbuf[rows(qstart, rad, dil), :]
            kb = kbuf[rows(kstart, 3 * rad, dil), :].astype(BF16)
            q2 = jnp.concatenate([jnp.where(head0, qb, 0.0), jnp.where(head0, 0.0, qb)], axis=0)
            scores.append(_dot_nt(q2.astype(BF16), kb))
        probs = []
        for (di, dil, qstart), s in zip(group, scores):
            kstart = tile + qstart - rad * dil
            s = s + bias_scr[di]
            n_prev = max(0, -(-(tile - kstart) // dil))
            n_upto = min(3 * rad, -(-(2 * tile - kstart) // dil))
            if n_prev > 0:
                s = jnp.where(col < jnp.where(first, n_prev, 0), NEG_INF, s)
            if n_upto < 3 * rad:
                s = jnp.where(col >= jnp.where(last, n_upto, 3 * rad), NEG_INF, s)
            mx = jnp.max(s, axis=-1, keepdims=True)
            pe = jnp.exp(s - mx)
            probs.append((mx, jnp.sum(pe, axis=-1, keepdims=True), pe.astype(BF16)))
        outs = []
        for (di, dil, qstart), (mx, ls, pe) in zip(group, probs):
            kstart = tile + qstart - rad * dil
            outs.append(_dot(pe, vbuf[rows(kstart, 3 * rad, dil), :].astype(BF16)))
        for (di, dil, qstart), (mx, ls, pe), o in zip(group, probs, outs):
            dst = rows(qstart, rad, dil)
            acc_scr[di, dst, :] = jnp.where(head0, o[0:rad], o[rad:2 * rad])
            m_scr[di, dst, :] = jnp.where(head0, mx[0:rad], mx[rad:2 * rad])
            l_scr[di, dst, :] = jnp.where(head0, ls[0:rad], ls[rad:2 * rad])

    m_all = jnp.maximum(jnp.maximum(m_scr[0], m_scr[1]), m_scr[2])
    num = jnp.zeros((tile, 128), F32)
    den = jnp.zeros((tile, 128), F32)
    for di in range(len(ATT_DILATIONS)):
        e = jnp.exp(m_scr[di] - m_all)
        num = num + acc_scr[di] * e
        den = den + l_scr[di] * e
    y_ref[...] = (num / den).astype(BF16)


def _attention_mixer(p, q_g, k_g, rel_bias, seq_len):
    T = p.shape[0]
    tile = ATT_TILE
    n_tiles = T // tile
    rad = ATT_RADIUS
    seg = np.kron(np.eye(2), np.full((ATT_HEAD_DIM, ATT_HEAD_DIM), 1.0 / ATT_HEAD_DIM))
    mseg = jnp.asarray(seg, BF16)
    bkt = jnp.asarray(_t5_bucket_table())
    qg2 = jnp.tile(q_g, 2)[None]
    kg2 = jnp.tile(k_g, 2)[None]

    def blk(col0, shift):
        return pl.BlockSpec((tile, 128),
                            lambda hp, i: (jnp.clip(i + shift, 0, n_tiles - 1), col0 * 4 + hp))

    const = lambda shape: pl.BlockSpec(shape, lambda hp, i: (0,) * len(shape))
    return pl.pallas_call(
        functools.partial(_att_kernel, seq_len=seq_len),
        grid=(4, n_tiles),
        in_specs=[
            pl.BlockSpec(memory_space=pltpu.SMEM),
            const((3, rad, 3 * rad)), const((1, 128)), const((1, 128)), const((128, 128)),
            blk(1, 0),
            blk(2, 0), blk(2, 1),
            blk(3, -1), blk(3, 0), blk(3, 1),
        ],
        out_specs=pl.BlockSpec((tile, 128), lambda hp, i: (i, hp)),
        out_shape=jax.ShapeDtypeStruct((T, GROUP), BF16),
        scratch_shapes=[
            pltpu.VMEM((3, 2 * rad, 3 * rad), F32),
            pltpu.VMEM((tile, 128), F32),
            pltpu.VMEM((3 * tile, 128), F32),
            pltpu.VMEM((3 * tile, 128), F32),
            pltpu.VMEM((3, tile, 128), F32),
            pltpu.VMEM((3, tile, 128), F32),
            pltpu.VMEM((3, tile, 128), F32),
        ],
        compiler_params=_cparams("arbitrary", "arbitrary"),
        name="dilated_attention",
    )(rel_bias, bkt, qg2, kg2, mseg, p, p, p, p, p, p)


def _softplus(x):
    return jnp.maximum(x, 0.0) + jnp.log1p(jnp.exp(-jnp.abs(x)))


def _dn_prep_kernel(x_ref, xp_ref, xn_ref, w_ref, o_ref, ext_scr, *, tm, seq_len):
    i = pl.program_id(0)
    first, last, _ = _seq_edges(i, seq_len // tm)
    ext_scr[0:8, :] = jnp.where(first, jnp.zeros_like(xp_ref[...]), xp_ref[...])
    ext_scr[8:8 + tm, :] = x_ref[...]
    ext_scr[8 + tm:16 + tm, :] = jnp.where(last, jnp.zeros_like(xn_ref[...]), xn_ref[...])
    for cb in range(3 * DN_HEADS):
        lanes = slice(cb * 128, (cb + 1) * 128)
        acc = ext_scr[pl.ds(6, tm), lanes] * w_ref[0:1, lanes]
        for kk in range(1, 4):
            acc = acc + ext_scr[pl.ds(6 + kk, tm), lanes] * w_ref[kk:kk + 1, lanes]
        y = _silu(acc)
        if cb < 2 * DN_HEADS:
            y = y * lax.rsqrt(jnp.sum(y * y, axis=-1, keepdims=True) + EPS)
        if cb < DN_HEADS:
            y = y * (DN_HEAD_DIM ** -0.5)
        o_ref[:, lanes] = y


def _dn_prep(p, conv_w, seq_len, tm=512):
    T = p.shape[0]
    n_tiles = T // tm
    width = 3 * GROUP
    prev_spec, next_spec = _halo_specs(tm, 8, width, 2, n_tiles)
    return pl.pallas_call(
        functools.partial(_dn_prep_kernel, tm=tm, seq_len=seq_len),
        grid=(n_tiles,),
        in_specs=[pl.BlockSpec((tm, width), lambda i: (i, 2)), prev_spec, next_spec,
                  pl.BlockSpec((4, width), lambda i: (0, 0))],
        out_specs=pl.BlockSpec((tm, width), lambda i: (i, 0)),
        out_shape=jax.ShapeDtypeStruct((T, width), F32),
        scratch_shapes=[pltpu.VMEM((tm + 16, width), F32)],
        compiler_params=_cparams("parallel"),
        name="deltanet_prep",
    )(p, p, p, conv_w)


DN_GROUP = 4
DN_ROWS = DN_HEADS * DN_CHUNK


def _dn_scan_kernel(xf_ref, xb_ref, gf_ref, gb_ref, rf_ref, rb_ref, alc_ref, dtc_ref, alr_ref, dtr_ref,
                    of_ref, ob_ref, s_scr):
    C = DN_CHUNK
    H = DN_HEADS
    R = DN_GROUP * C

    @pl.when(pl.program_id(1) == 0)
    def _():
        s_scr[...] = jnp.zeros_like(s_scr)

    row_in_chunk = lax.rem(lax.broadcasted_iota(jnp.int32, (R, 1), 0), C)
    lane_in_chunk = lax.rem(lax.broadcasted_iota(jnp.int32, (1, DN_ROWS), 1), C)
    rid = lax.broadcasted_iota(jnp.int32, (DN_ROWS, DN_ROWS), 0)
    cid = lax.broadcasted_iota(jnp.int32, (DN_ROWS, DN_ROWS), 1)
    same_head = (rid // C) == (cid // C)
    eye = (rid == cid).astype(F32)

    def stack_heads(x, col0):
        return jnp.concatenate([x[:, col0 + h * 128:col0 + (h + 1) * 128] for h in range(H)], axis=0)

    def stack_cols(x, lane0, rows=None):
        parts = []
        for h in range(H):
            c = x[:, lane0 + h:lane0 + h + 1]
            if rows is not None:
                c = c[rows:rows + 1, :]
            parts.append(jnp.broadcast_to(c, (C, 128)))
        return jnp.concatenate(parts, axis=0)

    refs = ((xf_ref, gf_ref, rf_ref, of_ref), (xb_ref, gb_ref, rb_ref, ob_ref))
    gates = []
    for d, (x_ref, gc_ref, gr_ref, o_ref) in enumerate(refs):
        reverse = d == 1
        gcol = gc_ref[...]
        g = -jnp.exp(alc_ref[...]) * _softplus(gcol + dtc_ref[...])
        beta = _sigmoid(gcol)
        gcum = g
        for sh in (1, 2, 4, 8, 16, 32):
            if reverse:
                gcum = gcum + jnp.where(row_in_chunk < C - sh, pltpu.roll(gcum, R - sh, 0), 0.0)
            else:
                gcum = gcum + jnp.where(row_in_chunk >= sh, pltpu.roll(gcum, sh, 0), 0.0)
        grow_all = -jnp.exp(alr_ref[...])[None] * _softplus(gr_ref[...] + dtr_ref[...][None])
        for sh in (1, 2, 4, 8, 16, 32):
            if reverse:
                grow_all = grow_all + jnp.where(lane_in_chunk < C - sh,
                                                pltpu.roll(grow_all, DN_ROWS - sh, 2), 0.0)
            else:
                grow_all = grow_all + jnp.where(lane_in_chunk >= sh, pltpu.roll(grow_all, sh, 2), 0.0)

        gates.append((gcum, beta, grow_all))

    units = []
    for step in range(DN_GROUP):
        for d, (x_ref, _, _, o_ref) in enumerate(refs):
            reverse = d == 1
            ci = DN_GROUP - 1 - step if reverse else step
            gcum, beta, grow_all = gates[d]
            rows = slice(ci * C, (ci + 1) * C)
            x = x_ref[rows, :]
            u = dict(d=d, rows=rows, o_ref=o_ref)
            u["q"] = stack_heads(x, 0)
            u["k"] = stack_heads(x, GROUP)
            v_st = stack_heads(x, 2 * GROUP)
            gc_c = gcum[rows, :]
            beta_st = stack_cols(beta[rows, :], 8 + d * H)
            gcol_st = stack_cols(gc_c, d * H)
            glast_st = stack_cols(gc_c, d * H, rows=0 if reverse else C - 1)
            grow = grow_all[ci, d:d + 1, :]
            tri = (rid <= cid) if reverse else (rid >= cid)
            u["incl"] = same_head & tri
            u["decay"] = jnp.where(u["incl"],
                                   jnp.exp(jnp.concatenate([gcol_st, gcol_st], axis=1) - grow), 0.0)
            kb_st = u["k"] * beta_st
            eg = jnp.exp(gcol_st)
            u["kq"] = jnp.concatenate([kb_st, u["q"]], axis=0).astype(BF16)
            u["rhs"] = jnp.concatenate([v_st * beta_st, kb_st * eg], axis=1).astype(BF16)
            u["qdec"] = u["q"] * eg
            u["kdec"] = (u["k"] * jnp.exp(glast_st - gcol_st)).astype(BF16)
            u["gl"] = jnp.exp(glast_st)
            units.append(u)

    for u in units:
        u["kk"] = _dot_nt(u["kq"], u["k"].astype(BF16))
    for u in units:
        strict = u["incl"] & (rid != cid)
        a = jnp.where(strict, u["kk"][0:DN_ROWS] * u["decay"], 0.0)
        u["attn"] = (u["kk"][DN_ROWS:2 * DN_ROWS] * u["decay"]).astype(BF16)
        u["pinv"] = eye - a
        u["a"] = a.astype(BF16)
    for u in units:
        u["apow"] = _dot(u["a"], u["a"])
    for _ in range(4):
        for u in units:
            ap = u["apow"].astype(BF16)
            u["both"] = _dot(jnp.concatenate([u["pinv"].astype(BF16), ap], axis=0), ap)
        for u in units:
            u["pinv"] = u["pinv"] + u["both"][0:DN_ROWS]
            u["apow"] = u["both"][DN_ROWS:2 * DN_ROWS]
    for u in units:
        u["last"] = _dot(u["pinv"].astype(BF16), u["apow"].astype(BF16))
    for u in units:
        u["uw"] = _dot((u["pinv"] + u["last"]).astype(BF16), u["rhs"])

    for step in range(DN_GROUP):
        pair = units[2 * step:2 * step + 2]
        for u in pair:
            d = u["d"]
            u["wq"] = []
            for h in range(H):
                hr = slice(h * C, (h + 1) * C)
                lhs = jnp.concatenate([u["uw"][hr, 128:256], u["qdec"][hr]], axis=0).astype(BF16)
                u["wq"].append(_dot(lhs, s_scr[d * H + h].astype(BF16)))
        for u in pair:
            u["vnew"] = [u["uw"][h * C:(h + 1) * C, 0:128] - u["wq"][h][0:C] for h in range(H)]
            vnew_st = jnp.concatenate(u["vnew"], axis=0).astype(BF16)
            u["o"] = jnp.concatenate([w[C:2 * C] for w in u["wq"]], axis=0) + _dot(u["attn"], vnew_st)
        for u in pair:
            d = u["d"]
            for h in range(H):
                hr = slice(h * C, (h + 1) * C)
                s_scr[d * H + h] = (s_scr[d * H + h] * u["gl"][h * C:h * C + 1, :]
                                    + _dot_tn(u["kdec"][hr], u["vnew"][h].astype(BF16)))
                u["o_ref"][u["rows"], h * 128:(h + 1) * 128] = u["o"][hr]


def _dn_scan(qkvn, pg, a_log, dt_bias, batch, seq_len):
    T = qkvn.shape[0]
    C, H, G = DN_CHUNK, DN_HEADS, DN_GROUP
    R = G * C
    nc = seq_len // C
    ncg = nc // G
    ab = pg[:, 0:4 * H].reshape(batch, nc, C, 2, 2, H)
    ab_row = jnp.transpose(ab, (0, 1, 3, 4, 5, 2)).reshape(batch * nc, 4, H * C)
    ab_row = jnp.pad(ab_row, ((0, 0), (0, 4), (0, 0)))
    pad_lanes = lambda v: jnp.pad(v.reshape(1, 2 * H), ((0, 0), (0, GATE_LANES - 2 * H)))
    row_param = lambda v: jnp.pad(jnp.repeat(v, C, axis=1), ((0, 6), (0, 0)))

    fwd = lambda b, c: (b * ncg + c, 0)
    bwd = lambda b, c: (b * ncg + ncg - 1 - c, 0)
    fwd3 = lambda b, c: (b * ncg + c, 0, 0)
    bwd3 = lambda b, c: (b * ncg + ncg - 1 - c, 0, 0)
    const = lambda shape: pl.BlockSpec(shape, lambda b, c: (0,) * len(shape))
    return pl.pallas_call(
        _dn_scan_kernel,
        grid=(batch, ncg),
        in_specs=[
            pl.BlockSpec((R, 3 * GROUP), fwd), pl.BlockSpec((R, 3 * GROUP), bwd),
            pl.BlockSpec((R, GATE_LANES), fwd), pl.BlockSpec((R, GATE_LANES), bwd),
            pl.BlockSpec((G, 8, H * C), fwd3), pl.BlockSpec((G, 8, H * C), bwd3),
            const((1, GATE_LANES)), const((1, GATE_LANES)), const((8, H * C)), const((8, H * C)),
        ],
        out_specs=[pl.BlockSpec((R, GROUP), fwd), pl.BlockSpec((R, GROUP), bwd)],
        out_shape=[jax.ShapeDtypeStruct((T, GROUP), F32), jax.ShapeDtypeStruct((T, GROUP), F32)],
        scratch_shapes=[pltpu.VMEM((2 * H, DN_HEAD_DIM, DN_HEAD_DIM), F32)],
        compiler_params=_cparams("arbitrary", "arbitrary"),
        name="deltanet_scan",
    )(qkvn, qkvn, pg, pg, ab_row, ab_row, pad_lanes(a_log), pad_lanes(dt_bias),
      row_param(a_log), row_param(dt_bias))


def _dn_post_kernel(of_ref, ob_ref, z_ref, g_ref, y_ref):
    for h in range(DN_HEADS):
        lanes = slice(h * 128, (h + 1) * 128)
        o = of_ref[:, lanes] + ob_ref[:, lanes]
        o = o * lax.rsqrt(jnp.mean(o * o, axis=-1, keepdims=True) + EPS) * g_ref[...]
        y_ref[:, lanes] = (o * _silu(z_ref[:, lanes])).astype(BF16)


def _dn_post(o_f, o_b, p, norm_g, tm=1024):
    T = p.shape[0]
    row = lambda cb: pl.BlockSpec((tm, GROUP), lambda i: (i, cb))
    return pl.pallas_call(
        _dn_post_kernel,
        grid=(T // tm,),
        in_specs=[row(0), row(0), row(9), pl.BlockSpec((1, DN_HEAD_DIM), lambda i: (0, 0))],
        out_specs=row(0),
        out_shape=jax.ShapeDtypeStruct((T, GROUP), BF16),
        compiler_params=_cparams("parallel"),
        name="deltanet_post",
    )(o_f, o_b, p, norm_g)


def _deltanet_mixer(p, pg, conv_w, a_log, dt_bias, norm_g, batch, seq_len):
    qkvn = _dn_prep(p, conv_w, seq_len)
    o_f, o_b = _dn_scan(qkvn, pg, a_log, dt_bias, batch, seq_len)
    return _dn_post(o_f, o_b, p, norm_g[None])


def _outproj_kernel(x_ref, ya_ref, yb_ref, yc_ref, yd_ref, w_ref, g_ref, xn_ref, hn_ref):
    acc = x_ref[...]
    for gi, y_ref in enumerate((ya_ref, yb_ref, yc_ref, yd_ref)):
        acc = acc + _dot(y_ref[...], w_ref[gi * GROUP:(gi + 1) * GROUP, :])
    xn_ref[...] = acc
    ms = jnp.mean(acc * acc, axis=-1, keepdims=True)
    hn_ref[...] = (acc * lax.rsqrt(ms + EPS) * g_ref[...]).astype(BF16)


def _out_proj(x2, ys, w_out, norm2_g, tm=512):
    T = x2.shape[0]
    row = lambda width: pl.BlockSpec((tm, width), lambda i: (i, 0))
    return pl.pallas_call(
        _outproj_kernel,
        grid=(T // tm,),
        in_specs=[row(D_MODEL), row(GROUP), row(GROUP), row(GROUP), row(GROUP),
                  pl.BlockSpec((D_MODEL, D_MODEL), lambda i: (0, 0)),
                  pl.BlockSpec((1, D_MODEL), lambda i: (0, 0))],
        out_specs=[row(D_MODEL), row(D_MODEL)],
        out_shape=[jax.ShapeDtypeStruct((T, D_MODEL), F32),
                   jax.ShapeDtypeStruct((T, D_MODEL), BF16)],
        compiler_params=_cparams("parallel"),
        name="out_proj",
    )(x2, *ys, w_out, norm2_g)


FFN_HALF = FFN_HIDDEN // 2
FFN_COL_CHUNK = 256


def _ffn_hidden_kernel(hn_ref, hp_ref, hx_ref, wg_ref, wu_ref, dw_ref, db_ref, a_ref, hext_scr, g_scr,
                       *, tm, seq_len):
    first, last, _ = _seq_edges(pl.program_id(1), seq_len // tm)
    hext_scr[0:16, :] = jnp.where(first, jnp.zeros_like(hp_ref[...]), hp_ref[...])
    hext_scr[16:16 + tm, :] = hn_ref[...]
    hext_scr[16 + tm:32 + tm, :] = jnp.where(last, jnp.zeros_like(hx_ref[...]), hx_ref[...])
    for c in range(FFN_HALF // FFN_COL_CHUNK):
        cols = slice(c * FFN_COL_CHUNK, (c + 1) * FFN_COL_CHUNK)
        g = g_scr.at[c % 2]
        g[...] = _dot(hext_scr[...], wg_ref[:, cols])
        up = _dot(hn_ref[...], wu_ref[:, cols])
        gate = (g[pl.ds(15, tm), :] * dw_ref[0:1, cols] + g[pl.ds(16, tm), :] * dw_ref[1:2, cols]
                + g[pl.ds(17, tm), :] * dw_ref[2:3, cols] + db_ref[:, cols])
        a_ref[:, cols] = (_silu(gate) * up).astype(BF16)


def _ffn_down_kernel(a_ref, xn_ref, wd_ref, o_ref):
    o_ref[...] = xn_ref[...] + _dot(a_ref[...], wd_ref[...])


def _ffn(hn, xn, w_gate, dw_w, dw_b, w_up, w_down, seq_len, tm_hidden=512, tm_down=256):
    T = hn.shape[0]
    n_tiles = T // tm_hidden
    r = tm_hidden // 16
    half_cols = lambda rows: pl.BlockSpec((rows, FFN_HALF), lambda h, i: (0, h),
                                          pipeline_mode=pl.Buffered(1))
    act = pl.pallas_call(
        functools.partial(_ffn_hidden_kernel, tm=tm_hidden, seq_len=seq_len),
        grid=(2, n_tiles),
        in_specs=[
            pl.BlockSpec((tm_hidden, D_MODEL), lambda h, i: (i, 0)),
            pl.BlockSpec((16, D_MODEL), lambda h, i: (jnp.maximum(i * r - 1, 0), 0)),
            pl.BlockSpec((16, D_MODEL), lambda h, i: (jnp.minimum((i + 1) * r, n_tiles * r - 1), 0)),
            half_cols(D_MODEL), half_cols(D_MODEL), half_cols(3), half_cols(1),
        ],
        out_specs=pl.BlockSpec((tm_hidden, FFN_HALF), lambda h, i: (i, h)),
        out_shape=jax.ShapeDtypeStruct((T, FFN_HIDDEN), BF16),
        scratch_shapes=[pltpu.VMEM((tm_hidden + 32, D_MODEL), BF16),
                        pltpu.VMEM((2, tm_hidden + 32, FFN_COL_CHUNK), F32)],
        compiler_params=_cparams("arbitrary", "arbitrary"),
        name="ffn_hidden",
    )(hn, hn, hn, w_gate, w_up, dw_w, dw_b)
    return pl.pallas_call(
        _ffn_down_kernel,
        grid=(T // tm_down,),
        in_specs=[
            pl.BlockSpec((tm_down, FFN_HIDDEN), lambda i: (i, 0)),
            pl.BlockSpec((tm_down, D_MODEL), lambda i: (i, 0)),
            pl.BlockSpec((FFN_HIDDEN, D_MODEL), lambda i: (0, 0), pipeline_mode=pl.Buffered(1)),
        ],
        out_specs=pl.BlockSpec((tm_down, D_MODEL), lambda i: (i, 0)),
        out_shape=jax.ShapeDtypeStruct((T, D_MODEL), F32),
        compiler_params=_cparams("parallel"),
        name="ffn_down",
    )(act, xn, w_down)


def _layer(x2, l, batch, seq_len, rel_bias, norm1_g, w_in, w_pool, pool_scale, att_q_g, att_k_g,
           conv_dw_w, conv_dw_b, conv_ln_g, conv_ln_b, conv_pw, dn_conv_w, dn_a_log, dn_dt_bias,
           dn_norm_g, w_out, norm2_g, ffn_w_gate, ffn_dw_w, ffn_dw_b, ffn_w_up, ffn_w_down):
    w_main = _weight_bf16(w_in, l, cols=PROJ_MAIN)
    w_gates = jnp.pad(w_in[l, :, PROJ_MAIN:], ((0, 0), (0, GATE_LANES - 4 * DN_HEADS))).astype(BF16)
    p, pg = _in_proj(x2, norm1_g[l][None], w_main, w_gates)
    ya = _pool_mixer(p, w_pool[l].astype(BF16), pool_scale[l][None], seq_len)
    yb = _attention_mixer(p, att_q_g[l], att_k_g[l], rel_bias, seq_len)
    yc = _conformer_mixer(p, conv_dw_w[l], conv_dw_b[l][None], conv_ln_g[l][None], conv_ln_b[l][None],
                          conv_pw[l].astype(BF16), seq_len)
    yd = _deltanet_mixer(p, pg, dn_conv_w[l], dn_a_log[l], dn_dt_bias[l], dn_norm_g[l], batch, seq_len)
    xn, hn = _out_proj(x2, (ya, yb, yc, yd), _weight_bf16(w_out, l), norm2_g[l][None])
    return _ffn(hn, xn, _weight_bf16(ffn_w_gate, l), ffn_dw_w[l], ffn_dw_b[l][None],
                _weight_bf16(ffn_w_up, l), _weight_bf16(ffn_w_down, l), seq_len)


def kernel(x, rel_bias, norm1_g, w_in, w_pool, pool_scale, att_q_g, att_k_g, conv_dw_w, conv_dw_b,
           conv_ln_g, conv_ln_b, conv_pw, dn_conv_w, dn_a_log, dn_dt_bias, dn_norm_g, w_out, norm2_g,
           ffn_w_gate, ffn_dw_w, ffn_dw_b, ffn_w_up, ffn_w_down):
    batch, seq_len, _ = x.shape
    x2 = x.reshape(batch * seq_len, D_MODEL)
    for l in range(norm1_g.shape[0]):
        x2 = _layer(x2, l, batch, seq_len, rel_bias, norm1_g, w_in, w_pool, pool_scale, att_q_g,
                    att_k_g, conv_dw_w, conv_dw_b, conv_ln_g, conv_ln_b, conv_pw, dn_conv_w, dn_a_log,
                    dn_dt_bias, dn_norm_g, w_out, norm2_g, ffn_w_gate, ffn_dw_w, ffn_dw_b, ffn_w_up,
                    ffn_w_down)
    return x2.reshape(batch, seq_len, D_MODEL)
```

```python
import functools
import math

import jax
import jax.numpy as jnp
import numpy as np
from jax import lax
from jax.experimental import pallas as pl
from jax.experimental.pallas import tpu as pltpu

F32 = jnp.float32
BF16 = jnp.bfloat16

D_MODEL = 2048
GROUP = 512
POOL_WINDOWS = (2, 4, 8, 16)
ATT_HEAD_DIM = 64
ATT_HEADS = 8
ATT_RADIUS = 64
ATT_DILATIONS = (1, 4, 16)
REL_BUCKETS = 32
REL_MAX_DIST = 1024
CONV_WIDTH = 31
DN_HEAD_DIM = 128
DN_HEADS = 4
DN_CHUNK = 64
FFN_HIDDEN = 5632
EPS = 1e-6
NEG_INF = -1e30

PROJ_MAIN = 10 * GROUP
GATE_LANES = 128

VMEM_LIMIT_BYTES = 56 * 1024 * 1024


def _cparams(*sem):
    return pltpu.CompilerParams(dimension_semantics=sem, vmem_limit_bytes=VMEM_LIMIT_BYTES)


def _sigmoid(x):
    return 1.0 / (1.0 + jnp.exp(-x))


def _silu(x):
    return x * _sigmoid(x)


def _dot(a, b):
    return jnp.dot(a, b, preferred_element_type=F32)


def _dot_nt(a, b):
    return lax.dot_general(a, b, (((1,), (1,)), ((), ())), preferred_element_type=F32)


def _dot_tn(a, b):
    return lax.dot_general(a, b, (((0,), (0,)), ((), ())), preferred_element_type=F32)


def _cast_kernel(w_ref, o_ref):
    o_ref[...] = w_ref[...].astype(BF16)


def _weight_bf16(w_stack, layer, cols=None, tr=512):
    _, rows, width = w_stack.shape
    cols = width if cols is None else cols
    tc = cols // 2 if cols >= 4096 else cols
    return pl.pallas_call(
        _cast_kernel,
        grid=(rows // tr, cols // tc),
        in_specs=[pl.BlockSpec((None, tr, tc), lambda i, j: (layer, i, j))],
        out_specs=pl.BlockSpec((tr, tc), lambda i, j: (i, j)),
        out_shape=jax.ShapeDtypeStruct((rows, cols), BF16),
        compiler_params=_cparams("parallel", "parallel"),
        name="weight_bf16",
    )(w_stack)


def _gate_cols_kernel(w_ref, o_ref, *, valid):
    lane = lax.broadcasted_iota(jnp.int32, o_ref.shape, 1)
    o_ref[...] = jnp.where(lane < valid, w_ref[...], 0.0).astype(BF16)


def _gate_weight_bf16(w_stack, layer):
    _, rows, width = w_stack.shape
    valid = width - PROJ_MAIN
    return pl.pallas_call(
        functools.partial(_gate_cols_kernel, valid=valid),
        grid=(1,),
        in_specs=[pl.BlockSpec((None, rows, GATE_LANES), lambda i: (layer, 0, PROJ_MAIN // GATE_LANES))],
        out_specs=pl.BlockSpec((rows, GATE_LANES), lambda i: (0, 0)),
        out_shape=jax.ShapeDtypeStruct((rows, GATE_LANES), BF16),
        compiler_params=_cparams("arbitrary"),
        name="gate_weight_bf16",
    )(w_stack)


def _inproj_kernel(x_ref, g_ref, w_ref, wg_ref, p_ref, pg_ref):
    x = x_ref[...]
    ms = jnp.mean(x * x, axis=-1, keepdims=True)
    h = (x * lax.rsqrt(ms + EPS) * g_ref[...]).astype(BF16)
    pg_ref[...] = _dot(h, wg_ref[...])
    p_ref[...] = _dot(h, w_ref[...])


def _in_proj(x2, norm_g, w_main, w_gate, tm=256):
    T = x2.shape[0]
    resident = lambda shape: pl.BlockSpec(shape, lambda i: (0, 0), pipeline_mode=pl.Buffered(1))
    return pl.pallas_call(
        _inproj_kernel,
        grid=(T // tm,),
        in_specs=[
            pl.BlockSpec((tm, D_MODEL), lambda i: (i, 0)),
            resident((1, D_MODEL)),
            resident((D_MODEL, PROJ_MAIN)),
            resident((D_MODEL, GATE_LANES)),
        ],
        out_specs=[
            pl.BlockSpec((tm, PROJ_MAIN), lambda i: (i, 0)),
            pl.BlockSpec((tm, GATE_LANES), lambda i: (i, 0)),
        ],
        out_shape=[
            jax.ShapeDtypeStruct((T, PROJ_MAIN), F32),
            jax.ShapeDtypeStruct((T, GATE_LANES), F32),
        ],
        compiler_params=_cparams("parallel"),
        name="in_proj",
    )(x2, norm_g, w_main, w_gate)


def _halo_specs(tm, halo, width, col_block, n_tiles):
    r = tm // halo
    last = n_tiles * r - 1
    prev_spec = pl.BlockSpec((halo, width), lambda i, *_: (jnp.maximum(i * r - 1, 0), col_block))
    next_spec = pl.BlockSpec((halo, width), lambda i, *_: (jnp.minimum((i + 1) * r, last), col_block))
    return prev_spec, next_spec


def _seq_edges(i, tiles_per_seq):
    k = lax.rem(i, tiles_per_seq)
    return k == 0, k == tiles_per_seq - 1, k


def _pool_kernel(u_ref, up_ref, un_ref, w_ref, sc_ref, y_ref, ext_scr, *, tm, seq_len):
    i = pl.program_id(0)
    first, last, k = _seq_edges(i, seq_len // tm)
    ext_scr[0:8, :] = jnp.where(first, jnp.zeros_like(up_ref[...]), up_ref[...])
    ext_scr[8:8 + tm, :] = u_ref[...]
    ext_scr[8 + tm:16 + tm, :] = jnp.where(last, jnp.zeros_like(un_ref[...]), un_ref[...])
    t = k * tm + lax.broadcasted_iota(jnp.int32, (tm, 1), 0)
    for gi, win in enumerate(POOL_WINDOWS):
        half = win // 2
        lanes = slice(gi * 128, (gi + 1) * 128)
        s = ext_scr[pl.ds(8 - half, tm), lanes]
        for kk in range(1, win):
            s = s + ext_scr[pl.ds(8 - half + kk, tm), lanes]
        cnt = (jnp.minimum(t + half, seq_len) - jnp.maximum(t - half, 0)).astype(F32)
        pooled = s / cnt - ext_scr[pl.ds(8, tm), lanes]
        y = _dot(pooled.astype(BF16), w_ref[gi]) * sc_ref[:, lanes]
        y_ref[:, lanes] = y.astype(BF16)


def _pool_mixer(p, w_pool, pool_scale, seq_len, tm=1024):
    T = p.shape[0]
    n_tiles = T // tm
    prev_spec, next_spec = _halo_specs(tm, 8, GROUP, 0, n_tiles)
    return pl.pallas_call(
        functools.partial(_pool_kernel, tm=tm, seq_len=seq_len),
        grid=(n_tiles,),
        in_specs=[
            pl.BlockSpec((tm, GROUP), lambda i: (i, 0)),
            prev_spec,
            next_spec,
            pl.BlockSpec((4, 128, 128), lambda i: (0, 0, 0)),
            pl.BlockSpec((1, GROUP), lambda i: (0, 0)),
        ],
        out_specs=pl.BlockSpec((tm, GROUP), lambda i: (i, 0)),
        out_shape=jax.ShapeDtypeStruct((T, GROUP), BF16),
        scratch_shapes=[pltpu.VMEM((tm + 16, GROUP), F32)],
        compiler_params=_cparams("parallel"),
        name="pool_mixer",
    )(p, p, p, w_pool, pool_scale)


def _conformer_kernel(v_ref, vp_ref, vn_ref, g_ref, gp_ref, gn_ref, dw_ref, db_ref, lg_ref, lb_ref,
                      pw_ref, y_ref, ext_scr, shift_scr, *, tm, seq_len):
    i = pl.program_id(0)
    first, last, _ = _seq_edges(i, seq_len // tm)
    hp = vp_ref[...] * _sigmoid(gp_ref[...])
    hn = vn_ref[...] * _sigmoid(gn_ref[...])
    ext_scr[0:16, :] = jnp.where(first, jnp.zeros_like(hp), hp)
    ext_scr[16:16 + tm, :] = v_ref[...] * _sigmoid(g_ref[...])
    ext_scr[16 + tm:32 + tm, :] = jnp.where(last, jnp.zeros_like(hn), hn)
    base = 16 - CONV_WIDTH // 2
    acc = db_ref[...]
    for b in range(8):
        taps = [(a, 8 * a + b - base) for a in range(5) if 0 <= 8 * a + b - base < CONV_WIDTH]
        rows = tm + 8 * taps[-1][0]
        shift_scr[b, 0:rows, :] = ext_scr[pl.ds(b, rows), :]
        for a, kk in taps:
            acc = acc + shift_scr[b, 8 * a:8 * a + tm, :] * dw_ref[kk:kk + 1, :]
    mu = jnp.mean(acc, axis=-1, keepdims=True)
    xc = acc - mu
    var = jnp.mean(xc * xc, axis=-1, keepdims=True)
    h = _silu(xc * lax.rsqrt(var + EPS) * lg_ref[...] + lb_ref[...])
    y_ref[...] = _dot(h.astype(BF16), pw_ref[...]).astype(BF16)


def _conformer_mixer(p, dw_w, dw_b, ln_g, ln_b, pw, seq_len, tm=512):
    T = p.shape[0]
    n_tiles = T // tm
    vprev, vnext = _halo_specs(tm, 16, GROUP, 4, n_tiles)
    gprev, gnext = _halo_specs(tm, 16, GROUP, 5, n_tiles)
    const = lambda shape: pl.BlockSpec(shape, lambda i: (0,) * len(shape))
    return pl.pallas_call(
        functools.partial(_conformer_kernel, tm=tm, seq_len=seq_len),
        grid=(n_tiles,),
        in_specs=[
            pl.BlockSpec((tm, GROUP), lambda i: (i, 4)), vprev, vnext,
            pl.BlockSpec((tm, GROUP), lambda i: (i, 5)), gprev, gnext,
            const((CONV_WIDTH, GROUP)), const((1, GROUP)), const((1, GROUP)), const((1, GROUP)),
            const((GROUP, GROUP)),
        ],
        out_specs=pl.BlockSpec((tm, GROUP), lambda i: (i, 0)),
        out_shape=jax.ShapeDtypeStruct((T, GROUP), BF16),
        scratch_shapes=[pltpu.VMEM((tm + 32, GROUP), F32), pltpu.VMEM((8, tm + 32, GROUP), F32)],
        compiler_params=_cparams("parallel"),
        name="conformer_mixer",
    )(p, p, p, p, p, p, dw_w, dw_b, ln_g, ln_b, pw)


ATT_TILE = 1024
ATT_BLOCK_GROUP = 8


def _t5_bucket_table():
    nb = REL_BUCKETS // 2
    max_exact = nb // 2
    i = np.arange(ATT_RADIUS)[:, None]
    j = np.arange(3 * ATT_RADIUS)[None, :]
    off = j - ATT_RADIUS - i
    tables = []
    for dil in ATT_DILATIONS:
        rel = off * dil
        n = np.abs(rel)
        nf = np.maximum(n, 1).astype(np.float32)
        large = max_exact + (np.log(nf / np.float32(max_exact)) / np.float32(math.log(REL_MAX_DIST / max_exact))
                             * np.float32(nb - max_exact)).astype(np.int32)
        large = np.minimum(large, nb - 1)
        bucket = np.where(rel > 0, nb, 0) + np.where(n < max_exact, n, large)
        tables.append(np.where(np.abs(off) <= ATT_RADIUS, bucket, -1))
    return np.stack(tables).astype(np.int32)


def _att_kernel(rb_ref, bkt_ref, qg_ref, kg_ref, mseg_ref, q_ref, kc_ref, kn_ref,
                vp_ref, vc_ref, vn_ref, y_ref, bias_scr, qbuf, kbuf, vbuf, acc_scr, m_scr, l_scr,
                *, seq_len):
    tile = ATT_TILE
    rad = ATT_RADIUS
    hp = pl.program_id(0)
    i = pl.program_id(1)

    def rms(x, g):
        ms = _dot((x * x).astype(BF16), mseg_ref[...])
        return x * lax.rsqrt(ms + EPS) * g

    @pl.when(i == 0)
    def _():
        for di in range(len(ATT_DILATIONS)):
            bkt = bkt_ref[di]
            for hh in range(2):
                b = jnp.zeros(bkt.shape, F32)
                for bb in range(REL_BUCKETS):
                    b = jnp.where(bkt == bb, rb_ref[bb, 2 * hp + hh], b)
                bias_scr[di, hh * rad:(hh + 1) * rad, :] = jnp.where(bkt < 0, NEG_INF, b)
        kbuf[0:tile, :] = jnp.zeros((tile, 128), F32)
        kbuf[tile:2 * tile, :] = rms(kc_ref[...], kg_ref[...])

    @pl.when(i > 0)
    def _():
        kbuf[0:tile, :] = kbuf[tile:2 * tile, :]
        kbuf[tile:2 * tile, :] = kbuf[2 * tile:3 * tile, :]

    first, last, _ = _seq_edges(i, seq_len // tile)

    qbuf[...] = rms(q_ref[...], qg_ref[...]) * (ATT_HEAD_DIM ** -0.5)
    kbuf[2 * tile:3 * tile, :] = rms(kn_ref[...], kg_ref[...])
    vbuf[0:tile, :] = vp_ref[...]
    vbuf[tile:2 * tile, :] = vc_ref[...]
    vbuf[2 * tile:3 * tile, :] = vn_ref[...]

    head0 = lax.broadcasted_iota(jnp.int32, (1, 128), 1) < ATT_HEAD_DIM
    col = lax.broadcasted_iota(jnp.int32, (1, 3 * rad), 1)

    def rows(start, size, dil):
        return pl.ds(start, size) if dil == 1 else pl.ds(start, size, stride=dil)

    blocks = [(di, dil, r + rad * dil * m)
              for di, dil in enumerate(ATT_DILATIONS) for r in range(dil) for m in range(tile // (rad * dil))]
    for g0 in range(0, len(blocks), ATT_BLOCK_GROUP):
        group = blocks[g0:g0 + ATT_BLOCK_GROUP]
        scores = []
        for di, dil, qstart in group:
            kstart = tile + qstart - rad * dil
            qb = qbuf[rows(qstart, rad, dil), :]
            kb = kbuf[rows(kstart, 3 * rad, dil), :].astype(BF16)
            q2 = jnp.concatenate([jnp.where(head0, qb, 0.0), jnp.where(head0, 0.0, qb)], axis=0)
            scores.append(_dot_nt(q2.astype(BF16), kb))
        probs = []
        for (di, dil, qstart), s in zip(group, scores):
            kstart = tile + qstart - rad * dil
            s = s + bias_scr[di]
            n_prev = max(0, -(-(tile - kstart) // dil))
            n_upto = min(3 * rad, -(-(2 * tile - kstart) // dil))
            if n_prev > 0:
                s = jnp.where(col < jnp.where(first, n_prev, 0), NEG_INF, s)
            if n_upto < 3 * rad:
                s = jnp.where(col >= jnp.where(last, n_upto, 3 * rad), NEG_INF, s)
            mx = jnp.max(s, axis=-1, keepdims=True)
            pe = jnp.exp(s - mx)
            probs.append((mx, jnp.sum(pe, axis=-1, keepdims=True), pe.astype(BF16)))
        outs = []
        for (di, dil, qstart), (mx, ls, pe) in zip(group, probs):
            kstart = tile + qstart - rad * dil
            outs.append(_dot(pe, vbuf[rows(kstart, 3 * rad, dil), :].astype(BF16)))
        for (di, dil, qstart), (mx, ls, pe), o in zip(group, probs, outs):
            dst = rows(qstart, rad, dil)
            acc_scr[di, dst, :] = jnp.where(head0, o[0:rad], o[rad:2 * rad])
            m_scr[di, dst, :] = jnp.where(head0, mx[0:rad], mx[rad:2 * rad])
            l_scr[di, dst, :] = jnp.where(head0, ls[0:rad], ls[rad:2 * rad])

    m_all = jnp.maximum(jnp.maximum(m_scr[0], m_scr[1]), m_scr[2])
    num = jnp.zeros((tile, 128), F32)
    den = jnp.zeros((tile, 128), F32)
    for di in range(len(ATT_DILATIONS)):
        e = jnp.exp(m_scr[di] - m_all)
        num = num + acc_scr[di] * e
        den = den + l_scr[di] * e
    y_ref[...] = (num / den).astype(BF16)


def _attention_mixer(p, q_g, k_g, rel_bias, seq_len):
    T = p.shape[0]
    tile = ATT_TILE
    n_tiles = T // tile
    rad = ATT_RADIUS
    seg = np.kron(np.eye(2), np.full((ATT_HEAD_DIM, ATT_HEAD_DIM), 1.0 / ATT_HEAD_DIM))
    mseg = jnp.asarray(seg, BF16)
    bkt = jnp.asarray(_t5_bucket_table())
    qg2 = jnp.tile(q_g, 2)[None]
    kg2 = jnp.tile(k_g, 2)[None]

    def blk(col0, shift):
        return pl.BlockSpec((tile, 128),
                            lambda hp, i: (jnp.clip(i + shift, 0, n_tiles - 1), col0 * 4 + hp))

    const = lambda shape: pl.BlockSpec(shape, lambda hp, i: (0,) * len(shape))
    return pl.pallas_call(
        functools.partial(_att_kernel, seq_len=seq_len),
        grid=(4, n_tiles),
        in_specs=[
            pl.BlockSpec(memory_space=pltpu.SMEM),
            const((3, rad, 3 * rad)), const((1, 128)), const((1, 128)), const((128, 128)),
            blk(1, 0),
            blk(2, 0), blk(2, 1),
            blk(3, -1), blk(3, 0), blk(3, 1),
        ],
        out_specs=pl.BlockSpec((tile, 128), lambda hp, i: (i, hp)),
        out_shape=jax.ShapeDtypeStruct((T, GROUP), BF16),
        scratch_shapes=[
            pltpu.VMEM((3, 2 * rad, 3 * rad), F32),
            pltpu.VMEM((tile, 128), F32),
            pltpu.VMEM((3 * tile, 128), F32),
            pltpu.VMEM((3 * tile, 128), F32),
            pltpu.VMEM((3, tile, 128), F32),
            pltpu.VMEM((3, tile, 128), F32),
            pltpu.VMEM((3, tile, 128), F32),
        ],
        compiler_params=_cparams("arbitrary", "arbitrary"),
        name="dilated_attention",
    )(rel_bias, bkt, qg2, kg2, mseg, p, p, p, p, p, p)


def _softplus(x):
    return jnp.maximum(x, 0.0) + jnp.log1p(jnp.exp(-jnp.abs(x)))


def _dn_prep_kernel(x_ref, xp_ref, xn_ref, w_ref, o_ref, ext_scr, *, tm, seq_len):
    i = pl.program_id(0)
    first, last, _ = _seq_edges(i, seq_len // tm)
    ext_scr[0:8, :] = jnp.where(first, jnp.zeros_like(xp_ref[...]), xp_ref[...])
    ext_scr[8:8 + tm, :] = x_ref[...]
    ext_scr[8 + tm:16 + tm, :] = jnp.where(last, jnp.zeros_like(xn_ref[...]), xn_ref[...])
    for cb in range(3 * DN_HEADS):
        lanes = slice(cb * 128, (cb + 1) * 128)
        acc = ext_scr[pl.ds(6, tm), lanes] * w_ref[0:1, lanes]
        for kk in range(1, 4):
            acc = acc + ext_scr[pl.ds(6 + kk, tm), lanes] * w_ref[kk:kk + 1, lanes]
        y = _silu(acc)
        if cb < 2 * DN_HEADS:
            y = y * lax.rsqrt(jnp.sum(y * y, axis=-1, keepdims=True) + EPS)
        if cb < DN_HEADS:
            y = y * (DN_HEAD_DIM ** -0.5)
        o_ref[:, lanes] = y


def _dn_prep(p, conv_w, seq_len, tm=512):
    T = p.shape[0]
    n_tiles = T // tm
    width = 3 * GROUP
    prev_spec, next_spec = _halo_specs(tm, 8, width, 2, n_tiles)
    return pl.pallas_call(
        functools.partial(_dn_prep_kernel, tm=tm, seq_len=seq_len),
        grid=(n_tiles,),
        in_specs=[pl.BlockSpec((tm, width), lambda i: (i, 2)), prev_spec, next_spec,
                  pl.BlockSpec((4, width), lambda i: (0, 0))],
        out_specs=pl.BlockSpec((tm, width), lambda i: (i, 0)),
        out_shape=jax.ShapeDtypeStruct((T, width), F32),
        scratch_shapes=[pltpu.VMEM((tm + 16, width), F32)],
        compiler_params=_cparams("parallel"),
        name="deltanet_prep",
    )(p, p, p, conv_w)


DN_GROUP = 8
DN_ROWS = DN_HEADS * DN_CHUNK


def _dn_scan_kernel(xf_ref, xb_ref, gf_ref, gb_ref, rf_ref, rb_ref, alc_ref, dtc_ref, alr_ref, dtr_ref,
                    of_ref, ob_ref, s_scr):
    C = DN_CHUNK
    H = DN_HEADS
    R = DN_GROUP * C

    @pl.when(pl.program_id(1) == 0)
    def _():
        s_scr[...] = jnp.zeros_like(s_scr)

    row_in_chunk = lax.rem(lax.broadcasted_iota(jnp.int32, (R, 1), 0), C)
    lane_in_chunk = lax.rem(lax.broadcasted_iota(jnp.int32, (1, DN_ROWS), 1), C)
    rid = lax.broadcasted_iota(jnp.int32, (DN_ROWS, DN_ROWS), 0)
    cid = lax.broadcasted_iota(jnp.int32, (DN_ROWS, DN_ROWS), 1)
    same_head = (rid // C) == (cid // C)
    eye = (rid == cid).astype(F32)

    def stack_heads(x, col0):
        return jnp.concatenate([x[:, col0 + h * 128:col0 + (h + 1) * 128] for h in range(H)], axis=0)

    def stack_cols(x, lane0, rows=None):
        parts = []
        for h in range(H):
            c = x[:, lane0 + h:lane0 + h + 1]
            if rows is not None:
                c = c[rows:rows + 1, :]
            parts.append(jnp.broadcast_to(c, (C, 128)))
        return jnp.concatenate(parts, axis=0)

    refs = ((xf_ref, gf_ref, rf_ref, of_ref), (xb_ref, gb_ref, rb_ref, ob_ref))
    gates = []
    for d, (x_ref, gc_ref, gr_ref, o_ref) in enumerate(refs):
        reverse = d == 1
        gcol = gc_ref[...]
        g = -jnp.exp(alc_ref[...]) * _softplus(gcol + dtc_ref[...])
        beta = _sigmoid(gcol)
        gcum = g
        for sh in (1, 2, 4, 8, 16, 32):
            if reverse:
                gcum = gcum + jnp.where(row_in_chunk < C - sh, pltpu.roll(gcum, R - sh, 0), 0.0)
            else:
                gcum = gcum + jnp.where(row_in_chunk >= sh, pltpu.roll(gcum, sh, 0), 0.0)
        grow_all = -jnp.exp(alr_ref[...])[None] * _softplus(gr_ref[...] + dtr_ref[...][None])
        for sh in (1, 2, 4, 8, 16, 32):
            if reverse:
                grow_all = grow_all + jnp.where(lane_in_chunk < C - sh,
                                                pltpu.roll(grow_all, DN_ROWS - sh, 2), 0.0)
            else:
                grow_all = grow_all + jnp.where(lane_in_chunk >= sh, pltpu.roll(grow_all, sh, 2), 0.0)

        gates.append((gcum, beta, grow_all))

    units = []
    for step in range(DN_GROUP):
        for d, (x_ref, _, _, o_ref) in enumerate(refs):
            reverse = d == 1
            ci = DN_GROUP - 1 - step if reverse else step
            gcum, beta, grow_all = gates[d]
            rows = slice(ci * C, (ci + 1) * C)
            x = x_ref[rows, :]
            u = dict(d=d, rows=rows, o_ref=o_ref)
            u["q"] = stack_heads(x, 0)
            u["k"] = stack_heads(x, GROUP)
            v_st = stack_heads(x, 2 * GROUP)
            gc_c = gcum[rows, :]
            beta_st = stack_cols(beta[rows, :], 8 + d * H)
            gcol_st = stack_cols(gc_c, d * H)
            glast_st = stack_cols(gc_c, d * H, rows=0 if reverse else C - 1)
            grow = grow_all[ci, d:d + 1, :]
            tri = (rid <= cid) if reverse else (rid >= cid)
            u["incl"] = same_head & tri
            u["decay"] = jnp.where(u["incl"],
                                   jnp.exp(jnp.concatenate([gcol_st, gcol_st], axis=1) - grow), 0.0)
            kb_st = u["k"] * beta_st
            eg = jnp.exp(gcol_st)
            u["kq"] = jnp.concatenate([kb_st, u["q"]], axis=0).astype(BF16)
            u["rhs"] = jnp.concatenate([v_st * beta_st, kb_st * eg], axis=1).astype(BF16)
            u["qdec"] = u["q"] * eg
            u["kdec"] = (u["k"] * jnp.exp(glast_st - gcol_st)).astype(BF16)
            u["gl"] = jnp.exp(glast_st)
            units.append(u)

    for u in units:
        u["kk"] = _dot_nt(u["kq"], u["k"].astype(BF16))
    for u in units:
        strict = u["incl"] & (rid != cid)
        a = jnp.where(strict, u["kk"][0:DN_ROWS] * u["decay"], 0.0)
        u["attn"] = (u["kk"][DN_ROWS:2 * DN_ROWS] * u["decay"]).astype(BF16)
        u["pinv"] = eye - a
        u["a"] = a.astype(BF16)
    for u in units:
        u["apow"] = _dot(u["a"], u["a"])
    for _ in range(4):
        for u in units:
            ap = u["apow"].astype(BF16)
            u["both"] = _dot(jnp.concatenate([u["pinv"].astype(BF16), ap], axis=0), ap)
        for u in units:
            u["pinv"] = u["pinv"] + u["both"][0:DN_ROWS]
            u["apow"] = u["both"][DN_ROWS:2 * DN_ROWS]
    for u in units:
        u["last"] = _dot(u["pinv"].astype(BF16), u["apow"].astype(BF16))
    for u in units:
        u["uw"] = _dot((u["pinv"] + u["last"]).astype(BF16), u["rhs"])

    for step in range(DN_GROUP):
        pair = units[2 * step:2 * step + 2]
        for u in pair:
            d = u["d"]
            u["wq"] = []
            for h in range(H):
                hr = slice(h * C, (h + 1) * C)
                lhs = jnp.concatenate([u["uw"][hr, 128:256], u["qdec"][hr]], axis=0).astype(BF16)
                u["wq"].append(_dot(lhs, s_scr[d * H + h].astype(BF16)))
        for u in pair:
            u["vnew"] = [u["uw"][h * C:(h + 1) * C, 0:128] - u["wq"][h][0:C] for h in range(H)]
            vnew_st = jnp.concatenate(u["vnew"], axis=0).astype(BF16)
            u["o"] = jnp.concatenate([w[C:2 * C] for w in u["wq"]], axis=0) + _dot(u["attn"], vnew_st)
        for u in pair:
            d = u["d"]
            for h in range(H):
                hr = slice(h * C, (h + 1) * C)
                s_scr[d * H + h] = (s_scr[d * H + h] * u["gl"][h * C:h * C + 1, :]
                                    + _dot_tn(u["kdec"][hr], u["vnew"][h].astype(BF16)))
                u["o_ref"][u["rows"], h * 128:(h + 1) * 128] = u["o"][hr]


def _dn_scan(qkvn, pg, a_log, dt_bias, batch, seq_len):
    T = qkvn.shape[0]
    C, H, G = DN_CHUNK, DN_HEADS, DN_GROUP
    R = G * C
    nc = seq_len // C
    ncg = nc // G
    ab = pg[:, 0:4 * H].reshape(batch, nc, C, 2, 2, H)
    ab_row = jnp.transpose(ab, (0, 1, 3, 4, 5, 2)).reshape(batch * nc, 4, H * C)
    ab_row = jnp.pad(ab_row, ((0, 0), (0, 4), (0, 0)))
    pad_lanes = lambda v: jnp.pad(v.reshape(1, 2 * H), ((0, 0), (0, GATE_LANES - 2 * H)))
    row_param = lambda v: jnp.pad(jnp.repeat(v, C, axis=1), ((0, 6), (0, 0)))

    fwd = lambda b, c: (b * ncg + c, 0)
    bwd = lambda b, c: (b * ncg + ncg - 1 - c, 0)
    fwd3 = lambda b, c: (b * ncg + c, 0, 0)
    bwd3 = lambda b, c: (b * ncg + ncg - 1 - c, 0, 0)
    const = lambda shape: pl.BlockSpec(shape, lambda b, c: (0,) * len(shape))
    return pl.pallas_call(
        _dn_scan_kernel,
        grid=(batch, ncg),
        in_specs=[
            pl.BlockSpec((R, 3 * GROUP), fwd), pl.BlockSpec((R, 3 * GROUP), bwd),
            pl.BlockSpec((R, GATE_LANES), fwd), pl.BlockSpec((R, GATE_LANES), bwd),
            pl.BlockSpec((G, 8, H * C), fwd3), pl.BlockSpec((G, 8, H * C), bwd3),
            const((1, GATE_LANES)), const((1, GATE_LANES)), const((8, H * C)), const((8, H * C)),
        ],
        out_specs=[pl.BlockSpec((R, GROUP), fwd), pl.BlockSpec((R, GROUP), bwd)],
        out_shape=[jax.ShapeDtypeStruct((T, GROUP), F32), jax.ShapeDtypeStruct((T, GROUP), F32)],
        scratch_shapes=[pltpu.VMEM((2 * H, DN_HEAD_DIM, DN_HEAD_DIM), F32)],
        compiler_params=_cparams("arbitrary", "arbitrary"),
        name="deltanet_scan",
    )(qkvn, qkvn, pg, pg, ab_row, ab_row, pad_lanes(a_log), pad_lanes(dt_bias),
      row_param(a_log), row_param(dt_bias))


def _dn_post_kernel(of_ref, ob_ref, z_ref, g_ref, y_ref):
    for h in range(DN_HEADS):
        lanes = slice(h * 128, (h + 1) * 128)
        o = of_ref[:, lanes] + ob_ref[:, lanes]
        o = o * lax.rsqrt(jnp.mean(o * o, axis=-1, keepdims=True) + EPS) * g_ref[...]
        y_ref[:, lanes] = (o * _silu(z_ref[:, lanes])).astype(BF16)


def _dn_post(o_f, o_b, p, norm_g, tm=1024):
    T = p.shape[0]
    row = lambda cb: pl.BlockSpec((tm, GROUP), lambda i: (i, cb))
    return pl.pallas_call(
        _dn_post_kernel,
        grid=(T // tm,),
        in_specs=[row(0), row(0), row(9), pl.BlockSpec((1, DN_HEAD_DIM), lambda i: (0, 0))],
        out_specs=row(0),
        out_shape=jax.ShapeDtypeStruct((T, GROUP), BF16),
        compiler_params=_cparams("parallel"),
        name="deltanet_post",
    )(o_f, o_b, p, norm_g)


def _deltanet_mixer(p, pg, conv_w, a_log, dt_bias, norm_g, batch, seq_len):
    qkvn = _dn_prep(p, conv_w, seq_len)
    o_f, o_b = _dn_scan(qkvn, pg, a_log, dt_bias, batch, seq_len)
    return _dn_post(o_f, o_b, p, norm_g[None])


def _outproj_kernel(x_ref, ya_ref, yb_ref, yc_ref, yd_ref, w_ref, g_ref, xn_ref, hn_ref, ycat_scr):
    for gi, y_ref in enumerate((ya_ref, yb_ref, yc_ref, yd_ref)):
        ycat_scr[:, gi * GROUP:(gi + 1) * GROUP] = y_ref[...]
    acc = x_ref[...] + _dot(ycat_scr[...], w_ref[...])
    xn_ref[...] = acc
    ms = jnp.mean(acc * acc, axis=-1, keepdims=True)
    hn_ref[...] = (acc * lax.rsqrt(ms + EPS) * g_ref[...]).astype(BF16)


def _out_proj(x2, ys, w_out, norm2_g, tm=512):
    T = x2.shape[0]
    row = lambda width: pl.BlockSpec((tm, width), lambda i: (i, 0))
    return pl.pallas_call(
        _outproj_kernel,
        grid=(T // tm,),
        in_specs=[row(D_MODEL), row(GROUP), row(GROUP), row(GROUP), row(GROUP),
                  pl.BlockSpec((D_MODEL, D_MODEL), lambda i: (0, 0), pipeline_mode=pl.Buffered(1)),
                  pl.BlockSpec((1, D_MODEL), lambda i: (0, 0))],
        out_specs=[row(D_MODEL), row(D_MODEL)],
        out_shape=[jax.ShapeDtypeStruct((T, D_MODEL), F32),
                   jax.ShapeDtypeStruct((T, D_MODEL), BF16)],
        scratch_shapes=[pltpu.VMEM((tm, D_MODEL), BF16)],
        compiler_params=_cparams("parallel"),
        name="out_proj",
    )(x2, *ys, w_out, norm2_g)


FFN_HALF = FFN_HIDDEN // 2
FFN_COL_CHUNK = 256


def _ffn_hidden_kernel(hn_ref, hp_ref, hx_ref, wg_ref, wu_ref, dw_ref, db_ref, a_ref, hext_scr, g_scr,
                       *, tm, seq_len):
    first, last, _ = _seq_edges(pl.program_id(1), seq_len // tm)
    hext_scr[0:16, :] = jnp.where(first, jnp.zeros_like(hp_ref[...]), hp_ref[...])
    hext_scr[16:16 + tm, :] = hn_ref[...]
    hext_scr[16 + tm:32 + tm, :] = jnp.where(last, jnp.zeros_like(hx_ref[...]), hx_ref[...])
    for c in range(FFN_HALF // FFN_COL_CHUNK):
        cols = slice(c * FFN_COL_CHUNK, (c + 1) * FFN_COL_CHUNK)
        g = g_scr.at[c % 2]
        g[...] = _dot(hext_scr[...], wg_ref[:, cols])
        up = _dot(hn_ref[...], wu_ref[:, cols])
        gate = (g[pl.ds(15, tm), :] * dw_ref[0:1, cols] + g[pl.ds(16, tm), :] * dw_ref[1:2, cols]
                + g[pl.ds(17, tm), :] * dw_ref[2:3, cols] + db_ref[:, cols])
        a_ref[:, cols] = (_silu(gate) * up).astype(BF16)


def _ffn_down_kernel(a_ref, xn_ref, wd_ref, o_ref):
    o_ref[...] = xn_ref[...] + _dot(a_ref[...], wd_ref[...])


def _ffn(hn, xn, w_gate, dw_w, dw_b, w_up, w_down, seq_len, tm_hidden=512, tm_down=256):
    T = hn.shape[0]
    n_tiles = T // tm_hidden
    r = tm_hidden // 16
    half_cols = lambda rows: pl.BlockSpec((rows, FFN_HALF), lambda h, i: (0, h),
                                          pipeline_mode=pl.Buffered(1))
    act = pl.pallas_call(
        functools.partial(_ffn_hidden_kernel, tm=tm_hidden, seq_len=seq_len),
        grid=(2, n_tiles),
        in_specs=[
            pl.BlockSpec((tm_hidden, D_MODEL), lambda h, i: (i, 0)),
            pl.BlockSpec((16, D_MODEL), lambda h, i: (jnp.maximum(i * r - 1, 0), 0)),
            pl.BlockSpec((16, D_MODEL), lambda h, i: (jnp.minimum((i + 1) * r, n_tiles * r - 1), 0)),
            half_cols(D_MODEL), half_cols(D_MODEL), half_cols(3), half_cols(1),
        ],
        out_specs=pl.BlockSpec((tm_hidden, FFN_HALF), lambda h, i: (i, h)),
        out_shape=jax.ShapeDtypeStruct((T, FFN_HIDDEN), BF16),
        scratch_shapes=[pltpu.VMEM((tm_hidden + 32, D_MODEL), BF16),
                        pltpu.VMEM((2, tm_hidden + 32, FFN_COL_CHUNK), F32)],
        compiler_params=_cparams("arbitrary", "arbitrary"),
        name="ffn_hidden",
    )(hn, hn, hn, w_gate, w_up, dw_w, dw_b)
    return pl.pallas_call(
        _ffn_down_kernel,
        grid=(T // tm_down,),
        in_specs=[
            pl.BlockSpec((tm_down, FFN_HIDDEN), lambda i: (i, 0)),
            pl.BlockSpec((tm_down, D_MODEL), lambda i: (i, 0)),
            pl.BlockSpec((FFN_HIDDEN, D_MODEL), lambda i: (0, 0), pipeline_mode=pl.Buffered(1)),
        ],
        out_specs=pl.BlockSpec((tm_down, D_MODEL), lambda i: (i, 0)),
        out_shape=jax.ShapeDtypeStruct((T, D_MODEL), F32),
        compiler_params=_cparams("parallel"),
        name="ffn_down",
    )(act, xn, w_down)


def _layer(x2, l, batch, seq_len, rel_bias, norm1_g, w_in, w_pool, pool_scale, att_q_g, att_k_g,
           conv_dw_w, conv_dw_b, conv_ln_g, conv_ln_b, conv_pw, dn_conv_w, dn_a_log, dn_dt_bias,
           dn_norm_g, w_out, norm2_g, ffn_w_gate, ffn_dw_w, ffn_dw_b, ffn_w_up, ffn_w_down):
    w_main = _weight_bf16(w_in, l, cols=PROJ_MAIN)
    w_gates = _gate_weight_bf16(w_in, l)
    p, pg = _in_proj(x2, norm1_g[l][None], w_main, w_gates)
    ya = _pool_mixer(p, w_pool[l].astype(BF16), pool_scale[l][None], seq_len)
    yb = _attention_mixer(p, att_q_g[l], att_k_g[l], rel_bias, seq_len)
    yc = _conformer_mixer(p, conv_dw_w[l], conv_dw_b[l][None], conv_ln_g[l][None], conv_ln_b[l][None],
                          conv_pw[l].astype(BF16), seq_len)
    yd = _deltanet_mixer(p, pg, dn_conv_w[l], dn_a_log[l], dn_dt_bias[l], dn_norm_g[l], batch, seq_len)
    xn, hn = _out_proj(x2, (ya, yb, yc, yd), _weight_bf16(w_out, l), norm2_g[l][None])
    return _ffn(hn, xn, _weight_bf16(ffn_w_gate, l), ffn_dw_w[l], ffn_dw_b[l][None],
                _weight_bf16(ffn_w_up, l), _weight_bf16(ffn_w_down, l), seq_len)


def kernel(x, rel_bias, norm1_g, w_in, w_pool, pool_scale, att_q_g, att_k_g, conv_dw_w, conv_dw_b,
           conv_ln_g, conv_ln_b, conv_pw, dn_conv_w, dn_a_log, dn_dt_bias, dn_norm_g, w_out, norm2_g,
           ffn_w_gate, ffn_dw_w, ffn_dw_b, ffn_w_up, ffn_w_down):
    batch, seq_len, _ = x.shape
    x2 = x.reshape(batch * seq_len, D_MODEL)
    for l in range(norm1_g.shape[0]):
        x2 = _layer(x2, l, batch, seq_len, rel_bias, norm1_g, w_in, w_pool, pool_scale, att_q_g,
                    att_k_g, conv_dw_w, conv_dw_b, conv_ln_g, conv_ln_b, conv_pw, dn_conv_w, dn_a_log,
                    dn_dt_bias, dn_norm_g, w_out, norm2_g, ffn_w_gate, ffn_dw_w, ffn_dw_b, ffn_w_up,
                    ffn_w_down)
    return x2.reshape(batch, seq_len, D_MODEL)
```

```python
import functools
import math

import jax
import jax.numpy as jnp
import numpy as np
from jax import lax
from jax.experimental import pallas as pl
from jax.experimental.pallas import tpu as pltpu

F32 = jnp.float32
BF16 = jnp.bfloat16

D_MODEL = 2048
GROUP = 512
POOL_WINDOWS = (2, 4, 8, 16)
ATT_HEAD_DIM = 64
ATT_HEADS = 8
ATT_RADIUS = 64
ATT_DILATIONS = (1, 4, 16)
REL_BUCKETS = 32
REL_MAX_DIST = 1024
CONV_WIDTH = 31
DN_HEAD_DIM = 128
DN_HEADS = 4
DN_CHUNK = 64
FFN_HIDDEN = 5632
EPS = 1e-6
NEG_INF = -1e30

PROJ_MAIN = 10 * GROUP
GATE_LANES = 128

VMEM_LIMIT_BYTES = 56 * 1024 * 1024


def _cparams(*sem):
    return pltpu.CompilerParams(dimension_semantics=sem, vmem_limit_bytes=VMEM_LIMIT_BYTES)


def _sigmoid(x):
    return 1.0 / (1.0 + jnp.exp(-x))


def _silu(x):
    return x * _sigmoid(x)


def _dot(a, b):
    return jnp.dot(a, b, preferred_element_type=F32)


def _dot_nt(a, b):
    return lax.dot_general(a, b, (((1,), (1,)), ((), ())), preferred_element_type=F32)


def _dot_tn(a, b):
    return lax.dot_general(a, b, (((0,), (0,)), ((), ())), preferred_element_type=F32)


def _cast_kernel(w_ref, o_ref):
    o_ref[...] = w_ref[...].astype(BF16)


def _weight_bf16(w_stack, layer, cols=None, tr=512):
    _, rows, width = w_stack.shape
    cols = width if cols is None else cols
    tc = cols // 2 if cols >= 4096 else cols
    return pl.pallas_call(
        _cast_kernel,
        grid=(rows // tr, cols // tc),
        in_specs=[pl.BlockSpec((None, tr, tc), lambda i, j: (layer, i, j))],
        out_specs=pl.BlockSpec((tr, tc), lambda i, j: (i, j)),
        out_shape=jax.ShapeDtypeStruct((rows, cols), BF16),
        compiler_params=_cparams("parallel", "parallel"),
        name="weight_bf16",
    )(w_stack)


def _gate_cols_kernel(w_ref, o_ref, *, valid):
    lane = lax.broadcasted_iota(jnp.int32, o_ref.shape, 1)
    o_ref[...] = jnp.where(lane < valid, w_ref[...], 0.0).astype(BF16)


def _gate_weight_bf16(w_stack, layer):
    _, rows, width = w_stack.shape
    valid = width - PROJ_MAIN
    return pl.pallas_call(
        functools.partial(_gate_cols_kernel, valid=valid),
        grid=(1,),
        in_specs=[pl.BlockSpec((None, rows, GATE_LANES), lambda i: (layer, 0, PROJ_MAIN // GATE_LANES))],
        out_specs=pl.BlockSpec((rows, GATE_LANES), lambda i: (0, 0)),
        out_shape=jax.ShapeDtypeStruct((rows, GATE_LANES), BF16),
        compiler_params=_cparams("arbitrary"),
        name="gate_weight_bf16",
    )(w_stack)


def _inproj_kernel(x_ref, g_ref, w_ref, wg_ref, p_ref, pg_ref):
    x = x_ref[...]
    ms = jnp.mean(x * x, axis=-1, keepdims=True)
    h = (x * lax.rsqrt(ms + EPS) * g_ref[...]).astype(BF16)
    pg_ref[...] = _dot(h, wg_ref[...])
    p_ref[...] = _dot(h, w_ref[...])


def _in_proj(x2, norm_g, w_main, w_gate, tm=256):
    T = x2.shape[0]
    resident = lambda shape: pl.BlockSpec(shape, lambda i: (0, 0), pipeline_mode=pl.Buffered(1))
    return pl.pallas_call(
        _inproj_kernel,
        grid=(T // tm,),
        in_specs=[
            pl.BlockSpec((tm, D_MODEL), lambda i: (i, 0)),
            resident((1, D_MODEL)),
            resident((D_MODEL, PROJ_MAIN)),
            resident((D_MODEL, GATE_LANES)),
        ],
        out_specs=[
            pl.BlockSpec((tm, PROJ_MAIN), lambda i: (i, 0)),
            pl.BlockSpec((tm, GATE_LANES), lambda i: (i, 0)),
        ],
        out_shape=[
            jax.ShapeDtypeStruct((T, PROJ_MAIN), F32),
            jax.ShapeDtypeStruct((T, GATE_LANES), F32),
        ],
        compiler_params=_cparams("parallel"),
        name="in_proj",
    )(x2, norm_g, w_main, w_gate)


def _halo_specs(tm, halo, width, col_block, n_tiles):
    r = tm // halo
    last = n_tiles * r - 1
    prev_spec = pl.BlockSpec((halo, width), lambda i, *_: (jnp.maximum(i * r - 1, 0), col_block))
    next_spec = pl.BlockSpec((halo, width), lambda i, *_: (jnp.minimum((i + 1) * r, last), col_block))
    return prev_spec, next_spec


def _seq_edges(i, tiles_per_seq):
    k = lax.rem(i, tiles_per_seq)
    return k == 0, k == tiles_per_seq - 1, k


def _pool_kernel(u_ref, up_ref, un_ref, w_ref, sc_ref, y_ref, ext_scr, *, tm, seq_len):
    i = pl.program_id(0)
    first, last, k = _seq_edges(i, seq_len // tm)
    ext_scr[0:8, :] = jnp.where(first, jnp.zeros_like(up_ref[...]), up_ref[...])
    ext_scr[8:8 + tm, :] = u_ref[...]
    ext_scr[8 + tm:16 + tm, :] = jnp.where(last, jnp.zeros_like(un_ref[...]), un_ref[...])
    t = k * tm + lax.broadcasted_iota(jnp.int32, (tm, 1), 0)
    for gi, win in enumerate(POOL_WINDOWS):
        half = win // 2
        lanes = slice(gi * 128, (gi + 1) * 128)
        s = ext_scr[pl.ds(8 - half, tm), lanes]
        for kk in range(1, win):
            s = s + ext_scr[pl.ds(8 - half + kk, tm), lanes]
        cnt = (jnp.minimum(t + half, seq_len) - jnp.maximum(t - half, 0)).astype(F32)
        pooled = s / cnt - ext_scr[pl.ds(8, tm), lanes]
        y = _dot(pooled.astype(BF16), w_ref[gi]) * sc_ref[:, lanes]
        y_ref[:, lanes] = y.astype(BF16)


def _pool_mixer(p, w_pool, pool_scale, seq_len, tm=1024):
    T = p.shape[0]
    n_tiles = T // tm
    prev_spec, next_spec = _halo_specs(tm, 8, GROUP, 0, n_tiles)
    return pl.pallas_call(
        functools.partial(_pool_kernel, tm=tm, seq_len=seq_len),
        grid=(n_tiles,),
        in_specs=[
            pl.BlockSpec((tm, GROUP), lambda i: (i, 0)),
            prev_spec,
            next_spec,
            pl.BlockSpec((4, 128, 128), lambda i: (0, 0, 0)),
            pl.BlockSpec((1, GROUP), lambda i: (0, 0)),
        ],
        out_specs=pl.BlockSpec((tm, GROUP), lambda i: (i, 0)),
        out_shape=jax.ShapeDtypeStruct((T, GROUP), BF16),
        scratch_shapes=[pltpu.VMEM((tm + 16, GROUP), F32)],
        compiler_params=_cparams("parallel"),
        name="pool_mixer",
    )(p, p, p, w_pool, pool_scale)


def _conformer_kernel(v_ref, vp_ref, vn_ref, g_ref, gp_ref, gn_ref, dw_ref, db_ref, lg_ref, lb_ref,
                      pw_ref, y_ref, ext_scr, shift_scr, *, tm, seq_len):
    i = pl.program_id(0)
    first, last, _ = _seq_edges(i, seq_len // tm)
    hp = vp_ref[...] * _sigmoid(gp_ref[...])
    hn = vn_ref[...] * _sigmoid(gn_ref[...])
    ext_scr[0:16, :] = jnp.where(first, jnp.zeros_like(hp), hp)
    ext_scr[16:16 + tm, :] = v_ref[...] * _sigmoid(g_ref[...])
    ext_scr[16 + tm:32 + tm, :] = jnp.where(last, jnp.zeros_like(hn), hn)
    base = 16 - CONV_WIDTH // 2
    acc = db_ref[...]
    for b in range(8):
        taps = [(a, 8 * a + b - base) for a in range(5) if 0 <= 8 * a + b - base < CONV_WIDTH]
        rows = tm + 8 * taps[-1][0]
        shift_scr[b, 0:rows, :] = ext_scr[pl.ds(b, rows), :]
        for a, kk in taps:
            acc = acc + shift_scr[b, 8 * a:8 * a + tm, :] * dw_ref[kk:kk + 1, :]
    mu = jnp.mean(acc, axis=-1, keepdims=True)
    xc = acc - mu
    var = jnp.mean(xc * xc, axis=-1, keepdims=True)
    h = _silu(xc * lax.rsqrt(var + EPS) * lg_ref[...] + lb_ref[...])
    y_ref[...] = _dot(h.astype(BF16), pw_ref[...]).astype(BF16)


def _conformer_mixer(p, dw_w, dw_b, ln_g, ln_b, pw, seq_len, tm=512):
    T = p.shape[0]
    n_tiles = T // tm
    vprev, vnext = _halo_specs(tm, 16, GROUP, 4, n_tiles)
    gprev, gnext = _halo_specs(tm, 16, GROUP, 5, n_tiles)
    const = lambda shape: pl.BlockSpec(shape, lambda i: (0,) * len(shape))
    return pl.pallas_call(
        functools.partial(_conformer_kernel, tm=tm, seq_len=seq_len),
        grid=(n_tiles,),
        in_specs=[
            pl.BlockSpec((tm, GROUP), lambda i: (i, 4)), vprev, vnext,
            pl.BlockSpec((tm, GROUP), lambda i: (i, 5)), gprev, gnext,
            const((CONV_WIDTH, GROUP)), const((1, GROUP)), const((1, GROUP)), const((1, GROUP)),
            const((GROUP, GROUP)),
        ],
        out_specs=pl.BlockSpec((tm, GROUP), lambda i: (i, 0)),
        out_shape=jax.ShapeDtypeStruct((T, GROUP), BF16),
        scratch_shapes=[pltpu.VMEM((tm + 32, GROUP), F32), pltpu.VMEM((8, tm + 32, GROUP), F32)],
        compiler_params=_cparams("parallel"),
        name="conformer_mixer",
    )(p, p, p, p, p, p, dw_w, dw_b, ln_g, ln_b, pw)


ATT_TILE = 1024
ATT_BLOCK_GROUP = 8


def _t5_bucket_table():
    nb = REL_BUCKETS // 2
    max_exact = nb // 2
    i = np.arange(ATT_RADIUS)[:, None]
    j = np.arange(3 * ATT_RADIUS)[None, :]
    off = j - ATT_RADIUS - i
    tables = []
    for dil in ATT_DILATIONS:
        rel = off * dil
        n = np.abs(rel)
        nf = np.maximum(n, 1).astype(np.float32)
        large = max_exact + (np.log(nf / np.float32(max_exact)) / np.float32(math.log(REL_MAX_DIST / max_exact))
                             * np.float32(nb - max_exact)).astype(np.int32)
        large = np.minimum(large, nb - 1)
        bucket = np.where(rel > 0, nb, 0) + np.where(n < max_exact, n, large)
        tables.append(np.where(np.abs(off) <= ATT_RADIUS, bucket, -1))
    return np.stack(tables).astype(np.int32)


def _att_kernel(rb_ref, bkt_ref, qg_ref, kg_ref, mseg_ref, q_ref, kc_ref, kn_ref,
                vp_ref, vc_ref, vn_ref, y_ref, bias_scr, qbuf, kbuf, vbuf, acc_scr, m_scr, l_scr,
                *, seq_len):
    tile = ATT_TILE
    rad = ATT_RADIUS
    hp = pl.program_id(0)
    i = pl.program_id(1)

    def rms(x, g):
        ms = _dot((x * x).astype(BF16), mseg_ref[...])
        return x * lax.rsqrt(ms + EPS) * g

    @pl.when(i == 0)
    def _():
        lane_head0 = lax.broadcasted_iota(jnp.int32, (1, 128), 1) < ATT_HEAD_DIM
        for di in range(len(ATT_DILATIONS)):
            bkt = bkt_ref[di]
            b = jnp.zeros(bkt.shape, F32)
            for bb in range(REL_BUCKETS):
                b = jnp.where(bkt == bb, jnp.where(lane_head0, rb_ref[bb, 2 * hp], rb_ref[bb, 2 * hp + 1]), b)
            bias_scr[di] = jnp.where(bkt < 0, NEG_INF, b)
        kbuf[0:tile, :] = jnp.zeros((tile, 128), F32)
        kbuf[tile:2 * tile, :] = rms(kc_ref[...], kg_ref[...])

    @pl.when(i > 0)
    def _():
        kbuf[0:tile, :] = kbuf[tile:2 * tile, :]
        kbuf[tile:2 * tile, :] = kbuf[2 * tile:3 * tile, :]

    first, last, _ = _seq_edges(i, seq_len // tile)

    qbuf[...] = rms(q_ref[...], qg_ref[...]) * (ATT_HEAD_DIM ** -0.5)
    kbuf[2 * tile:3 * tile, :] = rms(kn_ref[...], kg_ref[...])
    vbuf[0:tile, :] = vp_ref[...]
    vbuf[tile:2 * tile, :] = vc_ref[...]
    vbuf[2 * tile:3 * tile, :] = vn_ref[...]

    head0 = lax.broadcasted_iota(jnp.int32, (1, 128), 1) < ATT_HEAD_DIM
    key_row = lax.broadcasted_iota(jnp.int32, (3 * rad, 1), 0)
    ones_cols = jnp.ones((3 * rad, 128), BF16)

    def rows(start, size, dil):
        return pl.ds(start, size) if dil == 1 else pl.ds(start, size, stride=dil)

    blocks = [(di, dil, r + rad * dil * m)
              for di, dil in enumerate(ATT_DILATIONS) for r in range(dil) for m in range(tile // (rad * dil))]
    for g0 in range(0, len(blocks), ATT_BLOCK_GROUP):
        group = blocks[g0:g0 + ATT_BLOCK_GROUP]
        scores = []
        for di, dil, qstart in group:
            kstart = tile + qstart - rad * dil
            qb = qbuf[rows(qstart, rad, dil), :]
            kb = kbuf[rows(kstart, 3 * rad, dil), :].astype(BF16)
            q2 = jnp.concatenate([jnp.where(head0, qb, 0.0), jnp.where(head0, 0.0, qb)], axis=0)
            scores.append(_dot_nt(kb, q2.astype(BF16)))
        probs = []
        for (di, dil, qstart), s in zip(group, scores):
            kstart = tile + qstart - rad * dil
            s = s + bias_scr[di]
            n_prev = max(0, -(-(tile - kstart) // dil))
            n_upto = min(3 * rad, -(-(2 * tile - kstart) // dil))
            if n_prev > 0:
                s = jnp.where(key_row < jnp.where(first, n_prev, 0), NEG_INF, s)
            if n_upto < 3 * rad:
                s = jnp.where(key_row >= jnp.where(last, n_upto, 3 * rad), NEG_INF, s)
            mx = jnp.max(s, axis=0, keepdims=True)
            probs.append((mx, jnp.exp(s - mx).astype(BF16)))
        outs = []
        for (di, dil, qstart), (mx, pe) in zip(group, probs):
            kstart = tile + qstart - rad * dil
            vb = vbuf[rows(kstart, 3 * rad, dil), :].astype(BF16)
            outs.append(_dot_tn(pe, jnp.concatenate([vb, ones_cols], axis=1)))
        for (di, dil, qstart), (mx, pe), ov in zip(group, probs, outs):
            dst = rows(qstart, rad, dil)
            m_col = jnp.broadcast_to(mx, (2 * rad, 128)).T
            acc_scr[di, dst, :] = jnp.where(head0, ov[0:rad, 0:128], ov[rad:2 * rad, 0:128])
            l_scr[di, dst, :] = jnp.where(head0, ov[0:rad, 128:256], ov[rad:2 * rad, 128:256])
            m_scr[di, dst, :] = jnp.where(head0, m_col[0:rad], m_col[rad:2 * rad])

    m_all = jnp.maximum(jnp.maximum(m_scr[0], m_scr[1]), m_scr[2])
    num = jnp.zeros((tile, 128), F32)
    den = jnp.zeros((tile, 128), F32)
    for di in range(len(ATT_DILATIONS)):
        e = jnp.exp(m_scr[di] - m_all)
        num = num + acc_scr[di] * e
        den = den + l_scr[di] * e
    y_ref[...] = (num / den).astype(BF16)


def _attention_mixer(p, q_g, k_g, rel_bias, seq_len):
    T = p.shape[0]
    tile = ATT_TILE
    n_tiles = T // tile
    rad = ATT_RADIUS
    seg = np.kron(np.eye(2), np.full((ATT_HEAD_DIM, ATT_HEAD_DIM), 1.0 / ATT_HEAD_DIM))
    mseg = jnp.asarray(seg, BF16)
    bkt_t = np.transpose(_t5_bucket_table(), (0, 2, 1))
    bkt = jnp.asarray(np.concatenate([bkt_t, bkt_t], axis=2))
    qg2 = jnp.tile(q_g, 2)[None]
    kg2 = jnp.tile(k_g, 2)[None]

    def blk(col0, shift):
        return pl.BlockSpec((tile, 128),
                            lambda hp, i: (jnp.clip(i + shift, 0, n_tiles - 1), col0 * 4 + hp))

    const = lambda shape: pl.BlockSpec(shape, lambda hp, i: (0,) * len(shape))
    return pl.pallas_call(
        functools.partial(_att_kernel, seq_len=seq_len),
        grid=(4, n_tiles),
        in_specs=[
            pl.BlockSpec(memory_space=pltpu.SMEM),
            const((3, 3 * rad, 2 * rad)), const((1, 128)), const((1, 128)), const((128, 128)),
            blk(1, 0),
            blk(2, 0), blk(2, 1),
            blk(3, -1), blk(3, 0), blk(3, 1),
        ],
        out_specs=pl.BlockSpec((tile, 128), lambda hp, i: (i, hp)),
        out_shape=jax.ShapeDtypeStruct((T, GROUP), BF16),
        scratch_shapes=[
            pltpu.VMEM((3, 3 * rad, 2 * rad), F32),
            pltpu.VMEM((tile, 128), F32),
            pltpu.VMEM((3 * tile, 128), F32),
            pltpu.VMEM((3 * tile, 128), F32),
            pltpu.VMEM((3, tile, 128), F32),
            pltpu.VMEM((3, tile, 128), F32),
            pltpu.VMEM((3, tile, 128), F32),
        ],
        compiler_params=_cparams("arbitrary", "arbitrary"),
        name="dilated_attention",
    )(rel_bias, bkt, qg2, kg2, mseg, p, p, p, p, p, p)


def _softplus(x):
    return jnp.maximum(x, 0.0) + jnp.log1p(jnp.exp(-jnp.abs(x)))


def _dn_prep_kernel(x_ref, xp_ref, xn_ref, w_ref, o_ref, ext_scr, *, tm, seq_len):
    i = pl.program_id(0)
    first, last, _ = _seq_edges(i, seq_len // tm)
    ext_scr[0:8, :] = jnp.where(first, jnp.zeros_like(xp_ref[...]), xp_ref[...])
    ext_scr[8:8 + tm, :] = x_ref[...]
    ext_scr[8 + tm:16 + tm, :] = jnp.where(last, jnp.zeros_like(xn_ref[...]), xn_ref[...])
    for cb in range(3 * DN_HEADS):
        lanes = slice(cb * 128, (cb + 1) * 128)
        acc = ext_scr[pl.ds(6, tm), lanes] * w_ref[0:1, lanes]
        for kk in range(1, 4):
            acc = acc + ext_scr[pl.ds(6 + kk, tm), lanes] * w_ref[kk:kk + 1, lanes]
        y = _silu(acc)
        if cb < 2 * DN_HEADS:
            y = y * lax.rsqrt(jnp.sum(y * y, axis=-1, keepdims=True) + EPS)
        if cb < DN_HEADS:
            y = y * (DN_HEAD_DIM ** -0.5)
        o_ref[:, lanes] = y


def _dn_prep(p, conv_w, seq_len, tm=512):
    T = p.shape[0]
    n_tiles = T // tm
    width = 3 * GROUP
    prev_spec, next_spec = _halo_specs(tm, 8, width, 2, n_tiles)
    return pl.pallas_call(
        functools.partial(_dn_prep_kernel, tm=tm, seq_len=seq_len),
        grid=(n_tiles,),
        in_specs=[pl.BlockSpec((tm, width), lambda i: (i, 2)), prev_spec, next_spec,
                  pl.BlockSpec((4, width), lambda i: (0, 0))],
        out_specs=pl.BlockSpec((tm, width), lambda i: (i, 0)),
        out_shape=jax.ShapeDtypeStruct((T, width), F32),
        scratch_shapes=[pltpu.VMEM((tm + 16, width), F32)],
        compiler_params=_cparams("parallel"),
        name="deltanet_prep",
    )(p, p, p, conv_w)


DN_GROUP = 8
DN_ROWS = DN_HEADS * DN_CHUNK


def _dn_scan_kernel(xf_ref, xb_ref, gf_ref, gb_ref, rf_ref, rb_ref, alc_ref, dtc_ref, alr_ref, dtr_ref,
                    of_ref, ob_ref, s_scr):
    C = DN_CHUNK
    H = DN_HEADS
    R = DN_GROUP * C

    @pl.when(pl.program_id(1) == 0)
    def _():
        s_scr[...] = jnp.zeros_like(s_scr)

    row_in_chunk = lax.rem(lax.broadcasted_iota(jnp.int32, (R, 1), 0), C)
    lane_in_chunk = lax.rem(lax.broadcasted_iota(jnp.int32, (1, DN_ROWS), 1), C)
    rid = lax.broadcasted_iota(jnp.int32, (DN_ROWS, DN_ROWS), 0)
    cid = lax.broadcasted_iota(jnp.int32, (DN_ROWS, DN_ROWS), 1)
    same_head = (rid // C) == (cid // C)
    eye = (rid == cid).astype(F32)

    def stack_heads(x, col0):
        return jnp.concatenate([x[:, col0 + h * 128:col0 + (h + 1) * 128] for h in range(H)], axis=0)

    def stack_cols(x, lane0, rows=None):
        parts = []
        for h in range(H):
            c = x[:, lane0 + h:lane0 + h + 1]
            if rows is not None:
                c = c[rows:rows + 1, :]
            parts.append(jnp.broadcast_to(c, (C, 128)))
        return jnp.concatenate(parts, axis=0)

    refs = ((xf_ref, gf_ref, rf_ref, of_ref), (xb_ref, gb_ref, rb_ref, ob_ref))
    gates = []
    for d, (x_ref, gc_ref, gr_ref, o_ref) in enumerate(refs):
        reverse = d == 1
        gcol = gc_ref[...]
        g = -jnp.exp(alc_ref[...]) * _softplus(gcol + dtc_ref[...])
        beta = _sigmoid(gcol)
        gcum = g
        for sh in (1, 2, 4, 8, 16, 32):
            if reverse:
                gcum = gcum + jnp.where(row_in_chunk < C - sh, pltpu.roll(gcum, R - sh, 0), 0.0)
            else:
                gcum = gcum + jnp.where(row_in_chunk >= sh, pltpu.roll(gcum, sh, 0), 0.0)
        grow_all = -jnp.exp(alr_ref[...])[None] * _softplus(gr_ref[...] + dtr_ref[...][None])
        for sh in (1, 2, 4, 8, 16, 32):
            if reverse:
                grow_all = grow_all + jnp.where(lane_in_chunk < C - sh,
                                                pltpu.roll(grow_all, DN_ROWS - sh, 2), 0.0)
            else:
                grow_all = grow_all + jnp.where(lane_in_chunk >= sh, pltpu.roll(grow_all, sh, 2), 0.0)

        gates.append((gcum, beta, grow_all))

    units = []
    for step in range(DN_GROUP):
        for d, (x_ref, _, _, o_ref) in enumerate(refs):
            reverse = d == 1
            ci = DN_GROUP - 1 - step if reverse else step
            gcum, beta, grow_all = gates[d]
            rows = slice(ci * C, (ci + 1) * C)
            x = x_ref[rows, :]
            u = dict(d=d, rows=rows, o_ref=o_ref)
            u["q"] = stack_heads(x, 0)
            u["k"] = stack_heads(x, GROUP)
            v_st = stack_heads(x, 2 * GROUP)
            gc_c = gcum[rows, :]
            beta_st = stack_cols(beta[rows, :], 8 + d * H)
            gcol_st = stack_cols(gc_c, d * H)
            glast_st = stack_cols(gc_c, d * H, rows=0 if reverse else C - 1)
            grow = grow_all[ci, d:d + 1, :]
            tri = (rid <= cid) if reverse else (rid >= cid)
            u["incl"] = same_head & tri
            u["decay"] = jnp.where(u["incl"],
                                   jnp.exp(jnp.concatenate([gcol_st, gcol_st], axis=1) - grow), 0.0)
            kb_st = u["k"] * beta_st
            eg = jnp.exp(gcol_st)
            u["kq"] = jnp.concatenate([kb_st, u["q"]], axis=0).astype(BF16)
            u["rhs"] = jnp.concatenate([v_st * beta_st, kb_st * eg], axis=1).astype(BF16)
            u["qdec"] = u["q"] * eg
            u["kdec"] = (u["k"] * jnp.exp(glast_st - gcol_st)).astype(BF16)
            u["gl"] = jnp.exp(glast_st)
            units.append(u)

    for u in units:
        u["kk"] = _dot_nt(u["kq"], u["k"].astype(BF16))
    for u in units:
        strict = u["incl"] & (rid != cid)
        a = jnp.where(strict, u["kk"][0:DN_ROWS] * u["decay"], 0.0)
        u["attn"] = (u["kk"][DN_ROWS:2 * DN_ROWS] * u["decay"]).astype(BF16)
        u["pinv"] = eye - a
        u["a"] = a.astype(BF16)
    for u in units:
        u["apow"] = _dot(u["a"], u["a"])
    for _ in range(4):
        for u in units:
            ap = u["apow"].astype(BF16)
            u["both"] = _dot(jnp.concatenate([u["pinv"].astype(BF16), ap], axis=0), ap)
        for u in units:
            u["pinv"] = u["pinv"] + u["both"][0:DN_ROWS]
            u["apow"] = u["both"][DN_ROWS:2 * DN_ROWS]
    for u in units:
        u["last"] = _dot(u["pinv"].astype(BF16), u["apow"].astype(BF16))
    for u in units:
        u["uw"] = _dot((u["pinv"] + u["last"]).astype(BF16), u["rhs"])

    for step in range(DN_GROUP):
        pair = units[2 * step:2 * step + 2]
        for u in pair:
            d = u["d"]
            u["wq"] = []
            for h in range(H):
                hr = slice(h * C, (h + 1) * C)
                lhs = jnp.concatenate([u["uw"][hr, 128:256], u["qdec"][hr]], axis=0).astype(BF16)
                u["wq"].append(_dot(lhs, s_scr[d * H + h].astype(BF16)))
        for u in pair:
            u["vnew"] = [u["uw"][h * C:(h + 1) * C, 0:128] - u["wq"][h][0:C] for h in range(H)]
            vnew_st = jnp.concatenate(u["vnew"], axis=0).astype(BF16)
            u["o"] = jnp.concatenate([w[C:2 * C] for w in u["wq"]], axis=0) + _dot(u["attn"], vnew_st)
        for u in pair:
            d = u["d"]
            for h in range(H):
                hr = slice(h * C, (h + 1) * C)
                s_scr[d * H + h] = (s_scr[d * H + h] * u["gl"][h * C:h * C + 1, :]
                                    + _dot_tn(u["kdec"][hr], u["vnew"][h].astype(BF16)))
                u["o_ref"][u["rows"], h * 128:(h + 1) * 128] = u["o"][hr]


def _dn_scan(qkvn, pg, a_log, dt_bias, batch, seq_len):
    T = qkvn.shape[0]
    C, H, G = DN_CHUNK, DN_HEADS, DN_GROUP
    R = G * C
    nc = seq_len // C
    ncg = nc // G
    ab = pg[:, 0:4 * H].reshape(batch, nc, C, 2, 2, H)
    ab_row = jnp.transpose(ab, (0, 1, 3, 4, 5, 2)).reshape(batch * nc, 4, H * C)
    ab_row = jnp.pad(ab_row, ((0, 0), (0, 4), (0, 0)))
    pad_lanes = lambda v: jnp.pad(v.reshape(1, 2 * H), ((0, 0), (0, GATE_LANES - 2 * H)))
    row_param = lambda v: jnp.pad(jnp.repeat(v, C, axis=1), ((0, 6), (0, 0)))

    fwd = lambda b, c: (b * ncg + c, 0)
    bwd = lambda b, c: (b * ncg + ncg - 1 - c, 0)
    fwd3 = lambda b, c: (b * ncg + c, 0, 0)
    bwd3 = lambda b, c: (b * ncg + ncg - 1 - c, 0, 0)
    const = lambda shape: pl.BlockSpec(shape, lambda b, c: (0,) * len(shape))
    return pl.pallas_call(
        _dn_scan_kernel,
        grid=(batch, ncg),
        in_specs=[
            pl.BlockSpec((R, 3 * GROUP), fwd), pl.BlockSpec((R, 3 * GROUP), bwd),
            pl.BlockSpec((R, GATE_LANES), fwd), pl.BlockSpec((R, GATE_LANES), bwd),
            pl.BlockSpec((G, 8, H * C), fwd3), pl.BlockSpec((G, 8, H * C), bwd3),
            const((1, GATE_LANES)), const((1, GATE_LANES)), const((8, H * C)), const((8, H * C)),
        ],
        out_specs=[pl.BlockSpec((R, GROUP), fwd), pl.BlockSpec((R, GROUP), bwd)],
        out_shape=[jax.ShapeDtypeStruct((T, GROUP), F32), jax.ShapeDtypeStruct((T, GROUP), F32)],
        scratch_shapes=[pltpu.VMEM((2 * H, DN_HEAD_DIM, DN_HEAD_DIM), F32)],
        compiler_params=_cparams("arbitrary", "arbitrary"),
        name="deltanet_scan",
    )(qkvn, qkvn, pg, pg, ab_row, ab_row, pad_lanes(a_log), pad_lanes(dt_bias),
      row_param(a_log), row_param(dt_bias))


def _dn_post_kernel(of_ref, ob_ref, z_ref, g_ref, y_ref):
    for h in range(DN_HEADS):
        lanes = slice(h * 128, (h + 1) * 128)
        o = of_ref[:, lanes] + ob_ref[:, lanes]
        o = o * lax.rsqrt(jnp.mean(o * o, axis=-1, keepdims=True) + EPS) * g_ref[...]
        y_ref[:, lanes] = (o * _silu(z_ref[:, lanes])).astype(BF16)


def _dn_post(o_f, o_b, p, norm_g, tm=1024):
    T = p.shape[0]
    row = lambda cb: pl.BlockSpec((tm, GROUP), lambda i: (i, cb))
    return pl.pallas_call(
        _dn_post_kernel,
        grid=(T // tm,),
        in_specs=[row(0), row(0), row(9), pl.BlockSpec((1, DN_HEAD_DIM), lambda i: (0, 0))],
        out_specs=row(0),
        out_shape=jax.ShapeDtypeStruct((T, GROUP), BF16),
        compiler_params=_cparams("parallel"),
        name="deltanet_post",
    )(o_f, o_b, p, norm_g)


def _deltanet_mixer(p, pg, conv_w, a_log, dt_bias, norm_g, batch, seq_len):
    qkvn = _dn_prep(p, conv_w, seq_len)
    o_f, o_b = _dn_scan(qkvn, pg, a_log, dt_bias, batch, seq_len)
    return _dn_post(o_f, o_b, p, norm_g[None])


def _outproj_kernel(x_ref, ya_ref, yb_ref, yc_ref, yd_ref, w_ref, g_ref, xn_ref, hn_ref, ycat_scr):
    for gi, y_ref in enumerate((ya_ref, yb_ref, yc_ref, yd_ref)):
        ycat_scr[:, gi * GROUP:(gi + 1) * GROUP] = y_ref[...]
    acc = x_ref[...] + _dot(ycat_scr[...], w_ref[...])
    xn_ref[...] = acc
    ms = jnp.mean(acc * acc, axis=-1, keepdims=True)
    hn_ref[...] = (acc * lax.rsqrt(ms + EPS) * g_ref[...]).astype(BF16)


def _out_proj(x2, ys, w_out, norm2_g, tm=512):
    T = x2.shape[0]
    row = lambda width: pl.BlockSpec((tm, width), lambda i: (i, 0))
    return pl.pallas_call(
        _outproj_kernel,
        grid=(T // tm,),
        in_specs=[row(D_MODEL), row(GROUP), row(GROUP), row(GROUP), row(GROUP),
                  pl.BlockSpec((D_MODEL, D_MODEL), lambda i: (0, 0), pipeline_mode=pl.Buffered(1)),
                  pl.BlockSpec((1, D_MODEL), lambda i: (0, 0))],
        out_specs=[row(D_MODEL), row(D_MODEL)],
        out_shape=[jax.ShapeDtypeStruct((T, D_MODEL), F32),
                   jax.ShapeDtypeStruct((T, D_MODEL), BF16)],
        scratch_shapes=[pltpu.VMEM((tm, D_MODEL), BF16)],
        compiler_params=_cparams("parallel"),
        name="out_proj",
    )(x2, *ys, w_out, norm2_g)


FFN_HALF = FFN_HIDDEN // 2
FFN_COL_CHUNK = 256


def _ffn_hidden_kernel(hn_ref, hp_ref, hx_ref, wg_ref, wu_ref, dw_ref, db_ref, a_ref, hext_scr, g_scr,
                       *, tm, seq_len):
    first, last, _ = _seq_edges(pl.program_id(1), seq_len // tm)
    hext_scr[0:16, :] = jnp.where(first, jnp.zeros_like(hp_ref[...]), hp_ref[...])
    hext_scr[16:16 + tm, :] = hn_ref[...]
    hext_scr[16 + tm:32 + tm, :] = jnp.where(last, jnp.zeros_like(hx_ref[...]), hx_ref[...])
    for c in range(FFN_HALF // FFN_COL_CHUNK):
        cols = slice(c * FFN_COL_CHUNK, (c + 1) * FFN_COL_CHUNK)
        g = g_scr.at[c % 2]
        g[...] = _dot(hext_scr[...], wg_ref[:, cols])
        up = _dot(hn_ref[...], wu_ref[:, cols])
        gate = (g[pl.ds(15, tm), :] * dw_ref[0:1, cols] + g[pl.ds(16, tm), :] * dw_ref[1:2, cols]
                + g[pl.ds(17, tm), :] * dw_ref[2:3, cols] + db_ref[:, cols])
        a_ref[:, cols] = (_silu(gate) * up).astype(BF16)


def _ffn_down_kernel(a_ref, xn_ref, wd_ref, o_ref):
    o_ref[...] = xn_ref[...] + _dot(a_ref[...], wd_ref[...])


def _ffn(hn, xn, w_gate, dw_w, dw_b, w_up, w_down, seq_len, tm_hidden=1024, tm_down=256):
    T = hn.shape[0]
    n_tiles = T // tm_hidden
    r = tm_hidden // 16
    half_cols = lambda rows: pl.BlockSpec((rows, FFN_HALF), lambda h, i: (0, h),
                                          pipeline_mode=pl.Buffered(1))
    act = pl.pallas_call(
        functools.partial(_ffn_hidden_kernel, tm=tm_hidden, seq_len=seq_len),
        grid=(2, n_tiles),
        in_specs=[
            pl.BlockSpec((tm_hidden, D_MODEL), lambda h, i: (i, 0)),
            pl.BlockSpec((16, D_MODEL), lambda h, i: (jnp.maximum(i * r - 1, 0), 0)),
            pl.BlockSpec((16, D_MODEL), lambda h, i: (jnp.minimum((i + 1) * r, n_tiles * r - 1), 0)),
            half_cols(D_MODEL), half_cols(D_MODEL), half_cols(3), half_cols(1),
        ],
        out_specs=pl.BlockSpec((tm_hidden, FFN_HALF), lambda h, i: (i, h)),
        out_shape=jax.ShapeDtypeStruct((T, FFN_HIDDEN), BF16),
        scratch_shapes=[pltpu.VMEM((tm_hidden + 32, D_MODEL), BF16),
                        pltpu.VMEM((2, tm_hidden + 32, FFN_COL_CHUNK), F32)],
        compiler_params=_cparams("arbitrary", "arbitrary"),
        name="ffn_hidden",
    )(hn, hn, hn, w_gate, w_up, dw_w, dw_b)
    return pl.pallas_call(
        _ffn_down_kernel,
        grid=(T // tm_down,),
        in_specs=[
            pl.BlockSpec((tm_down, FFN_HIDDEN), lambda i: (i, 0)),
            pl.BlockSpec((tm_down, D_MODEL), lambda i: (i, 0)),
            pl.BlockSpec((FFN_HIDDEN, D_MODEL), lambda i: (0, 0), pipeline_mode=pl.Buffered(1)),
        ],
        out_specs=pl.BlockSpec((tm_down, D_MODEL), lambda i: (i, 0)),
        out_shape=jax.ShapeDtypeStruct((T, D_MODEL), F32),
        compiler_params=_cparams("parallel"),
        name="ffn_down",
    )(act, xn, w_down)


def _layer(x2, l, batch, seq_len, rel_bias, norm1_g, w_in, w_pool, pool_scale, att_q_g, att_k_g,
           conv_dw_w, conv_dw_b, conv_ln_g, conv_ln_b, conv_pw, dn_conv_w, dn_a_log, dn_dt_bias,
           dn_norm_g, w_out, norm2_g, ffn_w_gate, ffn_dw_w, ffn_dw_b, ffn_w_up, ffn_w_down):
    w_main = _weight_bf16(w_in, l, cols=PROJ_MAIN)
    w_gates = _gate_weight_bf16(w_in, l)
    p, pg = _in_proj(x2, norm1_g[l][None], w_main, w_gates)
    ya = _pool_mixer(p, w_pool[l].astype(BF16), pool_scale[l][None], seq_len)
    yb = _attention_mixer(p, att_q_g[l], att_k_g[l], rel_bias, seq_len)
    yc = _conformer_mixer(p, conv_dw_w[l], conv_dw_b[l][None], conv_ln_g[l][None], conv_ln_b[l][None],
                          conv_pw[l].astype(BF16), seq_len)
    yd = _deltanet_mixer(p, pg, dn_conv_w[l], dn_a_log[l], dn_dt_bias[l], dn_norm_g[l], batch, seq_len)
    xn, hn = _out_proj(x2, (ya, yb, yc, yd), _weight_bf16(w_out, l), norm2_g[l][None])
    return _ffn(hn, xn, _weight_bf16(ffn_w_gate, l), ffn_dw_w[l], ffn_dw_b[l][None],
                _weight_bf16(ffn_w_up, l), _weight_bf16(ffn_w_down, l), seq_len)


def kernel(x, rel_bias, norm1_g, w_in, w_pool, pool_scale, att_q_g, att_k_g, conv_dw_w, conv_dw_b,
           conv_ln_g, conv_ln_b, conv_pw, dn_conv_w, dn_a_log, dn_dt_bias, dn_norm_g, w_out, norm2_g,
           ffn_w_gate, ffn_dw_w, ffn_dw_b, ffn_w_up, ffn_w_down):
    batch, seq_len, _ = x.shape
    x2 = x.reshape(batch * seq_len, D_MODEL)
    for l in range(norm1_g.shape[0]):
        x2 = _layer(x2, l, batch, seq_len, rel_bias, norm1_g, w_in, w_pool, pool_scale, att_q_g,
                    att_k_g, conv_dw_w, conv_dw_b, conv_ln_g, conv_ln_b, conv_pw, dn_conv_w, dn_a_log,
                    dn_dt_bias, dn_norm_g, w_out, norm2_g, ffn_w_gate, ffn_dw_w, ffn_dw_b, ffn_w_up,
                    ffn_w_down)
    return x2.reshape(batch, seq_len, D_MODEL)
```

```python
import functools
import math

import jax
import jax.numpy as jnp
import numpy as np
from jax import lax
from jax.experimental import pallas as pl
from jax.experimental.pallas import tpu as pltpu

F32 = jnp.float32
BF16 = jnp.bfloat16

D_MODEL = 2048
GROUP = 512
POOL_WINDOWS = (2, 4, 8, 16)
ATT_HEAD_DIM = 64
ATT_HEADS = 8
ATT_RADIUS = 64
ATT_DILATIONS = (1, 4, 16)
REL_BUCKETS = 32
REL_MAX_DIST = 1024
CONV_WIDTH = 31
DN_HEAD_DIM = 128
DN_HEADS = 4
DN_CHUNK = 64
FFN_HIDDEN = 5632
EPS = 1e-6
NEG_INF = -1e30

PROJ_MAIN = 10 * GROUP
GATE_LANES = 128

VMEM_LIMIT_BYTES = 56 * 1024 * 1024


def _cparams(*sem):
    return pltpu.CompilerParams(dimension_semantics=sem, vmem_limit_bytes=VMEM_LIMIT_BYTES)


def _sigmoid(x):
    return 1.0 / (1.0 + jnp.exp(-x))


def _silu(x):
    return x * _sigmoid(x)


def _dot(a, b):
    return jnp.dot(a, b, preferred_element_type=F32)


def _dot_nt(a, b):
    return lax.dot_general(a, b, (((1,), (1,)), ((), ())), preferred_element_type=F32)


def _dot_tn(a, b):
    return lax.dot_general(a, b, (((0,), (0,)), ((), ())), preferred_element_type=F32)


def _cast_kernel(w_ref, o_ref):
    o_ref[...] = w_ref[...].astype(BF16)


def _weight_bf16(w_stack, layer, cols=None, tr=512):
    _, rows, width = w_stack.shape
    cols = width if cols is None else cols
    tc = cols // 2 if cols >= 4096 else cols
    return pl.pallas_call(
        _cast_kernel,
        grid=(rows // tr, cols // tc),
        in_specs=[pl.BlockSpec((None, tr, tc), lambda i, j: (layer, i, j))],
        out_specs=pl.BlockSpec((tr, tc), lambda i, j: (i, j)),
        out_shape=jax.ShapeDtypeStruct((rows, cols), BF16),
        compiler_params=_cparams("parallel", "parallel"),
        name="weight_bf16",
    )(w_stack)


def _gate_cols_kernel(w_ref, o_ref, *, valid):
    lane = lax.broadcasted_iota(jnp.int32, o_ref.shape, 1)
    o_ref[...] = jnp.where(lane < valid, w_ref[...], 0.0).astype(BF16)


def _gate_weight_bf16(w_stack, layer):
    _, rows, width = w_stack.shape
    valid = width - PROJ_MAIN
    return pl.pallas_call(
        functools.partial(_gate_cols_kernel, valid=valid),
        grid=(1,),
        in_specs=[pl.BlockSpec((None, rows, GATE_LANES), lambda i: (layer, 0, PROJ_MAIN // GATE_LANES))],
        out_specs=pl.BlockSpec((rows, GATE_LANES), lambda i: (0, 0)),
        out_shape=jax.ShapeDtypeStruct((rows, GATE_LANES), BF16),
        compiler_params=_cparams("arbitrary"),
        name="gate_weight_bf16",
    )(w_stack)


def _inproj_kernel(x_ref, g_ref, w_ref, wg_ref, p_ref, pg_ref):
    x = x_ref[...]
    ms = jnp.mean(x * x, axis=-1, keepdims=True)
    h = (x * lax.rsqrt(ms + EPS) * g_ref[...]).astype(BF16)
    pg_ref[...] = _dot(h, wg_ref[...])
    p_ref[...] = _dot(h, w_ref[...])


def _in_proj(x2, norm_g, w_main, w_gate, tm=256):
    T = x2.shape[0]
    resident = lambda shape: pl.BlockSpec(shape, lambda i: (0, 0), pipeline_mode=pl.Buffered(1))
    return pl.pallas_call(
        _inproj_kernel,
        grid=(T // tm,),
        in_specs=[
            pl.BlockSpec((tm, D_MODEL), lambda i: (i, 0)),
            resident((1, D_MODEL)),
            resident((D_MODEL, PROJ_MAIN)),
            resident((D_MODEL, GATE_LANES)),
        ],
        out_specs=[
            pl.BlockSpec((tm, PROJ_MAIN), lambda i: (i, 0)),
            pl.BlockSpec((tm, GATE_LANES), lambda i: (i, 0)),
        ],
        out_shape=[
            jax.ShapeDtypeStruct((T, PROJ_MAIN), F32),
            jax.ShapeDtypeStruct((T, GATE_LANES), F32),
        ],
        compiler_params=_cparams("parallel"),
        name="in_proj",
    )(x2, norm_g, w_main, w_gate)


def _halo_specs(tm, halo, width, col_block, n_tiles):
    r = tm // halo
    last = n_tiles * r - 1
    prev_spec = pl.BlockSpec((halo, width), lambda i, *_: (jnp.maximum(i * r - 1, 0), col_block))
    next_spec = pl.BlockSpec((halo, width), lambda i, *_: (jnp.minimum((i + 1) * r, last), col_block))
    return prev_spec, next_spec


def _seq_edges(i, tiles_per_seq):
    k = lax.rem(i, tiles_per_seq)
    return k == 0, k == tiles_per_seq - 1, k


def _pool_kernel(u_ref, up_ref, un_ref, w_ref, sc_ref, y_ref, ext_scr, *, tm, seq_len):
    i = pl.program_id(0)
    first, last, k = _seq_edges(i, seq_len // tm)
    ext_scr[0:8, :] = jnp.where(first, jnp.zeros_like(up_ref[...]), up_ref[...])
    ext_scr[8:8 + tm, :] = u_ref[...]
    ext_scr[8 + tm:16 + tm, :] = jnp.where(last, jnp.zeros_like(un_ref[...]), un_ref[...])
    t = k * tm + lax.broadcasted_iota(jnp.int32, (tm, 1), 0)
    for gi, win in enumerate(POOL_WINDOWS):
        half = win // 2
        lanes = slice(gi * 128, (gi + 1) * 128)
        s = ext_scr[pl.ds(8 - half, tm), lanes]
        for kk in range(1, win):
            s = s + ext_scr[pl.ds(8 - half + kk, tm), lanes]
        cnt = (jnp.minimum(t + half, seq_len) - jnp.maximum(t - half, 0)).astype(F32)
        pooled = s / cnt - ext_scr[pl.ds(8, tm), lanes]
        y = _dot(pooled.astype(BF16), w_ref[gi]) * sc_ref[:, lanes]
        y_ref[:, lanes] = y.astype(BF16)


def _pool_mixer(p, w_pool, pool_scale, seq_len, tm=1024):
    T = p.shape[0]
    n_tiles = T // tm
    prev_spec, next_spec = _halo_specs(tm, 8, GROUP, 0, n_tiles)
    return pl.pallas_call(
        functools.partial(_pool_kernel, tm=tm, seq_len=seq_len),
        grid=(n_tiles,),
        in_specs=[
            pl.BlockSpec((tm, GROUP), lambda i: (i, 0)),
            prev_spec,
            next_spec,
            pl.BlockSpec((4, 128, 128), lambda i: (0, 0, 0)),
            pl.BlockSpec((1, GROUP), lambda i: (0, 0)),
        ],
        out_specs=pl.BlockSpec((tm, GROUP), lambda i: (i, 0)),
        out_shape=jax.ShapeDtypeStruct((T, GROUP), BF16),
        scratch_shapes=[pltpu.VMEM((tm + 16, GROUP), F32)],
        compiler_params=_cparams("parallel"),
        name="pool_mixer",
    )(p, p, p, w_pool, pool_scale)


def _conformer_kernel(v_ref, vp_ref, vn_ref, g_ref, gp_ref, gn_ref, dw_ref, db_ref, lg_ref, lb_ref,
                      pw_ref, y_ref, ext_scr, shift_scr, *, tm, seq_len):
    i = pl.program_id(0)
    first, last, _ = _seq_edges(i, seq_len // tm)
    hp = vp_ref[...] * _sigmoid(gp_ref[...])
    hn = vn_ref[...] * _sigmoid(gn_ref[...])
    ext_scr[0:16, :] = jnp.where(first, jnp.zeros_like(hp), hp)
    ext_scr[16:16 + tm, :] = v_ref[...] * _sigmoid(g_ref[...])
    ext_scr[16 + tm:32 + tm, :] = jnp.where(last, jnp.zeros_like(hn), hn)
    base = 16 - CONV_WIDTH // 2
    acc = db_ref[...]
    for b in range(8):
        taps = [(a, 8 * a + b - base) for a in range(5) if 0 <= 8 * a + b - base < CONV_WIDTH]
        rows = tm + 8 * taps[-1][0]
        shift_scr[b, 0:rows, :] = ext_scr[pl.ds(b, rows), :]
        for a, kk in taps:
            acc = acc + shift_scr[b, 8 * a:8 * a + tm, :] * dw_ref[kk:kk + 1, :]
    mu = jnp.mean(acc, axis=-1, keepdims=True)
    xc = acc - mu
    var = jnp.mean(xc * xc, axis=-1, keepdims=True)
    h = _silu(xc * lax.rsqrt(var + EPS) * lg_ref[...] + lb_ref[...])
    y_ref[...] = _dot(h.astype(BF16), pw_ref[...]).astype(BF16)


def _conformer_mixer(p, dw_w, dw_b, ln_g, ln_b, pw, seq_len, tm=512):
    T = p.shape[0]
    n_tiles = T // tm
    vprev, vnext = _halo_specs(tm, 16, GROUP, 4, n_tiles)
    gprev, gnext = _halo_specs(tm, 16, GROUP, 5, n_tiles)
    const = lambda shape: pl.BlockSpec(shape, lambda i: (0,) * len(shape))
    return pl.pallas_call(
        functools.partial(_conformer_kernel, tm=tm, seq_len=seq_len),
        grid=(n_tiles,),
        in_specs=[
            pl.BlockSpec((tm, GROUP), lambda i: (i, 4)), vprev, vnext,
            pl.BlockSpec((tm, GROUP), lambda i: (i, 5)), gprev, gnext,
            const((CONV_WIDTH, GROUP)), const((1, GROUP)), const((1, GROUP)), const((1, GROUP)),
            const((GROUP, GROUP)),
        ],
        out_specs=pl.BlockSpec((tm, GROUP), lambda i: (i, 0)),
        out_shape=jax.ShapeDtypeStruct((T, GROUP), BF16),
        scratch_shapes=[pltpu.VMEM((tm + 32, GROUP), F32), pltpu.VMEM((8, tm + 32, GROUP), F32)],
        compiler_params=_cparams("parallel"),
        name="conformer_mixer",
    )(p, p, p, p, p, p, dw_w, dw_b, ln_g, ln_b, pw)


ATT_TILE = 1024
ATT_BLOCK_GROUP = 8


def _t5_bucket_table():
    nb = REL_BUCKETS // 2
    max_exact = nb // 2
    i = np.arange(ATT_RADIUS)[:, None]
    j = np.arange(3 * ATT_RADIUS)[None, :]
    off = j - ATT_RADIUS - i
    tables = []
    for dil in ATT_DILATIONS:
        rel = off * dil
        n = np.abs(rel)
        nf = np.maximum(n, 1).astype(np.float32)
        large = max_exact + (np.log(nf / np.float32(max_exact)) / np.float32(math.log(REL_MAX_DIST / max_exact))
                             * np.float32(nb - max_exact)).astype(np.int32)
        large = np.minimum(large, nb - 1)
        bucket = np.where(rel > 0, nb, 0) + np.where(n < max_exact, n, large)
        tables.append(np.where(np.abs(off) <= ATT_RADIUS, bucket, -1))
    return np.stack(tables).astype(np.int32)


def _att_kernel(rb_ref, bkt_ref, qg_ref, kg_ref, mseg_ref, q_ref, kc_ref, kn_ref,
                vp_ref, vc_ref, vn_ref, y_ref, bias_scr, qbuf, kbuf, vbuf, acc_scr, m_scr, l_scr,
                *, seq_len):
    tile = ATT_TILE
    rad = ATT_RADIUS
    hp = pl.program_id(0)
    i = pl.program_id(1)

    def rms(x, g):
        ms = _dot((x * x).astype(BF16), mseg_ref[...])
        return x * lax.rsqrt(ms + EPS) * g

    @pl.when(i == 0)
    def _():
        lane_head0 = lax.broadcasted_iota(jnp.int32, (1, 128), 1) < ATT_HEAD_DIM
        for di in range(len(ATT_DILATIONS)):
            bkt = bkt_ref[di]
            b = jnp.zeros(bkt.shape, F32)
            for bb in range(REL_BUCKETS):
                b = jnp.where(bkt == bb, jnp.where(lane_head0, rb_ref[bb, 2 * hp], rb_ref[bb, 2 * hp + 1]), b)
            bias_scr[di] = jnp.where(bkt < 0, NEG_INF, b)
        kbuf[0:tile, :] = jnp.zeros((tile, 128), F32)
        kbuf[tile:2 * tile, :] = rms(kc_ref[...], kg_ref[...])

    @pl.when(i > 0)
    def _():
        kbuf[0:tile, :] = kbuf[tile:2 * tile, :]
        kbuf[tile:2 * tile, :] = kbuf[2 * tile:3 * tile, :]

    first, last, _ = _seq_edges(i, seq_len // tile)

    qbuf[...] = rms(q_ref[...], qg_ref[...]) * (ATT_HEAD_DIM ** -0.5)
    kbuf[2 * tile:3 * tile, :] = rms(kn_ref[...], kg_ref[...])
    vbuf[0:tile, :] = vp_ref[...]
    vbuf[tile:2 * tile, :] = vc_ref[...]
    vbuf[2 * tile:3 * tile, :] = vn_ref[...]

    head0 = lax.broadcasted_iota(jnp.int32, (1, 128), 1) < ATT_HEAD_DIM
    key_row = lax.broadcasted_iota(jnp.int32, (3 * rad, 1), 0)
    ones_cols = jnp.ones((3 * rad, 128), BF16)

    def rows(start, size, dil):
        return pl.ds(start, size) if dil == 1 else pl.ds(start, size, stride=dil)

    blocks = [(di, dil, r + rad * dil * m)
              for di, dil in enumerate(ATT_DILATIONS) for r in range(dil) for m in range(tile // (rad * dil))]
    for g0 in range(0, len(blocks), ATT_BLOCK_GROUP):
        group = blocks[g0:g0 + ATT_BLOCK_GROUP]
        scores = []
        for di, dil, qstart in group:
            kstart = tile + qstart - rad * dil
            qb = qbuf[rows(qstart, rad, dil), :]
            kb = kbuf[rows(kstart, 3 * rad, dil), :].astype(BF16)
            q2 = jnp.concatenate([jnp.where(head0, qb, 0.0), jnp.where(head0, 0.0, qb)], axis=0)
            scores.append(_dot_nt(kb, q2.astype(BF16)))
        probs = []
        for (di, dil, qstart), s in zip(group, scores):
            kstart = tile + qstart - rad * dil
            s = s + bias_scr[di]
            n_prev = max(0, -(-(tile - kstart) // dil))
            n_upto = min(3 * rad, -(-(2 * tile - kstart) // dil))
            if n_prev > 0:
                s = jnp.where(key_row < jnp.where(first, n_prev, 0), NEG_INF, s)
            if n_upto < 3 * rad:
                s = jnp.where(key_row >= jnp.where(last, n_upto, 3 * rad), NEG_INF, s)
            mx = jnp.max(s, axis=0, keepdims=True)
            probs.append((mx, jnp.exp(s - mx).astype(BF16)))
        outs = []
        for (di, dil, qstart), (mx, pe) in zip(group, probs):
            kstart = tile + qstart - rad * dil
            vb = vbuf[rows(kstart, 3 * rad, dil), :].astype(BF16)
            outs.append(_dot_tn(pe, jnp.concatenate([vb, ones_cols], axis=1)))
        for (di, dil, qstart), (mx, pe), ov in zip(group, probs, outs):
            dst = rows(qstart, rad, dil)
            m_col = jnp.broadcast_to(mx, (2 * rad, 128)).T
            acc_scr[di, dst, :] = jnp.where(head0, ov[0:rad, 0:128], ov[rad:2 * rad, 0:128])
            l_scr[di, dst, :] = jnp.where(head0, ov[0:rad, 128:256], ov[rad:2 * rad, 128:256])
            m_scr[di, dst, :] = jnp.where(head0, m_col[0:rad], m_col[rad:2 * rad])

    m_all = jnp.maximum(jnp.maximum(m_scr[0], m_scr[1]), m_scr[2])
    num = jnp.zeros((tile, 128), F32)
    den = jnp.zeros((tile, 128), F32)
    for di in range(len(ATT_DILATIONS)):
        e = jnp.exp(m_scr[di] - m_all)
        num = num + acc_scr[di] * e
        den = den + l_scr[di] * e
    y_ref[...] = (num / den).astype(BF16)


def _attention_mixer(p, q_g, k_g, rel_bias, seq_len):
    T = p.shape[0]
    tile = ATT_TILE
    n_tiles = T // tile
    rad = ATT_RADIUS
    seg = np.kron(np.eye(2), np.full((ATT_HEAD_DIM, ATT_HEAD_DIM), 1.0 / ATT_HEAD_DIM))
    mseg = jnp.asarray(seg, BF16)
    bkt_t = np.transpose(_t5_bucket_table(), (0, 2, 1))
    bkt = jnp.asarray(np.concatenate([bkt_t, bkt_t], axis=2))
    qg2 = jnp.tile(q_g, 2)[None]
    kg2 = jnp.tile(k_g, 2)[None]

    def blk(col0, shift):
        return pl.BlockSpec((tile, 128),
                            lambda hp, i: (jnp.clip(i + shift, 0, n_tiles - 1), col0 * 4 + hp))

    const = lambda shape: pl.BlockSpec(shape, lambda hp, i: (0,) * len(shape))
    return pl.pallas_call(
        functools.partial(_att_kernel, seq_len=seq_len),
        grid=(4, n_tiles),
        in_specs=[
            pl.BlockSpec(memory_space=pltpu.SMEM),
            const((3, 3 * rad, 2 * rad)), const((1, 128)), const((1, 128)), const((128, 128)),
            blk(1, 0),
            blk(2, 0), blk(2, 1),
            blk(3, -1), blk(3, 0), blk(3, 1),
        ],
        out_specs=pl.BlockSpec((tile, 128), lambda hp, i: (i, hp)),
        out_shape=jax.ShapeDtypeStruct((T, GROUP), BF16),
        scratch_shapes=[
            pltpu.VMEM((3, 3 * rad, 2 * rad), F32),
            pltpu.VMEM((tile, 128), F32),
            pltpu.VMEM((3 * tile, 128), F32),
            pltpu.VMEM((3 * tile, 128), F32),
            pltpu.VMEM((3, tile, 128), F32),
            pltpu.VMEM((3, tile, 128), F32),
            pltpu.VMEM((3, tile, 128), F32),
        ],
        compiler_params=_cparams("arbitrary", "arbitrary"),
        name="dilated_attention",
    )(rel_bias, bkt, qg2, kg2, mseg, p, p, p, p, p, p)


def _softplus(x):
    return jnp.maximum(x, 0.0) + jnp.log1p(jnp.exp(-jnp.abs(x)))


def _dn_prep_kernel(x_ref, xp_ref, xn_ref, w_ref, o_ref, ext_scr, *, tm, seq_len):
    i = pl.program_id(0)
    first, last, _ = _seq_edges(i, seq_len // tm)
    ext_scr[0:8, :] = jnp.where(first, jnp.zeros_like(xp_ref[...]), xp_ref[...])
    ext_scr[8:8 + tm, :] = x_ref[...]
    ext_scr[8 + tm:16 + tm, :] = jnp.where(last, jnp.zeros_like(xn_ref[...]), xn_ref[...])
    for cb in range(3 * DN_HEADS):
        lanes = slice(cb * 128, (cb + 1) * 128)
        acc = ext_scr[pl.ds(6, tm), lanes] * w_ref[0:1, lanes]
        for kk in range(1, 4):
            acc = acc + ext_scr[pl.ds(6 + kk, tm), lanes] * w_ref[kk:kk + 1, lanes]
        y = _silu(acc)
        if cb < 2 * DN_HEADS:
            y = y * lax.rsqrt(jnp.sum(y * y, axis=-1, keepdims=True) + EPS)
        if cb < DN_HEADS:
            y = y * (DN_HEAD_DIM ** -0.5)
        o_ref[:, lanes] = y


def _dn_prep(p, conv_w, seq_len, tm=512):
    T = p.shape[0]
    n_tiles = T // tm
    width = 3 * GROUP
    prev_spec, next_spec = _halo_specs(tm, 8, width, 2, n_tiles)
    return pl.pallas_call(
        functools.partial(_dn_prep_kernel, tm=tm, seq_len=seq_len),
        grid=(n_tiles,),
        in_specs=[pl.BlockSpec((tm, width), lambda i: (i, 2)), prev_spec, next_spec,
                  pl.BlockSpec((4, width), lambda i: (0, 0))],
        out_specs=pl.BlockSpec((tm, width), lambda i: (i, 0)),
        out_shape=jax.ShapeDtypeStruct((T, width), F32),
        scratch_shapes=[pltpu.VMEM((tm + 16, width), F32)],
        compiler_params=_cparams("parallel"),
        name="deltanet_prep",
    )(p, p, p, conv_w)


DN_GROUP = 8
DN_ROWS = DN_HEADS * DN_CHUNK


def _dn_scan_kernel(xf_ref, xb_ref, gf_ref, gb_ref, rf_ref, rb_ref, alc_ref, dtc_ref, alr_ref, dtr_ref,
                    of_ref, ob_ref, s_scr):
    C = DN_CHUNK
    H = DN_HEADS
    R = DN_GROUP * C

    @pl.when(pl.program_id(1) == 0)
    def _():
        s_scr[...] = jnp.zeros_like(s_scr)

    row_in_chunk = lax.rem(lax.broadcasted_iota(jnp.int32, (R, 1), 0), C)
    lane_in_chunk = lax.rem(lax.broadcasted_iota(jnp.int32, (1, DN_ROWS), 1), C)
    rid = lax.broadcasted_iota(jnp.int32, (DN_ROWS, DN_ROWS), 0)
    cid = lax.broadcasted_iota(jnp.int32, (DN_ROWS, DN_ROWS), 1)
    same_head = (rid // C) == (cid // C)

    def stack_heads(x, col0):
        return jnp.concatenate([x[:, col0 + h * 128:col0 + (h + 1) * 128] for h in range(H)], axis=0)

    def stack_cols(x, lane0, rows=None):
        parts = []
        for h in range(H):
            c = x[:, lane0 + h:lane0 + h + 1]
            if rows is not None:
                c = c[rows:rows + 1, :]
            parts.append(jnp.broadcast_to(c, (C, 128)))
        return jnp.concatenate(parts, axis=0)

    refs = ((xf_ref, gf_ref, rf_ref, of_ref), (xb_ref, gb_ref, rb_ref, ob_ref))
    gates = []
    for d, (x_ref, gc_ref, gr_ref, o_ref) in enumerate(refs):
        reverse = d == 1
        gcol = gc_ref[...]
        g = -jnp.exp(alc_ref[...]) * _softplus(gcol + dtc_ref[...])
        beta = _sigmoid(gcol)
        gcum = g
        for sh in (1, 2, 4, 8, 16, 32):
            if reverse:
                gcum = gcum + jnp.where(row_in_chunk < C - sh, pltpu.roll(gcum, R - sh, 0), 0.0)
            else:
                gcum = gcum + jnp.where(row_in_chunk >= sh, pltpu.roll(gcum, sh, 0), 0.0)
        grow_all = -jnp.exp(alr_ref[...])[None] * _softplus(gr_ref[...] + dtr_ref[...][None])
        for sh in (1, 2, 4, 8, 16, 32):
            if reverse:
                grow_all = grow_all + jnp.where(lane_in_chunk < C - sh,
                                                pltpu.roll(grow_all, DN_ROWS - sh, 2), 0.0)
            else:
                grow_all = grow_all + jnp.where(lane_in_chunk >= sh, pltpu.roll(grow_all, sh, 2), 0.0)

        gates.append((gcum, beta, grow_all))

    units = []
    for step in range(DN_GROUP):
        for d, (x_ref, _, _, o_ref) in enumerate(refs):
            reverse = d == 1
            ci = DN_GROUP - 1 - step if reverse else step
            gcum, beta, grow_all = gates[d]
            rows = slice(ci * C, (ci + 1) * C)
            x = x_ref[rows, :]
            u = dict(d=d, rows=rows, o_ref=o_ref)
            u["q"] = stack_heads(x, 0)
            u["k"] = stack_heads(x, GROUP)
            v_st = stack_heads(x, 2 * GROUP)
            gc_c = gcum[rows, :]
            beta_st = stack_cols(beta[rows, :], 8 + d * H)
            gcol_st = stack_cols(gc_c, d * H)
            glast_st = stack_cols(gc_c, d * H, rows=0 if reverse else C - 1)
            grow = grow_all[ci, d:d + 1, :]
            tri = (rid <= cid) if reverse else (rid >= cid)
            u["incl"] = same_head & tri
            u["decay"] = jnp.where(u["incl"],
                                   jnp.exp(jnp.concatenate([gcol_st, gcol_st], axis=1) - grow), 0.0)
            kb_st = u["k"] * beta_st
            eg = jnp.exp(gcol_st)
            u["kq"] = jnp.concatenate([kb_st, u["q"]], axis=0).astype(BF16)
            u["rhs"] = jnp.concatenate([v_st * beta_st, kb_st * eg], axis=1).astype(BF16)
            u["qdec"] = u["q"] * eg
            u["kdec"] = (u["k"] * jnp.exp(glast_st - gcol_st)).astype(BF16)
            u["gl"] = jnp.exp(glast_st)
            units.append(u)

    for u in units:
        u["kk"] = _dot_nt(u["kq"], u["k"].astype(BF16))
    def side_by_side(m):
        return (m[0:C] + m[C:2 * C]) + (m[2 * C:3 * C] + m[3 * C:4 * C])

    def block_diag(m):
        return jnp.where(same_head, jnp.concatenate([m] * H, axis=0), jnp.zeros((), m.dtype))

    eye_sbs = (lax.broadcasted_iota(jnp.int32, (C, DN_ROWS), 0)
               == lax.rem(lax.broadcasted_iota(jnp.int32, (C, DN_ROWS), 1), C)).astype(F32)
    for u in units:
        strict = u["incl"] & (rid != cid)
        a = jnp.where(strict, u["kk"][0:DN_ROWS] * u["decay"], 0.0)
        u["attn"] = (u["kk"][DN_ROWS:2 * DN_ROWS] * u["decay"]).astype(BF16)
        a_sbs = side_by_side(a)
        u["pinv"] = eye_sbs - a_sbs
        u["a_sbs"] = a_sbs.astype(BF16)
        u["a"] = a.astype(BF16)
    for u in units:
        u["apow"] = _dot(u["a_sbs"], u["a"])
    for _ in range(4):
        for u in units:
            ap = u["apow"].astype(BF16)
            u["both"] = _dot(jnp.concatenate([u["pinv"].astype(BF16), ap], axis=0), block_diag(ap))
        for u in units:
            u["pinv"] = u["pinv"] + u["both"][0:C]
            u["apow"] = u["both"][C:2 * C]
    for u in units:
        u["last"] = _dot(u["pinv"].astype(BF16), block_diag(u["apow"].astype(BF16)))
    for u in units:
        u["uw"] = _dot(block_diag((u["pinv"] + u["last"]).astype(BF16)), u["rhs"])

    for step in range(DN_GROUP):
        pair = units[2 * step:2 * step + 2]
        for u in pair:
            d = u["d"]
            u["wq"] = []
            for h in range(H):
                hr = slice(h * C, (h + 1) * C)
                lhs = jnp.concatenate([u["uw"][hr, 128:256], u["qdec"][hr]], axis=0).astype(BF16)
                u["wq"].append(_dot(lhs, s_scr[d * H + h].astype(BF16)))
        for u in pair:
            u["vnew"] = [u["uw"][h * C:(h + 1) * C, 0:128] - u["wq"][h][0:C] for h in range(H)]
            vnew_st = jnp.concatenate(u["vnew"], axis=0).astype(BF16)
            u["o"] = jnp.concatenate([w[C:2 * C] for w in u["wq"]], axis=0) + _dot(u["attn"], vnew_st)
        for u in pair:
            d = u["d"]
            for h in range(H):
                hr = slice(h * C, (h + 1) * C)
                s_scr[d * H + h] = (s_scr[d * H + h] * u["gl"][h * C:h * C + 1, :]
                                    + _dot_tn(u["kdec"][hr], u["vnew"][h].astype(BF16)))
                u["o_ref"][u["rows"], h * 128:(h + 1) * 128] = u["o"][hr]


def _dn_scan(qkvn, pg, a_log, dt_bias, batch, seq_len):
    T = qkvn.shape[0]
    C, H, G = DN_CHUNK, DN_HEADS, DN_GROUP
    R = G * C
    nc = seq_len // C
    ncg = nc // G
    ab = pg[:, 0:4 * H].reshape(batch, nc, C, 2, 2, H)
    ab_row = jnp.transpose(ab, (0, 1, 3, 4, 5, 2)).reshape(batch * nc, 4, H * C)
    ab_row = jnp.pad(ab_row, ((0, 0), (0, 4), (0, 0)))
    pad_lanes = lambda v: jnp.pad(v.reshape(1, 2 * H), ((0, 0), (0, GATE_LANES - 2 * H)))
    row_param = lambda v: jnp.pad(jnp.repeat(v, C, axis=1), ((0, 6), (0, 0)))

    fwd = lambda b, c: (b * ncg + c, 0)
    bwd = lambda b, c: (b * ncg + ncg - 1 - c, 0)
    fwd3 = lambda b, c: (b * ncg + c, 0, 0)
    bwd3 = lambda b, c: (b * ncg + ncg - 1 - c, 0, 0)
    const = lambda shape: pl.BlockSpec(shape, lambda b, c: (0,) * len(shape))
    return pl.pallas_call(
        _dn_scan_kernel,
        grid=(batch, ncg),
        in_specs=[
            pl.BlockSpec((R, 3 * GROUP), fwd), pl.BlockSpec((R, 3 * GROUP), bwd),
            pl.BlockSpec((R, GATE_LANES), fwd), pl.BlockSpec((R, GATE_LANES), bwd),
            pl.BlockSpec((G, 8, H * C), fwd3), pl.BlockSpec((G, 8, H * C), bwd3),
            const((1, GATE_LANES)), const((1, GATE_LANES)), const((8, H * C)), const((8, H * C)),
        ],
        out_specs=[pl.BlockSpec((R, GROUP), fwd), pl.BlockSpec((R, GROUP), bwd)],
        out_shape=[jax.ShapeDtypeStruct((T, GROUP), F32), jax.ShapeDtypeStruct((T, GROUP), F32)],
        scratch_shapes=[pltpu.VMEM((2 * H, DN_HEAD_DIM, DN_HEAD_DIM), F32)],
        compiler_params=_cparams("arbitrary", "arbitrary"),
        name="deltanet_scan",
    )(qkvn, qkvn, pg, pg, ab_row, ab_row, pad_lanes(a_log), pad_lanes(dt_bias),
      row_param(a_log), row_param(dt_bias))


def _dn_post_kernel(of_ref, ob_ref, z_ref, g_ref, y_ref):
    for h in range(DN_HEADS):
        lanes = slice(h * 128, (h + 1) * 128)
        o = of_ref[:, lanes] + ob_ref[:, lanes]
        o = o * lax.rsqrt(jnp.mean(o * o, axis=-1, keepdims=True) + EPS) * g_ref[...]
        y_ref[:, lanes] = (o * _silu(z_ref[:, lanes])).astype(BF16)


def _dn_post(o_f, o_b, p, norm_g, tm=1024):
    T = p.shape[0]
    row = lambda cb: pl.BlockSpec((tm, GROUP), lambda i: (i, cb))
    return pl.pallas_call(
        _dn_post_kernel,
        grid=(T // tm,),
        in_specs=[row(0), row(0), row(9), pl.BlockSpec((1, DN_HEAD_DIM), lambda i: (0, 0))],
        out_specs=row(0),
        out_shape=jax.ShapeDtypeStruct((T, GROUP), BF16),
        compiler_params=_cparams("parallel"),
        name="deltanet_post",
    )(o_f, o_b, p, norm_g)


def _deltanet_mixer(p, pg, conv_w, a_log, dt_bias, norm_g, batch, seq_len):
    qkvn = _dn_prep(p, conv_w, seq_len)
    o_f, o_b = _dn_scan(qkvn, pg, a_log, dt_bias, batch, seq_len)
    return _dn_post(o_f, o_b, p, norm_g[None])


def _outproj_kernel(x_ref, ya_ref, yb_ref, yc_ref, yd_ref, w_ref, g_ref, xn_ref, hn_ref, ycat_scr):
    for gi, y_ref in enumerate((ya_ref, yb_ref, yc_ref, yd_ref)):
        ycat_scr[:, gi * GROUP:(gi + 1) * GROUP] = y_ref[...]
    acc = x_ref[...] + _dot(ycat_scr[...], w_ref[...])
    xn_ref[...] = acc
    ms = jnp.mean(acc * acc, axis=-1, keepdims=True)
    hn_ref[...] = (acc * lax.rsqrt(ms + EPS) * g_ref[...]).astype(BF16)


def _out_proj(x2, ys, w_out, norm2_g, tm=512):
    T = x2.shape[0]
    row = lambda width: pl.BlockSpec((tm, width), lambda i: (i, 0))
    return pl.pallas_call(
        _outproj_kernel,
        grid=(T // tm,),
        in_specs=[row(D_MODEL), row(GROUP), row(GROUP), row(GROUP), row(GROUP),
                  pl.BlockSpec((D_MODEL, D_MODEL), lambda i: (0, 0), pipeline_mode=pl.Buffered(1)),
                  pl.BlockSpec((1, D_MODEL), lambda i: (0, 0))],
        out_specs=[row(D_MODEL), row(D_MODEL)],
        out_shape=[jax.ShapeDtypeStruct((T, D_MODEL), F32),
                   jax.ShapeDtypeStruct((T, D_MODEL), BF16)],
        scratch_shapes=[pltpu.VMEM((tm, D_MODEL), BF16)],
        compiler_params=_cparams("parallel"),
        name="out_proj",
    )(x2, *ys, w_out, norm2_g)


FFN_HALF = FFN_HIDDEN // 2
FFN_COL_CHUNK = 256


def _ffn_hidden_kernel(hn_ref, hp_ref, hx_ref, wg_ref, wu_ref, dw_ref, db_ref, a_ref, hext_scr, g_scr,
                       *, tm, seq_len):
    first, last, _ = _seq_edges(pl.program_id(1), seq_len // tm)
    hext_scr[0:16, :] = jnp.where(first, jnp.zeros_like(hp_ref[...]), hp_ref[...])
    hext_scr[16:16 + tm, :] = hn_ref[...]
    hext_scr[16 + tm:32 + tm, :] = jnp.where(last, jnp.zeros_like(hx_ref[...]), hx_ref[...])
    for c in range(FFN_HALF // FFN_COL_CHUNK):
        cols = slice(c * FFN_COL_CHUNK, (c + 1) * FFN_COL_CHUNK)
        g = g_scr.at[c % 2]
        g[...] = _dot(hext_scr[...], wg_ref[:, cols])
        up = _dot(hn_ref[...], wu_ref[:, cols])
        gate = (g[pl.ds(15, tm), :] * dw_ref[0:1, cols] + g[pl.ds(16, tm), :] * dw_ref[1:2, cols]
                + g[pl.ds(17, tm), :] * dw_ref[2:3, cols] + db_ref[:, cols])
        a_ref[:, cols] = (_silu(gate) * up).astype(BF16)


def _ffn_down_kernel(a_ref, xn_ref, wd_ref, o_ref):
    o_ref[...] = xn_ref[...] + _dot(a_ref[...], wd_ref[...])


def _ffn(hn, xn, w_gate, dw_w, dw_b, w_up, w_down, seq_len, tm_hidden=1024, tm_down=256):
    T = hn.shape[0]
    n_tiles = T // tm_hidden
    r = tm_hidden // 16
    half_cols = lambda rows: pl.BlockSpec((rows, FFN_HALF), lambda h, i: (0, h),
                                          pipeline_mode=pl.Buffered(1))
    act = pl.pallas_call(
        functools.partial(_ffn_hidden_kernel, tm=tm_hidden, seq_len=seq_len),
        grid=(2, n_tiles),
        in_specs=[
            pl.BlockSpec((tm_hidden, D_MODEL), lambda h, i: (i, 0)),
            pl.BlockSpec((16, D_MODEL), lambda h, i: (jnp.maximum(i * r - 1, 0), 0)),
            pl.BlockSpec((16, D_MODEL), lambda h, i: (jnp.minimum((i + 1) * r, n_tiles * r - 1), 0)),
            half_cols(D_MODEL), half_cols(D_MODEL), half_cols(3), half_cols(1),
        ],
        out_specs=pl.BlockSpec((tm_hidden, FFN_HALF), lambda h, i: (i, h)),
        out_shape=jax.ShapeDtypeStruct((T, FFN_HIDDEN), BF16),
        scratch_shapes=[pltpu.VMEM((tm_hidden + 32, D_MODEL), BF16),
                        pltpu.VMEM((2, tm_hidden + 32, FFN_COL_CHUNK), F32)],
        compiler_params=_cparams("arbitrary", "arbitrary"),
        name="ffn_hidden",
    )(hn, hn, hn, w_gate, w_up, dw_w, dw_b)
    return pl.pallas_call(
        _ffn_down_kernel,
        grid=(T // tm_down,),
        in_specs=[
            pl.BlockSpec((tm_down, FFN_HIDDEN), lambda i: (i, 0)),
            pl.BlockSpec((tm_down, D_MODEL), lambda i: (i, 0)),
            pl.BlockSpec((FFN_HIDDEN, D_MODEL), lambda i: (0, 0), pipeline_mode=pl.Buffered(1)),
        ],
        out_specs=pl.BlockSpec((tm_down, D_MODEL), lambda i: (i, 0)),
        out_shape=jax.ShapeDtypeStruct((T, D_MODEL), F32),
        compiler_params=_cparams("parallel"),
        name="ffn_down",
    )(act, xn, w_down)


def _layer(x2, l, batch, seq_len, rel_bias, norm1_g, w_in, w_pool, pool_scale, att_q_g, att_k_g,
           conv_dw_w, conv_dw_b, conv_ln_g, conv_ln_b, conv_pw, dn_conv_w, dn_a_log, dn_dt_bias,
           dn_norm_g, w_out, norm2_g, ffn_w_gate, ffn_dw_w, ffn_dw_b, ffn_w_up, ffn_w_down):
    w_main = _weight_bf16(w_in, l, cols=PROJ_MAIN)
    w_gates = _gate_weight_bf16(w_in, l)
    p, pg = _in_proj(x2, norm1_g[l][None], w_main, w_gates)
    ya = _pool_mixer(p, w_pool[l].astype(BF16), pool_scale[l][None], seq_len)
    yb = _attention_mixer(p, att_q_g[l], att_k_g[l], rel_bias, seq_len)
    yc = _conformer_mixer(p, conv_dw_w[l], conv_dw_b[l][None], conv_ln_g[l][None], conv_ln_b[l][None],
                          conv_pw[l].astype(BF16), seq_len)
    yd = _deltanet_mixer(p, pg, dn_conv_w[l], dn_a_log[l], dn_dt_bias[l], dn_norm_g[l], batch, seq_len)
    xn, hn = _out_proj(x2, (ya, yb, yc, yd), _weight_bf16(w_out, l), norm2_g[l][None])
    return _ffn(hn, xn, _weight_bf16(ffn_w_gate, l), ffn_dw_w[l], ffn_dw_b[l][None],
                _weight_bf16(ffn_w_up, l), _weight_bf16(ffn_w_down, l), seq_len)


def kernel(x, rel_bias, norm1_g, w_in, w_pool, pool_scale, att_q_g, att_k_g, conv_dw_w, conv_dw_b,
           conv_ln_g, conv_ln_b, conv_pw, dn_conv_w, dn_a_log, dn_dt_bias, dn_norm_g, w_out, norm2_g,
           ffn_w_gate, ffn_dw_w, ffn_dw_b, ffn_w_up, ffn_w_down):
    batch, seq_len, _ = x.shape
    x2 = x.reshape(batch * seq_len, D_MODEL)
    for l in range(norm1_g.shape[0]):
        x2 = _layer(x2, l, batch, seq_len, rel_bias, norm1_g, w_in, w_pool, pool_scale, att_q_g,
                    att_k_g, conv_dw_w, conv_dw_b, conv_ln_g, conv_ln_b, conv_pw, dn_conv_w, dn_a_log,
                    dn_dt_bias, dn_norm_g, w_out, norm2_g, ffn_w_gate, ffn_dw_w, ffn_dw_b, ffn_w_up,
                    ffn_w_down)
    return x2.reshape(batch, seq_len, D_MODEL)
```

```python
import functools
import math

import jax
import jax.numpy as jnp
import numpy as np
from jax import lax
from jax.experimental import pallas as pl
from jax.experimental.pallas import tpu as pltpu

F32 = jnp.float32
BF16 = jnp.bfloat16

D_MODEL = 2048
GROUP = 512
POOL_WINDOWS = (2, 4, 8, 16)
ATT_HEAD_DIM = 64
ATT_HEADS = 8
ATT_RADIUS = 64
ATT_DILATIONS = (1, 4, 16)
REL_BUCKETS = 32
REL_MAX_DIST = 1024
CONV_WIDTH = 31
DN_HEAD_DIM = 128
DN_HEADS = 4
DN_CHUNK = 64
FFN_HIDDEN = 5632
EPS = 1e-6
NEG_INF = -1e30

PROJ_MAIN = 10 * GROUP
GATE_LANES = 128

VMEM_LIMIT_BYTES = 56 * 1024 * 1024


def _cparams(*sem):
    return pltpu.CompilerParams(dimension_semantics=sem, vmem_limit_bytes=VMEM_LIMIT_BYTES)


def _sigmoid(x):
    return 1.0 / (1.0 + jnp.exp(-x))


def _silu(x):
    return x * _sigmoid(x)


def _dot(a, b):
    return jnp.dot(a, b, preferred_element_type=F32)


def _dot_nt(a, b):
    return lax.dot_general(a, b, (((1,), (1,)), ((), ())), preferred_element_type=F32)


def _dot_tn(a, b):
    return lax.dot_general(a, b, (((0,), (0,)), ((), ())), preferred_element_type=F32)


def _cast_kernel(w_ref, o_ref):
    o_ref[...] = w_ref[...].astype(BF16)


def _weight_bf16(w_stack, layer, cols=None, tr=512):
    _, rows, width = w_stack.shape
    cols = width if cols is None else cols
    tc = cols // 2 if cols >= 4096 else cols
    return pl.pallas_call(
        _cast_kernel,
        grid=(rows // tr, cols // tc),
        in_specs=[pl.BlockSpec((None, tr, tc), lambda i, j: (layer, i, j))],
        out_specs=pl.BlockSpec((tr, tc), lambda i, j: (i, j)),
        out_shape=jax.ShapeDtypeStruct((rows, cols), BF16),
        compiler_params=_cparams("parallel", "parallel"),
        name="weight_bf16",
    )(w_stack)


def _gate_cols_kernel(w_ref, o_ref, *, valid):
    lane = lax.broadcasted_iota(jnp.int32, o_ref.shape, 1)
    o_ref[...] = jnp.where(lane < valid, w_ref[...], 0.0).astype(BF16)


def _gate_weight_bf16(w_stack, layer):
    _, rows, width = w_stack.shape
    valid = width - PROJ_MAIN
    return pl.pallas_call(
        functools.partial(_gate_cols_kernel, valid=valid),
        grid=(1,),
        in_specs=[pl.BlockSpec((None, rows, GATE_LANES), lambda i: (layer, 0, PROJ_MAIN // GATE_LANES))],
        out_specs=pl.BlockSpec((rows, GATE_LANES), lambda i: (0, 0)),
        out_shape=jax.ShapeDtypeStruct((rows, GATE_LANES), BF16),
        compiler_params=_cparams("arbitrary"),
        name="gate_weight_bf16",
    )(w_stack)


def _inproj_kernel(x_ref, g_ref, w_ref, wg_ref, p_ref, pg_ref):
    x = x_ref[...]
    ms = jnp.mean(x * x, axis=-1, keepdims=True)
    h = (x * lax.rsqrt(ms + EPS) * g_ref[...]).astype(BF16)
    pg_ref[...] = _dot(h, wg_ref[...])
    p_ref[...] = _dot(h, w_ref[...])


def _in_proj(x2, norm_g, w_main, w_gate, tm=256):
    T = x2.shape[0]
    resident = lambda shape: pl.BlockSpec(shape, lambda i: (0, 0), pipeline_mode=pl.Buffered(1))
    return pl.pallas_call(
        _inproj_kernel,
        grid=(T // tm,),
        in_specs=[
            pl.BlockSpec((tm, D_MODEL), lambda i: (i, 0)),
            resident((1, D_MODEL)),
            resident((D_MODEL, PROJ_MAIN)),
            resident((D_MODEL, GATE_LANES)),
        ],
        out_specs=[
            pl.BlockSpec((tm, PROJ_MAIN), lambda i: (i, 0)),
            pl.BlockSpec((tm, GATE_LANES), lambda i: (i, 0)),
        ],
        out_shape=[
            jax.ShapeDtypeStruct((T, PROJ_MAIN), F32),
            jax.ShapeDtypeStruct((T, GATE_LANES), F32),
        ],
        compiler_params=_cparams("parallel"),
        name="in_proj",
    )(x2, norm_g, w_main, w_gate)


def _halo_specs(tm, halo, width, col_block, n_tiles):
    r = tm // halo
    last = n_tiles * r - 1
    prev_spec = pl.BlockSpec((halo, width), lambda i, *_: (jnp.maximum(i * r - 1, 0), col_block))
    next_spec = pl.BlockSpec((halo, width), lambda i, *_: (jnp.minimum((i + 1) * r, last), col_block))
    return prev_spec, next_spec


def _seq_edges(i, tiles_per_seq):
    k = lax.rem(i, tiles_per_seq)
    return k == 0, k == tiles_per_seq - 1, k


def _pool_kernel(u_ref, up_ref, un_ref, w_ref, sc_ref, y_ref, ext_scr, *, tm, seq_len):
    i = pl.program_id(0)
    first, last, k = _seq_edges(i, seq_len // tm)
    ext_scr[0:8, :] = jnp.where(first, jnp.zeros_like(up_ref[...]), up_ref[...])
    ext_scr[8:8 + tm, :] = u_ref[...]
    ext_scr[8 + tm:16 + tm, :] = jnp.where(last, jnp.zeros_like(un_ref[...]), un_ref[...])
    t = k * tm + lax.broadcasted_iota(jnp.int32, (tm, 1), 0)
    for gi, win in enumerate(POOL_WINDOWS):
        half = win // 2
        lanes = slice(gi * 128, (gi + 1) * 128)
        s = ext_scr[pl.ds(8 - half, tm), lanes]
        for kk in range(1, win):
            s = s + ext_scr[pl.ds(8 - half + kk, tm), lanes]
        cnt = (jnp.minimum(t + half, seq_len) - jnp.maximum(t - half, 0)).astype(F32)
        pooled = s / cnt - ext_scr[pl.ds(8, tm), lanes]
        y = _dot(pooled.astype(BF16), w_ref[gi]) * sc_ref[:, lanes]
        y_ref[:, lanes] = y.astype(BF16)


def _pool_mixer(p, w_pool, pool_scale, seq_len, tm=1024):
    T = p.shape[0]
    n_tiles = T // tm
    prev_spec, next_spec = _halo_specs(tm, 8, GROUP, 0, n_tiles)
    return pl.pallas_call(
        functools.partial(_pool_kernel, tm=tm, seq_len=seq_len),
        grid=(n_tiles,),
        in_specs=[
            pl.BlockSpec((tm, GROUP), lambda i: (i, 0)),
            prev_spec,
            next_spec,
            pl.BlockSpec((4, 128, 128), lambda i: (0, 0, 0)),
            pl.BlockSpec((1, GROUP), lambda i: (0, 0)),
        ],
        out_specs=pl.BlockSpec((tm, GROUP), lambda i: (i, 0)),
        out_shape=jax.ShapeDtypeStruct((T, GROUP), BF16),
        scratch_shapes=[pltpu.VMEM((tm + 16, GROUP), F32)],
        compiler_params=_cparams("parallel"),
        name="pool_mixer",
    )(p, p, p, w_pool, pool_scale)


def _conformer_kernel(v_ref, vp_ref, vn_ref, g_ref, gp_ref, gn_ref, dw_ref, db_ref, lg_ref, lb_ref,
                      pw_ref, y_ref, ext_scr, shift_scr, *, tm, seq_len):
    i = pl.program_id(0)
    first, last, _ = _seq_edges(i, seq_len // tm)
    hp = vp_ref[...] * _sigmoid(gp_ref[...])
    hn = vn_ref[...] * _sigmoid(gn_ref[...])
    ext_scr[0:16, :] = jnp.where(first, jnp.zeros_like(hp), hp)
    ext_scr[16:16 + tm, :] = v_ref[...] * _sigmoid(g_ref[...])
    ext_scr[16 + tm:32 + tm, :] = jnp.where(last, jnp.zeros_like(hn), hn)
    base = 16 - CONV_WIDTH // 2
    acc = db_ref[...]
    for b in range(8):
        taps = [(a, 8 * a + b - base) for a in range(5) if 0 <= 8 * a + b - base < CONV_WIDTH]
        rows = tm + 8 * taps[-1][0]
        shift_scr[b, 0:rows, :] = ext_scr[pl.ds(b, rows), :]
        for a, kk in taps:
            acc = acc + shift_scr[b, 8 * a:8 * a + tm, :] * dw_ref[kk:kk + 1, :]
    mu = jnp.mean(acc, axis=-1, keepdims=True)
    xc = acc - mu
    var = jnp.mean(xc * xc, axis=-1, keepdims=True)
    h = _silu(xc * lax.rsqrt(var + EPS) * lg_ref[...] + lb_ref[...])
    y_ref[...] = _dot(h.astype(BF16), pw_ref[...]).astype(BF16)


def _conformer_mixer(p, dw_w, dw_b, ln_g, ln_b, pw, seq_len, tm=512):
    T = p.shape[0]
    n_tiles = T // tm
    vprev, vnext = _halo_specs(tm, 16, GROUP, 4, n_tiles)
    gprev, gnext = _halo_specs(tm, 16, GROUP, 5, n_tiles)
    const = lambda shape: pl.BlockSpec(shape, lambda i: (0,) * len(shape))
    return pl.pallas_call(
        functools.partial(_conformer_kernel, tm=tm, seq_len=seq_len),
        grid=(n_tiles,),
        in_specs=[
            pl.BlockSpec((tm, GROUP), lambda i: (i, 4)), vprev, vnext,
            pl.BlockSpec((tm, GROUP), lambda i: (i, 5)), gprev, gnext,
            const((CONV_WIDTH, GROUP)), const((1, GROUP)), const((1, GROUP)), const((1, GROUP)),
            const((GROUP, GROUP)),
        ],
        out_specs=pl.BlockSpec((tm, GROUP), lambda i: (i, 0)),
        out_shape=jax.ShapeDtypeStruct((T, GROUP), BF16),
        scratch_shapes=[pltpu.VMEM((tm + 32, GROUP), F32), pltpu.VMEM((8, tm + 32, GROUP), F32)],
        compiler_params=_cparams("parallel"),
        name="conformer_mixer",
    )(p, p, p, p, p, p, dw_w, dw_b, ln_g, ln_b, pw)


ATT_TILE = 1024
ATT_BLOCK_GROUP = 8


def _t5_bucket_table():
    nb = REL_BUCKETS // 2
    max_exact = nb // 2
    i = np.arange(ATT_RADIUS)[:, None]
    j = np.arange(3 * ATT_RADIUS)[None, :]
    off = j - ATT_RADIUS - i
    tables = []
    for dil in ATT_DILATIONS:
        rel = off * dil
        n = np.abs(rel)
        nf = np.maximum(n, 1).astype(np.float32)
        large = max_exact + (np.log(nf / np.float32(max_exact)) / np.float32(math.log(REL_MAX_DIST / max_exact))
                             * np.float32(nb - max_exact)).astype(np.int32)
        large = np.minimum(large, nb - 1)
        bucket = np.where(rel > 0, nb, 0) + np.where(n < max_exact, n, large)
        tables.append(np.where(np.abs(off) <= ATT_RADIUS, bucket, -1))
    return np.stack(tables).astype(np.int32)


def _att_kernel(rb_ref, bkt_ref, qg_ref, kg_ref, mseg_ref, q_ref, kc_ref, kn_ref,
                vp_ref, vc_ref, vn_ref, y_ref, bias_scr, qbuf, kbuf, vbuf, acc_scr, m_scr, l_scr,
                *, seq_len):
    tile = ATT_TILE
    rad = ATT_RADIUS
    hp = pl.program_id(0)
    i = pl.program_id(1)

    def rms(x, g):
        ms = _dot((x * x).astype(BF16), mseg_ref[...])
        return x * lax.rsqrt(ms + EPS) * g

    @pl.when(i == 0)
    def _():
        lane_head0 = lax.broadcasted_iota(jnp.int32, (1, 128), 1) < ATT_HEAD_DIM
        for di in range(len(ATT_DILATIONS)):
            bkt = bkt_ref[di]
            b = jnp.zeros(bkt.shape, F32)
            for bb in range(REL_BUCKETS):
                b = jnp.where(bkt == bb, jnp.where(lane_head0, rb_ref[bb, 2 * hp], rb_ref[bb, 2 * hp + 1]), b)
            bias_scr[di] = jnp.where(bkt < 0, NEG_INF, b)
        kbuf[0:tile, :] = jnp.zeros((tile, 128), F32)
        kbuf[tile:2 * tile, :] = rms(kc_ref[...], kg_ref[...])

    @pl.when(i > 0)
    def _():
        kbuf[0:tile, :] = kbuf[tile:2 * tile, :]
        kbuf[tile:2 * tile, :] = kbuf[2 * tile:3 * tile, :]

    first, last, _ = _seq_edges(i, seq_len // tile)

    qbuf[...] = rms(q_ref[...], qg_ref[...]) * (ATT_HEAD_DIM ** -0.5)
    kbuf[2 * tile:3 * tile, :] = rms(kn_ref[...], kg_ref[...])
    vbuf[0:tile, :] = vp_ref[...]
    vbuf[tile:2 * tile, :] = vc_ref[...]
    vbuf[2 * tile:3 * tile, :] = vn_ref[...]

    head0 = lax.broadcasted_iota(jnp.int32, (1, 128), 1) < ATT_HEAD_DIM
    key_row = lax.broadcasted_iota(jnp.int32, (3 * rad, 1), 0)
    ones_cols = jnp.ones((3 * rad, 128), BF16)

    def rows(start, size, dil):
        return pl.ds(start, size) if dil == 1 else pl.ds(start, size, stride=dil)

    blocks = [(di, dil, r + rad * dil * m)
              for di, dil in enumerate(ATT_DILATIONS) for r in range(dil) for m in range(tile // (rad * dil))]
    for g0 in range(0, len(blocks), ATT_BLOCK_GROUP):
        group = blocks[g0:g0 + ATT_BLOCK_GROUP]
        scores = []
        for di, dil, qstart in group:
            kstart = tile + qstart - rad * dil
            qb = qbuf[rows(qstart, rad, dil), :]
            kb = kbuf[rows(kstart, 3 * rad, dil), :].astype(BF16)
            q2 = jnp.concatenate([jnp.where(head0, qb, 0.0), jnp.where(head0, 0.0, qb)], axis=0)
            scores.append(_dot_nt(kb, q2.astype(BF16)))
        probs = []
        for (di, dil, qstart), s in zip(group, scores):
            kstart = tile + qstart - rad * dil
            s = s + bias_scr[di]
            n_prev = max(0, -(-(tile - kstart) // dil))
            n_upto = min(3 * rad, -(-(2 * tile - kstart) // dil))
            if n_prev > 0:
                s = jnp.where(key_row < jnp.where(first, n_prev, 0), NEG_INF, s)
            if n_upto < 3 * rad:
                s = jnp.where(key_row >= jnp.where(last, n_upto, 3 * rad), NEG_INF, s)
            mx = jnp.max(s, axis=0, keepdims=True)
            probs.append((mx, jnp.exp(s - mx).astype(BF16)))
        outs = []
        for (di, dil, qstart), (mx, pe) in zip(group, probs):
            kstart = tile + qstart - rad * dil
            vb = vbuf[rows(kstart, 3 * rad, dil), :].astype(BF16)
            outs.append(_dot_tn(pe, jnp.concatenate([vb, ones_cols], axis=1)))
        for (di, dil, qstart), (mx, pe), ov in zip(group, probs, outs):
            dst = rows(qstart, rad, dil)
            m_col = jnp.broadcast_to(mx, (2 * rad, 128)).T
            acc_scr[di, dst, :] = jnp.where(head0, ov[0:rad, 0:128], ov[rad:2 * rad, 0:128])
            l_scr[di, dst, :] = jnp.where(head0, ov[0:rad, 128:256], ov[rad:2 * rad, 128:256])
            m_scr[di, dst, :] = jnp.where(head0, m_col[0:rad], m_col[rad:2 * rad])

    m_all = jnp.maximum(jnp.maximum(m_scr[0], m_scr[1]), m_scr[2])
    num = jnp.zeros((tile, 128), F32)
    den = jnp.zeros((tile, 128), F32)
    for di in range(len(ATT_DILATIONS)):
        e = jnp.exp(m_scr[di] - m_all)
        num = num + acc_scr[di] * e
        den = den + l_scr[di] * e
    y_ref[...] = (num / den).astype(BF16)


def _attention_mixer(p, q_g, k_g, rel_bias, seq_len):
    T = p.shape[0]
    tile = ATT_TILE
    n_tiles = T // tile
    rad = ATT_RADIUS
    seg = np.kron(np.eye(2), np.full((ATT_HEAD_DIM, ATT_HEAD_DIM), 1.0 / ATT_HEAD_DIM))
    mseg = jnp.asarray(seg, BF16)
    bkt_t = np.transpose(_t5_bucket_table(), (0, 2, 1))
    bkt = jnp.asarray(np.concatenate([bkt_t, bkt_t], axis=2))
    qg2 = jnp.tile(q_g, 2)[None]
    kg2 = jnp.tile(k_g, 2)[None]

    def blk(col0, shift):
        return pl.BlockSpec((tile, 128),
                            lambda hp, i: (jnp.clip(i + shift, 0, n_tiles - 1), col0 * 4 + hp))

    const = lambda shape: pl.BlockSpec(shape, lambda hp, i: (0,) * len(shape))
    return pl.pallas_call(
        functools.partial(_att_kernel, seq_len=seq_len),
        grid=(4, n_tiles),
        in_specs=[
            pl.BlockSpec(memory_space=pltpu.SMEM),
            const((3, 3 * rad, 2 * rad)), const((1, 128)), const((1, 128)), const((128, 128)),
            blk(1, 0),
            blk(2, 0), blk(2, 1),
            blk(3, -1), blk(3, 0), blk(3, 1),
        ],
        out_specs=pl.BlockSpec((tile, 128), lambda hp, i: (i, hp)),
        out_shape=jax.ShapeDtypeStruct((T, GROUP), BF16),
        scratch_shapes=[
            pltpu.VMEM((3, 3 * rad, 2 * rad), F32),
            pltpu.VMEM((tile, 128), F32),
            pltpu.VMEM((3 * tile, 128), F32),
            pltpu.VMEM((3 * tile, 128), F32),
            pltpu.VMEM((3, tile, 128), F32),
            pltpu.VMEM((3, tile, 128), F32),
            pltpu.VMEM((3, tile, 128), F32),
        ],
        compiler_params=_cparams("arbitrary", "arbitrary"),
        name="dilated_attention",
    )(rel_bias, bkt, qg2, kg2, mseg, p, p, p, p, p, p)


def _softplus(x):
    return jnp.maximum(x, 0.0) + jnp.log1p(jnp.exp(-jnp.abs(x)))


def _dn_prep_kernel(x_ref, xp_ref, xn_ref, w_ref, o_ref, ext_scr, *, tm, seq_len):
    i = pl.program_id(0)
    first, last, _ = _seq_edges(i, seq_len // tm)
    ext_scr[0:8, :] = jnp.where(first, jnp.zeros_like(xp_ref[...]), xp_ref[...])
    ext_scr[8:8 + tm, :] = x_ref[...]
    ext_scr[8 + tm:16 + tm, :] = jnp.where(last, jnp.zeros_like(xn_ref[...]), xn_ref[...])
    for cb in range(3 * DN_HEADS):
        lanes = slice(cb * 128, (cb + 1) * 128)
        acc = ext_scr[pl.ds(6, tm), lanes] * w_ref[0:1, lanes]
        for kk in range(1, 4):
            acc = acc + ext_scr[pl.ds(6 + kk, tm), lanes] * w_ref[kk:kk + 1, lanes]
        y = _silu(acc)
        if cb < 2 * DN_HEADS:
            y = y * lax.rsqrt(jnp.sum(y * y, axis=-1, keepdims=True) + EPS)
        if cb < DN_HEADS:
            y = y * (DN_HEAD_DIM ** -0.5)
        o_ref[:, lanes] = y


def _dn_prep(p, conv_w, seq_len, tm=512):
    T = p.shape[0]
    n_tiles = T // tm
    width = 3 * GROUP
    prev_spec, next_spec = _halo_specs(tm, 8, width, 2, n_tiles)
    return pl.pallas_call(
        functools.partial(_dn_prep_kernel, tm=tm, seq_len=seq_len),
        grid=(n_tiles,),
        in_specs=[pl.BlockSpec((tm, width), lambda i: (i, 2)), prev_spec, next_spec,
                  pl.BlockSpec((4, width), lambda i: (0, 0))],
        out_specs=pl.BlockSpec((tm, width), lambda i: (i, 0)),
        out_shape=jax.ShapeDtypeStruct((T, width), F32),
        scratch_shapes=[pltpu.VMEM((tm + 16, width), F32)],
        compiler_params=_cparams("parallel"),
        name="deltanet_prep",
    )(p, p, p, conv_w)


DN_GROUP = 8
DN_ROWS = DN_HEADS * DN_CHUNK


def _dn_scan_kernel(xf_ref, xb_ref, gf_ref, gb_ref, rf_ref, rb_ref, alc_ref, dtc_ref, alr_ref, dtr_ref,
                    of_ref, ob_ref, s_scr):
    C = DN_CHUNK
    H = DN_HEADS
    R = DN_GROUP * C

    @pl.when(pl.program_id(1) == 0)
    def _():
        s_scr[...] = jnp.zeros_like(s_scr)

    row_in_chunk = lax.rem(lax.broadcasted_iota(jnp.int32, (R, 1), 0), C)
    lane_in_chunk = lax.rem(lax.broadcasted_iota(jnp.int32, (1, DN_ROWS), 1), C)
    rid = lax.broadcasted_iota(jnp.int32, (DN_ROWS, DN_ROWS), 0)
    cid = lax.broadcasted_iota(jnp.int32, (DN_ROWS, DN_ROWS), 1)
    same_head = (rid // C) == (cid // C)
    row_sbs = lax.broadcasted_iota(jnp.int32, (C, DN_ROWS), 0)
    col_sbs = lax.rem(lax.broadcasted_iota(jnp.int32, (C, DN_ROWS), 1), C)
    lane_head = lax.broadcasted_iota(jnp.int32, (1, DN_ROWS), 1) // C

    def block_diag(m):
        return jnp.where(same_head, jnp.concatenate([m] * H, axis=0), jnp.zeros((), m.dtype))

    def side_by_side_blocks(x):
        return jnp.concatenate([x[h * C:(h + 1) * C] for h in range(H)], axis=1)

    def block_diag_blocks(x):
        zero = jnp.zeros((C, 128), x.dtype)
        return jnp.concatenate(
            [jnp.concatenate([x[h * C:(h + 1) * C] if g == h else zero for g in range(H)], axis=1)
             for h in range(H)], axis=0)

    def stack_heads(x, col0):
        return jnp.concatenate([x[:, col0 + h * 128:col0 + (h + 1) * 128] for h in range(H)], axis=0)

    def stack_cols(x, lane0, rows=None):
        parts = []
        for h in range(H):
            c = x[:, lane0 + h:lane0 + h + 1]
            if rows is not None:
                c = c[rows:rows + 1, :]
            parts.append(jnp.broadcast_to(c, (C, 128)))
        return jnp.concatenate(parts, axis=0)

    refs = ((xf_ref, gf_ref, rf_ref, of_ref), (xb_ref, gb_ref, rb_ref, ob_ref))
    gates = []
    for d, (x_ref, gc_ref, gr_ref, o_ref) in enumerate(refs):
        reverse = d == 1
        gcol = gc_ref[...]
        g = -jnp.exp(alc_ref[...]) * _softplus(gcol + dtc_ref[...])
        beta = _sigmoid(gcol)
        gcum = g
        for sh in (1, 2, 4, 8, 16, 32):
            if reverse:
                gcum = gcum + jnp.where(row_in_chunk < C - sh, pltpu.roll(gcum, R - sh, 0), 0.0)
            else:
                gcum = gcum + jnp.where(row_in_chunk >= sh, pltpu.roll(gcum, sh, 0), 0.0)
        grow_all = -jnp.exp(alr_ref[...])[None] * _softplus(gr_ref[...] + dtr_ref[...][None])
        for sh in (1, 2, 4, 8, 16, 32):
            if reverse:
                grow_all = grow_all + jnp.where(lane_in_chunk < C - sh,
                                                pltpu.roll(grow_all, DN_ROWS - sh, 2), 0.0)
            else:
                grow_all = grow_all + jnp.where(lane_in_chunk >= sh, pltpu.roll(grow_all, sh, 2), 0.0)

        gates.append((gcum, beta, grow_all))

    units = []
    for step in range(DN_GROUP):
        for d, (x_ref, _, _, o_ref) in enumerate(refs):
            reverse = d == 1
            ci = DN_GROUP - 1 - step if reverse else step
            gcum, beta, grow_all = gates[d]
            rows = slice(ci * C, (ci + 1) * C)
            x = x_ref[rows, :]
            u = dict(d=d, rows=rows, o_ref=o_ref)
            u["q"] = stack_heads(x, 0)
            u["k"] = stack_heads(x, GROUP)
            v_st = stack_heads(x, 2 * GROUP)
            gc_c = gcum[rows, :]
            beta_st = stack_cols(beta[rows, :], 8 + d * H)
            gcol_st = stack_cols(gc_c, d * H)
            glast_st = stack_cols(gc_c, d * H, rows=0 if reverse else C - 1)
            grow = grow_all[ci, d:d + 1, :]
            gcol_sbs = jnp.broadcast_to(gc_c[:, d * H + H - 1:d * H + H], (C, DN_ROWS))
            for h in range(H - 2, -1, -1):
                gcol_sbs = jnp.where(lane_head <= h, gc_c[:, d * H + h:d * H + h + 1], gcol_sbs)
            u["incl"] = (row_sbs <= col_sbs) if reverse else (row_sbs >= col_sbs)
            u["decay"] = jnp.where(u["incl"], jnp.exp(gcol_sbs - grow), 0.0)
            kb_st = u["k"] * beta_st
            eg = jnp.exp(gcol_st)
            u["kq"] = jnp.concatenate([side_by_side_blocks(kb_st), side_by_side_blocks(u["q"])],
                                      axis=0).astype(BF16)
            u["kbd"] = block_diag_blocks(u["k"].astype(BF16))
            u["rhs"] = jnp.concatenate([v_st * beta_st, kb_st * eg], axis=1).astype(BF16)
            u["qdec"] = u["q"] * eg
            u["kdec"] = (u["k"] * jnp.exp(glast_st - gcol_st)).astype(BF16)
            u["gl"] = jnp.exp(glast_st)
            units.append(u)

    for u in units:
        u["kk"] = _dot_nt(u["kq"], u["kbd"])
    eye_sbs = (row_sbs == col_sbs).astype(F32)
    for u in units:
        strict = u["incl"] & (row_sbs != col_sbs)
        a_sbs = jnp.where(strict, u["kk"][0:C] * u["decay"], 0.0)
        u["attn"] = (u["kk"][C:2 * C] * u["decay"]).astype(BF16)
        u["pinv"] = eye_sbs - a_sbs
        u["a_sbs"] = a_sbs.astype(BF16)
    for u in units:
        u["apow"] = _dot(u["a_sbs"], block_diag(u["a_sbs"]))
    for _ in range(4):
        for u in units:
            ap = u["apow"].astype(BF16)
            u["both"] = _dot(jnp.concatenate([u["pinv"].astype(BF16), ap], axis=0), block_diag(ap))
        for u in units:
            u["pinv"] = u["pinv"] + u["both"][0:C]
            u["apow"] = u["both"][C:2 * C]
    for u in units:
        u["last"] = _dot(u["pinv"].astype(BF16), block_diag(u["apow"].astype(BF16)))
    for u in units:
        u["uw"] = _dot(block_diag((u["pinv"] + u["last"]).astype(BF16)), u["rhs"])

    for step in range(DN_GROUP):
        pair = units[2 * step:2 * step + 2]
        for u in pair:
            d = u["d"]
            u["wq"] = []
            for h in range(H):
                hr = slice(h * C, (h + 1) * C)
                lhs = jnp.concatenate([u["uw"][hr, 128:256], u["qdec"][hr]], axis=0).astype(BF16)
                u["wq"].append(_dot(lhs, s_scr[d * H + h].astype(BF16)))
        for u in pair:
            u["vnew"] = [(u["uw"][h * C:(h + 1) * C, 0:128] - u["wq"][h][0:C]).astype(BF16) for h in range(H)]
            vnew_bd = block_diag_blocks(jnp.concatenate(u["vnew"], axis=0))
            u["o"] = jnp.concatenate([w[C:2 * C] for w in u["wq"]], axis=1) + _dot(u["attn"], vnew_bd)
        for u in pair:
            d = u["d"]
            for h in range(H):
                hr = slice(h * C, (h + 1) * C)
                s_scr[d * H + h] = (s_scr[d * H + h] * u["gl"][h * C:h * C + 1, :]
                                    + _dot_tn(u["kdec"][hr], u["vnew"][h]))
            u["o_ref"][u["rows"], :] = u["o"]


def _dn_scan(qkvn, pg, a_log, dt_bias, batch, seq_len):
    T = qkvn.shape[0]
    C, H, G = DN_CHUNK, DN_HEADS, DN_GROUP
    R = G * C
    nc = seq_len // C
    ncg = nc // G
    ab = pg[:, 0:4 * H].reshape(batch, nc, C, 2, 2, H)
    ab_row = jnp.transpose(ab, (0, 1, 3, 4, 5, 2)).reshape(batch * nc, 4, H * C)
    ab_row = jnp.pad(ab_row, ((0, 0), (0, 4), (0, 0)))
    pad_lanes = lambda v: jnp.pad(v.reshape(1, 2 * H), ((0, 0), (0, GATE_LANES - 2 * H)))
    row_param = lambda v: jnp.pad(jnp.repeat(v, C, axis=1), ((0, 6), (0, 0)))

    fwd = lambda b, c: (b * ncg + c, 0)
    bwd = lambda b, c: (b * ncg + ncg - 1 - c, 0)
    fwd3 = lambda b, c: (b * ncg + c, 0, 0)
    bwd3 = lambda b, c: (b * ncg + ncg - 1 - c, 0, 0)
    const = lambda shape: pl.BlockSpec(shape, lambda b, c: (0,) * len(shape))
    return pl.pallas_call(
        _dn_scan_kernel,
        grid=(batch, ncg),
        in_specs=[
            pl.BlockSpec((R, 3 * GROUP), fwd), pl.BlockSpec((R, 3 * GROUP), bwd),
            pl.BlockSpec((R, GATE_LANES), fwd), pl.BlockSpec((R, GATE_LANES), bwd),
            pl.BlockSpec((G, 8, H * C), fwd3), pl.BlockSpec((G, 8, H * C), bwd3),
            const((1, GATE_LANES)), const((1, GATE_LANES)), const((8, H * C)), const((8, H * C)),
        ],
        out_specs=[pl.BlockSpec((R, GROUP), fwd), pl.BlockSpec((R, GROUP), bwd)],
        out_shape=[jax.ShapeDtypeStruct((T, GROUP), F32), jax.ShapeDtypeStruct((T, GROUP), F32)],
        scratch_shapes=[pltpu.VMEM((2 * H, DN_HEAD_DIM, DN_HEAD_DIM), F32)],
        compiler_params=_cparams("arbitrary", "arbitrary"),
        name="deltanet_scan",
    )(qkvn, qkvn, pg, pg, ab_row, ab_row, pad_lanes(a_log), pad_lanes(dt_bias),
      row_param(a_log), row_param(dt_bias))


def _dn_post_kernel(of_ref, ob_ref, z_ref, g_ref, y_ref):
    for h in range(DN_HEADS):
        lanes = slice(h * 128, (h + 1) * 128)
        o = of_ref[:, lanes] + ob_ref[:, lanes]
        o = o * lax.rsqrt(jnp.mean(o * o, axis=-1, keepdims=True) + EPS) * g_ref[...]
        y_ref[:, lanes] = (o * _silu(z_ref[:, lanes])).astype(BF16)


def _dn_post(o_f, o_b, p, norm_g, tm=1024):
    T = p.shape[0]
    row = lambda cb: pl.BlockSpec((tm, GROUP), lambda i: (i, cb))
    return pl.pallas_call(
        _dn_post_kernel,
        grid=(T // tm,),
        in_specs=[row(0), row(0), row(9), pl.BlockSpec((1, DN_HEAD_DIM), lambda i: (0, 0))],
        out_specs=row(0),
        out_shape=jax.ShapeDtypeStruct((T, GROUP), BF16),
        compiler_params=_cparams("parallel"),
        name="deltanet_post",
    )(o_f, o_b, p, norm_g)


def _deltanet_mixer(p, pg, conv_w, a_log, dt_bias, norm_g, batch, seq_len):
    qkvn = _dn_prep(p, conv_w, seq_len)
    o_f, o_b = _dn_scan(qkvn, pg, a_log, dt_bias, batch, seq_len)
    return _dn_post(o_f, o_b, p, norm_g[None])


def _outproj_kernel(x_ref, ya_ref, yb_ref, yc_ref, yd_ref, w_ref, g_ref, xn_ref, hn_ref, ycat_scr):
    for gi, y_ref in enumerate((ya_ref, yb_ref, yc_ref, yd_ref)):
        ycat_scr[:, gi * GROUP:(gi + 1) * GROUP] = y_ref[...]
    acc = x_ref[...] + _dot(ycat_scr[...], w_ref[...])
    xn_ref[...] = acc
    ms = jnp.mean(acc * acc, axis=-1, keepdims=True)
    hn_ref[...] = (acc * lax.rsqrt(ms + EPS) * g_ref[...]).astype(BF16)


def _out_proj(x2, ys, w_out, norm2_g, tm=512):
    T = x2.shape[0]
    row = lambda width: pl.BlockSpec((tm, width), lambda i: (i, 0))
    return pl.pallas_call(
        _outproj_kernel,
        grid=(T // tm,),
        in_specs=[row(D_MODEL), row(GROUP), row(GROUP), row(GROUP), row(GROUP),
                  pl.BlockSpec((D_MODEL, D_MODEL), lambda i: (0, 0), pipeline_mode=pl.Buffered(1)),
                  pl.BlockSpec((1, D_MODEL), lambda i: (0, 0))],
        out_specs=[row(D_MODEL), row(D_MODEL)],
        out_shape=[jax.ShapeDtypeStruct((T, D_MODEL), F32),
                   jax.ShapeDtypeStruct((T, D_MODEL), BF16)],
        scratch_shapes=[pltpu.VMEM((tm, D_MODEL), BF16)],
        compiler_params=_cparams("parallel"),
        name="out_proj",
    )(x2, *ys, w_out, norm2_g)


FFN_HALF = FFN_HIDDEN // 2
FFN_COL_CHUNK = 256


def _ffn_hidden_kernel(hn_ref, hp_ref, hx_ref, wg_ref, wu_ref, dw_ref, db_ref, a_ref, hext_scr, g_scr,
                       *, tm, seq_len):
    first, last, _ = _seq_edges(pl.program_id(1), seq_len // tm)
    hext_scr[0:16, :] = jnp.where(first, jnp.zeros_like(hp_ref[...]), hp_ref[...])
    hext_scr[16:16 + tm, :] = hn_ref[...]
    hext_scr[16 + tm:32 + tm, :] = jnp.where(last, jnp.zeros_like(hx_ref[...]), hx_ref[...])
    for c in range(FFN_HALF // FFN_COL_CHUNK):
        cols = slice(c * FFN_COL_CHUNK, (c + 1) * FFN_COL_CHUNK)
        g = g_scr.at[c % 2]
        g[...] = _dot(hext_scr[...], wg_ref[:, cols])
        up = _dot(hn_ref[...], wu_ref[:, cols])
        gate = (g[pl.ds(15, tm), :] * dw_ref[0:1, cols] + g[pl.ds(16, tm), :] * dw_ref[1:2, cols]
                + g[pl.ds(17, tm), :] * dw_ref[2:3, cols] + db_ref[:, cols])
        a_ref[:, cols] = (_silu(gate) * up).astype(BF16)


def _ffn_down_kernel(a_ref, xn_ref, wd_ref, o_ref):
    o_ref[...] = xn_ref[...] + _dot(a_ref[...], wd_ref[...])


def _ffn(hn, xn, w_gate, dw_w, dw_b, w_up, w_down, seq_len, tm_hidden=1024, tm_down=256):
    T = hn.shape[0]
    n_tiles = T // tm_hidden
    r = tm_hidden // 16
    half_cols = lambda rows: pl.BlockSpec((rows, FFN_HALF), lambda h, i: (0, h),
                                          pipeline_mode=pl.Buffered(1))
    act = pl.pallas_call(
        functools.partial(_ffn_hidden_kernel, tm=tm_hidden, seq_len=seq_len),
        grid=(2, n_tiles),
        in_specs=[
            pl.BlockSpec((tm_hidden, D_MODEL), lambda h, i: (i, 0)),
            pl.BlockSpec((16, D_MODEL), lambda h, i: (jnp.maximum(i * r - 1, 0), 0)),
            pl.BlockSpec((16, D_MODEL), lambda h, i: (jnp.minimum((i + 1) * r, n_tiles * r - 1), 0)),
            half_cols(D_MODEL), half_cols(D_MODEL), half_cols(3), half_cols(1),
        ],
        out_specs=pl.BlockSpec((tm_hidden, FFN_HALF), lambda h, i: (i, h)),
        out_shape=jax.ShapeDtypeStruct((T, FFN_HIDDEN), BF16),
        scratch_shapes=[pltpu.VMEM((tm_hidden + 32, D_MODEL), BF16),
                        pltpu.VMEM((2, tm_hidden + 32, FFN_COL_CHUNK), F32)],
        compiler_params=_cparams("arbitrary", "arbitrary"),
        name="ffn_hidden",
    )(hn, hn, hn, w_gate, w_up, dw_w, dw_b)
    return pl.pallas_call(
        _ffn_down_kernel,
        grid=(T // tm_down,),
        in_specs=[
            pl.BlockSpec((tm_down, FFN_HIDDEN), lambda i: (i, 0)),
            pl.BlockSpec((tm_down, D_MODEL), lambda i: (i, 0)),
            pl.BlockSpec((FFN_HIDDEN, D_MODEL), lambda i: (0, 0), pipeline_mode=pl.Buffered(1)),
        ],
        out_specs=pl.BlockSpec((tm_down, D_MODEL), lambda i: (i, 0)),
        out_shape=jax.ShapeDtypeStruct((T, D_MODEL), F32),
        compiler_params=_cparams("parallel"),
        name="ffn_down",
    )(act, xn, w_down)


def _layer(x2, l, batch, seq_len, rel_bias, norm1_g, w_in, w_pool, pool_scale, att_q_g, att_k_g,
           conv_dw_w, conv_dw_b, conv_ln_g, conv_ln_b, conv_pw, dn_conv_w, dn_a_log, dn_dt_bias,
           dn_norm_g, w_out, norm2_g, ffn_w_gate, ffn_dw_w, ffn_dw_b, ffn_w_up, ffn_w_down):
    w_main = _weight_bf16(w_in, l, cols=PROJ_MAIN)
    w_gates = _gate_weight_bf16(w_in, l)
    p, pg = _in_proj(x2, norm1_g[l][None], w_main, w_gates)
    ya = _pool_mixer(p, w_pool[l].astype(BF16), pool_scale[l][None], seq_len)
    yb = _attention_mixer(p, att_q_g[l], att_k_g[l], rel_bias, seq_len)
    yc = _conformer_mixer(p, conv_dw_w[l], conv_dw_b[l][None], conv_ln_g[l][None], conv_ln_b[l][None],
                          conv_pw[l].astype(BF16), seq_len)
    yd = _deltanet_mixer(p, pg, dn_conv_w[l], dn_a_log[l], dn_dt_bias[l], dn_norm_g[l], batch, seq_len)
    xn, hn = _out_proj(x2, (ya, yb, yc, yd), _weight_bf16(w_out, l), norm2_g[l][None])
    return _ffn(hn, xn, _weight_bf16(ffn_w_gate, l), ffn_dw_w[l], ffn_dw_b[l][None],
                _weight_bf16(ffn_w_up, l), _weight_bf16(ffn_w_down, l), seq_len)


def kernel(x, rel_bias, norm1_g, w_in, w_pool, pool_scale, att_q_g, att_k_g, conv_dw_w, conv_dw_b,
           conv_ln_g, conv_ln_b, conv_pw, dn_conv_w, dn_a_log, dn_dt_bias, dn_norm_g, w_out, norm2_g,
           ffn_w_gate, ffn_dw_w, ffn_dw_b, ffn_w_up, ffn_w_down):
    batch, seq_len, _ = x.shape
    x2 = x.reshape(batch * seq_len, D_MODEL)
    for l in range(norm1_g.shape[0]):
        x2 = _layer(x2, l, batch, seq_len, rel_bias, norm1_g, w_in, w_pool, pool_scale, att_q_g,
                    att_k_g, conv_dw_w, conv_dw_b, conv_ln_g, conv_ln_b, conv_pw, dn_conv_w, dn_a_log,
                    dn_dt_bias, dn_norm_g, w_out, norm2_g, ffn_w_gate, ffn_dw_w, ffn_dw_b, ffn_w_up,
                    ffn_w_down)
    return x2.reshape(batch, seq_len, D_MODEL)
```

```python
import functools
import math

import jax
import jax.numpy as jnp
import numpy as np
from jax import lax
from jax.experimental import pallas as pl
from jax.experimental.pallas import tpu as pltpu

F32 = jnp.float32
BF16 = jnp.bfloat16

D_MODEL = 2048
GROUP = 512
POOL_WINDOWS = (2, 4, 8, 16)
ATT_HEAD_DIM = 64
ATT_HEADS = 8
ATT_RADIUS = 64
ATT_DILATIONS = (1, 4, 16)
REL_BUCKETS = 32
REL_MAX_DIST = 1024
CONV_WIDTH = 31
DN_HEAD_DIM = 128
DN_HEADS = 4
DN_CHUNK = 64
FFN_HIDDEN = 5632
EPS = 1e-6
NEG_INF = -1e30

PROJ_MAIN = 10 * GROUP
GATE_LANES = 128

VMEM_LIMIT_BYTES = 56 * 1024 * 1024


def _cparams(*sem):
    return pltpu.CompilerParams(dimension_semantics=sem, vmem_limit_bytes=VMEM_LIMIT_BYTES)


def _sigmoid(x):
    return 1.0 / (1.0 + jnp.exp(-x))


def _silu(x):
    return x * _sigmoid(x)


def _dot(a, b):
    return jnp.dot(a, b, preferred_element_type=F32)


def _dot_nt(a, b):
    return lax.dot_general(a, b, (((1,), (1,)), ((), ())), preferred_element_type=F32)


def _dot_tn(a, b):
    return lax.dot_general(a, b, (((0,), (0,)), ((), ())), preferred_element_type=F32)


def _cast_kernel(w_ref, o_ref):
    o_ref[...] = w_ref[...].astype(BF16)


def _weight_bf16(w_stack, layer, cols=None, tr=512):
    _, rows, width = w_stack.shape
    cols = width if cols is None else cols
    tc = cols // 2 if cols >= 4096 else cols
    return pl.pallas_call(
        _cast_kernel,
        grid=(rows // tr, cols // tc),
        in_specs=[pl.BlockSpec((None, tr, tc), lambda i, j: (layer, i, j))],
        out_specs=pl.BlockSpec((tr, tc), lambda i, j: (i, j)),
        out_shape=jax.ShapeDtypeStruct((rows, cols), BF16),
        compiler_params=_cparams("parallel", "parallel"),
        name="weight_bf16",
    )(w_stack)


def _gate_cols_kernel(w_ref, o_ref, *, valid):
    lane = lax.broadcasted_iota(jnp.int32, o_ref.shape, 1)
    o_ref[...] = jnp.where(lane < valid, w_ref[...], 0.0).astype(BF16)


def _gate_weight_bf16(w_stack, layer):
    _, rows, width = w_stack.shape
    valid = width - PROJ_MAIN
    return pl.pallas_call(
        functools.partial(_gate_cols_kernel, valid=valid),
        grid=(1,),
        in_specs=[pl.BlockSpec((None, rows, GATE_LANES), lambda i: (layer, 0, PROJ_MAIN // GATE_LANES))],
        out_specs=pl.BlockSpec((rows, GATE_LANES), lambda i: (0, 0)),
        out_shape=jax.ShapeDtypeStruct((rows, GATE_LANES), BF16),
        compiler_params=_cparams("arbitrary"),
        name="gate_weight_bf16",
    )(w_stack)


def _inproj_kernel(x_ref, g_ref, w_ref, wg_ref, p_ref, pg_ref):
    x = x_ref[...]
    ms = jnp.mean(x * x, axis=-1, keepdims=True)
    h = (x * lax.rsqrt(ms + EPS) * g_ref[...]).astype(BF16)
    pg_ref[...] = _dot(h, wg_ref[...])
    p_ref[...] = _dot(h, w_ref[...])


def _in_proj(x2, norm_g, w_main, w_gate, tm=512):
    T = x2.shape[0]
    resident = lambda shape: pl.BlockSpec(shape, lambda i: (0, 0), pipeline_mode=pl.Buffered(1))
    return pl.pallas_call(
        _inproj_kernel,
        grid=(T // tm,),
        in_specs=[
            pl.BlockSpec((tm, D_MODEL), lambda i: (i, 0)),
            resident((1, D_MODEL)),
            resident((D_MODEL, PROJ_MAIN)),
            resident((D_MODEL, GATE_LANES)),
        ],
        out_specs=[
            pl.BlockSpec((tm, PROJ_MAIN), lambda i: (i, 0)),
            pl.BlockSpec((tm, GATE_LANES), lambda i: (i, 0)),
        ],
        out_shape=[
            jax.ShapeDtypeStruct((T, PROJ_MAIN), F32),
            jax.ShapeDtypeStruct((T, GATE_LANES), F32),
        ],
        compiler_params=_cparams("parallel"),
        name="in_proj",
    )(x2, norm_g, w_main, w_gate)


def _halo_specs(tm, halo, width, col_block, n_tiles):
    r = tm // halo
    last = n_tiles * r - 1
    prev_spec = pl.BlockSpec((halo, width), lambda i, *_: (jnp.maximum(i * r - 1, 0), col_block))
    next_spec = pl.BlockSpec((halo, width), lambda i, *_: (jnp.minimum((i + 1) * r, last), col_block))
    return prev_spec, next_spec


def _seq_edges(i, tiles_per_seq):
    k = lax.rem(i, tiles_per_seq)
    return k == 0, k == tiles_per_seq - 1, k


def _pool_kernel(u_ref, up_ref, un_ref, w_ref, sc_ref, y_ref, ext_scr, *, tm, seq_len):
    i = pl.program_id(0)
    first, last, k = _seq_edges(i, seq_len // tm)
    ext_scr[0:8, :] = jnp.where(first, jnp.zeros_like(up_ref[...]), up_ref[...])
    ext_scr[8:8 + tm, :] = u_ref[...]
    ext_scr[8 + tm:16 + tm, :] = jnp.where(last, jnp.zeros_like(un_ref[...]), un_ref[...])
    t = k * tm + lax.broadcasted_iota(jnp.int32, (tm, 1), 0)
    for gi, win in enumerate(POOL_WINDOWS):
        half = win // 2
        lanes = slice(gi * 128, (gi + 1) * 128)
        s = ext_scr[pl.ds(8 - half, tm), lanes]
        for kk in range(1, win):
            s = s + ext_scr[pl.ds(8 - half + kk, tm), lanes]
        cnt = (jnp.minimum(t + half, seq_len) - jnp.maximum(t - half, 0)).astype(F32)
        pooled = s / cnt - ext_scr[pl.ds(8, tm), lanes]
        y = _dot(pooled.astype(BF16), w_ref[gi]) * sc_ref[:, lanes]
        y_ref[:, lanes] = y.astype(BF16)


def _pool_mixer(p, w_pool, pool_scale, seq_len, tm=1024):
    T = p.shape[0]
    n_tiles = T // tm
    prev_spec, next_spec = _halo_specs(tm, 8, GROUP, 0, n_tiles)
    return pl.pallas_call(
        functools.partial(_pool_kernel, tm=tm, seq_len=seq_len),
        grid=(n_tiles,),
        in_specs=[
            pl.BlockSpec((tm, GROUP), lambda i: (i, 0)),
            prev_spec,
            next_spec,
            pl.BlockSpec((4, 128, 128), lambda i: (0, 0, 0)),
            pl.BlockSpec((1, GROUP), lambda i: (0, 0)),
        ],
        out_specs=pl.BlockSpec((tm, GROUP), lambda i: (i, 0)),
        out_shape=jax.ShapeDtypeStruct((T, GROUP), BF16),
        scratch_shapes=[pltpu.VMEM((tm + 16, GROUP), F32)],
        compiler_params=_cparams("parallel"),
        name="pool_mixer",
    )(p, p, p, w_pool, pool_scale)


def _conformer_kernel(v_ref, vp_ref, vn_ref, g_ref, gp_ref, gn_ref, dw_ref, db_ref, lg_ref, lb_ref,
                      pw_ref, y_ref, ext_scr, shift_scr, *, tm, seq_len):
    i = pl.program_id(0)
    first, last, _ = _seq_edges(i, seq_len // tm)
    hp = vp_ref[...] * _sigmoid(gp_ref[...])
    hn = vn_ref[...] * _sigmoid(gn_ref[...])
    ext_scr[0:16, :] = jnp.where(first, jnp.zeros_like(hp), hp)
    ext_scr[16:16 + tm, :] = v_ref[...] * _sigmoid(g_ref[...])
    ext_scr[16 + tm:32 + tm, :] = jnp.where(last, jnp.zeros_like(hn), hn)
    base = 16 - CONV_WIDTH // 2
    acc = db_ref[...]
    for b in range(8):
        taps = [(a, 8 * a + b - base) for a in range(5) if 0 <= 8 * a + b - base < CONV_WIDTH]
        rows = tm + 8 * taps[-1][0]
        shift_scr[b, 0:rows, :] = ext_scr[pl.ds(b, rows), :]
        for a, kk in taps:
            acc = acc + shift_scr[b, 8 * a:8 * a + tm, :] * dw_ref[kk:kk + 1, :]
    mu = jnp.mean(acc, axis=-1, keepdims=True)
    xc = acc - mu
    var = jnp.mean(xc * xc, axis=-1, keepdims=True)
    h = _silu(xc * lax.rsqrt(var + EPS) * lg_ref[...] + lb_ref[...])
    y_ref[...] = _dot(h.astype(BF16), pw_ref[...]).astype(BF16)


def _conformer_mixer(p, dw_w, dw_b, ln_g, ln_b, pw, seq_len, tm=512):
    T = p.shape[0]
    n_tiles = T // tm
    vprev, vnext = _halo_specs(tm, 16, GROUP, 4, n_tiles)
    gprev, gnext = _halo_specs(tm, 16, GROUP, 5, n_tiles)
    const = lambda shape: pl.BlockSpec(shape, lambda i: (0,) * len(shape))
    return pl.pallas_call(
        functools.partial(_conformer_kernel, tm=tm, seq_len=seq_len),
        grid=(n_tiles,),
        in_specs=[
            pl.BlockSpec((tm, GROUP), lambda i: (i, 4)), vprev, vnext,
            pl.BlockSpec((tm, GROUP), lambda i: (i, 5)), gprev, gnext,
            const((CONV_WIDTH, GROUP)), const((1, GROUP)), const((1, GROUP)), const((1, GROUP)),
            const((GROUP, GROUP)),
        ],
        out_specs=pl.BlockSpec((tm, GROUP), lambda i: (i, 0)),
        out_shape=jax.ShapeDtypeStruct((T, GROUP), BF16),
        scratch_shapes=[pltpu.VMEM((tm + 32, GROUP), F32), pltpu.VMEM((8, tm + 32, GROUP), F32)],
        compiler_params=_cparams("parallel"),
        name="conformer_mixer",
    )(p, p, p, p, p, p, dw_w, dw_b, ln_g, ln_b, pw)


ATT_TILE = 1024
LOG2E = math.log2(math.e)
ATT_BLOCK_GROUP = 8


def _t5_bucket_table():
    nb = REL_BUCKETS // 2
    max_exact = nb // 2
    i = np.arange(ATT_RADIUS)[:, None]
    j = np.arange(3 * ATT_RADIUS)[None, :]
    off = j - ATT_RADIUS - i
    tables = []
    for dil in ATT_DILATIONS:
        rel = off * dil
        n = np.abs(rel)
        nf = np.maximum(n, 1).astype(np.float32)
        large = max_exact + (np.log(nf / np.float32(max_exact)) / np.float32(math.log(REL_MAX_DIST / max_exact))
                             * np.float32(nb - max_exact)).astype(np.int32)
        large = np.minimum(large, nb - 1)
        bucket = np.where(rel > 0, nb, 0) + np.where(n < max_exact, n, large)
        tables.append(np.where(np.abs(off) <= ATT_RADIUS, bucket, -1))
    return np.stack(tables).astype(np.int32)


def _att_kernel(rb_ref, bkt_ref, qg_ref, kg_ref, mseg_ref, q_ref, kc_ref, kn_ref,
                vp_ref, vc_ref, vn_ref, y_ref, bias_scr, qbuf, kbuf, vbuf, acc_scr, m_scr, l_scr,
                *, seq_len):
    tile = ATT_TILE
    rad = ATT_RADIUS
    hp = pl.program_id(0)
    i = pl.program_id(1)

    def rms(x, g):
        ms = _dot((x * x).astype(BF16), mseg_ref[...])
        return x * lax.rsqrt(ms + EPS) * g

    @pl.when(i == 0)
    def _():
        lane_head0 = lax.broadcasted_iota(jnp.int32, (1, 128), 1) < ATT_HEAD_DIM
        for di in range(len(ATT_DILATIONS)):
            bkt = bkt_ref[di]
            b = jnp.zeros(bkt.shape, F32)
            for bb in range(REL_BUCKETS):
                b = jnp.where(bkt == bb, jnp.where(lane_head0, rb_ref[bb, 2 * hp], rb_ref[bb, 2 * hp + 1]), b)
            bias_scr[di] = jnp.where(bkt < 0, NEG_INF, b * LOG2E)
        kbuf[0:tile, :] = jnp.zeros((tile, 128), F32)
        kbuf[tile:2 * tile, :] = rms(kc_ref[...], kg_ref[...])

    @pl.when(i > 0)
    def _():
        kbuf[0:tile, :] = kbuf[tile:2 * tile, :]
        kbuf[tile:2 * tile, :] = kbuf[2 * tile:3 * tile, :]

    first, last, _ = _seq_edges(i, seq_len // tile)

    qbuf[...] = rms(q_ref[...], qg_ref[...]) * (ATT_HEAD_DIM ** -0.5 * LOG2E)
    kbuf[2 * tile:3 * tile, :] = rms(kn_ref[...], kg_ref[...])
    vbuf[0:tile, :] = vp_ref[...]
    vbuf[tile:2 * tile, :] = vc_ref[...]
    vbuf[2 * tile:3 * tile, :] = vn_ref[...]

    head0 = lax.broadcasted_iota(jnp.int32, (1, 128), 1) < ATT_HEAD_DIM
    key_row = lax.broadcasted_iota(jnp.int32, (3 * rad, 1), 0)
    ones_cols = jnp.ones((3 * rad, 128), BF16)

    def rows(start, size, dil):
        return pl.ds(start, size) if dil == 1 else pl.ds(start, size, stride=dil)

    blocks = [(di, dil, r + rad * dil * m)
              for di, dil in enumerate(ATT_DILATIONS) for r in range(dil) for m in range(tile // (rad * dil))]
    for g0 in range(0, len(blocks), ATT_BLOCK_GROUP):
        group = blocks[g0:g0 + ATT_BLOCK_GROUP]
        scores = []
        for di, dil, qstart in group:
            kstart = tile + qstart - rad * dil
            qb = qbuf[rows(qstart, rad, dil), :]
            kb = kbuf[rows(kstart, 3 * rad, dil), :].astype(BF16)
            q2 = jnp.concatenate([jnp.where(head0, qb, 0.0), jnp.where(head0, 0.0, qb)], axis=0)
            scores.append(_dot_nt(kb, q2.astype(BF16)))
        probs = []
        for (di, dil, qstart), s in zip(group, scores):
            kstart = tile + qstart - rad * dil
            s = s + bias_scr[di]
            n_prev = max(0, -(-(tile - kstart) // dil))
            n_upto = min(3 * rad, -(-(2 * tile - kstart) // dil))
            if n_prev > 0:
                s = jnp.where(key_row < jnp.where(first, n_prev, 0), NEG_INF, s)
            if n_upto < 3 * rad:
                s = jnp.where(key_row >= jnp.where(last, n_upto, 3 * rad), NEG_INF, s)
            mx = jnp.max(s, axis=0, keepdims=True)
            probs.append((mx, jnp.exp2(s - mx).astype(BF16)))
        outs = []
        for (di, dil, qstart), (mx, pe) in zip(group, probs):
            kstart = tile + qstart - rad * dil
            vb = vbuf[rows(kstart, 3 * rad, dil), :].astype(BF16)
            outs.append(_dot_tn(pe, jnp.concatenate([vb, ones_cols], axis=1)))
        for (di, dil, qstart), (mx, pe), ov in zip(group, probs, outs):
            dst = rows(qstart, rad, dil)
            m_col = jnp.broadcast_to(mx, (2 * rad, 128)).T
            acc_scr[di, dst, :] = jnp.where(head0, ov[0:rad, 0:128], ov[rad:2 * rad, 0:128])
            l_scr[di, dst, :] = jnp.where(head0, ov[0:rad, 128:256], ov[rad:2 * rad, 128:256])
            m_scr[di, dst, :] = jnp.where(head0, m_col[0:rad], m_col[rad:2 * rad])

    m_all = jnp.maximum(jnp.maximum(m_scr[0], m_scr[1]), m_scr[2])
    num = jnp.zeros((tile, 128), F32)
    den = jnp.zeros((tile, 128), F32)
    for di in range(len(ATT_DILATIONS)):
        e = jnp.exp2(m_scr[di] - m_all)
        num = num + acc_scr[di] * e
        den = den + l_scr[di] * e
    y_ref[...] = (num / den).astype(BF16)


def _attention_mixer(p, q_g, k_g, rel_bias, seq_len):
    T = p.shape[0]
    tile = ATT_TILE
    n_tiles = T // tile
    rad = ATT_RADIUS
    seg = np.kron(np.eye(2), np.full((ATT_HEAD_DIM, ATT_HEAD_DIM), 1.0 / ATT_HEAD_DIM))
    mseg = jnp.asarray(seg, BF16)
    bkt_t = np.transpose(_t5_bucket_table(), (0, 2, 1))
    bkt = jnp.asarray(np.concatenate([bkt_t, bkt_t], axis=2))
    qg2 = jnp.tile(q_g, 2)[None]
    kg2 = jnp.tile(k_g, 2)[None]

    def blk(col0, shift):
        return pl.BlockSpec((tile, 128),
                            lambda hp, i: (jnp.clip(i + shift, 0, n_tiles - 1), col0 * 4 + hp))

    const = lambda shape: pl.BlockSpec(shape, lambda hp, i: (0,) * len(shape))
    return pl.pallas_call(
        functools.partial(_att_kernel, seq_len=seq_len),
        grid=(4, n_tiles),
        in_specs=[
            pl.BlockSpec(memory_space=pltpu.SMEM),
            const((3, 3 * rad, 2 * rad)), const((1, 128)), const((1, 128)), const((128, 128)),
            blk(1, 0),
            blk(2, 0), blk(2, 1),
            blk(3, -1), blk(3, 0), blk(3, 1),
        ],
        out_specs=pl.BlockSpec((tile, 128), lambda hp, i: (i, hp)),
        out_shape=jax.ShapeDtypeStruct((T, GROUP), BF16),
        scratch_shapes=[
            pltpu.VMEM((3, 3 * rad, 2 * rad), F32),
            pltpu.VMEM((tile, 128), F32),
            pltpu.VMEM((3 * tile, 128), F32),
            pltpu.VMEM((3 * tile, 128), F32),
            pltpu.VMEM((3, tile, 128), F32),
            pltpu.VMEM((3, tile, 128), F32),
            pltpu.VMEM((3, tile, 128), F32),
        ],
        compiler_params=_cparams("arbitrary", "arbitrary"),
        name="dilated_attention",
    )(rel_bias, bkt, qg2, kg2, mseg, p, p, p, p, p, p)


def _softplus(x):
    return jnp.maximum(x, 0.0) + jnp.log1p(jnp.exp(-jnp.abs(x)))


def _dn_prep_kernel(x_ref, xp_ref, xn_ref, w_ref, o_ref, ext_scr, *, tm, seq_len):
    i = pl.program_id(0)
    first, last, _ = _seq_edges(i, seq_len // tm)
    ext_scr[0:8, :] = jnp.where(first, jnp.zeros_like(xp_ref[...]), xp_ref[...])
    ext_scr[8:8 + tm, :] = x_ref[...]
    ext_scr[8 + tm:16 + tm, :] = jnp.where(last, jnp.zeros_like(xn_ref[...]), xn_ref[...])
    for cb in range(3 * DN_HEADS):
        lanes = slice(cb * 128, (cb + 1) * 128)
        acc = ext_scr[pl.ds(6, tm), lanes] * w_ref[0:1, lanes]
        for kk in range(1, 4):
            acc = acc + ext_scr[pl.ds(6 + kk, tm), lanes] * w_ref[kk:kk + 1, lanes]
        y = _silu(acc)
        if cb < 2 * DN_HEADS:
            y = y * lax.rsqrt(jnp.sum(y * y, axis=-1, keepdims=True) + EPS)
        if cb < DN_HEADS:
            y = y * (DN_HEAD_DIM ** -0.5)
        o_ref[:, lanes] = y


def _dn_prep(p, conv_w, seq_len, tm=512):
    T = p.shape[0]
    n_tiles = T // tm
    width = 3 * GROUP
    prev_spec, next_spec = _halo_specs(tm, 8, width, 2, n_tiles)
    return pl.pallas_call(
        functools.partial(_dn_prep_kernel, tm=tm, seq_len=seq_len),
        grid=(n_tiles,),
        in_specs=[pl.BlockSpec((tm, width), lambda i: (i, 2)), prev_spec, next_spec,
                  pl.BlockSpec((4, width), lambda i: (0, 0))],
        out_specs=pl.BlockSpec((tm, width), lambda i: (i, 0)),
        out_shape=jax.ShapeDtypeStruct((T, width), F32),
        scratch_shapes=[pltpu.VMEM((tm + 16, width), F32)],
        compiler_params=_cparams("parallel"),
        name="deltanet_prep",
    )(p, p, p, conv_w)


DN_GROUP = 8
DN_ROWS = DN_HEADS * DN_CHUNK


def _dn_scan_kernel(xf_ref, xb_ref, gf_ref, gb_ref, rf_ref, rb_ref, alc_ref, dtc_ref, alr_ref, dtr_ref,
                    of_ref, ob_ref, s_scr):
    C = DN_CHUNK
    H = DN_HEADS
    R = DN_GROUP * C

    @pl.when(pl.program_id(1) == 0)
    def _():
        s_scr[...] = jnp.zeros_like(s_scr)

    row_in_chunk = lax.rem(lax.broadcasted_iota(jnp.int32, (R, 1), 0), C)
    lane_in_chunk = lax.rem(lax.broadcasted_iota(jnp.int32, (1, DN_ROWS), 1), C)
    rid = lax.broadcasted_iota(jnp.int32, (DN_ROWS, DN_ROWS), 0)
    cid = lax.broadcasted_iota(jnp.int32, (DN_ROWS, DN_ROWS), 1)
    same_head = (rid // C) == (cid // C)
    row_sbs = lax.broadcasted_iota(jnp.int32, (C, DN_ROWS), 0)
    col_sbs = lax.rem(lax.broadcasted_iota(jnp.int32, (C, DN_ROWS), 1), C)
    lane_head = lax.broadcasted_iota(jnp.int32, (1, DN_ROWS), 1) // C

    def block_diag(m):
        return jnp.where(same_head, jnp.concatenate([m] * H, axis=0), jnp.zeros((), m.dtype))

    def side_by_side_blocks(x):
        return jnp.concatenate([x[h * C:(h + 1) * C] for h in range(H)], axis=1)

    def block_diag_blocks(x):
        zero = jnp.zeros((C, 128), x.dtype)
        return jnp.concatenate(
            [jnp.concatenate([x[h * C:(h + 1) * C] if g == h else zero for g in range(H)], axis=1)
             for h in range(H)], axis=0)

    def stack_heads(x, col0):
        return jnp.concatenate([x[:, col0 + h * 128:col0 + (h + 1) * 128] for h in range(H)], axis=0)

    def stack_cols(x, lane0, rows=None):
        parts = []
        for h in range(H):
            c = x[:, lane0 + h:lane0 + h + 1]
            if rows is not None:
                c = c[rows:rows + 1, :]
            parts.append(jnp.broadcast_to(c, (C, 128)))
        return jnp.concatenate(parts, axis=0)

    refs = ((xf_ref, gf_ref, rf_ref, of_ref), (xb_ref, gb_ref, rb_ref, ob_ref))
    gates = []
    for d, (x_ref, gc_ref, gr_ref, o_ref) in enumerate(refs):
        reverse = d == 1
        gcol = gc_ref[...]
        g = -jnp.exp(alc_ref[...]) * _softplus(gcol + dtc_ref[...])
        beta = _sigmoid(gcol)
        gcum = g
        for sh in (1, 2, 4, 8, 16, 32):
            if reverse:
                gcum = gcum + jnp.where(row_in_chunk < C - sh, pltpu.roll(gcum, R - sh, 0), 0.0)
            else:
                gcum = gcum + jnp.where(row_in_chunk >= sh, pltpu.roll(gcum, sh, 0), 0.0)
        grow_all = -jnp.exp(alr_ref[...])[None] * _softplus(gr_ref[...] + dtr_ref[...][None])
        for sh in (1, 2, 4, 8, 16, 32):
            if reverse:
                grow_all = grow_all + jnp.where(lane_in_chunk < C - sh,
                                                pltpu.roll(grow_all, DN_ROWS - sh, 2), 0.0)
            else:
                grow_all = grow_all + jnp.where(lane_in_chunk >= sh, pltpu.roll(grow_all, sh, 2), 0.0)

        gates.append((gcum, beta, grow_all))

    units = []
    for step in range(DN_GROUP):
        for d, (x_ref, _, _, o_ref) in enumerate(refs):
            reverse = d == 1
            ci = DN_GROUP - 1 - step if reverse else step
            gcum, beta, grow_all = gates[d]
            rows = slice(ci * C, (ci + 1) * C)
            x = x_ref[rows, :]
            u = dict(d=d, rows=rows, o_ref=o_ref)
            u["q"] = stack_heads(x, 0)
            u["k"] = stack_heads(x, GROUP)
            v_st = stack_heads(x, 2 * GROUP)
            gc_c = gcum[rows, :]
            beta_st = stack_cols(beta[rows, :], 8 + d * H)
            gcol_st = stack_cols(gc_c, d * H)
            glast_st = stack_cols(gc_c, d * H, rows=0 if reverse else C - 1)
            grow = grow_all[ci, d:d + 1, :]
            gcol_sbs = jnp.broadcast_to(gc_c[:, d * H + H - 1:d * H + H], (C, DN_ROWS))
            for h in range(H - 2, -1, -1):
                gcol_sbs = jnp.where(lane_head <= h, gc_c[:, d * H + h:d * H + h + 1], gcol_sbs)
            u["incl"] = (row_sbs <= col_sbs) if reverse else (row_sbs >= col_sbs)
            u["decay"] = jnp.where(u["incl"], jnp.exp(gcol_sbs - grow), 0.0)
            kb_st = u["k"] * beta_st
            eg = jnp.exp(gcol_st)
            u["kq"] = jnp.concatenate([side_by_side_blocks(kb_st), side_by_side_blocks(u["q"])],
                                      axis=0).astype(BF16)
            u["kbd"] = block_diag_blocks(u["k"].astype(BF16))
            u["rhs"] = jnp.concatenate([v_st * beta_st, kb_st * eg], axis=1).astype(BF16)
            u["qdec"] = u["q"] * eg
            u["kdec"] = (u["k"] * jnp.exp(glast_st - gcol_st)).astype(BF16)
            u["gl"] = jnp.exp(glast_st)
            units.append(u)

    for u in units:
        u["kk"] = _dot_nt(u["kq"], u["kbd"])
    eye_sbs = (row_sbs == col_sbs).astype(F32)
    for u in units:
        strict = u["incl"] & (row_sbs != col_sbs)
        a_sbs = jnp.where(strict, u["kk"][0:C] * u["decay"], 0.0)
        u["attn"] = (u["kk"][C:2 * C] * u["decay"]).astype(BF16)
        u["pinv"] = eye_sbs - a_sbs
        u["a_sbs"] = a_sbs.astype(BF16)
    for u in units:
        u["apow"] = _dot(u["a_sbs"], block_diag(u["a_sbs"]))
    for _ in range(4):
        for u in units:
            ap = u["apow"].astype(BF16)
            u["both"] = _dot(jnp.concatenate([u["pinv"].astype(BF16), ap], axis=0), block_diag(ap))
        for u in units:
            u["pinv"] = u["pinv"] + u["both"][0:C]
            u["apow"] = u["both"][C:2 * C]
    for u in units:
        u["last"] = _dot(u["pinv"].astype(BF16), block_diag(u["apow"].astype(BF16)))
    for u in units:
        u["uw"] = _dot(block_diag((u["pinv"] + u["last"]).astype(BF16)), u["rhs"])

    for step in range(DN_GROUP):
        pair = units[2 * step:2 * step + 2]
        for u in pair:
            d = u["d"]
            u["wq"] = []
            for h in range(H):
                hr = slice(h * C, (h + 1) * C)
                lhs = jnp.concatenate([u["uw"][hr, 128:256], u["qdec"][hr]], axis=0).astype(BF16)
                u["wq"].append(_dot(lhs, s_scr[d * H + h].astype(BF16)))
        for u in pair:
            u["vnew"] = [(u["uw"][h * C:(h + 1) * C, 0:128] - u["wq"][h][0:C]).astype(BF16) for h in range(H)]
            vnew_bd = block_diag_blocks(jnp.concatenate(u["vnew"], axis=0))
            u["o"] = jnp.concatenate([w[C:2 * C] for w in u["wq"]], axis=1) + _dot(u["attn"], vnew_bd)
        for u in pair:
            d = u["d"]
            for h in range(H):
                hr = slice(h * C, (h + 1) * C)
                s_scr[d * H + h] = (s_scr[d * H + h] * u["gl"][h * C:h * C + 1, :]
                                    + _dot_tn(u["kdec"][hr], u["vnew"][h]))
            u["o_ref"][u["rows"], :] = u["o"]


def _dn_scan(qkvn, pg, a_log, dt_bias, batch, seq_len):
    T = qkvn.shape[0]
    C, H, G = DN_CHUNK, DN_HEADS, DN_GROUP
    R = G * C
    nc = seq_len // C
    ncg = nc // G
    ab = pg[:, 0:4 * H].reshape(batch, nc, C, 2, 2, H)
    ab_row = jnp.transpose(ab, (0, 1, 3, 4, 5, 2)).reshape(batch * nc, 4, H * C)
    ab_row = jnp.pad(ab_row, ((0, 0), (0, 4), (0, 0)))
    pad_lanes = lambda v: jnp.pad(v.reshape(1, 2 * H), ((0, 0), (0, GATE_LANES - 2 * H)))
    row_param = lambda v: jnp.pad(jnp.repeat(v, C, axis=1), ((0, 6), (0, 0)))

    fwd = lambda b, c: (b * ncg + c, 0)
    bwd = lambda b, c: (b * ncg + ncg - 1 - c, 0)
    fwd3 = lambda b, c: (b * ncg + c, 0, 0)
    bwd3 = lambda b, c: (b * ncg + ncg - 1 - c, 0, 0)
    const = lambda shape: pl.BlockSpec(shape, lambda b, c: (0,) * len(shape))
    return pl.pallas_call(
        _dn_scan_kernel,
        grid=(batch, ncg),
        in_specs=[
            pl.BlockSpec((R, 3 * GROUP), fwd), pl.BlockSpec((R, 3 * GROUP), bwd),
            pl.BlockSpec((R, GATE_LANES), fwd), pl.BlockSpec((R, GATE_LANES), bwd),
            pl.BlockSpec((G, 8, H * C), fwd3), pl.BlockSpec((G, 8, H * C), bwd3),
            const((1, GATE_LANES)), const((1, GATE_LANES)), const((8, H * C)), const((8, H * C)),
        ],
        out_specs=[pl.BlockSpec((R, GROUP), fwd), pl.BlockSpec((R, GROUP), bwd)],
        out_shape=[jax.ShapeDtypeStruct((T, GROUP), F32), jax.ShapeDtypeStruct((T, GROUP), F32)],
        scratch_shapes=[pltpu.VMEM((2 * H, DN_HEAD_DIM, DN_HEAD_DIM), F32)],
        compiler_params=_cparams("arbitrary", "arbitrary"),
        name="deltanet_scan",
    )(qkvn, qkvn, pg, pg, ab_row, ab_row, pad_lanes(a_log), pad_lanes(dt_bias),
      row_param(a_log), row_param(dt_bias))


def _dn_post_kernel(of_ref, ob_ref, z_ref, g_ref, y_ref):
    for h in range(DN_HEADS):
        lanes = slice(h * 128, (h + 1) * 128)
        o = of_ref[:, lanes] + ob_ref[:, lanes]
        o = o * lax.rsqrt(jnp.mean(o * o, axis=-1, keepdims=True) + EPS) * g_ref[...]
        y_ref[:, lanes] = (o * _silu(z_ref[:, lanes])).astype(BF16)


def _dn_post(o_f, o_b, p, norm_g, tm=1024):
    T = p.shape[0]
    row = lambda cb: pl.BlockSpec((tm, GROUP), lambda i: (i, cb))
    return pl.pallas_call(
        _dn_post_kernel,
        grid=(T // tm,),
        in_specs=[row(0), row(0), row(9), pl.BlockSpec((1, DN_HEAD_DIM), lambda i: (0, 0))],
        out_specs=row(0),
        out_shape=jax.ShapeDtypeStruct((T, GROUP), BF16),
        compiler_params=_cparams("parallel"),
        name="deltanet_post",
    )(o_f, o_b, p, norm_g)


def _deltanet_mixer(p, pg, conv_w, a_log, dt_bias, norm_g, batch, seq_len):
    qkvn = _dn_prep(p, conv_w, seq_len)
    o_f, o_b = _dn_scan(qkvn, pg, a_log, dt_bias, batch, seq_len)
    return _dn_post(o_f, o_b, p, norm_g[None])


def _outproj_kernel(x_ref, ya_ref, yb_ref, yc_ref, yd_ref, w_ref, g_ref, xn_ref, hn_ref, ycat_scr):
    for gi, y_ref in enumerate((ya_ref, yb_ref, yc_ref, yd_ref)):
        ycat_scr[:, gi * GROUP:(gi + 1) * GROUP] = y_ref[...]
    acc = x_ref[...] + _dot(ycat_scr[...], w_ref[...])
    xn_ref[...] = acc
    ms = jnp.mean(acc * acc, axis=-1, keepdims=True)
    hn_ref[...] = (acc * lax.rsqrt(ms + EPS) * g_ref[...]).astype(BF16)


def _out_proj(x2, ys, w_out, norm2_g, tm=512):
    T = x2.shape[0]
    row = lambda width: pl.BlockSpec((tm, width), lambda i: (i, 0))
    return pl.pallas_call(
        _outproj_kernel,
        grid=(T // tm,),
        in_specs=[row(D_MODEL), row(GROUP), row(GROUP), row(GROUP), row(GROUP),
                  pl.BlockSpec((D_MODEL, D_MODEL), lambda i: (0, 0), pipeline_mode=pl.Buffered(1)),
                  pl.BlockSpec((1, D_MODEL), lambda i: (0, 0))],
        out_specs=[row(D_MODEL), row(D_MODEL)],
        out_shape=[jax.ShapeDtypeStruct((T, D_MODEL), F32),
                   jax.ShapeDtypeStruct((T, D_MODEL), BF16)],
        scratch_shapes=[pltpu.VMEM((tm, D_MODEL), BF16)],
        compiler_params=_cparams("parallel"),
        name="out_proj",
    )(x2, *ys, w_out, norm2_g)


FFN_HALF = FFN_HIDDEN // 2
FFN_COL_CHUNK = 256


def _ffn_hidden_kernel(hn_ref, hp_ref, hx_ref, wg_ref, wu_ref, dw_ref, db_ref, a_ref, hext_scr, g_scr,
                       *, tm, seq_len):
    first, last, _ = _seq_edges(pl.program_id(1), seq_len // tm)
    hext_scr[0:16, :] = jnp.where(first, jnp.zeros_like(hp_ref[...]), hp_ref[...])
    hext_scr[16:16 + tm, :] = hn_ref[...]
    hext_scr[16 + tm:32 + tm, :] = jnp.where(last, jnp.zeros_like(hx_ref[...]), hx_ref[...])
    for c in range(FFN_HALF // FFN_COL_CHUNK):
        cols = slice(c * FFN_COL_CHUNK, (c + 1) * FFN_COL_CHUNK)
        g = g_scr.at[c % 2]
        g[...] = _dot(hext_scr[...], wg_ref[:, cols])
        up = _dot(hn_ref[...], wu_ref[:, cols])
        gate = (g[pl.ds(15, tm), :] * dw_ref[0:1, cols] + g[pl.ds(16, tm), :] * dw_ref[1:2, cols]
                + g[pl.ds(17, tm), :] * dw_ref[2:3, cols] + db_ref[:, cols])
        a_ref[:, cols] = (_silu(gate) * up).astype(BF16)


def _ffn_down_kernel(a_ref, xn_ref, wd_ref, o_ref):
    o_ref[...] = xn_ref[...] + _dot(a_ref[...], wd_ref[...])


def _ffn(hn, xn, w_gate, dw_w, dw_b, w_up, w_down, seq_len, tm_hidden=1024, tm_down=512):
    T = hn.shape[0]
    n_tiles = T // tm_hidden
    r = tm_hidden // 16
    half_cols = lambda rows: pl.BlockSpec((rows, FFN_HALF), lambda h, i: (0, h),
                                          pipeline_mode=pl.Buffered(1))
    act = pl.pallas_call(
        functools.partial(_ffn_hidden_kernel, tm=tm_hidden, seq_len=seq_len),
        grid=(2, n_tiles),
        in_specs=[
            pl.BlockSpec((tm_hidden, D_MODEL), lambda h, i: (i, 0)),
            pl.BlockSpec((16, D_MODEL), lambda h, i: (jnp.maximum(i * r - 1, 0), 0)),
            pl.BlockSpec((16, D_MODEL), lambda h, i: (jnp.minimum((i + 1) * r, n_tiles * r - 1), 0)),
            half_cols(D_MODEL), half_cols(D_MODEL), half_cols(3), half_cols(1),
        ],
        out_specs=pl.BlockSpec((tm_hidden, FFN_HALF), lambda h, i: (i, h)),
        out_shape=jax.ShapeDtypeStruct((T, FFN_HIDDEN), BF16),
        scratch_shapes=[pltpu.VMEM((tm_hidden + 32, D_MODEL), BF16),
                        pltpu.VMEM((2, tm_hidden + 32, FFN_COL_CHUNK), F32)],
        compiler_params=_cparams("arbitrary", "arbitrary"),
        name="ffn_hidden",
    )(hn, hn, hn, w_gate, w_up, dw_w, dw_b)
    return pl.pallas_call(
        _ffn_down_kernel,
        grid=(T // tm_down,),
        in_specs=[
            pl.BlockSpec((tm_down, FFN_HIDDEN), lambda i: (i, 0)),
            pl.BlockSpec((tm_down, D_MODEL), lambda i: (i, 0)),
            pl.BlockSpec((FFN_HIDDEN, D_MODEL), lambda i: (0, 0), pipeline_mode=pl.Buffered(1)),
        ],
        out_specs=pl.BlockSpec((tm_down, D_MODEL), lambda i: (i, 0)),
        out_shape=jax.ShapeDtypeStruct((T, D_MODEL), F32),
        compiler_params=_cparams("parallel"),
        name="ffn_down",
    )(act, xn, w_down)


def _layer(x2, l, batch, seq_len, rel_bias, norm1_g, w_in, w_pool, pool_scale, att_q_g, att_k_g,
           conv_dw_w, conv_dw_b, conv_ln_g, conv_ln_b, conv_pw, dn_conv_w, dn_a_log, dn_dt_bias,
           dn_norm_g, w_out, norm2_g, ffn_w_gate, ffn_dw_w, ffn_dw_b, ffn_w_up, ffn_w_down):
    w_main = _weight_bf16(w_in, l, cols=PROJ_MAIN)
    w_gates = _gate_weight_bf16(w_in, l)
    p, pg = _in_proj(x2, norm1_g[l][None], w_main, w_gates)
    ya = _pool_mixer(p, w_pool[l].astype(BF16), pool_scale[l][None], seq_len)
    yb = _attention_mixer(p, att_q_g[l], att_k_g[l], rel_bias, seq_len)
    yc = _conformer_mixer(p, conv_dw_w[l], conv_dw_b[l][None], conv_ln_g[l][None], conv_ln_b[l][None],
                          conv_pw[l].astype(BF16), seq_len)
    yd = _deltanet_mixer(p, pg, dn_conv_w[l], dn_a_log[l], dn_dt_bias[l], dn_norm_g[l], batch, seq_len)
    xn, hn = _out_proj(x2, (ya, yb, yc, yd), _weight_bf16(w_out, l), norm2_g[l][None])
    return _ffn(hn, xn, _weight_bf16(ffn_w_gate, l), ffn_dw_w[l], ffn_dw_b[l][None],
                _weight_bf16(ffn_w_up, l), _weight_bf16(ffn_w_down, l), seq_len)


def kernel(x, rel_bias, norm1_g, w_in, w_pool, pool_scale, att_q_g, att_k_g, conv_dw_w, conv_dw_b,
           conv_ln_g, conv_ln_b, conv_pw, dn_conv_w, dn_a_log, dn_dt_bias, dn_norm_g, w_out, norm2_g,
           ffn_w_gate, ffn_dw_w, ffn_dw_b, ffn_w_up, ffn_w_down):
    batch, seq_len, _ = x.shape
    x2 = x.reshape(batch * seq_len, D_MODEL)
    for l in range(norm1_g.shape[0]):
        x2 = _layer(x2, l, batch, seq_len, rel_bias, norm1_g, w_in, w_pool, pool_scale, att_q_g,
                    att_k_g, conv_dw_w, conv_dw_b, conv_ln_g, conv_ln_b, conv_pw, dn_conv_w, dn_a_log,
                    dn_dt_bias, dn_norm_g, w_out, norm2_g, ffn_w_gate, ffn_dw_w, ffn_dw_b, ffn_w_up,
                    ffn_w_down)
    return x2.reshape(batch, seq_len, D_MODEL)
```

```python
import functools
import math

import jax
import jax.numpy as jnp
import numpy as np
from jax import lax
from jax.experimental import pallas as pl
from jax.experimental.pallas import tpu as pltpu

F32 = jnp.float32
BF16 = jnp.bfloat16

D_MODEL = 2048
GROUP = 512
POOL_WINDOWS = (2, 4, 8, 16)
ATT_HEAD_DIM = 64
ATT_HEADS = 8
ATT_RADIUS = 64
ATT_DILATIONS = (1, 4, 16)
REL_BUCKETS = 32
REL_MAX_DIST = 1024
CONV_WIDTH = 31
DN_HEAD_DIM = 128
DN_HEADS = 4
DN_CHUNK = 64
FFN_HIDDEN = 5632
EPS = 1e-6
NEG_INF = -1e30

PROJ_MAIN = 10 * GROUP
GATE_LANES = 128

VMEM_LIMIT_BYTES = 56 * 1024 * 1024


def _cparams(*sem):
    return pltpu.CompilerParams(dimension_semantics=sem, vmem_limit_bytes=VMEM_LIMIT_BYTES)


def _sigmoid(x):
    return 1.0 / (1.0 + jnp.exp(-x))


def _silu(x):
    return x * _sigmoid(x)


def _dot(a, b):
    return jnp.dot(a, b, preferred_element_type=F32)


def _dot_nt(a, b):
    return lax.dot_general(a, b, (((1,), (1,)), ((), ())), preferred_element_type=F32)


def _dot_tn(a, b):
    return lax.dot_general(a, b, (((0,), (0,)), ((), ())), preferred_element_type=F32)


def _cast_kernel(w_ref, o_ref):
    o_ref[...] = w_ref[...].astype(BF16)


def _weight_bf16(w_stack, layer, cols=None, tr=512):
    _, rows, width = w_stack.shape
    cols = width if cols is None else cols
    tc = cols // 2 if cols >= 4096 else cols
    return pl.pallas_call(
        _cast_kernel,
        grid=(rows // tr, cols // tc),
        in_specs=[pl.BlockSpec((None, tr, tc), lambda i, j: (layer, i, j))],
        out_specs=pl.BlockSpec((tr, tc), lambda i, j: (i, j)),
        out_shape=jax.ShapeDtypeStruct((rows, cols), BF16),
        compiler_params=_cparams("parallel", "parallel"),
        name="weight_bf16",
    )(w_stack)


def _gate_cols_kernel(w_ref, o_ref, *, valid):
    lane = lax.broadcasted_iota(jnp.int32, o_ref.shape, 1)
    o_ref[...] = jnp.where(lane < valid, w_ref[...], 0.0).astype(BF16)


def _gate_weight_bf16(w_stack, layer):
    _, rows, width = w_stack.shape
    valid = width - PROJ_MAIN
    return pl.pallas_call(
        functools.partial(_gate_cols_kernel, valid=valid),
        grid=(1,),
        in_specs=[pl.BlockSpec((None, rows, GATE_LANES), lambda i: (layer, 0, PROJ_MAIN // GATE_LANES))],
        out_specs=pl.BlockSpec((rows, GATE_LANES), lambda i: (0, 0)),
        out_shape=jax.ShapeDtypeStruct((rows, GATE_LANES), BF16),
        compiler_params=_cparams("arbitrary"),
        name="gate_weight_bf16",
    )(w_stack)


def _inproj_kernel(x_ref, g_ref, w_ref, wg_ref, p_ref, pg_ref):
    x = x_ref[...]
    ms = jnp.mean(x * x, axis=-1, keepdims=True)
    h = (x * lax.rsqrt(ms + EPS) * g_ref[...]).astype(BF16)
    pg_ref[...] = _dot(h, wg_ref[...])
    p_ref[...] = _dot(h, w_ref[...])


def _in_proj(x2, norm_g, w_main, w_gate, tm=512):
    T = x2.shape[0]
    resident = lambda shape: pl.BlockSpec(shape, lambda i: (0, 0), pipeline_mode=pl.Buffered(1))
    return pl.pallas_call(
        _inproj_kernel,
        grid=(T // tm,),
        in_specs=[
            pl.BlockSpec((tm, D_MODEL), lambda i: (i, 0)),
            resident((1, D_MODEL)),
            resident((D_MODEL, PROJ_MAIN)),
            resident((D_MODEL, GATE_LANES)),
        ],
        out_specs=[
            pl.BlockSpec((tm, PROJ_MAIN), lambda i: (i, 0)),
            pl.BlockSpec((tm, GATE_LANES), lambda i: (i, 0)),
        ],
        out_shape=[
            jax.ShapeDtypeStruct((T, PROJ_MAIN), F32),
            jax.ShapeDtypeStruct((T, GATE_LANES), F32),
        ],
        compiler_params=_cparams("parallel"),
        name="in_proj",
    )(x2, norm_g, w_main, w_gate)


def _halo_specs(tm, halo, width, col_block, n_tiles):
    r = tm // halo
    last = n_tiles * r - 1
    prev_spec = pl.BlockSpec((halo, width), lambda i, *_: (jnp.maximum(i * r - 1, 0), col_block))
    next_spec = pl.BlockSpec((halo, width), lambda i, *_: (jnp.minimum((i + 1) * r, last), col_block))
    return prev_spec, next_spec


def _seq_edges(i, tiles_per_seq):
    k = lax.rem(i, tiles_per_seq)
    return k == 0, k == tiles_per_seq - 1, k


def _pool_kernel(u_ref, up_ref, un_ref, w_ref, sc_ref, y_ref, ext_scr, *, tm, seq_len):
    i = pl.program_id(0)
    first, last, k = _seq_edges(i, seq_len // tm)
    ext_scr[0:8, :] = jnp.where(first, jnp.zeros_like(up_ref[...]), up_ref[...])
    ext_scr[8:8 + tm, :] = u_ref[...]
    ext_scr[8 + tm:16 + tm, :] = jnp.where(last, jnp.zeros_like(un_ref[...]), un_ref[...])
    t = k * tm + lax.broadcasted_iota(jnp.int32, (tm, 1), 0)
    for gi, win in enumerate(POOL_WINDOWS):
        half = win // 2
        lanes = slice(gi * 128, (gi + 1) * 128)
        s = ext_scr[pl.ds(8 - half, tm), lanes]
        for kk in range(1, win):
            s = s + ext_scr[pl.ds(8 - half + kk, tm), lanes]
        cnt = (jnp.minimum(t + half, seq_len) - jnp.maximum(t - half, 0)).astype(F32)
        pooled = s / cnt - ext_scr[pl.ds(8, tm), lanes]
        y = _dot(pooled.astype(BF16), w_ref[gi]) * sc_ref[:, lanes]
        y_ref[:, lanes] = y.astype(BF16)


def _pool_mixer(p, w_pool, pool_scale, seq_len, tm=1024):
    T = p.shape[0]
    n_tiles = T // tm
    prev_spec, next_spec = _halo_specs(tm, 8, GROUP, 0, n_tiles)
    return pl.pallas_call(
        functools.partial(_pool_kernel, tm=tm, seq_len=seq_len),
        grid=(n_tiles,),
        in_specs=[
            pl.BlockSpec((tm, GROUP), lambda i: (i, 0)),
            prev_spec,
            next_spec,
            pl.BlockSpec((4, 128, 128), lambda i: (0, 0, 0)),
            pl.BlockSpec((1, GROUP), lambda i: (0, 0)),
        ],
        out_specs=pl.BlockSpec((tm, GROUP), lambda i: (i, 0)),
        out_shape=jax.ShapeDtypeStruct((T, GROUP), BF16),
        scratch_shapes=[pltpu.VMEM((tm + 16, GROUP), F32)],
        compiler_params=_cparams("parallel"),
        name="pool_mixer",
    )(p, p, p, w_pool, pool_scale)


def _conformer_kernel(v_ref, vp_ref, vn_ref, g_ref, gp_ref, gn_ref, dw_ref, db_ref, lg_ref, lb_ref,
                      pw_ref, y_ref, ext_scr, shift_scr, *, tm, seq_len):
    i = pl.program_id(0)
    first, last, _ = _seq_edges(i, seq_len // tm)
    hp = vp_ref[...] * _sigmoid(gp_ref[...])
    hn = vn_ref[...] * _sigmoid(gn_ref[...])
    ext_scr[0:16, :] = jnp.where(first, jnp.zeros_like(hp), hp)
    ext_scr[16:16 + tm, :] = v_ref[...] * _sigmoid(g_ref[...])
    ext_scr[16 + tm:32 + tm, :] = jnp.where(last, jnp.zeros_like(hn), hn)
    base = 16 - CONV_WIDTH // 2
    acc = db_ref[...]
    for b in range(8):
        taps = [(a, 8 * a + b - base) for a in range(5) if 0 <= 8 * a + b - base < CONV_WIDTH]
        rows = tm + 8 * taps[-1][0]
        shift_scr[b, 0:rows, :] = ext_scr[pl.ds(b, rows), :]
        for a, kk in taps:
            acc = acc + shift_scr[b, 8 * a:8 * a + tm, :] * dw_ref[kk:kk + 1, :]
    mu = jnp.mean(acc, axis=-1, keepdims=True)
    xc = acc - mu
    var = jnp.mean(xc * xc, axis=-1, keepdims=True)
    h = _silu(xc * lax.rsqrt(var + EPS) * lg_ref[...] + lb_ref[...])
    y_ref[...] = _dot(h.astype(BF16), pw_ref[...]).astype(BF16)


def _conformer_mixer(p, dw_w, dw_b, ln_g, ln_b, pw, seq_len, tm=512):
    T = p.shape[0]
    n_tiles = T // tm
    vprev, vnext = _halo_specs(tm, 16, GROUP, 4, n_tiles)
    gprev, gnext = _halo_specs(tm, 16, GROUP, 5, n_tiles)
    const = lambda shape: pl.BlockSpec(shape, lambda i: (0,) * len(shape))
    return pl.pallas_call(
        functools.partial(_conformer_kernel, tm=tm, seq_len=seq_len),
        grid=(n_tiles,),
        in_specs=[
            pl.BlockSpec((tm, GROUP), lambda i: (i, 4)), vprev, vnext,
            pl.BlockSpec((tm, GROUP), lambda i: (i, 5)), gprev, gnext,
            const((CONV_WIDTH, GROUP)), const((1, GROUP)), const((1, GROUP)), const((1, GROUP)),
            const((GROUP, GROUP)),
        ],
        out_specs=pl.BlockSpec((tm, GROUP), lambda i: (i, 0)),
        out_shape=jax.ShapeDtypeStruct((T, GROUP), BF16),
        scratch_shapes=[pltpu.VMEM((tm + 32, GROUP), F32), pltpu.VMEM((8, tm + 32, GROUP), F32)],
        compiler_params=_cparams("parallel"),
        name="conformer_mixer",
    )(p, p, p, p, p, p, dw_w, dw_b, ln_g, ln_b, pw)


ATT_TILE = 1024
LOG2E = math.log2(math.e)
ATT_PIPELINE_DEPTH = 3


def _t5_bucket_table():
    nb = REL_BUCKETS // 2
    max_exact = nb // 2
    i = np.arange(ATT_RADIUS)[:, None]
    j = np.arange(3 * ATT_RADIUS)[None, :]
    off = j - ATT_RADIUS - i
    tables = []
    for dil in ATT_DILATIONS:
        rel = off * dil
        n = np.abs(rel)
        nf = np.maximum(n, 1).astype(np.float32)
        large = max_exact + (np.log(nf / np.float32(max_exact)) / np.float32(math.log(REL_MAX_DIST / max_exact))
                             * np.float32(nb - max_exact)).astype(np.int32)
        large = np.minimum(large, nb - 1)
        bucket = np.where(rel > 0, nb, 0) + np.where(n < max_exact, n, large)
        tables.append(np.where(np.abs(off) <= ATT_RADIUS, bucket, -1))
    return np.stack(tables).astype(np.int32)


def _att_kernel(rb_ref, bkt_ref, qg_ref, kg_ref, mseg_ref, q_ref, kc_ref, kn_ref,
                vp_ref, vc_ref, vn_ref, y_ref, bias_scr, qbuf, kbuf, vbuf, acc_scr, m_scr, l_scr,
                *, seq_len):
    tile = ATT_TILE
    rad = ATT_RADIUS
    hp = pl.program_id(0)
    i = pl.program_id(1)

    def rms(x, g):
        ms = _dot((x * x).astype(BF16), mseg_ref[...])
        return x * lax.rsqrt(ms + EPS) * g

    @pl.when(i == 0)
    def _():
        lane_head0 = lax.broadcasted_iota(jnp.int32, (1, 128), 1) < ATT_HEAD_DIM
        for di in range(len(ATT_DILATIONS)):
            bkt = bkt_ref[di]
            b = jnp.zeros(bkt.shape, F32)
            for bb in range(REL_BUCKETS):
                b = jnp.where(bkt == bb, jnp.where(lane_head0, rb_ref[bb, 2 * hp], rb_ref[bb, 2 * hp + 1]), b)
            bias_scr[di] = jnp.where(bkt < 0, NEG_INF, b * LOG2E)
        kbuf[0:tile, :] = jnp.zeros((tile, 128), F32)
        kbuf[tile:2 * tile, :] = rms(kc_ref[...], kg_ref[...])

    @pl.when(i > 0)
    def _():
        kbuf[0:tile, :] = kbuf[tile:2 * tile, :]
        kbuf[tile:2 * tile, :] = kbuf[2 * tile:3 * tile, :]

    first, last, _ = _seq_edges(i, seq_len // tile)

    qbuf[...] = rms(q_ref[...], qg_ref[...]) * (ATT_HEAD_DIM ** -0.5 * LOG2E)
    kbuf[2 * tile:3 * tile, :] = rms(kn_ref[...], kg_ref[...])
    vbuf[0:tile, :] = vp_ref[...]
    vbuf[tile:2 * tile, :] = vc_ref[...]
    vbuf[2 * tile:3 * tile, :] = vn_ref[...]

    head0 = lax.broadcasted_iota(jnp.int32, (1, 128), 1) < ATT_HEAD_DIM
    key_row = lax.broadcasted_iota(jnp.int32, (3 * rad, 1), 0)
    ones_cols = jnp.ones((3 * rad, 128), BF16)

    def rows(start, size, dil):
        return pl.ds(start, size) if dil == 1 else pl.ds(start, size, stride=dil)

    blocks = [(di, dil, r + rad * dil * m)
              for di, dil in enumerate(ATT_DILATIONS) for r in range(dil) for m in range(tile // (rad * dil))]
    def score_stage(blk):
        di, dil, qstart = blk
        kstart = tile + qstart - rad * dil
        qb = qbuf[rows(qstart, rad, dil), :]
        kb = kbuf[rows(kstart, 3 * rad, dil), :].astype(BF16)
        q2 = jnp.concatenate([jnp.where(head0, qb, 0.0), jnp.where(head0, 0.0, qb)], axis=0)
        return _dot_nt(kb, q2.astype(BF16))

    def softmax_stage(blk, s):
        di, dil, qstart = blk
        kstart = tile + qstart - rad * dil
        s = s + bias_scr[di]
        n_prev = max(0, -(-(tile - kstart) // dil))
        n_upto = min(3 * rad, -(-(2 * tile - kstart) // dil))
        if n_prev > 0:
            s = jnp.where(key_row < jnp.where(first, n_prev, 0), NEG_INF, s)
        if n_upto < 3 * rad:
            s = jnp.where(key_row >= jnp.where(last, n_upto, 3 * rad), NEG_INF, s)
        mx = jnp.max(s, axis=0, keepdims=True)
        return mx, jnp.exp2(s - mx).astype(BF16)

    def value_stage(blk, pe):
        di, dil, qstart = blk
        kstart = tile + qstart - rad * dil
        vb = vbuf[rows(kstart, 3 * rad, dil), :].astype(BF16)
        return _dot_tn(pe, jnp.concatenate([vb, ones_cols], axis=1))

    def store_stage(blk, mx, ov):
        di, dil, qstart = blk
        dst = rows(qstart, rad, dil)
        m_col = jnp.broadcast_to(mx, (2 * rad, 128)).T
        acc_scr[di, dst, :] = jnp.where(head0, ov[0:rad, 0:128], ov[rad:2 * rad, 0:128])
        l_scr[di, dst, :] = jnp.where(head0, ov[0:rad, 128:256], ov[rad:2 * rad, 128:256])
        m_scr[di, dst, :] = jnp.where(head0, m_col[0:rad], m_col[rad:2 * rad])

    n_blk = len(blocks)
    scores = {b: score_stage(blocks[b]) for b in range(min(ATT_PIPELINE_DEPTH, n_blk))}
    pending = None
    for b in range(n_blk):
        mx, pe = softmax_stage(blocks[b], scores.pop(b))
        ov = value_stage(blocks[b], pe)
        if b + ATT_PIPELINE_DEPTH < n_blk:
            scores[b + ATT_PIPELINE_DEPTH] = score_stage(blocks[b + ATT_PIPELINE_DEPTH])
        if pending is not None:
            store_stage(*pending)
        pending = (blocks[b], mx, ov)
    store_stage(*pending)

    m_all = jnp.maximum(jnp.maximum(m_scr[0], m_scr[1]), m_scr[2])
    num = jnp.zeros((tile, 128), F32)
    den = jnp.zeros((tile, 128), F32)
    for di in range(len(ATT_DILATIONS)):
        e = jnp.exp2(m_scr[di] - m_all)
        num = num + acc_scr[di] * e
        den = den + l_scr[di] * e
    y_ref[...] = (num / den).astype(BF16)


def _attention_mixer(p, q_g, k_g, rel_bias, seq_len):
    T = p.shape[0]
    tile = ATT_TILE
    n_tiles = T // tile
    rad = ATT_RADIUS
    seg = np.kron(np.eye(2), np.full((ATT_HEAD_DIM, ATT_HEAD_DIM), 1.0 / ATT_HEAD_DIM))
    mseg = jnp.asarray(seg, BF16)
    bkt_t = np.transpose(_t5_bucket_table(), (0, 2, 1))
    bkt = jnp.asarray(np.concatenate([bkt_t, bkt_t], axis=2))
    qg2 = jnp.tile(q_g, 2)[None]
    kg2 = jnp.tile(k_g, 2)[None]

    def blk(col0, shift):
        return pl.BlockSpec((tile, 128),
                            lambda hp, i: (jnp.clip(i + shift, 0, n_tiles - 1), col0 * 4 + hp))

    const = lambda shape: pl.BlockSpec(shape, lambda hp, i: (0,) * len(shape))
    return pl.pallas_call(
        functools.partial(_att_kernel, seq_len=seq_len),
        grid=(4, n_tiles),
        in_specs=[
            pl.BlockSpec(memory_space=pltpu.SMEM),
            const((3, 3 * rad, 2 * rad)), const((1, 128)), const((1, 128)), const((128, 128)),
            blk(1, 0),
            blk(2, 0), blk(2, 1),
            blk(3, -1), blk(3, 0), blk(3, 1),
        ],
        out_specs=pl.BlockSpec((tile, 128), lambda hp, i: (i, hp)),
        out_shape=jax.ShapeDtypeStruct((T, GROUP), BF16),
        scratch_shapes=[
            pltpu.VMEM((3, 3 * rad, 2 * rad), F32),
            pltpu.VMEM((tile, 128), F32),
            pltpu.VMEM((3 * tile, 128), F32),
            pltpu.VMEM((3 * tile, 128), F32),
            pltpu.VMEM((3, tile, 128), F32),
            pltpu.VMEM((3, tile, 128), F32),
            pltpu.VMEM((3, tile, 128), F32),
        ],
        compiler_params=_cparams("arbitrary", "arbitrary"),
        name="dilated_attention",
    )(rel_bias, bkt, qg2, kg2, mseg, p, p, p, p, p, p)


def _softplus(x):
    return jnp.maximum(x, 0.0) + jnp.log1p(jnp.exp(-jnp.abs(x)))


def _dn_prep_kernel(x_ref, xp_ref, xn_ref, w_ref, o_ref, ext_scr, *, tm, seq_len):
    i = pl.program_id(0)
    first, last, _ = _seq_edges(i, seq_len // tm)
    ext_scr[0:8, :] = jnp.where(first, jnp.zeros_like(xp_ref[...]), xp_ref[...])
    ext_scr[8:8 + tm, :] = x_ref[...]
    ext_scr[8 + tm:16 + tm, :] = jnp.where(last, jnp.zeros_like(xn_ref[...]), xn_ref[...])
    for cb in range(3 * DN_HEADS):
        lanes = slice(cb * 128, (cb + 1) * 128)
        acc = ext_scr[pl.ds(6, tm), lanes] * w_ref[0:1, lanes]
        for kk in range(1, 4):
            acc = acc + ext_scr[pl.ds(6 + kk, tm), lanes] * w_ref[kk:kk + 1, lanes]
        y = _silu(acc)
        if cb < 2 * DN_HEADS:
            y = y * lax.rsqrt(jnp.sum(y * y, axis=-1, keepdims=True) + EPS)
        if cb < DN_HEADS:
            y = y * (DN_HEAD_DIM ** -0.5)
        o_ref[:, lanes] = y


def _dn_prep(p, conv_w, seq_len, tm=512):
    T = p.shape[0]
    n_tiles = T // tm
    width = 3 * GROUP
    prev_spec, next_spec = _halo_specs(tm, 8, width, 2, n_tiles)
    return pl.pallas_call(
        functools.partial(_dn_prep_kernel, tm=tm, seq_len=seq_len),
        grid=(n_tiles,),
        in_specs=[pl.BlockSpec((tm, width), lambda i: (i, 2)), prev_spec, next_spec,
                  pl.BlockSpec((4, width), lambda i: (0, 0))],
        out_specs=pl.BlockSpec((tm, width), lambda i: (i, 0)),
        out_shape=jax.ShapeDtypeStruct((T, width), F32),
        scratch_shapes=[pltpu.VMEM((tm + 16, width), F32)],
        compiler_params=_cparams("parallel"),
        name="deltanet_prep",
    )(p, p, p, conv_w)


DN_GROUP = 8
DN_ROWS = DN_HEADS * DN_CHUNK
DN_SCAN_LAG_STEPS = 4


def _dn_scan_kernel(xf_ref, xb_ref, gf_ref, gb_ref, rf_ref, rb_ref, alc_ref, dtc_ref, alr_ref, dtr_ref,
                    of_ref, ob_ref, s_scr):
    C = DN_CHUNK
    H = DN_HEADS
    R = DN_GROUP * C

    @pl.when(pl.program_id(1) == 0)
    def _():
        s_scr[...] = jnp.zeros_like(s_scr)

    row_in_chunk = lax.rem(lax.broadcasted_iota(jnp.int32, (R, 1), 0), C)
    lane_in_chunk = lax.rem(lax.broadcasted_iota(jnp.int32, (1, DN_ROWS), 1), C)
    rid = lax.broadcasted_iota(jnp.int32, (DN_ROWS, DN_ROWS), 0)
    cid = lax.broadcasted_iota(jnp.int32, (DN_ROWS, DN_ROWS), 1)
    same_head = (rid // C) == (cid // C)
    row_sbs = lax.broadcasted_iota(jnp.int32, (C, DN_ROWS), 0)
    col_sbs = lax.rem(lax.broadcasted_iota(jnp.int32, (C, DN_ROWS), 1), C)
    lane_head = lax.broadcasted_iota(jnp.int32, (1, DN_ROWS), 1) // C

    def block_diag(m):
        return jnp.where(same_head, jnp.concatenate([m] * H, axis=0), jnp.zeros((), m.dtype))

    def side_by_side_blocks(x):
        return jnp.concatenate([x[h * C:(h + 1) * C] for h in range(H)], axis=1)

    def block_diag_blocks(x):
        zero = jnp.zeros((C, 128), x.dtype)
        return jnp.concatenate(
            [jnp.concatenate([x[h * C:(h + 1) * C] if g == h else zero for g in range(H)], axis=1)
             for h in range(H)], axis=0)

    def stack_heads(x, col0):
        return jnp.concatenate([x[:, col0 + h * 128:col0 + (h + 1) * 128] for h in range(H)], axis=0)

    def stack_cols(x, lane0, rows=None):
        parts = []
        for h in range(H):
            c = x[:, lane0 + h:lane0 + h + 1]
            if rows is not None:
                c = c[rows:rows + 1, :]
            parts.append(jnp.broadcast_to(c, (C, 128)))
        return jnp.concatenate(parts, axis=0)

    refs = ((xf_ref, gf_ref, rf_ref, of_ref), (xb_ref, gb_ref, rb_ref, ob_ref))
    gates = []
    for d, (x_ref, gc_ref, gr_ref, o_ref) in enumerate(refs):
        reverse = d == 1
        gcol = gc_ref[...]
        g = -jnp.exp(alc_ref[...]) * _softplus(gcol + dtc_ref[...])
        beta = _sigmoid(gcol)
        gcum = g
        for sh in (1, 2, 4, 8, 16, 32):
            if reverse:
                gcum = gcum + jnp.where(row_in_chunk < C - sh, pltpu.roll(gcum, R - sh, 0), 0.0)
            else:
                gcum = gcum + jnp.where(row_in_chunk >= sh, pltpu.roll(gcum, sh, 0), 0.0)
        grow_all = -jnp.exp(alr_ref[...])[None] * _softplus(gr_ref[...] + dtr_ref[...][None])
        for sh in (1, 2, 4, 8, 16, 32):
            if reverse:
                grow_all = grow_all + jnp.where(lane_in_chunk < C - sh,
                                                pltpu.roll(grow_all, DN_ROWS - sh, 2), 0.0)
            else:
                grow_all = grow_all + jnp.where(lane_in_chunk >= sh, pltpu.roll(grow_all, sh, 2), 0.0)

        gates.append((gcum, beta, grow_all))

    def fill_operands(u):
        d = u["d"]
        x_ref, _, _, o_ref = refs[d]
        reverse = d == 1
        ci = DN_GROUP - 1 - u["step"] if reverse else u["step"]
        gcum, beta, grow_all = gates[d]
        rows = slice(ci * C, (ci + 1) * C)
        x = x_ref[rows, :]
        u.update(rows=rows, o_ref=o_ref)
        u["q"] = stack_heads(x, 0)
        u["k"] = stack_heads(x, GROUP)
        v_st = stack_heads(x, 2 * GROUP)
        gc_c = gcum[rows, :]
        beta_st = stack_cols(beta[rows, :], 8 + d * H)
        gcol_st = stack_cols(gc_c, d * H)
        glast_st = stack_cols(gc_c, d * H, rows=0 if reverse else C - 1)
        grow = grow_all[ci, d:d + 1, :]
        gcol_sbs = jnp.broadcast_to(gc_c[:, d * H + H - 1:d * H + H], (C, DN_ROWS))
        for h in range(H - 2, -1, -1):
            gcol_sbs = jnp.where(lane_head <= h, gc_c[:, d * H + h:d * H + h + 1], gcol_sbs)
        u["incl"] = (row_sbs <= col_sbs) if reverse else (row_sbs >= col_sbs)
        u["decay"] = jnp.where(u["incl"], jnp.exp(gcol_sbs - grow), 0.0)
        kb_st = u["k"] * beta_st
        eg = jnp.exp(gcol_st)
        u["kq"] = jnp.concatenate([side_by_side_blocks(kb_st), side_by_side_blocks(u["q"])],
                                  axis=0).astype(BF16)
        u["kbd"] = block_diag_blocks(u["k"].astype(BF16))
        u["rhs"] = jnp.concatenate([v_st * beta_st, kb_st * eg], axis=1).astype(BF16)
        u["qdec"] = u["q"] * eg
        u["kdec"] = (u["k"] * jnp.exp(glast_st - gcol_st)).astype(BF16)
        u["gl"] = jnp.exp(glast_st)

    eye_sbs = (row_sbs == col_sbs).astype(F32)

    def prep_stages(units):
        for u in units:
            fill_operands(u)
        yield
        for n, u in enumerate(units):
            u["kk"] = _dot_nt(u["kq"], u["kbd"])
            if n == len(units) // 2 - 1:
                yield
        yield
        for u in units:
            strict = u["incl"] & (row_sbs != col_sbs)
            a_sbs = jnp.where(strict, u["kk"][0:C] * u["decay"], 0.0)
            u["attn"] = (u["kk"][C:2 * C] * u["decay"]).astype(BF16)
            u["pinv"] = eye_sbs - a_sbs
            u["a_sbs"] = a_sbs.astype(BF16)
        for u in units:
            u["apow"] = _dot(u["a_sbs"], block_diag(u["a_sbs"]))
        yield
        for _ in range(4):
            for n, u in enumerate(units):
                ap = u["apow"].astype(BF16)
                u["both"] = _dot(jnp.concatenate([u["pinv"].astype(BF16), ap], axis=0), block_diag(ap))
                if n == len(units) // 2 - 1:
                    yield
            yield
            for u in units:
                u["pinv"] = u["pinv"] + u["both"][0:C]
                u["apow"] = u["both"][C:2 * C]
        for u in units:
            u["last"] = _dot(u["pinv"].astype(BF16), block_diag(u["apow"].astype(BF16)))
        yield
        for u in units:
            u["uw"] = _dot(block_diag((u["pinv"] + u["last"]).astype(BF16)), u["rhs"])
        yield

    def scan_stages(units):
        for i in range(0, len(units), 2):
            pair = units[i:i + 2]
            for u in pair:
                d = u["d"]
                u["wq"] = []
                for h in range(H):
                    hr = slice(h * C, (h + 1) * C)
                    lhs = jnp.concatenate([u["uw"][hr, 128:256], u["qdec"][hr]], axis=0).astype(BF16)
                    u["wq"].append(_dot(lhs, s_scr[d * H + h].astype(BF16)))
            yield
            for u in pair:
                u["vnew"] = [(u["uw"][h * C:(h + 1) * C, 0:128] - u["wq"][h][0:C]).astype(BF16)
                             for h in range(H)]
                vnew_bd = block_diag_blocks(jnp.concatenate(u["vnew"], axis=0))
                u["o"] = jnp.concatenate([w[C:2 * C] for w in u["wq"]], axis=1) + _dot(u["attn"], vnew_bd)
            yield
            for u in pair:
                d = u["d"]
                for h in range(H):
                    hr = slice(h * C, (h + 1) * C)
                    s_scr[d * H + h] = (s_scr[d * H + h] * u["gl"][h * C:h * C + 1, :]
                                        + _dot_tn(u["kdec"][hr], u["vnew"][h]))
                u["o_ref"][u["rows"], :] = u["o"]
            yield

    def run_together(*stage_generators):
        live = list(stage_generators)
        while live:
            for gen in list(live):
                if next(gen, StopIteration) is StopIteration:
                    live.remove(gen)

    batches = [[dict(step=step, d=d) for step in range(s0, s0 + DN_SCAN_LAG_STEPS) for d in range(2)]
               for s0 in range(0, DN_GROUP, DN_SCAN_LAG_STEPS)]
    run_together(prep_stages(batches[0]))
    for prev, cur in zip(batches[:-1], batches[1:]):
        run_together(prep_stages(cur), scan_stages(prev))
    run_together(scan_stages(batches[-1]))


def _dn_scan(qkvn, pg, a_log, dt_bias, batch, seq_len):
    T = qkvn.shape[0]
    C, H, G = DN_CHUNK, DN_HEADS, DN_GROUP
    R = G * C
    nc = seq_len // C
    ncg = nc // G
    ab = pg[:, 0:4 * H].reshape(batch, nc, C, 2, 2, H)
    ab_row = jnp.transpose(ab, (0, 1, 3, 4, 5, 2)).reshape(batch * nc, 4, H * C)
    ab_row = jnp.pad(ab_row, ((0, 0), (0, 4), (0, 0)))
    pad_lanes = lambda v: jnp.pad(v.reshape(1, 2 * H), ((0, 0), (0, GATE_LANES - 2 * H)))
    row_param = lambda v: jnp.pad(jnp.repeat(v, C, axis=1), ((0, 6), (0, 0)))

    fwd = lambda b, c: (b * ncg + c, 0)
    bwd = lambda b, c: (b * ncg + ncg - 1 - c, 0)
    fwd3 = lambda b, c: (b * ncg + c, 0, 0)
    bwd3 = lambda b, c: (b * ncg + ncg - 1 - c, 0, 0)
    const = lambda shape: pl.BlockSpec(shape, lambda b, c: (0,) * len(shape))
    return pl.pallas_call(
        _dn_scan_kernel,
        grid=(batch, ncg),
        in_specs=[
            pl.BlockSpec((R, 3 * GROUP), fwd), pl.BlockSpec((R, 3 * GROUP), bwd),
            pl.BlockSpec((R, GATE_LANES), fwd), pl.BlockSpec((R, GATE_LANES), bwd),
            pl.BlockSpec((G, 8, H * C), fwd3), pl.BlockSpec((G, 8, H * C), bwd3),
            const((1, GATE_LANES)), const((1, GATE_LANES)), const((8, H * C)), const((8, H * C)),
        ],
        out_specs=[pl.BlockSpec((R, GROUP), fwd), pl.BlockSpec((R, GROUP), bwd)],
        out_shape=[jax.ShapeDtypeStruct((T, GROUP), F32), jax.ShapeDtypeStruct((T, GROUP), F32)],
        scratch_shapes=[pltpu.VMEM((2 * H, DN_HEAD_DIM, DN_HEAD_DIM), F32)],
        compiler_params=_cparams("arbitrary", "arbitrary"),
        name="deltanet_scan",
    )(qkvn, qkvn, pg, pg, ab_row, ab_row, pad_lanes(a_log), pad_lanes(dt_bias),
      row_param(a_log), row_param(dt_bias))


def _dn_post_kernel(of_ref, ob_ref, z_ref, g_ref, y_ref):
    for h in range(DN_HEADS):
        lanes = slice(h * 128, (h + 1) * 128)
        o = of_ref[:, lanes] + ob_ref[:, lanes]
        o = o * lax.rsqrt(jnp.mean(o * o, axis=-1, keepdims=True) + EPS) * g_ref[...]
        y_ref[:, lanes] = (o * _silu(z_ref[:, lanes])).astype(BF16)


def _dn_post(o_f, o_b, p, norm_g, tm=1024):
    T = p.shape[0]
    row = lambda cb: pl.BlockSpec((tm, GROUP), lambda i: (i, cb))
    return pl.pallas_call(
        _dn_post_kernel,
        grid=(T // tm,),
        in_specs=[row(0), row(0), row(9), pl.BlockSpec((1, DN_HEAD_DIM), lambda i: (0, 0))],
        out_specs=row(0),
        out_shape=jax.ShapeDtypeStruct((T, GROUP), BF16),
        compiler_params=_cparams("parallel"),
        name="deltanet_post",
    )(o_f, o_b, p, norm_g)


def _deltanet_mixer(p, pg, conv_w, a_log, dt_bias, norm_g, batch, seq_len):
    qkvn = _dn_prep(p, conv_w, seq_len)
    o_f, o_b = _dn_scan(qkvn, pg, a_log, dt_bias, batch, seq_len)
    return _dn_post(o_f, o_b, p, norm_g[None])


def _outproj_kernel(x_ref, ya_ref, yb_ref, yc_ref, yd_ref, w_ref, g_ref, xn_ref, hn_ref, ycat_scr):
    for gi, y_ref in enumerate((ya_ref, yb_ref, yc_ref, yd_ref)):
        ycat_scr[:, gi * GROUP:(gi + 1) * GROUP] = y_ref[...]
    half = x_ref.shape[0] // 2
    for r in (slice(0, half), slice(half, 2 * half)):
        acc = x_ref[r, :] + _dot(ycat_scr[r, :], w_ref[...])
        xn_ref[r, :] = acc
        ms = jnp.mean(acc * acc, axis=-1, keepdims=True)
        hn_ref[r, :] = (acc * lax.rsqrt(ms + EPS) * g_ref[...]).astype(BF16)


def _out_proj(x2, ys, w_out, norm2_g, tm=512):
    T = x2.shape[0]
    row = lambda width: pl.BlockSpec((tm, width), lambda i: (i, 0))
    return pl.pallas_call(
        _outproj_kernel,
        grid=(T // tm,),
        in_specs=[row(D_MODEL), row(GROUP), row(GROUP), row(GROUP), row(GROUP),
                  pl.BlockSpec((D_MODEL, D_MODEL), lambda i: (0, 0), pipeline_mode=pl.Buffered(1)),
                  pl.BlockSpec((1, D_MODEL), lambda i: (0, 0))],
        out_specs=[row(D_MODEL), row(D_MODEL)],
        out_shape=[jax.ShapeDtypeStruct((T, D_MODEL), F32),
                   jax.ShapeDtypeStruct((T, D_MODEL), BF16)],
        scratch_shapes=[pltpu.VMEM((tm, D_MODEL), BF16)],
        compiler_params=_cparams("parallel"),
        name="out_proj",
    )(x2, *ys, w_out, norm2_g)


FFN_HALF = FFN_HIDDEN // 2
FFN_COL_CHUNK = 256


def _ffn_hidden_kernel(hn_ref, hp_ref, hx_ref, wg_ref, wu_ref, dw_ref, db_ref, a_ref, hext_scr, g_scr,
                       *, tm, seq_len):
    first, last, _ = _seq_edges(pl.program_id(1), seq_len // tm)
    hext_scr[0:16, :] = jnp.where(first, jnp.zeros_like(hp_ref[...]), hp_ref[...])
    hext_scr[16:16 + tm, :] = hn_ref[...]
    hext_scr[16 + tm:32 + tm, :] = jnp.where(last, jnp.zeros_like(hx_ref[...]), hx_ref[...])
    for c in range(FFN_HALF // FFN_COL_CHUNK):
        cols = slice(c * FFN_COL_CHUNK, (c + 1) * FFN_COL_CHUNK)
        g = g_scr.at[c % 2]
        g[...] = _dot(hext_scr[...], wg_ref[:, cols])
        up = _dot(hn_ref[...], wu_ref[:, cols])
        gate = (g[pl.ds(15, tm), :] * dw_ref[0:1, cols] + g[pl.ds(16, tm), :] * dw_ref[1:2, cols]
                + g[pl.ds(17, tm), :] * dw_ref[2:3, cols] + db_ref[:, cols])
        a_ref[:, cols] = (_silu(gate) * up).astype(BF16)


def _ffn_down_kernel(a_ref, xn_ref, wd_ref, o_ref):
    o_ref[...] = xn_ref[...] + _dot(a_ref[...], wd_ref[...])


def _ffn(hn, xn, w_gate, dw_w, dw_b, w_up, w_down, seq_len, tm_hidden=1024, tm_down=512):
    T = hn.shape[0]
    n_tiles = T // tm_hidden
    r = tm_hidden // 16
    half_cols = lambda rows: pl.BlockSpec((rows, FFN_HALF), lambda h, i: (0, h),
                                          pipeline_mode=pl.Buffered(1))
    act = pl.pallas_call(
        functools.partial(_ffn_hidden_kernel, tm=tm_hidden, seq_len=seq_len),
        grid=(2, n_tiles),
        in_specs=[
            pl.BlockSpec((tm_hidden, D_MODEL), lambda h, i: (i, 0)),
            pl.BlockSpec((16, D_MODEL), lambda h, i: (jnp.maximum(i * r - 1, 0), 0)),
            pl.BlockSpec((16, D_MODEL), lambda h, i: (jnp.minimum((i + 1) * r, n_tiles * r - 1), 0)),
            half_cols(D_MODEL), half_cols(D_MODEL), half_cols(3), half_cols(1),
        ],
        out_specs=pl.BlockSpec((tm_hidden, FFN_HALF), lambda h, i: (i, h)),
        out_shape=jax.ShapeDtypeStruct((T, FFN_HIDDEN), BF16),
        scratch_shapes=[pltpu.VMEM((tm_hidden + 32, D_MODEL), BF16),
                        pltpu.VMEM((2, tm_hidden + 32, FFN_COL_CHUNK), F32)],
        compiler_params=_cparams("arbitrary", "arbitrary"),
        name="ffn_hidden",
    )(hn, hn, hn, w_gate, w_up, dw_w, dw_b)
    return pl.pallas_call(
        _ffn_down_kernel,
        grid=(T // tm_down,),
        in_specs=[
            pl.BlockSpec((tm_down, FFN_HIDDEN), lambda i: (i, 0)),
            pl.BlockSpec((tm_down, D_MODEL), lambda i: (i, 0)),
            pl.BlockSpec((FFN_HIDDEN, D_MODEL), lambda i: (0, 0), pipeline_mode=pl.Buffered(1)),
        ],
        out_specs=pl.BlockSpec((tm_down, D_MODEL), lambda i: (i, 0)),
        out_shape=jax.ShapeDtypeStruct((T, D_MODEL), F32),
        compiler_params=_cparams("parallel"),
        name="ffn_down",
    )(act, xn, w_down)


def _layer(x2, l, batch, seq_len, rel_bias, norm1_g, w_in, w_pool, pool_scale, att_q_g, att_k_g,
           conv_dw_w, conv_dw_b, conv_ln_g, conv_ln_b, conv_pw, dn_conv_w, dn_a_log, dn_dt_bias,
           dn_norm_g, w_out, norm2_g, ffn_w_gate, ffn_dw_w, ffn_dw_b, ffn_w_up, ffn_w_down):
    w_main = _weight_bf16(w_in, l, cols=PROJ_MAIN)
    w_gates = _gate_weight_bf16(w_in, l)
    p, pg = _in_proj(x2, norm1_g[l][None], w_main, w_gates)
    ya = _pool_mixer(p, w_pool[l].astype(BF16), pool_scale[l][None], seq_len)
    yb = _attention_mixer(p, att_q_g[l], att_k_g[l], rel_bias, seq_len)
    yc = _conformer_mixer(p, conv_dw_w[l], conv_dw_b[l][None], conv_ln_g[l][None], conv_ln_b[l][None],
                          conv_pw[l].astype(BF16), seq_len)
    yd = _deltanet_mixer(p, pg, dn_conv_w[l], dn_a_log[l], dn_dt_bias[l], dn_norm_g[l], batch, seq_len)
    xn, hn = _out_proj(x2, (ya, yb, yc, yd), _weight_bf16(w_out, l), norm2_g[l][None])
    return _ffn(hn, xn, _weight_bf16(ffn_w_gate, l), ffn_dw_w[l], ffn_dw_b[l][None],
                _weight_bf16(ffn_w_up, l), _weight_bf16(ffn_w_down, l), seq_len)


def kernel(x, rel_bias, norm1_g, w_in, w_pool, pool_scale, att_q_g, att_k_g, conv_dw_w, conv_dw_b,
           conv_ln_g, conv_ln_b, conv_pw, dn_conv_w, dn_a_log, dn_dt_bias, dn_norm_g, w_out, norm2_g,
           ffn_w_gate, ffn_dw_w, ffn_dw_b, ffn_w_up, ffn_w_down):
    batch, seq_len, _ = x.shape
    x2 = x.reshape(batch * seq_len, D_MODEL)
    for l in range(norm1_g.shape[0]):
        x2 = _layer(x2, l, batch, seq_len, rel_bias, norm1_g, w_in, w_pool, pool_scale, att_q_g,
                    att_k_g, conv_dw_w, conv_dw_b, conv_ln_g, conv_ln_b, conv_pw, dn_conv_w, dn_a_log,
                    dn_dt_bias, dn_norm_g, w_out, norm2_g, ffn_w_gate, ffn_dw_w, ffn_dw_b, ffn_w_up,
                    ffn_w_down)
    return x2.reshape(batch, seq_len, D_MODEL)
```

```python
import functools
import math

import jax
import jax.numpy as jnp
import numpy as np
from jax import lax
from jax.experimental import pallas as pl
from jax.experimental.pallas import tpu as pltpu

F32 = jnp.float32
BF16 = jnp.bfloat16

D_MODEL = 2048
GROUP = 512
POOL_WINDOWS = (2, 4, 8, 16)
ATT_HEAD_DIM = 64
ATT_HEADS = 8
ATT_RADIUS = 64
ATT_DILATIONS = (1, 4, 16)
REL_BUCKETS = 32
REL_MAX_DIST = 1024
CONV_WIDTH = 31
DN_HEAD_DIM = 128
DN_HEADS = 4
DN_CHUNK = 64
FFN_HIDDEN = 5632
EPS = 1e-6
NEG_INF = -1e30

PROJ_MAIN = 10 * GROUP
GATE_LANES = 128

VMEM_LIMIT_BYTES = 56 * 1024 * 1024


def _cparams(*sem):
    return pltpu.CompilerParams(dimension_semantics=sem, vmem_limit_bytes=VMEM_LIMIT_BYTES)


def _sigmoid(x):
    return 1.0 / (1.0 + jnp.exp(-x))


def _silu(x):
    return x * _sigmoid(x)


def _dot(a, b):
    return jnp.dot(a, b, preferred_element_type=F32)


def _dot_nt(a, b):
    return lax.dot_general(a, b, (((1,), (1,)), ((), ())), preferred_element_type=F32)


def _dot_tn(a, b):
    return lax.dot_general(a, b, (((0,), (0,)), ((), ())), preferred_element_type=F32)


def _cast_kernel(w_ref, o_ref):
    o_ref[...] = w_ref[...].astype(BF16)


def _weight_bf16(w_stack, layer, cols=None, tr=512):
    _, rows, width = w_stack.shape
    cols = width if cols is None else cols
    tc = cols // 2 if cols >= 4096 else cols
    return pl.pallas_call(
        _cast_kernel,
        grid=(rows // tr, cols // tc),
        in_specs=[pl.BlockSpec((None, tr, tc), lambda i, j: (layer, i, j))],
        out_specs=pl.BlockSpec((tr, tc), lambda i, j: (i, j)),
        out_shape=jax.ShapeDtypeStruct((rows, cols), BF16),
        compiler_params=_cparams("parallel", "parallel"),
        name="weight_bf16",
    )(w_stack)


def _gate_cols_kernel(w_ref, o_ref, *, valid):
    lane = lax.broadcasted_iota(jnp.int32, o_ref.shape, 1)
    o_ref[...] = jnp.where(lane < valid, w_ref[...], 0.0).astype(BF16)


def _gate_weight_bf16(w_stack, layer):
    _, rows, width = w_stack.shape
    valid = width - PROJ_MAIN
    return pl.pallas_call(
        functools.partial(_gate_cols_kernel, valid=valid),
        grid=(1,),
        in_specs=[pl.BlockSpec((None, rows, GATE_LANES), lambda i: (layer, 0, PROJ_MAIN // GATE_LANES))],
        out_specs=pl.BlockSpec((rows, GATE_LANES), lambda i: (0, 0)),
        out_shape=jax.ShapeDtypeStruct((rows, GATE_LANES), BF16),
        compiler_params=_cparams("arbitrary"),
        name="gate_weight_bf16",
    )(w_stack)


def _inproj_kernel(x_ref, g_ref, w_ref, wg_ref, p_ref, pg_ref):
    x = x_ref[...]
    ms = jnp.mean(x * x, axis=-1, keepdims=True)
    h = (x * lax.rsqrt(ms + EPS) * g_ref[...]).astype(BF16)
    pg_ref[...] = _dot(h, wg_ref[...])
    p_ref[...] = _dot(h, w_ref[...])


def _in_proj(x2, norm_g, w_main, w_gate, tm=512):
    T = x2.shape[0]
    resident = lambda shape: pl.BlockSpec(shape, lambda i: (0, 0), pipeline_mode=pl.Buffered(1))
    return pl.pallas_call(
        _inproj_kernel,
        grid=(T // tm,),
        in_specs=[
            pl.BlockSpec((tm, D_MODEL), lambda i: (i, 0)),
            resident((1, D_MODEL)),
            resident((D_MODEL, PROJ_MAIN)),
            resident((D_MODEL, GATE_LANES)),
        ],
        out_specs=[
            pl.BlockSpec((tm, PROJ_MAIN), lambda i: (i, 0)),
            pl.BlockSpec((tm, GATE_LANES), lambda i: (i, 0)),
        ],
        out_shape=[
            jax.ShapeDtypeStruct((T, PROJ_MAIN), F32),
            jax.ShapeDtypeStruct((T, GATE_LANES), F32),
        ],
        compiler_params=_cparams("parallel"),
        name="in_proj",
    )(x2, norm_g, w_main, w_gate)


def _halo_specs(tm, halo, width, col_block, n_tiles):
    r = tm // halo
    last = n_tiles * r - 1
    prev_spec = pl.BlockSpec((halo, width), lambda i, *_: (jnp.maximum(i * r - 1, 0), col_block))
    next_spec = pl.BlockSpec((halo, width), lambda i, *_: (jnp.minimum((i + 1) * r, last), col_block))
    return prev_spec, next_spec


def _seq_edges(i, tiles_per_seq):
    k = lax.rem(i, tiles_per_seq)
    return k == 0, k == tiles_per_seq - 1, k


def _pool_kernel(u_ref, up_ref, un_ref, w_ref, sc_ref, y_ref, ext_scr, *, tm, seq_len):
    i = pl.program_id(0)
    first, last, k = _seq_edges(i, seq_len // tm)
    ext_scr[0:8, :] = jnp.where(first, jnp.zeros_like(up_ref[...]), up_ref[...])
    ext_scr[8:8 + tm, :] = u_ref[...]
    ext_scr[8 + tm:16 + tm, :] = jnp.where(last, jnp.zeros_like(un_ref[...]), un_ref[...])
    t = k * tm + lax.broadcasted_iota(jnp.int32, (tm, 1), 0)
    for gi, win in enumerate(POOL_WINDOWS):
        half = win // 2
        lanes = slice(gi * 128, (gi + 1) * 128)
        s = ext_scr[pl.ds(8 - half, tm), lanes]
        for kk in range(1, win):
            s = s + ext_scr[pl.ds(8 - half + kk, tm), lanes]
        cnt = (jnp.minimum(t + half, seq_len) - jnp.maximum(t - half, 0)).astype(F32)
        pooled = s / cnt - ext_scr[pl.ds(8, tm), lanes]
        y = _dot(pooled.astype(BF16), w_ref[gi]) * sc_ref[:, lanes]
        y_ref[:, lanes] = y.astype(BF16)


def _pool_mixer(p, w_pool, pool_scale, seq_len, tm=1024):
    T = p.shape[0]
    n_tiles = T // tm
    prev_spec, next_spec = _halo_specs(tm, 8, GROUP, 0, n_tiles)
    return pl.pallas_call(
        functools.partial(_pool_kernel, tm=tm, seq_len=seq_len),
        grid=(n_tiles,),
        in_specs=[
            pl.BlockSpec((tm, GROUP), lambda i: (i, 0)),
            prev_spec,
            next_spec,
            pl.BlockSpec((4, 128, 128), lambda i: (0, 0, 0)),
            pl.BlockSpec((1, GROUP), lambda i: (0, 0)),
        ],
        out_specs=pl.BlockSpec((tm, GROUP), lambda i: (i, 0)),
        out_shape=jax.ShapeDtypeStruct((T, GROUP), BF16),
        scratch_shapes=[pltpu.VMEM((tm + 16, GROUP), F32)],
        compiler_params=_cparams("parallel"),
        name="pool_mixer",
    )(p, p, p, w_pool, pool_scale)


def _conformer_kernel(v_ref, vp_ref, vn_ref, g_ref, gp_ref, gn_ref, dw_ref, db_ref, lg_ref, lb_ref,
                      pw_ref, y_ref, ext_scr, shift_scr, *, tm, seq_len):
    i = pl.program_id(0)
    first, last, _ = _seq_edges(i, seq_len // tm)
    hp = vp_ref[...] * _sigmoid(gp_ref[...])
    hn = vn_ref[...] * _sigmoid(gn_ref[...])
    ext_scr[0:16, :] = jnp.where(first, jnp.zeros_like(hp), hp)
    ext_scr[16:16 + tm, :] = v_ref[...] * _sigmoid(g_ref[...])
    ext_scr[16 + tm:32 + tm, :] = jnp.where(last, jnp.zeros_like(hn), hn)
    base = 16 - CONV_WIDTH // 2
    acc = db_ref[...]
    for b in range(8):
        taps = [(a, 8 * a + b - base) for a in range(5) if 0 <= 8 * a + b - base < CONV_WIDTH]
        rows = tm + 8 * taps[-1][0]
        shift_scr[b, 0:rows, :] = ext_scr[pl.ds(b, rows), :]
        for a, kk in taps:
            acc = acc + shift_scr[b, 8 * a:8 * a + tm, :] * dw_ref[kk:kk + 1, :]
    mu = jnp.mean(acc, axis=-1, keepdims=True)
    xc = acc - mu
    var = jnp.mean(xc * xc, axis=-1, keepdims=True)
    h = _silu(xc * lax.rsqrt(var + EPS) * lg_ref[...] + lb_ref[...])
    y_ref[...] = _dot(h.astype(BF16), pw_ref[...]).astype(BF16)


def _conformer_mixer(p, dw_w, dw_b, ln_g, ln_b, pw, seq_len, tm=1024):
    T = p.shape[0]
    n_tiles = T // tm
    vprev, vnext = _halo_specs(tm, 16, GROUP, 4, n_tiles)
    gprev, gnext = _halo_specs(tm, 16, GROUP, 5, n_tiles)
    const = lambda shape: pl.BlockSpec(shape, lambda i: (0,) * len(shape))
    return pl.pallas_call(
        functools.partial(_conformer_kernel, tm=tm, seq_len=seq_len),
        grid=(n_tiles,),
        in_specs=[
            pl.BlockSpec((tm, GROUP), lambda i: (i, 4)), vprev, vnext,
            pl.BlockSpec((tm, GROUP), lambda i: (i, 5)), gprev, gnext,
            const((CONV_WIDTH, GROUP)), const((1, GROUP)), const((1, GROUP)), const((1, GROUP)),
            const((GROUP, GROUP)),
        ],
        out_specs=pl.BlockSpec((tm, GROUP), lambda i: (i, 0)),
        out_shape=jax.ShapeDtypeStruct((T, GROUP), BF16),
        scratch_shapes=[pltpu.VMEM((tm + 32, GROUP), F32), pltpu.VMEM((8, tm + 32, GROUP), F32)],
        compiler_params=_cparams("parallel"),
        name="conformer_mixer",
    )(p, p, p, p, p, p, dw_w, dw_b, ln_g, ln_b, pw)


ATT_TILE = 1024
LOG2E = math.log2(math.e)
ATT_PIPELINE_DEPTH = 3


def _t5_bucket_table():
    nb = REL_BUCKETS // 2
    max_exact = nb // 2
    i = np.arange(ATT_RADIUS)[:, None]
    j = np.arange(3 * ATT_RADIUS)[None, :]
    off = j - ATT_RADIUS - i
    tables = []
    for dil in ATT_DILATIONS:
        rel = off * dil
        n = np.abs(rel)
        nf = np.maximum(n, 1).astype(np.float32)
        large = max_exact + (np.log(nf / np.float32(max_exact)) / np.float32(math.log(REL_MAX_DIST / max_exact))
                             * np.float32(nb - max_exact)).astype(np.int32)
        large = np.minimum(large, nb - 1)
        bucket = np.where(rel > 0, nb, 0) + np.where(n < max_exact, n, large)
        tables.append(np.where(np.abs(off) <= ATT_RADIUS, bucket, -1))
    return np.stack(tables).astype(np.int32)


def _att_kernel(rb_ref, bkt_ref, qg_ref, kg_ref, mseg_ref, q_ref, kc_ref, kn_ref,
                vp_ref, vc_ref, vn_ref, y_ref, bias_scr, qbuf, kbuf, vbuf, acc_scr, m_scr, l_scr,
                *, seq_len):
    tile = ATT_TILE
    rad = ATT_RADIUS
    hp = pl.program_id(0)
    i = pl.program_id(1)

    def rms(x, g):
        ms = _dot((x * x).astype(BF16), mseg_ref[...])
        return x * lax.rsqrt(ms + EPS) * g

    @pl.when(i == 0)
    def _():
        lane_head0 = lax.broadcasted_iota(jnp.int32, (1, 128), 1) < ATT_HEAD_DIM
        for di in range(len(ATT_DILATIONS)):
            bkt = bkt_ref[di]
            b = jnp.zeros(bkt.shape, F32)
            for bb in range(REL_BUCKETS):
                b = jnp.where(bkt == bb, jnp.where(lane_head0, rb_ref[bb, 2 * hp], rb_ref[bb, 2 * hp + 1]), b)
            bias_scr[di] = jnp.where(bkt < 0, NEG_INF, b * LOG2E)
        kbuf[0:tile, :] = jnp.zeros((tile, 128), F32)
        kbuf[tile:2 * tile, :] = rms(kc_ref[...], kg_ref[...])

    @pl.when(i > 0)
    def _():
        kbuf[0:tile, :] = kbuf[tile:2 * tile, :]
        kbuf[tile:2 * tile, :] = kbuf[2 * tile:3 * tile, :]

    first, last, _ = _seq_edges(i, seq_len // tile)

    qbuf[...] = rms(q_ref[...], qg_ref[...]) * (ATT_HEAD_DIM ** -0.5 * LOG2E)
    kbuf[2 * tile:3 * tile, :] = rms(kn_ref[...], kg_ref[...])
    vbuf[0:tile, :] = vp_ref[...]
    vbuf[tile:2 * tile, :] = vc_ref[...]
    vbuf[2 * tile:3 * tile, :] = vn_ref[...]

    head0 = lax.broadcasted_iota(jnp.int32, (1, 128), 1) < ATT_HEAD_DIM
    key_row = lax.broadcasted_iota(jnp.int32, (3 * rad, 1), 0)
    ones_cols = jnp.ones((3 * rad, 128), BF16)

    def rows(start, size, dil):
        return pl.ds(start, size) if dil == 1 else pl.ds(start, size, stride=dil)

    blocks = [(di, dil, r + rad * dil * m)
              for di, dil in enumerate(ATT_DILATIONS) for r in range(dil) for m in range(tile // (rad * dil))]
    def score_stage(blk):
        di, dil, qstart = blk
        kstart = tile + qstart - rad * dil
        qb = qbuf[rows(qstart, rad, dil), :]
        kb = kbuf[rows(kstart, 3 * rad, dil), :].astype(BF16)
        q2 = jnp.concatenate([jnp.where(head0, qb, 0.0), jnp.where(head0, 0.0, qb)], axis=0)
        return _dot_nt(kb, q2.astype(BF16))

    def softmax_stage(blk, s):
        di, dil, qstart = blk
        kstart = tile + qstart - rad * dil
        s = s + bias_scr[di]
        n_prev = max(0, -(-(tile - kstart) // dil))
        n_upto = min(3 * rad, -(-(2 * tile - kstart) // dil))
        if n_prev > 0:
            s = jnp.where(key_row < jnp.where(first, n_prev, 0), NEG_INF, s)
        if n_upto < 3 * rad:
            s = jnp.where(key_row >= jnp.where(last, n_upto, 3 * rad), NEG_INF, s)
        mx = jnp.max(s, axis=0, keepdims=True)
        return mx, jnp.exp2(s - mx).astype(BF16)

    def value_stage(blk, pe):
        di, dil, qstart = blk
        kstart = tile + qstart - rad * dil
        vb = vbuf[rows(kstart, 3 * rad, dil), :].astype(BF16)
        return _dot_tn(pe, jnp.concatenate([vb, ones_cols], axis=1))

    def store_stage(blk, mx, ov):
        di, dil, qstart = blk
        dst = rows(qstart, rad, dil)
        m_col = jnp.broadcast_to(mx, (2 * rad, 128)).T
        acc_scr[di, dst, :] = jnp.where(head0, ov[0:rad, 0:128], ov[rad:2 * rad, 0:128])
        l_scr[di, dst, :] = jnp.where(head0, ov[0:rad, 128:256], ov[rad:2 * rad, 128:256])
        m_scr[di, dst, :] = jnp.where(head0, m_col[0:rad], m_col[rad:2 * rad])

    n_blk = len(blocks)
    scores = {b: score_stage(blocks[b]) for b in range(min(ATT_PIPELINE_DEPTH, n_blk))}
    pending = None
    for b in range(n_blk):
        mx, pe = softmax_stage(blocks[b], scores.pop(b))
        ov = value_stage(blocks[b], pe)
        if b + ATT_PIPELINE_DEPTH < n_blk:
            scores[b + ATT_PIPELINE_DEPTH] = score_stage(blocks[b + ATT_PIPELINE_DEPTH])
        if pending is not None:
            store_stage(*pending)
        pending = (blocks[b], mx, ov)
    store_stage(*pending)

    m_all = jnp.maximum(jnp.maximum(m_scr[0], m_scr[1]), m_scr[2])
    num = jnp.zeros((tile, 128), F32)
    den = jnp.zeros((tile, 128), F32)
    for di in range(len(ATT_DILATIONS)):
        e = jnp.exp2(m_scr[di] - m_all)
        num = num + acc_scr[di] * e
        den = den + l_scr[di] * e
    y_ref[...] = (num / den).astype(BF16)


def _attention_mixer(p, q_g, k_g, rel_bias, seq_len):
    T = p.shape[0]
    tile = ATT_TILE
    n_tiles = T // tile
    rad = ATT_RADIUS
    seg = np.kron(np.eye(2), np.full((ATT_HEAD_DIM, ATT_HEAD_DIM), 1.0 / ATT_HEAD_DIM))
    mseg = jnp.asarray(seg, BF16)
    bkt_t = np.transpose(_t5_bucket_table(), (0, 2, 1))
    bkt = jnp.asarray(np.concatenate([bkt_t, bkt_t], axis=2))
    qg2 = jnp.tile(q_g, 2)[None]
    kg2 = jnp.tile(k_g, 2)[None]

    def blk(col0, shift):
        return pl.BlockSpec((tile, 128),
                            lambda hp, i: (jnp.clip(i + shift, 0, n_tiles - 1), col0 * 4 + hp))

    const = lambda shape: pl.BlockSpec(shape, lambda hp, i: (0,) * len(shape))
    return pl.pallas_call(
        functools.partial(_att_kernel, seq_len=seq_len),
        grid=(4, n_tiles),
        in_specs=[
            pl.BlockSpec(memory_space=pltpu.SMEM),
            const((3, 3 * rad, 2 * rad)), const((1, 128)), const((1, 128)), const((128, 128)),
            blk(1, 0),
            blk(2, 0), blk(2, 1),
            blk(3, -1), blk(3, 0), blk(3, 1),
        ],
        out_specs=pl.BlockSpec((tile, 128), lambda hp, i: (i, hp)),
        out_shape=jax.ShapeDtypeStruct((T, GROUP), BF16),
        scratch_shapes=[
            pltpu.VMEM((3, 3 * rad, 2 * rad), F32),
            pltpu.VMEM((tile, 128), F32),
            pltpu.VMEM((3 * tile, 128), F32),
            pltpu.VMEM((3 * tile, 128), F32),
            pltpu.VMEM((3, tile, 128), F32),
            pltpu.VMEM((3, tile, 128), F32),
            pltpu.VMEM((3, tile, 128), F32),
        ],
        compiler_params=_cparams("arbitrary", "arbitrary"),
        name="dilated_attention",
    )(rel_bias, bkt, qg2, kg2, mseg, p, p, p, p, p, p)


def _softplus(x):
    return jnp.maximum(x, 0.0) + jnp.log1p(jnp.exp(-jnp.abs(x)))


def _dn_prep_kernel(x_ref, xp_ref, xn_ref, w_ref, o_ref, ext_scr, *, tm, seq_len):
    i = pl.program_id(0)
    first, last, _ = _seq_edges(i, seq_len // tm)
    ext_scr[0:8, :] = jnp.where(first, jnp.zeros_like(xp_ref[...]), xp_ref[...])
    ext_scr[8:8 + tm, :] = x_ref[...]
    ext_scr[8 + tm:16 + tm, :] = jnp.where(last, jnp.zeros_like(xn_ref[...]), xn_ref[...])
    for cb in range(3 * DN_HEADS):
        lanes = slice(cb * 128, (cb + 1) * 128)
        acc = ext_scr[pl.ds(6, tm), lanes] * w_ref[0:1, lanes]
        for kk in range(1, 4):
            acc = acc + ext_scr[pl.ds(6 + kk, tm), lanes] * w_ref[kk:kk + 1, lanes]
        y = _silu(acc)
        if cb < 2 * DN_HEADS:
            y = y * lax.rsqrt(jnp.sum(y * y, axis=-1, keepdims=True) + EPS)
        if cb < DN_HEADS:
            y = y * (DN_HEAD_DIM ** -0.5)
        o_ref[:, lanes] = y


def _dn_prep(p, conv_w, seq_len, tm=512):
    T = p.shape[0]
    n_tiles = T // tm
    width = 3 * GROUP
    prev_spec, next_spec = _halo_specs(tm, 8, width, 2, n_tiles)
    return pl.pallas_call(
        functools.partial(_dn_prep_kernel, tm=tm, seq_len=seq_len),
        grid=(n_tiles,),
        in_specs=[pl.BlockSpec((tm, width), lambda i: (i, 2)), prev_spec, next_spec,
                  pl.BlockSpec((4, width), lambda i: (0, 0))],
        out_specs=pl.BlockSpec((tm, width), lambda i: (i, 0)),
        out_shape=jax.ShapeDtypeStruct((T, width), F32),
        scratch_shapes=[pltpu.VMEM((tm + 16, width), F32)],
        compiler_params=_cparams("parallel"),
        name="deltanet_prep",
    )(p, p, p, conv_w)


DN_GROUP = 8
DN_ROWS = DN_HEADS * DN_CHUNK
DN_SCAN_LAG_STEPS = 4


def _dn_scan_kernel(xf_ref, xb_ref, gf_ref, gb_ref, rf_ref, rb_ref, alc_ref, dtc_ref, alr_ref, dtr_ref,
                    of_ref, ob_ref, s_scr):
    C = DN_CHUNK
    H = DN_HEADS
    R = DN_GROUP * C

    @pl.when(pl.program_id(1) == 0)
    def _():
        s_scr[...] = jnp.zeros_like(s_scr)

    row_in_chunk = lax.rem(lax.broadcasted_iota(jnp.int32, (R, 1), 0), C)
    lane_in_chunk = lax.rem(lax.broadcasted_iota(jnp.int32, (1, DN_ROWS), 1), C)
    rid = lax.broadcasted_iota(jnp.int32, (DN_ROWS, DN_ROWS), 0)
    cid = lax.broadcasted_iota(jnp.int32, (DN_ROWS, DN_ROWS), 1)
    same_head = (rid // C) == (cid // C)
    row_sbs = lax.broadcasted_iota(jnp.int32, (C, DN_ROWS), 0)
    col_sbs = lax.rem(lax.broadcasted_iota(jnp.int32, (C, DN_ROWS), 1), C)
    lane_head = lax.broadcasted_iota(jnp.int32, (1, DN_ROWS), 1) // C

    def block_diag(m):
        return jnp.where(same_head, jnp.concatenate([m] * H, axis=0), jnp.zeros((), m.dtype))

    def side_by_side_blocks(x):
        return jnp.concatenate([x[h * C:(h + 1) * C] for h in range(H)], axis=1)

    def block_diag_blocks(x):
        zero = jnp.zeros((C, 128), x.dtype)
        return jnp.concatenate(
            [jnp.concatenate([x[h * C:(h + 1) * C] if g == h else zero for g in range(H)], axis=1)
             for h in range(H)], axis=0)

    def stack_heads(x, col0):
        return jnp.concatenate([x[:, col0 + h * 128:col0 + (h + 1) * 128] for h in range(H)], axis=0)

    def stack_cols(x, lane0, rows=None):
        parts = []
        for h in range(H):
            c = x[:, lane0 + h:lane0 + h + 1]
            if rows is not None:
                c = c[rows:rows + 1, :]
            parts.append(jnp.broadcast_to(c, (C, 128)))
        return jnp.concatenate(parts, axis=0)

    refs = ((xf_ref, gf_ref, rf_ref, of_ref), (xb_ref, gb_ref, rb_ref, ob_ref))
    gates = []
    for d, (x_ref, gc_ref, gr_ref, o_ref) in enumerate(refs):
        reverse = d == 1
        gcol = gc_ref[...]
        g = -jnp.exp(alc_ref[...]) * _softplus(gcol + dtc_ref[...])
        beta = _sigmoid(gcol)
        gcum = g
        for sh in (1, 2, 4, 8, 16, 32):
            if reverse:
                gcum = gcum + jnp.where(row_in_chunk < C - sh, pltpu.roll(gcum, R - sh, 0), 0.0)
            else:
                gcum = gcum + jnp.where(row_in_chunk >= sh, pltpu.roll(gcum, sh, 0), 0.0)
        grow_all = -jnp.exp(alr_ref[...])[None] * _softplus(gr_ref[...] + dtr_ref[...][None])
        for sh in (1, 2, 4, 8, 16, 32):
            if reverse:
                grow_all = grow_all + jnp.where(lane_in_chunk < C - sh,
                                                pltpu.roll(grow_all, DN_ROWS - sh, 2), 0.0)
            else:
                grow_all = grow_all + jnp.where(lane_in_chunk >= sh, pltpu.roll(grow_all, sh, 2), 0.0)

        gates.append((gcum, beta, grow_all))

    def fill_operands(u):
        d = u["d"]
        x_ref, _, _, o_ref = refs[d]
        reverse = d == 1
        ci = DN_GROUP - 1 - u["step"] if reverse else u["step"]
        gcum, beta, grow_all = gates[d]
        rows = slice(ci * C, (ci + 1) * C)
        x = x_ref[rows, :]
        u.update(rows=rows, o_ref=o_ref)
        u["q"] = stack_heads(x, 0)
        u["k"] = stack_heads(x, GROUP)
        v_st = stack_heads(x, 2 * GROUP)
        gc_c = gcum[rows, :]
        beta_st = stack_cols(beta[rows, :], 8 + d * H)
        gcol_st = stack_cols(gc_c, d * H)
        glast_st = stack_cols(gc_c, d * H, rows=0 if reverse else C - 1)
        grow = grow_all[ci, d:d + 1, :]
        gcol_sbs = jnp.broadcast_to(gc_c[:, d * H + H - 1:d * H + H], (C, DN_ROWS))
        for h in range(H - 2, -1, -1):
            gcol_sbs = jnp.where(lane_head <= h, gc_c[:, d * H + h:d * H + h + 1], gcol_sbs)
        u["incl"] = (row_sbs <= col_sbs) if reverse else (row_sbs >= col_sbs)
        u["decay"] = jnp.where(u["incl"], jnp.exp(gcol_sbs - grow), 0.0)
        kb_st = u["k"] * beta_st
        eg = jnp.exp(gcol_st)
        u["kq"] = jnp.concatenate([side_by_side_blocks(kb_st), side_by_side_blocks(u["q"])],
                                  axis=0).astype(BF16)
        u["kbd"] = block_diag_blocks(u["k"].astype(BF16))
        u["rhs"] = jnp.concatenate([v_st * beta_st, kb_st * eg], axis=1).astype(BF16)
        u["qdec"] = u["q"] * eg
        u["kdec"] = (u["k"] * jnp.exp(glast_st - gcol_st)).astype(BF16)
        u["gl"] = jnp.exp(glast_st)

    eye_sbs = (row_sbs == col_sbs).astype(F32)

    def prep_stages(units):
        for u in units:
            fill_operands(u)
        yield
        for n, u in enumerate(units):
            u["kk"] = _dot_nt(u["kq"], u["kbd"])
            if n == len(units) // 2 - 1:
                yield
        yield
        for u in units:
            strict = u["incl"] & (row_sbs != col_sbs)
            a_sbs = jnp.where(strict, u["kk"][0:C] * u["decay"], 0.0)
            u["attn"] = (u["kk"][C:2 * C] * u["decay"]).astype(BF16)
            u["pinv"] = eye_sbs - a_sbs
            u["a_sbs"] = a_sbs.astype(BF16)
        for u in units:
            u["apow"] = _dot(u["a_sbs"], block_diag(u["a_sbs"]))
        yield
        for _ in range(4):
            for n, u in enumerate(units):
                ap = u["apow"].astype(BF16)
                u["both"] = _dot(jnp.concatenate([u["pinv"].astype(BF16), ap], axis=0), block_diag(ap))
                if n == len(units) // 2 - 1:
                    yield
            yield
            for u in units:
                u["pinv"] = u["pinv"] + u["both"][0:C]
                u["apow"] = u["both"][C:2 * C]
        for u in units:
            u["last"] = _dot(u["pinv"].astype(BF16), block_diag(u["apow"].astype(BF16)))
        yield
        for u in units:
            u["uw"] = _dot(block_diag((u["pinv"] + u["last"]).astype(BF16)), u["rhs"])
        yield

    def scan_stages(units):
        for i in range(0, len(units), 2):
            pair = units[i:i + 2]
            for u in pair:
                d = u["d"]
                u["wq"] = []
                for h in range(H):
                    hr = slice(h * C, (h + 1) * C)
                    lhs = jnp.concatenate([u["uw"][hr, 128:256], u["qdec"][hr]], axis=0).astype(BF16)
                    u["wq"].append(_dot(lhs, s_scr[d * H + h].astype(BF16)))
            yield
            for u in pair:
                u["vnew"] = [(u["uw"][h * C:(h + 1) * C, 0:128] - u["wq"][h][0:C]).astype(BF16)
                             for h in range(H)]
                vnew_bd = block_diag_blocks(jnp.concatenate(u["vnew"], axis=0))
                u["o"] = jnp.concatenate([w[C:2 * C] for w in u["wq"]], axis=1) + _dot(u["attn"], vnew_bd)
            yield
            for u in pair:
                d = u["d"]
                for h in range(H):
                    hr = slice(h * C, (h + 1) * C)
                    s_scr[d * H + h] = (s_scr[d * H + h] * u["gl"][h * C:h * C + 1, :]
                                        + _dot_tn(u["kdec"][hr], u["vnew"][h]))
                u["o_ref"][u["rows"], :] = u["o"]
            yield

    def run_together(*stage_generators):
        live = list(stage_generators)
        while live:
            for gen in list(live):
                if next(gen, StopIteration) is StopIteration:
                    live.remove(gen)

    batches = [[dict(step=step, d=d) for step in range(s0, s0 + DN_SCAN_LAG_STEPS) for d in range(2)]
               for s0 in range(0, DN_GROUP, DN_SCAN_LAG_STEPS)]
    run_together(prep_stages(batches[0]))
    for prev, cur in zip(batches[:-1], batches[1:]):
        run_together(prep_stages(cur), scan_stages(prev))
    run_together(scan_stages(batches[-1]))


def _dn_scan(qkvn, pg, a_log, dt_bias, batch, seq_len):
    T = qkvn.shape[0]
    C, H, G = DN_CHUNK, DN_HEADS, DN_GROUP
    R = G * C
    nc = seq_len // C
    ncg = nc // G
    ab = pg[:, 0:4 * H].reshape(batch, nc, C, 2, 2, H)
    ab_row = jnp.transpose(ab, (0, 1, 3, 4, 5, 2)).reshape(batch * nc, 4, H * C)
    ab_row = jnp.pad(ab_row, ((0, 0), (0, 4), (0, 0)))
    pad_lanes = lambda v: jnp.pad(v.reshape(1, 2 * H), ((0, 0), (0, GATE_LANES - 2 * H)))
    row_param = lambda v: jnp.pad(jnp.repeat(v, C, axis=1), ((0, 6), (0, 0)))

    fwd = lambda b, c: (b * ncg + c, 0)
    bwd = lambda b, c: (b * ncg + ncg - 1 - c, 0)
    fwd3 = lambda b, c: (b * ncg + c, 0, 0)
    bwd3 = lambda b, c: (b * ncg + ncg - 1 - c, 0, 0)
    const = lambda shape: pl.BlockSpec(shape, lambda b, c: (0,) * len(shape))
    return pl.pallas_call(
        _dn_scan_kernel,
        grid=(batch, ncg),
        in_specs=[
            pl.BlockSpec((R, 3 * GROUP), fwd), pl.BlockSpec((R, 3 * GROUP), bwd),
            pl.BlockSpec((R, GATE_LANES), fwd), pl.BlockSpec((R, GATE_LANES), bwd),
            pl.BlockSpec((G, 8, H * C), fwd3), pl.BlockSpec((G, 8, H * C), bwd3),
            const((1, GATE_LANES)), const((1, GATE_LANES)), const((8, H * C)), const((8, H * C)),
        ],
        out_specs=[pl.BlockSpec((R, GROUP), fwd), pl.BlockSpec((R, GROUP), bwd)],
        out_shape=[jax.ShapeDtypeStruct((T, GROUP), F32), jax.ShapeDtypeStruct((T, GROUP), F32)],
        scratch_shapes=[pltpu.VMEM((2 * H, DN_HEAD_DIM, DN_HEAD_DIM), F32)],
        compiler_params=_cparams("arbitrary", "arbitrary"),
        name="deltanet_scan",
    )(qkvn, qkvn, pg, pg, ab_row, ab_row, pad_lanes(a_log), pad_lanes(dt_bias),
      row_param(a_log), row_param(dt_bias))


def _dn_post_kernel(of_ref, ob_ref, z_ref, g_ref, y_ref):
    for h in range(DN_HEADS):
        lanes = slice(h * 128, (h + 1) * 128)
        o = of_ref[:, lanes] + ob_ref[:, lanes]
        o = o * lax.rsqrt(jnp.mean(o * o, axis=-1, keepdims=True) + EPS) * g_ref[...]
        y_ref[:, lanes] = (o * _silu(z_ref[:, lanes])).astype(BF16)


def _dn_post(o_f, o_b, p, norm_g, tm=1024):
    T = p.shape[0]
    row = lambda cb: pl.BlockSpec((tm, GROUP), lambda i: (i, cb))
    return pl.pallas_call(
        _dn_post_kernel,
        grid=(T // tm,),
        in_specs=[row(0), row(0), row(9), pl.BlockSpec((1, DN_HEAD_DIM), lambda i: (0, 0))],
        out_specs=row(0),
        out_shape=jax.ShapeDtypeStruct((T, GROUP), BF16),
        compiler_params=_cparams("parallel"),
        name="deltanet_post",
    )(o_f, o_b, p, norm_g)


def _deltanet_mixer(p, pg, conv_w, a_log, dt_bias, norm_g, batch, seq_len):
    qkvn = _dn_prep(p, conv_w, seq_len)
    o_f, o_b = _dn_scan(qkvn, pg, a_log, dt_bias, batch, seq_len)
    return _dn_post(o_f, o_b, p, norm_g[None])


def _outproj_kernel(x_ref, ya_ref, yb_ref, yc_ref, yd_ref, w_ref, g_ref, xn_ref, hn_ref, ycat_scr):
    for gi, y_ref in enumerate((ya_ref, yb_ref, yc_ref, yd_ref)):
        ycat_scr[:, gi * GROUP:(gi + 1) * GROUP] = y_ref[...]
    acc = x_ref[...] + _dot(ycat_scr[...], w_ref[...])
    xn_ref[...] = acc
    ms = jnp.mean(acc * acc, axis=-1, keepdims=True)
    hn_ref[...] = (acc * lax.rsqrt(ms + EPS) * g_ref[...]).astype(BF16)


def _out_proj(x2, ys, w_out, norm2_g, tm=512):
    T = x2.shape[0]
    row = lambda width: pl.BlockSpec((tm, width), lambda i: (i, 0))
    return pl.pallas_call(
        _outproj_kernel,
        grid=(T // tm,),
        in_specs=[row(D_MODEL), row(GROUP), row(GROUP), row(GROUP), row(GROUP),
                  pl.BlockSpec((D_MODEL, D_MODEL), lambda i: (0, 0), pipeline_mode=pl.Buffered(1)),
                  pl.BlockSpec((1, D_MODEL), lambda i: (0, 0))],
        out_specs=[row(D_MODEL), row(D_MODEL)],
        out_shape=[jax.ShapeDtypeStruct((T, D_MODEL), F32),
                   jax.ShapeDtypeStruct((T, D_MODEL), BF16)],
        scratch_shapes=[pltpu.VMEM((tm, D_MODEL), BF16)],
        compiler_params=_cparams("parallel"),
        name="out_proj",
    )(x2, *ys, w_out, norm2_g)


FFN_HALF = FFN_HIDDEN // 2
FFN_COL_CHUNK = 256


def _ffn_hidden_kernel(hn_ref, hp_ref, hx_ref, wg_ref, wu_ref, dw_ref, db_ref, a_ref, hext_scr, g_scr,
                       *, tm, seq_len):
    first, last, _ = _seq_edges(pl.program_id(1), seq_len // tm)
    hext_scr[0:16, :] = jnp.where(first, jnp.zeros_like(hp_ref[...]), hp_ref[...])
    hext_scr[16:16 + tm, :] = hn_ref[...]
    hext_scr[16 + tm:32 + tm, :] = jnp.where(last, jnp.zeros_like(hx_ref[...]), hx_ref[...])
    for c in range(FFN_HALF // FFN_COL_CHUNK):
        cols = slice(c * FFN_COL_CHUNK, (c + 1) * FFN_COL_CHUNK)
        g = g_scr.at[c % 2]
        g[...] = _dot(hext_scr[...], wg_ref[:, cols])
        up = _dot(hn_ref[...], wu_ref[:, cols])
        gate = (g[pl.ds(15, tm), :] * dw_ref[0:1, cols] + g[pl.ds(16, tm), :] * dw_ref[1:2, cols]
                + g[pl.ds(17, tm), :] * dw_ref[2:3, cols] + db_ref[:, cols])
        a_ref[:, cols] = (_silu(gate) * up).astype(BF16)


def _ffn_down_kernel(a_ref, xn_ref, wd_ref, o_ref):
    o_ref[...] = xn_ref[...] + _dot(a_ref[...], wd_ref[...])


def _ffn(hn, xn, w_gate, dw_w, dw_b, w_up, w_down, seq_len, tm_hidden=1024, tm_down=512):
    T = hn.shape[0]
    n_tiles = T // tm_hidden
    r = tm_hidden // 16
    half_cols = lambda rows: pl.BlockSpec((rows, FFN_HALF), lambda h, i: (0, h),
                                          pipeline_mode=pl.Buffered(1))
    act = pl.pallas_call(
        functools.partial(_ffn_hidden_kernel, tm=tm_hidden, seq_len=seq_len),
        grid=(2, n_tiles),
        in_specs=[
            pl.BlockSpec((tm_hidden, D_MODEL), lambda h, i: (i, 0)),
            pl.BlockSpec((16, D_MODEL), lambda h, i: (jnp.maximum(i * r - 1, 0), 0)),
            pl.BlockSpec((16, D_MODEL), lambda h, i: (jnp.minimum((i + 1) * r, n_tiles * r - 1), 0)),
            half_cols(D_MODEL), half_cols(D_MODEL), half_cols(3), half_cols(1),
        ],
        out_specs=pl.BlockSpec((tm_hidden, FFN_HALF), lambda h, i: (i, h)),
        out_shape=jax.ShapeDtypeStruct((T, FFN_HIDDEN), BF16),
        scratch_shapes=[pltpu.VMEM((tm_hidden + 32, D_MODEL), BF16),
                        pltpu.VMEM((2, tm_hidden + 32, FFN_COL_CHUNK), F32)],
        compiler_params=_cparams("arbitrary", "arbitrary"),
        name="ffn_hidden",
    )(hn, hn, hn, w_gate, w_up, dw_w, dw_b)
    return pl.pallas_call(
        _ffn_down_kernel,
        grid=(T // tm_down,),
        in_specs=[
            pl.BlockSpec((tm_down, FFN_HIDDEN), lambda i: (i, 0)),
            pl.BlockSpec((tm_down, D_MODEL), lambda i: (i, 0)),
            pl.BlockSpec((FFN_HIDDEN, D_MODEL), lambda i: (0, 0), pipeline_mode=pl.Buffered(1)),
        ],
        out_specs=pl.BlockSpec((tm_down, D_MODEL), lambda i: (i, 0)),
        out_shape=jax.ShapeDtypeStruct((T, D_MODEL), F32),
        compiler_params=_cparams("parallel"),
        name="ffn_down",
    )(act, xn, w_down)


def _layer(x2, l, batch, seq_len, rel_bias, norm1_g, w_in, w_pool, pool_scale, att_q_g, att_k_g,
           conv_dw_w, conv_dw_b, conv_ln_g, conv_ln_b, conv_pw, dn_conv_w, dn_a_log, dn_dt_bias,
           dn_norm_g, w_out, norm2_g, ffn_w_gate, ffn_dw_w, ffn_dw_b, ffn_w_up, ffn_w_down):
    w_main = _weight_bf16(w_in, l, cols=PROJ_MAIN)
    w_gates = _gate_weight_bf16(w_in, l)
    p, pg = _in_proj(x2, norm1_g[l][None], w_main, w_gates)
    ya = _pool_mixer(p, w_pool[l].astype(BF16), pool_scale[l][None], seq_len)
    yb = _attention_mixer(p, att_q_g[l], att_k_g[l], rel_bias, seq_len)
    yc = _conformer_mixer(p, conv_dw_w[l], conv_dw_b[l][None], conv_ln_g[l][None], conv_ln_b[l][None],
                          conv_pw[l].astype(BF16), seq_len)
    yd = _deltanet_mixer(p, pg, dn_conv_w[l], dn_a_log[l], dn_dt_bias[l], dn_norm_g[l], batch, seq_len)
    xn, hn = _out_proj(x2, (ya, yb, yc, yd), _weight_bf16(w_out, l), norm2_g[l][None])
    return _ffn(hn, xn, _weight_bf16(ffn_w_gate, l), ffn_dw_w[l], ffn_dw_b[l][None],
                _weight_bf16(ffn_w_up, l), _weight_bf16(ffn_w_down, l), seq_len)


def kernel(x, rel_bias, norm1_g, w_in, w_pool, pool_scale, att_q_g, att_k_g, conv_dw_w, conv_dw_b,
           conv_ln_g, conv_ln_b, conv_pw, dn_conv_w, dn_a_log, dn_dt_bias, dn_norm_g, w_out, norm2_g,
           ffn_w_gate, ffn_dw_w, ffn_dw_b, ffn_w_up, ffn_w_down):
    batch, seq_len, _ = x.shape
    x2 = x.reshape(batch * seq_len, D_MODEL)
    for l in range(norm1_g.shape[0]):
        x2 = _layer(x2, l, batch, seq_len, rel_bias, norm1_g, w_in, w_pool, pool_scale, att_q_g,
                    att_k_g, conv_dw_w, conv_dw_b, conv_ln_g, conv_ln_b, conv_pw, dn_conv_w, dn_a_log,
                    dn_dt_bias, dn_norm_g, w_out, norm2_g, ffn_w_gate, ffn_dw_w, ffn_dw_b, ffn_w_up,
                    ffn_w_down)
    return x2.reshape(batch, seq_len, D_MODEL)
```

```python
import functools
import math

import jax
import jax.numpy as jnp
import numpy as np
from jax import lax
from jax.experimental import pallas as pl
from jax.experimental.pallas import tpu as pltpu

F32 = jnp.float32
BF16 = jnp.bfloat16

D_MODEL = 2048
GROUP = 512
POOL_WINDOWS = (2, 4, 8, 16)
ATT_HEAD_DIM = 64
ATT_HEADS = 8
ATT_RADIUS = 64
ATT_DILATIONS = (1, 4, 16)
REL_BUCKETS = 32
REL_MAX_DIST = 1024
CONV_WIDTH = 31
DN_HEAD_DIM = 128
DN_HEADS = 4
DN_CHUNK = 64
FFN_HIDDEN = 5632
EPS = 1e-6
NEG_INF = -1e30

PROJ_MAIN = 10 * GROUP
GATE_LANES = 128

VMEM_LIMIT_BYTES = 56 * 1024 * 1024


def _cparams(*sem):
    return pltpu.CompilerParams(dimension_semantics=sem, vmem_limit_bytes=VMEM_LIMIT_BYTES)


def _sigmoid(x):
    return 1.0 / (1.0 + jnp.exp(-x))


def _silu(x):
    return x * _sigmoid(x)


def _dot(a, b):
    return jnp.dot(a, b, preferred_element_type=F32)


def _dot_nt(a, b):
    return lax.dot_general(a, b, (((1,), (1,)), ((), ())), preferred_element_type=F32)


def _dot_tn(a, b):
    return lax.dot_general(a, b, (((0,), (0,)), ((), ())), preferred_element_type=F32)


def _cast_kernel(w_ref, o_ref):
    o_ref[...] = w_ref[...].astype(BF16)


def _weight_bf16(w_stack, layer, cols=None, tr=512):
    _, rows, width = w_stack.shape
    cols = width if cols is None else cols
    tc = cols // 2 if cols >= 4096 else cols
    return pl.pallas_call(
        _cast_kernel,
        grid=(rows // tr, cols // tc),
        in_specs=[pl.BlockSpec((None, tr, tc), lambda i, j: (layer, i, j))],
        out_specs=pl.BlockSpec((tr, tc), lambda i, j: (i, j)),
        out_shape=jax.ShapeDtypeStruct((rows, cols), BF16),
        compiler_params=_cparams("parallel", "parallel"),
        name="weight_bf16",
    )(w_stack)


def _gate_cols_kernel(w_ref, o_ref, *, valid):
    lane = lax.broadcasted_iota(jnp.int32, o_ref.shape, 1)
    o_ref[...] = jnp.where(lane < valid, w_ref[...], 0.0).astype(BF16)


def _gate_weight_bf16(w_stack, layer):
    _, rows, width = w_stack.shape
    valid = width - PROJ_MAIN
    return pl.pallas_call(
        functools.partial(_gate_cols_kernel, valid=valid),
        grid=(1,),
        in_specs=[pl.BlockSpec((None, rows, GATE_LANES), lambda i: (layer, 0, PROJ_MAIN // GATE_LANES))],
        out_specs=pl.BlockSpec((rows, GATE_LANES), lambda i: (0, 0)),
        out_shape=jax.ShapeDtypeStruct((rows, GATE_LANES), BF16),
        compiler_params=_cparams("arbitrary"),
        name="gate_weight_bf16",
    )(w_stack)


def _inproj_kernel(x_ref, g_ref, w_ref, wg_ref, p_ref, pg_ref):
    x = x_ref[...]
    ms = jnp.mean(x * x, axis=-1, keepdims=True)
    h = (x * lax.rsqrt(ms + EPS) * g_ref[...]).astype(BF16)
    pg_ref[...] = _dot(h, wg_ref[...])
    p_ref[...] = _dot(h, w_ref[...])


def _in_proj(x2, norm_g, w_main, w_gate, tm=512):
    T = x2.shape[0]
    resident = lambda shape: pl.BlockSpec(shape, lambda i: (0, 0), pipeline_mode=pl.Buffered(1))
    return pl.pallas_call(
        _inproj_kernel,
        grid=(T // tm,),
        in_specs=[
            pl.BlockSpec((tm, D_MODEL), lambda i: (i, 0)),
            resident((1, D_MODEL)),
            resident((D_MODEL, PROJ_MAIN)),
            resident((D_MODEL, GATE_LANES)),
        ],
        out_specs=[
            pl.BlockSpec((tm, PROJ_MAIN), lambda i: (i, 0)),
            pl.BlockSpec((tm, GATE_LANES), lambda i: (i, 0)),
        ],
        out_shape=[
            jax.ShapeDtypeStruct((T, PROJ_MAIN), F32),
            jax.ShapeDtypeStruct((T, GATE_LANES), F32),
        ],
        compiler_params=_cparams("parallel"),
        name="in_proj",
    )(x2, norm_g, w_main, w_gate)


def _halo_specs(tm, halo, width, col_block, n_tiles):
    r = tm // halo
    last = n_tiles * r - 1
    prev_spec = pl.BlockSpec((halo, width), lambda i, *_: (jnp.maximum(i * r - 1, 0), col_block))
    next_spec = pl.BlockSpec((halo, width), lambda i, *_: (jnp.minimum((i + 1) * r, last), col_block))
    return prev_spec, next_spec


def _seq_edges(i, tiles_per_seq):
    k = lax.rem(i, tiles_per_seq)
    return k == 0, k == tiles_per_seq - 1, k


def _pool_kernel(u_ref, up_ref, un_ref, w_ref, sc_ref, y_ref, ext_scr, *, tm, seq_len):
    i = pl.program_id(0)
    first, last, k = _seq_edges(i, seq_len // tm)
    ext_scr[0:8, :] = jnp.where(first, jnp.zeros_like(up_ref[...]), up_ref[...])
    ext_scr[8:8 + tm, :] = u_ref[...]
    ext_scr[8 + tm:16 + tm, :] = jnp.where(last, jnp.zeros_like(un_ref[...]), un_ref[...])
    t = k * tm + lax.broadcasted_iota(jnp.int32, (tm, 1), 0)
    for gi, win in enumerate(POOL_WINDOWS):
        half = win // 2
        lanes = slice(gi * 128, (gi + 1) * 128)
        s = ext_scr[pl.ds(8 - half, tm), lanes]
        for kk in range(1, win):
            s = s + ext_scr[pl.ds(8 - half + kk, tm), lanes]
        cnt = (jnp.minimum(t + half, seq_len) - jnp.maximum(t - half, 0)).astype(F32)
        pooled = s / cnt - ext_scr[pl.ds(8, tm), lanes]
        y = _dot(pooled.astype(BF16), w_ref[gi]) * sc_ref[:, lanes]
        y_ref[:, lanes] = y.astype(BF16)


def _pool_mixer(p, w_pool, pool_scale, seq_len, tm=1024):
    T = p.shape[0]
    n_tiles = T // tm
    prev_spec, next_spec = _halo_specs(tm, 8, GROUP, 0, n_tiles)
    return pl.pallas_call(
        functools.partial(_pool_kernel, tm=tm, seq_len=seq_len),
        grid=(n_tiles,),
        in_specs=[
            pl.BlockSpec((tm, GROUP), lambda i: (i, 0)),
            prev_spec,
            next_spec,
            pl.BlockSpec((4, 128, 128), lambda i: (0, 0, 0)),
            pl.BlockSpec((1, GROUP), lambda i: (0, 0)),
        ],
        out_specs=pl.BlockSpec((tm, GROUP), lambda i: (i, 0)),
        out_shape=jax.ShapeDtypeStruct((T, GROUP), BF16),
        scratch_shapes=[pltpu.VMEM((tm + 16, GROUP), F32)],
        compiler_params=_cparams("parallel"),
        name="pool_mixer",
    )(p, p, p, w_pool, pool_scale)


def _conformer_kernel(v_ref, vp_ref, vn_ref, g_ref, gp_ref, gn_ref, dw_ref, db_ref, lg_ref, lb_ref,
                      pw_ref, y_ref, ext_scr, shift_scr, *, tm, seq_len):
    i = pl.program_id(0)
    first, last, _ = _seq_edges(i, seq_len // tm)
    hp = vp_ref[...] * _sigmoid(gp_ref[...])
    hn = vn_ref[...] * _sigmoid(gn_ref[...])
    ext_scr[0:16, :] = jnp.where(first, jnp.zeros_like(hp), hp)
    ext_scr[16:16 + tm, :] = v_ref[...] * _sigmoid(g_ref[...])
    ext_scr[16 + tm:32 + tm, :] = jnp.where(last, jnp.zeros_like(hn), hn)
    base = 16 - CONV_WIDTH // 2
    acc = db_ref[...]
    for b in range(8):
        taps = [(a, 8 * a + b - base) for a in range(5) if 0 <= 8 * a + b - base < CONV_WIDTH]
        rows = tm + 8 * taps[-1][0]
        shift_scr[b, 0:rows, :] = ext_scr[pl.ds(b, rows), :]
        for a, kk in taps:
            acc = acc + shift_scr[b, 8 * a:8 * a + tm, :] * dw_ref[kk:kk + 1, :]
    mu = jnp.mean(acc, axis=-1, keepdims=True)
    xc = acc - mu
    var = jnp.mean(xc * xc, axis=-1, keepdims=True)
    h = _silu(xc * lax.rsqrt(var + EPS) * lg_ref[...] + lb_ref[...])
    y_ref[...] = _dot(h.astype(BF16), pw_ref[...]).astype(BF16)


def _conformer_mixer(p, dw_w, dw_b, ln_g, ln_b, pw, seq_len, tm=1024):
    T = p.shape[0]
    n_tiles = T // tm
    vprev, vnext = _halo_specs(tm, 16, GROUP, 4, n_tiles)
    gprev, gnext = _halo_specs(tm, 16, GROUP, 5, n_tiles)
    const = lambda shape: pl.BlockSpec(shape, lambda i: (0,) * len(shape))
    return pl.pallas_call(
        functools.partial(_conformer_kernel, tm=tm, seq_len=seq_len),
        grid=(n_tiles,),
        in_specs=[
            pl.BlockSpec((tm, GROUP), lambda i: (i, 4)), vprev, vnext,
            pl.BlockSpec((tm, GROUP), lambda i: (i, 5)), gprev, gnext,
            const((CONV_WIDTH, GROUP)), const((1, GROUP)), const((1, GROUP)), const((1, GROUP)),
            const((GROUP, GROUP)),
        ],
        out_specs=pl.BlockSpec((tm, GROUP), lambda i: (i, 0)),
        out_shape=jax.ShapeDtypeStruct((T, GROUP), BF16),
        scratch_shapes=[pltpu.VMEM((tm + 32, GROUP), F32), pltpu.VMEM((8, tm + 32, GROUP), F32)],
        compiler_params=_cparams("parallel"),
        name="conformer_mixer",
    )(p, p, p, p, p, p, dw_w, dw_b, ln_g, ln_b, pw)


ATT_TILE = 1024
LOG2E = math.log2(math.e)
ATT_PIPELINE_DEPTH = 3


def _t5_bucket_table():
    nb = REL_BUCKETS // 2
    max_exact = nb // 2
    i = np.arange(ATT_RADIUS)[:, None]
    j = np.arange(3 * ATT_RADIUS)[None, :]
    off = j - ATT_RADIUS - i
    tables = []
    for dil in ATT_DILATIONS:
        rel = off * dil
        n = np.abs(rel)
        nf = np.maximum(n, 1).astype(np.float32)
        large = max_exact + (np.log(nf / np.float32(max_exact)) / np.float32(math.log(REL_MAX_DIST / max_exact))
                             * np.float32(nb - max_exact)).astype(np.int32)
        large = np.minimum(large, nb - 1)
        bucket = np.where(rel > 0, nb, 0) + np.where(n < max_exact, n, large)
        tables.append(np.where(np.abs(off) <= ATT_RADIUS, bucket, -1))
    return np.stack(tables).astype(np.int32)


def _att_kernel(rb_ref, bkt_ref, qg_ref, kg_ref, mseg_ref, q_ref, kc_ref, kn_ref,
                vp_ref, vc_ref, vn_ref, y_ref, bias_scr, qbuf, kbuf, vbuf, acc_scr, m_scr, l_scr,
                *, seq_len):
    tile = ATT_TILE
    rad = ATT_RADIUS
    hp = pl.program_id(0)
    i = pl.program_id(1)

    def rms(x, g):
        ms = _dot((x * x).astype(BF16), mseg_ref[...])
        return x * lax.rsqrt(ms + EPS) * g

    @pl.when(i == 0)
    def _():
        lane_head0 = lax.broadcasted_iota(jnp.int32, (1, 128), 1) < ATT_HEAD_DIM
        for di in range(len(ATT_DILATIONS)):
            bkt = bkt_ref[di]
            b = jnp.zeros(bkt.shape, F32)
            for bb in range(REL_BUCKETS):
                b = jnp.where(bkt == bb, jnp.where(lane_head0, rb_ref[bb, 2 * hp], rb_ref[bb, 2 * hp + 1]), b)
            bias_scr[di] = jnp.where(bkt < 0, NEG_INF, b * LOG2E)
        kbuf[0:tile, :] = jnp.zeros((tile, 128), F32)
        kbuf[tile:2 * tile, :] = rms(kc_ref[...], kg_ref[...])

    @pl.when(i > 0)
    def _():
        kbuf[0:tile, :] = kbuf[tile:2 * tile, :]
        kbuf[tile:2 * tile, :] = kbuf[2 * tile:3 * tile, :]

    first, last, _ = _seq_edges(i, seq_len // tile)

    qbuf[...] = rms(q_ref[...], qg_ref[...]) * (ATT_HEAD_DIM ** -0.5 * LOG2E)
    kbuf[2 * tile:3 * tile, :] = rms(kn_ref[...], kg_ref[...])
    vbuf[0:tile, :] = vp_ref[...]
    vbuf[tile:2 * tile, :] = vc_ref[...]
    vbuf[2 * tile:3 * tile, :] = vn_ref[...]

    head0 = lax.broadcasted_iota(jnp.int32, (1, 128), 1) < ATT_HEAD_DIM
    key_row = lax.broadcasted_iota(jnp.int32, (3 * rad, 1), 0)
    ones_cols = jnp.ones((3 * rad, 128), BF16)

    def rows(start, size, dil):
        return pl.ds(start, size) if dil == 1 else pl.ds(start, size, stride=dil)

    blocks = [(di, dil, r + rad * dil * m)
              for di, dil in enumerate(ATT_DILATIONS) for r in range(dil) for m in range(tile // (rad * dil))]
    def score_stage(blk):
        di, dil, qstart = blk
        kstart = tile + qstart - rad * dil
        qb = qbuf[rows(qstart, rad, dil), :]
        kb = kbuf[rows(kstart, 3 * rad, dil), :].astype(BF16)
        q2 = jnp.concatenate([jnp.where(head0, qb, 0.0), jnp.where(head0, 0.0, qb)], axis=0)
        return _dot_nt(kb, q2.astype(BF16))

    def softmax_stage(blk, s):
        di, dil, qstart = blk
        kstart = tile + qstart - rad * dil
        s = s + bias_scr[di]
        n_prev = max(0, -(-(tile - kstart) // dil))
        n_upto = min(3 * rad, -(-(2 * tile - kstart) // dil))
        if n_prev > 0:
            s = jnp.where(key_row < jnp.where(first, n_prev, 0), NEG_INF, s)
        if n_upto < 3 * rad:
            s = jnp.where(key_row >= jnp.where(last, n_upto, 3 * rad), NEG_INF, s)
        mx = jnp.max(s, axis=0, keepdims=True)
        return mx, jnp.exp2(s - mx).astype(BF16)

    def value_stage(blk, pe):
        di, dil, qstart = blk
        kstart = tile + qstart - rad * dil
        vb = vbuf[rows(kstart, 3 * rad, dil), :].astype(BF16)
        return _dot_tn(pe, jnp.concatenate([vb, ones_cols], axis=1))

    def store_stage(blk, mx, ov):
        di, dil, qstart = blk
        dst = rows(qstart, rad, dil)
        m_col = jnp.broadcast_to(mx, (2 * rad, 128)).T
        acc_scr[di, dst, :] = jnp.where(head0, ov[0:rad, 0:128], ov[rad:2 * rad, 0:128])
        l_scr[di, dst, :] = jnp.where(head0, ov[0:rad, 128:256], ov[rad:2 * rad, 128:256])
        m_scr[di, dst, :] = jnp.where(head0, m_col[0:rad], m_col[rad:2 * rad])

    n_blk = len(blocks)
    scores = {b: score_stage(blocks[b]) for b in range(min(ATT_PIPELINE_DEPTH, n_blk))}
    pending = None
    for b in range(n_blk):
        mx, pe = softmax_stage(blocks[b], scores.pop(b))
        ov = value_stage(blocks[b], pe)
        if b + ATT_PIPELINE_DEPTH < n_blk:
            scores[b + ATT_PIPELINE_DEPTH] = score_stage(blocks[b + ATT_PIPELINE_DEPTH])
        if pending is not None:
            store_stage(*pending)
        pending = (blocks[b], mx, ov)
    store_stage(*pending)

    m_all = jnp.maximum(jnp.maximum(m_scr[0], m_scr[1]), m_scr[2])
    num = jnp.zeros((tile, 128), F32)
    den = jnp.zeros((tile, 128), F32)
    for di in range(len(ATT_DILATIONS)):
        e = jnp.exp2(m_scr[di] - m_all)
        num = num + acc_scr[di] * e
        den = den + l_scr[di] * e
    y_ref[...] = (num / den).astype(BF16)


def _attention_mixer(p, q_g, k_g, rel_bias, seq_len):
    T = p.shape[0]
    tile = ATT_TILE
    n_tiles = T // tile
    rad = ATT_RADIUS
    seg = np.kron(np.eye(2), np.full((ATT_HEAD_DIM, ATT_HEAD_DIM), 1.0 / ATT_HEAD_DIM))
    mseg = jnp.asarray(seg, BF16)
    bkt_t = np.transpose(_t5_bucket_table(), (0, 2, 1))
    bkt = jnp.asarray(np.concatenate([bkt_t, bkt_t], axis=2))
    qg2 = jnp.tile(q_g, 2)[None]
    kg2 = jnp.tile(k_g, 2)[None]

    def blk(col0, shift):
        return pl.BlockSpec((tile, 128),
                            lambda hp, i: (jnp.clip(i + shift, 0, n_tiles - 1), col0 * 4 + hp))

    const = lambda shape: pl.BlockSpec(shape, lambda hp, i: (0,) * len(shape))
    return pl.pallas_call(
        functools.partial(_att_kernel, seq_len=seq_len),
        grid=(4, n_tiles),
        in_specs=[
            pl.BlockSpec(memory_space=pltpu.SMEM),
            const((3, 3 * rad, 2 * rad)), const((1, 128)), const((1, 128)), const((128, 128)),
            blk(1, 0),
            blk(2, 0), blk(2, 1),
            blk(3, -1), blk(3, 0), blk(3, 1),
        ],
        out_specs=pl.BlockSpec((tile, 128), lambda hp, i: (i, hp)),
        out_shape=jax.ShapeDtypeStruct((T, GROUP), BF16),
        scratch_shapes=[
            pltpu.VMEM((3, 3 * rad, 2 * rad), F32),
            pltpu.VMEM((tile, 128), F32),
            pltpu.VMEM((3 * tile, 128), F32),
            pltpu.VMEM((3 * tile, 128), F32),
            pltpu.VMEM((3, tile, 128), F32),
            pltpu.VMEM((3, tile, 128), F32),
            pltpu.VMEM((3, tile, 128), F32),
        ],
        compiler_params=_cparams("arbitrary", "arbitrary"),
        name="dilated_attention",
    )(rel_bias, bkt, qg2, kg2, mseg, p, p, p, p, p, p)


def _softplus(x):
    return jnp.maximum(x, 0.0) + jnp.log1p(jnp.exp(-jnp.abs(x)))


def _dn_prep_kernel(x_ref, xp_ref, xn_ref, w_ref, o_ref, ext_scr, *, tm, seq_len):
    i = pl.program_id(0)
    first, last, _ = _seq_edges(i, seq_len // tm)
    ext_scr[0:8, :] = jnp.where(first, jnp.zeros_like(xp_ref[...]), xp_ref[...])
    ext_scr[8:8 + tm, :] = x_ref[...]
    ext_scr[8 + tm:16 + tm, :] = jnp.where(last, jnp.zeros_like(xn_ref[...]), xn_ref[...])
    for cb in range(3 * DN_HEADS):
        lanes = slice(cb * 128, (cb + 1) * 128)
        acc = ext_scr[pl.ds(6, tm), lanes] * w_ref[0:1, lanes]
        for kk in range(1, 4):
            acc = acc + ext_scr[pl.ds(6 + kk, tm), lanes] * w_ref[kk:kk + 1, lanes]
        y = _silu(acc)
        if cb < 2 * DN_HEADS:
            y = y * lax.rsqrt(jnp.sum(y * y, axis=-1, keepdims=True) + EPS)
        if cb < DN_HEADS:
            y = y * (DN_HEAD_DIM ** -0.5)
        o_ref[:, lanes] = y


def _dn_prep(p, conv_w, seq_len, tm=512):
    T = p.shape[0]
    n_tiles = T // tm
    width = 3 * GROUP
    prev_spec, next_spec = _halo_specs(tm, 8, width, 2, n_tiles)
    return pl.pallas_call(
        functools.partial(_dn_prep_kernel, tm=tm, seq_len=seq_len),
        grid=(n_tiles,),
        in_specs=[pl.BlockSpec((tm, width), lambda i: (i, 2)), prev_spec, next_spec,
                  pl.BlockSpec((4, width), lambda i: (0, 0))],
        out_specs=pl.BlockSpec((tm, width), lambda i: (i, 0)),
        out_shape=jax.ShapeDtypeStruct((T, width), F32),
        scratch_shapes=[pltpu.VMEM((tm + 16, width), F32)],
        compiler_params=_cparams("parallel"),
        name="deltanet_prep",
    )(p, p, p, conv_w)


DN_GROUP = 16
DN_ROWS = DN_HEADS * DN_CHUNK
DN_SCAN_LAG_STEPS = 4


def _dn_scan_kernel(xf_ref, xb_ref, gf_ref, gb_ref, rf_ref, rb_ref, alc_ref, dtc_ref, alr_ref, dtr_ref,
                    of_ref, ob_ref, s_scr):
    C = DN_CHUNK
    H = DN_HEADS
    R = DN_GROUP * C

    @pl.when(pl.program_id(1) == 0)
    def _():
        s_scr[...] = jnp.zeros_like(s_scr)

    row_in_chunk = lax.rem(lax.broadcasted_iota(jnp.int32, (R, 1), 0), C)
    lane_in_chunk = lax.rem(lax.broadcasted_iota(jnp.int32, (1, DN_ROWS), 1), C)
    rid = lax.broadcasted_iota(jnp.int32, (DN_ROWS, DN_ROWS), 0)
    cid = lax.broadcasted_iota(jnp.int32, (DN_ROWS, DN_ROWS), 1)
    same_head = (rid // C) == (cid // C)
    row_sbs = lax.broadcasted_iota(jnp.int32, (C, DN_ROWS), 0)
    col_sbs = lax.rem(lax.broadcasted_iota(jnp.int32, (C, DN_ROWS), 1), C)
    lane_head = lax.broadcasted_iota(jnp.int32, (1, DN_ROWS), 1) // C

    def block_diag(m):
        return jnp.where(same_head, jnp.concatenate([m] * H, axis=0), jnp.zeros((), m.dtype))

    def side_by_side_blocks(x):
        return jnp.concatenate([x[h * C:(h + 1) * C] for h in range(H)], axis=1)

    def block_diag_blocks(x):
        zero = jnp.zeros((C, 128), x.dtype)
        return jnp.concatenate(
            [jnp.concatenate([x[h * C:(h + 1) * C] if g == h else zero for g in range(H)], axis=1)
             for h in range(H)], axis=0)

    def stack_heads(x, col0):
        return jnp.concatenate([x[:, col0 + h * 128:col0 + (h + 1) * 128] for h in range(H)], axis=0)

    def stack_cols(x, lane0, rows=None):
        parts = []
        for h in range(H):
            c = x[:, lane0 + h:lane0 + h + 1]
            if rows is not None:
                c = c[rows:rows + 1, :]
            parts.append(jnp.broadcast_to(c, (C, 128)))
        return jnp.concatenate(parts, axis=0)

    refs = ((xf_ref, gf_ref, rf_ref, of_ref), (xb_ref, gb_ref, rb_ref, ob_ref))
    gates = []
    for d, (x_ref, gc_ref, gr_ref, o_ref) in enumerate(refs):
        reverse = d == 1
        gcol = gc_ref[...]
        g = -jnp.exp(alc_ref[...]) * _softplus(gcol + dtc_ref[...])
        beta = _sigmoid(gcol)
        gcum = g
        for sh in (1, 2, 4, 8, 16, 32):
            if reverse:
                gcum = gcum + jnp.where(row_in_chunk < C - sh, pltpu.roll(gcum, R - sh, 0), 0.0)
            else:
                gcum = gcum + jnp.where(row_in_chunk >= sh, pltpu.roll(gcum, sh, 0), 0.0)
        grow_all = -jnp.exp(alr_ref[...])[None] * _softplus(gr_ref[...] + dtr_ref[...][None])
        for sh in (1, 2, 4, 8, 16, 32):
            if reverse:
                grow_all = grow_all + jnp.where(lane_in_chunk < C - sh,
                                                pltpu.roll(grow_all, DN_ROWS - sh, 2), 0.0)
            else:
                grow_all = grow_all + jnp.where(lane_in_chunk >= sh, pltpu.roll(grow_all, sh, 2), 0.0)

        gates.append((gcum, beta, grow_all))

    def fill_operands(u):
        d = u["d"]
        x_ref, _, _, o_ref = refs[d]
        reverse = d == 1
        ci = DN_GROUP - 1 - u["step"] if reverse else u["step"]
        gcum, beta, grow_all = gates[d]
        rows = slice(ci * C, (ci + 1) * C)
        x = x_ref[rows, :]
        u.update(rows=rows, o_ref=o_ref)
        u["q"] = stack_heads(x, 0)
        u["k"] = stack_heads(x, GROUP)
        v_st = stack_heads(x, 2 * GROUP)
        gc_c = gcum[rows, :]
        beta_st = stack_cols(beta[rows, :], 8 + d * H)
        gcol_st = stack_cols(gc_c, d * H)
        glast_st = stack_cols(gc_c, d * H, rows=0 if reverse else C - 1)
        grow = grow_all[ci, d:d + 1, :]
        gcol_sbs = jnp.broadcast_to(gc_c[:, d * H + H - 1:d * H + H], (C, DN_ROWS))
        for h in range(H - 2, -1, -1):
            gcol_sbs = jnp.where(lane_head <= h, gc_c[:, d * H + h:d * H + h + 1], gcol_sbs)
        u["incl"] = (row_sbs <= col_sbs) if reverse else (row_sbs >= col_sbs)
        u["decay"] = jnp.where(u["incl"], jnp.exp(gcol_sbs - grow), 0.0)
        kb_st = u["k"] * beta_st
        eg = jnp.exp(gcol_st)
        u["kq"] = jnp.concatenate([side_by_side_blocks(kb_st), side_by_side_blocks(u["q"])],
                                  axis=0).astype(BF16)
        u["kbd"] = block_diag_blocks(u["k"].astype(BF16))
        u["rhs"] = jnp.concatenate([v_st * beta_st, kb_st * eg], axis=1).astype(BF16)
        u["qdec"] = u["q"] * eg
        u["kdec"] = (u["k"] * jnp.exp(glast_st - gcol_st)).astype(BF16)
        u["gl"] = jnp.exp(glast_st)

    eye_sbs = (row_sbs == col_sbs).astype(F32)

    def prep_stages(units):
        for u in units:
            fill_operands(u)
        yield
        for n, u in enumerate(units):
            u["kk"] = _dot_nt(u["kq"], u["kbd"])
            if n == len(units) // 2 - 1:
                yield
        yield
        for u in units:
            strict = u["incl"] & (row_sbs != col_sbs)
            a_sbs = jnp.where(strict, u["kk"][0:C] * u["decay"], 0.0)
            u["attn"] = (u["kk"][C:2 * C] * u["decay"]).astype(BF16)
            u["pinv"] = eye_sbs - a_sbs
            u["a_sbs"] = a_sbs.astype(BF16)
        for u in units:
            u["apow"] = _dot(u["a_sbs"], block_diag(u["a_sbs"]))
        yield
        for _ in range(4):
            for n, u in enumerate(units):
                ap = u["apow"].astype(BF16)
                u["both"] = _dot(jnp.concatenate([u["pinv"].astype(BF16), ap], axis=0), block_diag(ap))
                if n == len(units) // 2 - 1:
                    yield
            yield
            for u in units:
                u["pinv"] = u["pinv"] + u["both"][0:C]
                u["apow"] = u["both"][C:2 * C]
        for u in units:
            u["last"] = _dot(u["pinv"].astype(BF16), block_diag(u["apow"].astype(BF16)))
        yield
        for u in units:
            u["uw"] = _dot(block_diag((u["pinv"] + u["last"]).astype(BF16)), u["rhs"])
        yield

    def scan_stages(units):
        for i in range(0, len(units), 2):
            pair = units[i:i + 2]
            for u in pair:
                d = u["d"]
                u["wq"] = []
                for h in range(H):
                    hr = slice(h * C, (h + 1) * C)
                    lhs = jnp.concatenate([u["uw"][hr, 128:256], u["qdec"][hr]], axis=0).astype(BF16)
                    u["wq"].append(_dot(lhs, s_scr[d * H + h].astype(BF16)))
            yield
            for u in pair:
                u["vnew"] = [(u["uw"][h * C:(h + 1) * C, 0:128] - u["wq"][h][0:C]).astype(BF16)
                             for h in range(H)]
                vnew_bd = block_diag_blocks(jnp.concatenate(u["vnew"], axis=0))
                u["o"] = jnp.concatenate([w[C:2 * C] for w in u["wq"]], axis=1) + _dot(u["attn"], vnew_bd)
            yield
            for u in pair:
                d = u["d"]
                for h in range(H):
                    hr = slice(h * C, (h + 1) * C)
                    s_scr[d * H + h] = (s_scr[d * H + h] * u["gl"][h * C:h * C + 1, :]
                                        + _dot_tn(u["kdec"][hr], u["vnew"][h]))
                u["o_ref"][u["rows"], :] = u["o"]
            yield

    def run_together(*stage_generators):
        live = list(stage_generators)
        while live:
            for gen in list(live):
                if next(gen, StopIteration) is StopIteration:
                    live.remove(gen)

    batches = [[dict(step=step, d=d) for step in range(s0, s0 + DN_SCAN_LAG_STEPS) for d in range(2)]
               for s0 in range(0, DN_GROUP, DN_SCAN_LAG_STEPS)]
    run_together(prep_stages(batches[0]))
    for prev, cur in zip(batches[:-1], batches[1:]):
        run_together(prep_stages(cur), scan_stages(prev))
    run_together(scan_stages(batches[-1]))


def _dn_scan(qkvn, pg, a_log, dt_bias, batch, seq_len):
    T = qkvn.shape[0]
    C, H, G = DN_CHUNK, DN_HEADS, DN_GROUP
    R = G * C
    nc = seq_len // C
    ncg = nc // G
    ab = pg[:, 0:4 * H].reshape(batch, nc, C, 2, 2, H)
    ab_row = jnp.transpose(ab, (0, 1, 3, 4, 5, 2)).reshape(batch * nc, 4, H * C)
    ab_row = jnp.pad(ab_row, ((0, 0), (0, 4), (0, 0)))
    pad_lanes = lambda v: jnp.pad(v.reshape(1, 2 * H), ((0, 0), (0, GATE_LANES - 2 * H)))
    row_param = lambda v: jnp.pad(jnp.repeat(v, C, axis=1), ((0, 6), (0, 0)))

    fwd = lambda b, c: (b * ncg + c, 0)
    bwd = lambda b, c: (b * ncg + ncg - 1 - c, 0)
    fwd3 = lambda b, c: (b * ncg + c, 0, 0)
    bwd3 = lambda b, c: (b * ncg + ncg - 1 - c, 0, 0)
    const = lambda shape: pl.BlockSpec(shape, lambda b, c: (0,) * len(shape))
    return pl.pallas_call(
        _dn_scan_kernel,
        grid=(batch, ncg),
        in_specs=[
            pl.BlockSpec((R, 3 * GROUP), fwd), pl.BlockSpec((R, 3 * GROUP), bwd),
            pl.BlockSpec((R, GATE_LANES), fwd), pl.BlockSpec((R, GATE_LANES), bwd),
            pl.BlockSpec((G, 8, H * C), fwd3), pl.BlockSpec((G, 8, H * C), bwd3),
            const((1, GATE_LANES)), const((1, GATE_LANES)), const((8, H * C)), const((8, H * C)),
        ],
        out_specs=[pl.BlockSpec((R, GROUP), fwd), pl.BlockSpec((R, GROUP), bwd)],
        out_shape=[jax.ShapeDtypeStruct((T, GROUP), F32), jax.ShapeDtypeStruct((T, GROUP), F32)],
        scratch_shapes=[pltpu.VMEM((2 * H, DN_HEAD_DIM, DN_HEAD_DIM), F32)],
        compiler_params=_cparams("arbitrary", "arbitrary"),
        name="deltanet_scan",
    )(qkvn, qkvn, pg, pg, ab_row, ab_row, pad_lanes(a_log), pad_lanes(dt_bias),
      row_param(a_log), row_param(dt_bias))


def _dn_post_kernel(of_ref, ob_ref, z_ref, g_ref, y_ref):
    for h in range(DN_HEADS):
        lanes = slice(h * 128, (h + 1) * 128)
        o = of_ref[:, lanes] + ob_ref[:, lanes]
        o = o * lax.rsqrt(jnp.mean(o * o, axis=-1, keepdims=True) + EPS) * g_ref[...]
        y_ref[:, lanes] = (o * _silu(z_ref[:, lanes])).astype(BF16)


def _dn_post(o_f, o_b, p, norm_g, tm=1024):
    T = p.shape[0]
    row = lambda cb: pl.BlockSpec((tm, GROUP), lambda i: (i, cb))
    return pl.pallas_call(
        _dn_post_kernel,
        grid=(T // tm,),
        in_specs=[row(0), row(0), row(9), pl.BlockSpec((1, DN_HEAD_DIM), lambda i: (0, 0))],
        out_specs=row(0),
        out_shape=jax.ShapeDtypeStruct((T, GROUP), BF16),
        compiler_params=_cparams("parallel"),
        name="deltanet_post",
    )(o_f, o_b, p, norm_g)


def _deltanet_mixer(p, pg, conv_w, a_log, dt_bias, norm_g, batch, seq_len):
    qkvn = _dn_prep(p, conv_w, seq_len)
    o_f, o_b = _dn_scan(qkvn, pg, a_log, dt_bias, batch, seq_len)
    return _dn_post(o_f, o_b, p, norm_g[None])


def _outproj_kernel(x_ref, ya_ref, yb_ref, yc_ref, yd_ref, w_ref, g_ref, xn_ref, hn_ref, ycat_scr):
    for gi, y_ref in enumerate((ya_ref, yb_ref, yc_ref, yd_ref)):
        ycat_scr[:, gi * GROUP:(gi + 1) * GROUP] = y_ref[...]
    acc = x_ref[...] + _dot(ycat_scr[...], w_ref[...])
    xn_ref[...] = acc
    ms = jnp.mean(acc * acc, axis=-1, keepdims=True)
    hn_ref[...] = (acc * lax.rsqrt(ms + EPS) * g_ref[...]).astype(BF16)


def _out_proj(x2, ys, w_out, norm2_g, tm=512):
    T = x2.shape[0]
    row = lambda width: pl.BlockSpec((tm, width), lambda i: (i, 0))
    return pl.pallas_call(
        _outproj_kernel,
        grid=(T // tm,),
        in_specs=[row(D_MODEL), row(GROUP), row(GROUP), row(GROUP), row(GROUP),
                  pl.BlockSpec((D_MODEL, D_MODEL), lambda i: (0, 0), pipeline_mode=pl.Buffered(1)),
                  pl.BlockSpec((1, D_MODEL), lambda i: (0, 0))],
        out_specs=[row(D_MODEL), row(D_MODEL)],
        out_shape=[jax.ShapeDtypeStruct((T, D_MODEL), F32),
                   jax.ShapeDtypeStruct((T, D_MODEL), BF16)],
        scratch_shapes=[pltpu.VMEM((tm, D_MODEL), BF16)],
        compiler_params=_cparams("parallel"),
        name="out_proj",
    )(x2, *ys, w_out, norm2_g)


FFN_HALF = FFN_HIDDEN // 2
FFN_COL_CHUNK = 256


def _ffn_hidden_kernel(hn_ref, hp_ref, hx_ref, wg_ref, wu_ref, dw_ref, db_ref, a_ref, hext_scr, g_scr,
                       *, tm, seq_len):
    first, last, _ = _seq_edges(pl.program_id(1), seq_len // tm)
    hext_scr[0:16, :] = jnp.where(first, jnp.zeros_like(hp_ref[...]), hp_ref[...])
    hext_scr[16:16 + tm, :] = hn_ref[...]
    hext_scr[16 + tm:32 + tm, :] = jnp.where(last, jnp.zeros_like(hx_ref[...]), hx_ref[...])
    for c in range(FFN_HALF // FFN_COL_CHUNK):
        cols = slice(c * FFN_COL_CHUNK, (c + 1) * FFN_COL_CHUNK)
        g = g_scr.at[c % 2]
        g[...] = _dot(hext_scr[...], wg_ref[:, cols])
        up = _dot(hn_ref[...], wu_ref[:, cols])
        gate = (g[pl.ds(15, tm), :] * dw_ref[0:1, cols] + g[pl.ds(16, tm), :] * dw_ref[1:2, cols]
                + g[pl.ds(17, tm), :] * dw_ref[2:3, cols] + db_ref[:, cols])
        a_ref[:, cols] = (_silu(gate) * up).astype(BF16)


def _ffn_down_kernel(a_ref, xn_ref, wd_ref, o_ref):
    o_ref[...] = xn_ref[...] + _dot(a_ref[...], wd_ref[...])


def _ffn(hn, xn, w_gate, dw_w, dw_b, w_up, w_down, seq_len, tm_hidden=1024, tm_down=512):
    T = hn.shape[0]
    n_tiles = T // tm_hidden
    r = tm_hidden // 16
    half_cols = lambda rows: pl.BlockSpec((rows, FFN_HALF), lambda h, i: (0, h),
                                          pipeline_mode=pl.Buffered(1))
    act = pl.pallas_call(
        functools.partial(_ffn_hidden_kernel, tm=tm_hidden, seq_len=seq_len),
        grid=(2, n_tiles),
        in_specs=[
            pl.BlockSpec((tm_hidden, D_MODEL), lambda h, i: (i, 0)),
            pl.BlockSpec((16, D_MODEL), lambda h, i: (jnp.maximum(i * r - 1, 0), 0)),
            pl.BlockSpec((16, D_MODEL), lambda h, i: (jnp.minimum((i + 1) * r, n_tiles * r - 1), 0)),
            half_cols(D_MODEL), half_cols(D_MODEL), half_cols(3), half_cols(1),
        ],
        out_specs=pl.BlockSpec((tm_hidden, FFN_HALF), lambda h, i: (i, h)),
        out_shape=jax.ShapeDtypeStruct((T, FFN_HIDDEN), BF16),
        scratch_shapes=[pltpu.VMEM((tm_hidden + 32, D_MODEL), BF16),
                        pltpu.VMEM((2, tm_hidden + 32, FFN_COL_CHUNK), F32)],
        compiler_params=_cparams("arbitrary", "arbitrary"),
        name="ffn_hidden",
    )(hn, hn, hn, w_gate, w_up, dw_w, dw_b)
    return pl.pallas_call(
        _ffn_down_kernel,
        grid=(T // tm_down,),
        in_specs=[
            pl.BlockSpec((tm_down, FFN_HIDDEN), lambda i: (i, 0)),
            pl.BlockSpec((tm_down, D_MODEL), lambda i: (i, 0)),
            pl.BlockSpec((FFN_HIDDEN, D_MODEL), lambda i: (0, 0), pipeline_mode=pl.Buffered(1)),
        ],
        out_specs=pl.BlockSpec((tm_down, D_MODEL), lambda i: (i, 0)),
        out_shape=jax.ShapeDtypeStruct((T, D_MODEL), F32),
        compiler_params=_cparams("parallel"),
        name="ffn_down",
    )(act, xn, w_down)


def _layer(x2, l, batch, seq_len, rel_bias, norm1_g, w_in, w_pool, pool_scale, att_q_g, att_k_g,
           conv_dw_w, conv_dw_b, conv_ln_g, conv_ln_b, conv_pw, dn_conv_w, dn_a_log, dn_dt_bias,
           dn_norm_g, w_out, norm2_g, ffn_w_gate, ffn_dw_w, ffn_dw_b, ffn_w_up, ffn_w_down):
    w_main = _weight_bf16(w_in, l, cols=PROJ_MAIN)
    w_gates = _gate_weight_bf16(w_in, l)
    p, pg = _in_proj(x2, norm1_g[l][None], w_main, w_gates)
    ya = _pool_mixer(p, w_pool[l].astype(BF16), pool_scale[l][None], seq_len)
    yb = _attention_mixer(p, att_q_g[l], att_k_g[l], rel_bias, seq_len)
    yc = _conformer_mixer(p, conv_dw_w[l], conv_dw_b[l][None], conv_ln_g[l][None], conv_ln_b[l][None],
                          conv_pw[l].astype(BF16), seq_len)
    yd = _deltanet_mixer(p, pg, dn_conv_w[l], dn_a_log[l], dn_dt_bias[l], dn_norm_g[l], batch, seq_len)
    xn, hn = _out_proj(x2, (ya, yb, yc, yd), _weight_bf16(w_out, l), norm2_g[l][None])
    return _ffn(hn, xn, _weight_bf16(ffn_w_gate, l), ffn_dw_w[l], ffn_dw_b[l][None],
                _weight_bf16(ffn_w_up, l), _weight_bf16(ffn_w_down, l), seq_len)


def kernel(x, rel_bias, norm1_g, w_in, w_pool, pool_scale, att_q_g, att_k_g, conv_dw_w, conv_dw_b,
           conv_ln_g, conv_ln_b, conv_pw, dn_conv_w, dn_a_log, dn_dt_bias, dn_norm_g, w_out, norm2_g,
           ffn_w_gate, ffn_dw_w, ffn_dw_b, ffn_w_up, ffn_w_down):
    batch, seq_len, _ = x.shape
    x2 = x.reshape(batch * seq_len, D_MODEL)
    for l in range(norm1_g.shape[0]):
        x2 = _layer(x2, l, batch, seq_len, rel_bias, norm1_g, w_in, w_pool, pool_scale, att_q_g,
                    att_k_g, conv_dw_w, conv_dw_b, conv_ln_g, conv_ln_b, conv_pw, dn_conv_w, dn_a_log,
                    dn_dt_bias, dn_norm_g, w_out, norm2_g, ffn_w_gate, ffn_dw_w, ffn_dw_b, ffn_w_up,
                    ffn_w_down)
    return x2.reshape(batch, seq_len, D_MODEL)
```

```python
import functools
import math

import jax
import jax.numpy as jnp
import numpy as np
from jax import lax
from jax.experimental import pallas as pl
from jax.experimental.pallas import tpu as pltpu

F32 = jnp.float32
BF16 = jnp.bfloat16

D_MODEL = 2048
GROUP = 512
POOL_WINDOWS = (2, 4, 8, 16)
ATT_HEAD_DIM = 64
ATT_HEADS = 8
ATT_RADIUS = 64
ATT_DILATIONS = (1, 4, 16)
REL_BUCKETS = 32
REL_MAX_DIST = 1024
CONV_WIDTH = 31
DN_HEAD_DIM = 128
DN_HEADS = 4
DN_CHUNK = 64
FFN_HIDDEN = 5632
EPS = 1e-6
NEG_INF = -1e30

PROJ_MAIN = 10 * GROUP
GATE_LANES = 128

VMEM_LIMIT_BYTES = 56 * 1024 * 1024


def _cparams(*sem):
    return pltpu.CompilerParams(dimension_semantics=sem, vmem_limit_bytes=VMEM_LIMIT_BYTES)


def _sigmoid(x):
    return 1.0 / (1.0 + jnp.exp(-x))


def _silu(x):
    return x * _sigmoid(x)


def _dot(a, b):
    return jnp.dot(a, b, preferred_element_type=F32)


def _dot_nt(a, b):
    return lax.dot_general(a, b, (((1,), (1,)), ((), ())), preferred_element_type=F32)


def _dot_tn(a, b):
    return lax.dot_general(a, b, (((0,), (0,)), ((), ())), preferred_element_type=F32)


def _cast_kernel(w_ref, o_ref):
    o_ref[...] = w_ref[...].astype(BF16)


def _weight_bf16(w_stack, layer, cols=None, tr=512):
    _, rows, width = w_stack.shape
    cols = width if cols is None else cols
    tc = cols // 2 if cols >= 4096 else cols
    return pl.pallas_call(
        _cast_kernel,
        grid=(rows // tr, cols // tc),
        in_specs=[pl.BlockSpec((None, tr, tc), lambda i, j: (layer, i, j))],
        out_specs=pl.BlockSpec((tr, tc), lambda i, j: (i, j)),
        out_shape=jax.ShapeDtypeStruct((rows, cols), BF16),
        compiler_params=_cparams("parallel", "parallel"),
        name="weight_bf16",
    )(w_stack)


def _gate_cols_kernel(w_ref, o_ref, *, valid):
    lane = lax.broadcasted_iota(jnp.int32, o_ref.shape, 1)
    o_ref[...] = jnp.where(lane < valid, w_ref[...], 0.0).astype(BF16)


def _gate_weight_bf16(w_stack, layer):
    _, rows, width = w_stack.shape
    valid = width - PROJ_MAIN
    return pl.pallas_call(
        functools.partial(_gate_cols_kernel, valid=valid),
        grid=(1,),
        in_specs=[pl.BlockSpec((None, rows, GATE_LANES), lambda i: (layer, 0, PROJ_MAIN // GATE_LANES))],
        out_specs=pl.BlockSpec((rows, GATE_LANES), lambda i: (0, 0)),
        out_shape=jax.ShapeDtypeStruct((rows, GATE_LANES), BF16),
        compiler_params=_cparams("arbitrary"),
        name="gate_weight_bf16",
    )(w_stack)


def _inproj_kernel(x_ref, g_ref, w_ref, wg_ref, p_ref, pg_ref):
    x = x_ref[...]
    ms = jnp.mean(x * x, axis=-1, keepdims=True)
    h = (x * lax.rsqrt(ms + EPS) * g_ref[...]).astype(BF16)
    pg_ref[...] = _dot(h, wg_ref[...])
    p_ref[...] = _dot(h, w_ref[...])


def _in_proj(x2, norm_g, w_main, w_gate, tm=512):
    T = x2.shape[0]
    resident = lambda shape: pl.BlockSpec(shape, lambda i: (0, 0), pipeline_mode=pl.Buffered(1))
    return pl.pallas_call(
        _inproj_kernel,
        grid=(T // tm,),
        in_specs=[
            pl.BlockSpec((tm, D_MODEL), lambda i: (i, 0)),
            resident((1, D_MODEL)),
            resident((D_MODEL, PROJ_MAIN)),
            resident((D_MODEL, GATE_LANES)),
        ],
        out_specs=[
            pl.BlockSpec((tm, PROJ_MAIN), lambda i: (i, 0)),
            pl.BlockSpec((tm, GATE_LANES), lambda i: (i, 0)),
        ],
        out_shape=[
            jax.ShapeDtypeStruct((T, PROJ_MAIN), F32),
            jax.ShapeDtypeStruct((T, GATE_LANES), F32),
        ],
        compiler_params=_cparams("parallel"),
        name="in_proj",
    )(x2, norm_g, w_main, w_gate)


def _halo_specs(tm, halo, width, col_block, n_tiles):
    r = tm // halo
    last = n_tiles * r - 1
    prev_spec = pl.BlockSpec((halo, width), lambda i, *_: (jnp.maximum(i * r - 1, 0), col_block))
    next_spec = pl.BlockSpec((halo, width), lambda i, *_: (jnp.minimum((i + 1) * r, last), col_block))
    return prev_spec, next_spec


def _seq_edges(i, tiles_per_seq):
    k = lax.rem(i, tiles_per_seq)
    return k == 0, k == tiles_per_seq - 1, k


def _pool_kernel(u_ref, up_ref, un_ref, w_ref, sc_ref, y_ref, ext_scr, *, tm, seq_len):
    i = pl.program_id(0)
    first, last, k = _seq_edges(i, seq_len // tm)
    ext_scr[0:8, :] = jnp.where(first, jnp.zeros_like(up_ref[...]), up_ref[...])
    ext_scr[8:8 + tm, :] = u_ref[...]
    ext_scr[8 + tm:16 + tm, :] = jnp.where(last, jnp.zeros_like(un_ref[...]), un_ref[...])
    t = k * tm + lax.broadcasted_iota(jnp.int32, (tm, 1), 0)
    for gi, win in enumerate(POOL_WINDOWS):
        half = win // 2
        lanes = slice(gi * 128, (gi + 1) * 128)
        s = ext_scr[pl.ds(8 - half, tm), lanes]
        for kk in range(1, win):
            s = s + ext_scr[pl.ds(8 - half + kk, tm), lanes]
        cnt = (jnp.minimum(t + half, seq_len) - jnp.maximum(t - half, 0)).astype(F32)
        pooled = s / cnt - ext_scr[pl.ds(8, tm), lanes]
        y = _dot(pooled.astype(BF16), w_ref[gi]) * sc_ref[:, lanes]
        y_ref[:, lanes] = y.astype(BF16)


def _pool_mixer(p, w_pool, pool_scale, seq_len, tm=1024):
    T = p.shape[0]
    n_tiles = T // tm
    prev_spec, next_spec = _halo_specs(tm, 8, GROUP, 0, n_tiles)
    return pl.pallas_call(
        functools.partial(_pool_kernel, tm=tm, seq_len=seq_len),
        grid=(n_tiles,),
        in_specs=[
            pl.BlockSpec((tm, GROUP), lambda i: (i, 0)),
            prev_spec,
            next_spec,
            pl.BlockSpec((4, 128, 128), lambda i: (0, 0, 0)),
            pl.BlockSpec((1, GROUP), lambda i: (0, 0)),
        ],
        out_specs=pl.BlockSpec((tm, GROUP), lambda i: (i, 0)),
        out_shape=jax.ShapeDtypeStruct((T, GROUP), BF16),
        scratch_shapes=[pltpu.VMEM((tm + 16, GROUP), F32)],
        compiler_params=_cparams("parallel"),
        name="pool_mixer",
    )(p, p, p, w_pool, pool_scale)


def _conformer_kernel(v_ref, vp_ref, vn_ref, g_ref, gp_ref, gn_ref, dw_ref, db_ref, lg_ref, lb_ref,
                      pw_ref, y_ref, ext_scr, shift_scr, *, tm, seq_len):
    i = pl.program_id(0)
    first, last, _ = _seq_edges(i, seq_len // tm)
    hp = vp_ref[...] * _sigmoid(gp_ref[...])
    hn = vn_ref[...] * _sigmoid(gn_ref[...])
    ext_scr[0:16, :] = jnp.where(first, jnp.zeros_like(hp), hp)
    ext_scr[16:16 + tm, :] = v_ref[...] * _sigmoid(g_ref[...])
    ext_scr[16 + tm:32 + tm, :] = jnp.where(last, jnp.zeros_like(hn), hn)
    base = 16 - CONV_WIDTH // 2
    acc = db_ref[...]
    for b in range(8):
        taps = [(a, 8 * a + b - base) for a in range(5) if 0 <= 8 * a + b - base < CONV_WIDTH]
        rows = tm + 8 * taps[-1][0]
        shift_scr[b, 0:rows, :] = ext_scr[pl.ds(b, rows), :]
        for a, kk in taps:
            acc = acc + shift_scr[b, 8 * a:8 * a + tm, :] * dw_ref[kk:kk + 1, :]
    mu = jnp.mean(acc, axis=-1, keepdims=True)
    xc = acc - mu
    var = jnp.mean(xc * xc, axis=-1, keepdims=True)
    h = _silu(xc * lax.rsqrt(var + EPS) * lg_ref[...] + lb_ref[...])
    y_ref[...] = _dot(h.astype(BF16), pw_ref[...]).astype(BF16)


def _conformer_mixer(p, dw_w, dw_b, ln_g, ln_b, pw, seq_len, tm=1024):
    T = p.shape[0]
    n_tiles = T // tm
    vprev, vnext = _halo_specs(tm, 16, GROUP, 4, n_tiles)
    gprev, gnext = _halo_specs(tm, 16, GROUP, 5, n_tiles)
    const = lambda shape: pl.BlockSpec(shape, lambda i: (0,) * len(shape))
    return pl.pallas_call(
        functools.partial(_conformer_kernel, tm=tm, seq_len=seq_len),
        grid=(n_tiles,),
        in_specs=[
            pl.BlockSpec((tm, GROUP), lambda i: (i, 4)), vprev, vnext,
            pl.BlockSpec((tm, GROUP), lambda i: (i, 5)), gprev, gnext,
            const((CONV_WIDTH, GROUP)), const((1, GROUP)), const((1, GROUP)), const((1, GROUP)),
            const((GROUP, GROUP)),
        ],
        out_specs=pl.BlockSpec((tm, GROUP), lambda i: (i, 0)),
        out_shape=jax.ShapeDtypeStruct((T, GROUP), BF16),
        scratch_shapes=[pltpu.VMEM((tm + 32, GROUP), F32), pltpu.VMEM((8, tm + 32, GROUP), F32)],
        compiler_params=_cparams("parallel"),
        name="conformer_mixer",
    )(p, p, p, p, p, p, dw_w, dw_b, ln_g, ln_b, pw)


ATT_TILE = 2048
LOG2E = math.log2(math.e)
ATT_PIPELINE_DEPTH = 3


def _t5_bucket_table():
    nb = REL_BUCKETS // 2
    max_exact = nb // 2
    i = np.arange(ATT_RADIUS)[:, None]
    j = np.arange(3 * ATT_RADIUS)[None, :]
    off = j - ATT_RADIUS - i
    tables = []
    for dil in ATT_DILATIONS:
        rel = off * dil
        n = np.abs(rel)
        nf = np.maximum(n, 1).astype(np.float32)
        large = max_exact + (np.log(nf / np.float32(max_exact)) / np.float32(math.log(REL_MAX_DIST / max_exact))
                             * np.float32(nb - max_exact)).astype(np.int32)
        large = np.minimum(large, nb - 1)
        bucket = np.where(rel > 0, nb, 0) + np.where(n < max_exact, n, large)
        tables.append(np.where(np.abs(off) <= ATT_RADIUS, bucket, -1))
    return np.stack(tables).astype(np.int32)


def _att_kernel(rb_ref, bkt_ref, qg_ref, kg_ref, mseg_ref, q_ref, kc_ref, kn_ref,
                vp_ref, vc_ref, vn_ref, y_ref, bias_scr, qbuf, kbuf, vbuf, acc_scr, m_scr, l_scr,
                *, seq_len):
    tile = ATT_TILE
    rad = ATT_RADIUS
    hp = pl.program_id(0)
    i = pl.program_id(1)

    def rms(x, g):
        ms = _dot((x * x).astype(BF16), mseg_ref[...])
        return x * lax.rsqrt(ms + EPS) * g

    @pl.when(i == 0)
    def _():
        lane_head0 = lax.broadcasted_iota(jnp.int32, (1, 128), 1) < ATT_HEAD_DIM
        for di in range(len(ATT_DILATIONS)):
            bkt = bkt_ref[di]
            b = jnp.zeros(bkt.shape, F32)
            for bb in range(REL_BUCKETS):
                b = jnp.where(bkt == bb, jnp.where(lane_head0, rb_ref[bb, 2 * hp], rb_ref[bb, 2 * hp + 1]), b)
            bias_scr[di] = jnp.where(bkt < 0, NEG_INF, b * LOG2E)
        kbuf[0:tile, :] = jnp.zeros((tile, 128), F32)
        kbuf[tile:2 * tile, :] = rms(kc_ref[...], kg_ref[...])

    @pl.when(i > 0)
    def _():
        kbuf[0:tile, :] = kbuf[tile:2 * tile, :]
        kbuf[tile:2 * tile, :] = kbuf[2 * tile:3 * tile, :]

    first, last, _ = _seq_edges(i, seq_len // tile)

    qbuf[...] = rms(q_ref[...], qg_ref[...]) * (ATT_HEAD_DIM ** -0.5 * LOG2E)
    kbuf[2 * tile:3 * tile, :] = rms(kn_ref[...], kg_ref[...])
    vbuf[0:tile, :] = vp_ref[...]
    vbuf[tile:2 * tile, :] = vc_ref[...]
    vbuf[2 * tile:3 * tile, :] = vn_ref[...]

    head0 = lax.broadcasted_iota(jnp.int32, (1, 128), 1) < ATT_HEAD_DIM
    key_row = lax.broadcasted_iota(jnp.int32, (3 * rad, 1), 0)
    ones_cols = jnp.ones((3 * rad, 128), BF16)

    def rows(start, size, dil):
        return pl.ds(start, size) if dil == 1 else pl.ds(start, size, stride=dil)

    blocks = [(di, dil, r + rad * dil * m)
              for di, dil in enumerate(ATT_DILATIONS) for r in range(dil) for m in range(tile // (rad * dil))]
    def score_stage(blk):
        di, dil, qstart = blk
        kstart = tile + qstart - rad * dil
        qb = qbuf[rows(qstart, rad, dil), :]
        kb = kbuf[rows(kstart, 3 * rad, dil), :].astype(BF16)
        q2 = jnp.concatenate([jnp.where(head0, qb, 0.0), jnp.where(head0, 0.0, qb)], axis=0)
        return _dot_nt(kb, q2.astype(BF16))

    def softmax_stage(blk, s):
        di, dil, qstart = blk
        kstart = tile + qstart - rad * dil
        s = s + bias_scr[di]
        n_prev = max(0, -(-(tile - kstart) // dil))
        n_upto = min(3 * rad, -(-(2 * tile - kstart) // dil))
        if n_prev > 0:
            s = jnp.where(key_row < jnp.where(first, n_prev, 0), NEG_INF, s)
        if n_upto < 3 * rad:
            s = jnp.where(key_row >= jnp.where(last, n_upto, 3 * rad), NEG_INF, s)
        mx = jnp.max(s, axis=0, keepdims=True)
        return mx, jnp.exp2(s - mx).astype(BF16)

    def value_stage(blk, pe):
        di, dil, qstart = blk
        kstart = tile + qstart - rad * dil
        vb = vbuf[rows(kstart, 3 * rad, dil), :].astype(BF16)
        return _dot_tn(pe, jnp.concatenate([vb, ones_cols], axis=1))

    def store_stage(blk, mx, ov):
        di, dil, qstart = blk
        dst = rows(qstart, rad, dil)
        m_col = jnp.broadcast_to(mx, (2 * rad, 128)).T
        acc_scr[di, dst, :] = jnp.where(head0, ov[0:rad, 0:128], ov[rad:2 * rad, 0:128])
        l_scr[di, dst, :] = jnp.where(head0, ov[0:rad, 128:256], ov[rad:2 * rad, 128:256])
        m_scr[di, dst, :] = jnp.where(head0, m_col[0:rad], m_col[rad:2 * rad])

    n_blk = len(blocks)
    scores = {b: score_stage(blocks[b]) for b in range(min(ATT_PIPELINE_DEPTH, n_blk))}
    pending = None
    for b in range(n_blk):
        mx, pe = softmax_stage(blocks[b], scores.pop(b))
        ov = value_stage(blocks[b], pe)
        if b + ATT_PIPELINE_DEPTH < n_blk:
            scores[b + ATT_PIPELINE_DEPTH] = score_stage(blocks[b + ATT_PIPELINE_DEPTH])
        if pending is not None:
            store_stage(*pending)
        pending = (blocks[b], mx, ov)
    store_stage(*pending)

    m_all = jnp.maximum(jnp.maximum(m_scr[0], m_scr[1]), m_scr[2])
    num = jnp.zeros((tile, 128), F32)
    den = jnp.zeros((tile, 128), F32)
    for di in range(len(ATT_DILATIONS)):
        e = jnp.exp2(m_scr[di] - m_all)
        num = num + acc_scr[di] * e
        den = den + l_scr[di] * e
    y_ref[...] = (num / den).astype(BF16)


def _attention_mixer(p, q_g, k_g, rel_bias, seq_len):
    T = p.shape[0]
    tile = ATT_TILE
    n_tiles = T // tile
    rad = ATT_RADIUS
    seg = np.kron(np.eye(2), np.full((ATT_HEAD_DIM, ATT_HEAD_DIM), 1.0 / ATT_HEAD_DIM))
    mseg = jnp.asarray(seg, BF16)
    bkt_t = np.transpose(_t5_bucket_table(), (0, 2, 1))
    bkt = jnp.asarray(np.concatenate([bkt_t, bkt_t], axis=2))
    qg2 = jnp.tile(q_g, 2)[None]
    kg2 = jnp.tile(k_g, 2)[None]

    def blk(col0, shift):
        return pl.BlockSpec((tile, 128),
                            lambda hp, i: (jnp.clip(i + shift, 0, n_tiles - 1), col0 * 4 + hp))

    const = lambda shape: pl.BlockSpec(shape, lambda hp, i: (0,) * len(shape))
    return pl.pallas_call(
        functools.partial(_att_kernel, seq_len=seq_len),
        grid=(4, n_tiles),
        in_specs=[
            pl.BlockSpec(memory_space=pltpu.SMEM),
            const((3, 3 * rad, 2 * rad)), const((1, 128)), const((1, 128)), const((128, 128)),
            blk(1, 0),
            blk(2, 0), blk(2, 1),
            blk(3, -1), blk(3, 0), blk(3, 1),
        ],
        out_specs=pl.BlockSpec((tile, 128), lambda hp, i: (i, hp)),
        out_shape=jax.ShapeDtypeStruct((T, GROUP), BF16),
        scratch_shapes=[
            pltpu.VMEM((3, 3 * rad, 2 * rad), F32),
            pltpu.VMEM((tile, 128), F32),
            pltpu.VMEM((3 * tile, 128), F32),
            pltpu.VMEM((3 * tile, 128), F32),
            pltpu.VMEM((3, tile, 128), F32),
            pltpu.VMEM((3, tile, 128), F32),
            pltpu.VMEM((3, tile, 128), F32),
        ],
        compiler_params=_cparams("arbitrary", "arbitrary"),
        name="dilated_attention",
    )(rel_bias, bkt, qg2, kg2, mseg, p, p, p, p, p, p)


def _softplus(x):
    return jnp.maximum(x, 0.0) + jnp.log1p(jnp.exp(-jnp.abs(x)))


def _dn_prep_kernel(x_ref, xp_ref, xn_ref, w_ref, o_ref, ext_scr, *, tm, seq_len):
    i = pl.program_id(0)
    first, last, _ = _seq_edges(i, seq_len // tm)
    ext_scr[0:8, :] = jnp.where(first, jnp.zeros_like(xp_ref[...]), xp_ref[...])
    ext_scr[8:8 + tm, :] = x_ref[...]
    ext_scr[8 + tm:16 + tm, :] = jnp.where(last, jnp.zeros_like(xn_ref[...]), xn_ref[...])
    for cb in range(3 * DN_HEADS):
        lanes = slice(cb * 128, (cb + 1) * 128)
        acc = ext_scr[pl.ds(6, tm), lanes] * w_ref[0:1, lanes]
        for kk in range(1, 4):
            acc = acc + ext_scr[pl.ds(6 + kk, tm), lanes] * w_ref[kk:kk + 1, lanes]
        y = _silu(acc)
        if cb < 2 * DN_HEADS:
            y = y * lax.rsqrt(jnp.sum(y * y, axis=-1, keepdims=True) + EPS)
        if cb < DN_HEADS:
            y = y * (DN_HEAD_DIM ** -0.5)
        o_ref[:, lanes] = y


def _dn_prep(p, conv_w, seq_len, tm=512):
    T = p.shape[0]
    n_tiles = T // tm
    width = 3 * GROUP
    prev_spec, next_spec = _halo_specs(tm, 8, width, 2, n_tiles)
    return pl.pallas_call(
        functools.partial(_dn_prep_kernel, tm=tm, seq_len=seq_len),
        grid=(n_tiles,),
        in_specs=[pl.BlockSpec((tm, width), lambda i: (i, 2)), prev_spec, next_spec,
                  pl.BlockSpec((4, width), lambda i: (0, 0))],
        out_specs=pl.BlockSpec((tm, width), lambda i: (i, 0)),
        out_shape=jax.ShapeDtypeStruct((T, width), F32),
        scratch_shapes=[pltpu.VMEM((tm + 16, width), F32)],
        compiler_params=_cparams("parallel"),
        name="deltanet_prep",
    )(p, p, p, conv_w)


DN_GROUP = 16
DN_ROWS = DN_HEADS * DN_CHUNK
DN_SCAN_LAG_STEPS = 4


def _dn_scan_kernel(xf_ref, xb_ref, gf_ref, gb_ref, rf_ref, rb_ref, alc_ref, dtc_ref, alr_ref, dtr_ref,
                    of_ref, ob_ref, s_scr):
    C = DN_CHUNK
    H = DN_HEADS
    R = DN_GROUP * C

    @pl.when(pl.program_id(1) == 0)
    def _():
        s_scr[...] = jnp.zeros_like(s_scr)

    row_in_chunk = lax.rem(lax.broadcasted_iota(jnp.int32, (R, 1), 0), C)
    lane_in_chunk = lax.rem(lax.broadcasted_iota(jnp.int32, (1, DN_ROWS), 1), C)
    rid = lax.broadcasted_iota(jnp.int32, (DN_ROWS, DN_ROWS), 0)
    cid = lax.broadcasted_iota(jnp.int32, (DN_ROWS, DN_ROWS), 1)
    same_head = (rid // C) == (cid // C)
    row_sbs = lax.broadcasted_iota(jnp.int32, (C, DN_ROWS), 0)
    col_sbs = lax.rem(lax.broadcasted_iota(jnp.int32, (C, DN_ROWS), 1), C)
    lane_head = lax.broadcasted_iota(jnp.int32, (1, DN_ROWS), 1) // C

    def block_diag(m):
        return jnp.where(same_head, jnp.concatenate([m] * H, axis=0), jnp.zeros((), m.dtype))

    def side_by_side_blocks(x):
        return jnp.concatenate([x[h * C:(h + 1) * C] for h in range(H)], axis=1)

    def block_diag_blocks(x):
        zero = jnp.zeros((C, 128), x.dtype)
        return jnp.concatenate(
            [jnp.concatenate([x[h * C:(h + 1) * C] if g == h else zero for g in range(H)], axis=1)
             for h in range(H)], axis=0)

    def stack_heads(x, col0):
        return jnp.concatenate([x[:, col0 + h * 128:col0 + (h + 1) * 128] for h in range(H)], axis=0)

    def stack_cols(x, lane0, rows=None):
        parts = []
        for h in range(H):
            c = x[:, lane0 + h:lane0 + h + 1]
            if rows is not None:
                c = c[rows:rows + 1, :]
            parts.append(jnp.broadcast_to(c, (C, 128)))
        return jnp.concatenate(parts, axis=0)

    refs = ((xf_ref, gf_ref, rf_ref, of_ref), (xb_ref, gb_ref, rb_ref, ob_ref))
    gates = []
    for d, (x_ref, gc_ref, gr_ref, o_ref) in enumerate(refs):
        reverse = d == 1
        gcol = gc_ref[...]
        g = -jnp.exp(alc_ref[...]) * _softplus(gcol + dtc_ref[...])
        beta = _sigmoid(gcol)
        gcum = g
        for sh in (1, 2, 4, 8, 16, 32):
            if reverse:
                gcum = gcum + jnp.where(row_in_chunk < C - sh, pltpu.roll(gcum, R - sh, 0), 0.0)
            else:
                gcum = gcum + jnp.where(row_in_chunk >= sh, pltpu.roll(gcum, sh, 0), 0.0)
        grow_all = -jnp.exp(alr_ref[...])[None] * _softplus(gr_ref[...] + dtr_ref[...][None])
        for sh in (1, 2, 4, 8, 16, 32):
            if reverse:
                grow_all = grow_all + jnp.where(lane_in_chunk < C - sh,
                                                pltpu.roll(grow_all, DN_ROWS - sh, 2), 0.0)
            else:
                grow_all = grow_all + jnp.where(lane_in_chunk >= sh, pltpu.roll(grow_all, sh, 2), 0.0)

        gates.append((gcum, beta, grow_all))

    def fill_operands(u):
        d = u["d"]
        x_ref, _, _, o_ref = refs[d]
        reverse = d == 1
        ci = DN_GROUP - 1 - u["step"] if reverse else u["step"]
        gcum, beta, grow_all = gates[d]
        rows = slice(ci * C, (ci + 1) * C)
        x = x_ref[rows, :]
        u.update(rows=rows, o_ref=o_ref)
        u["q"] = stack_heads(x, 0)
        u["k"] = stack_heads(x, GROUP)
        v_st = stack_heads(x, 2 * GROUP)
        gc_c = gcum[rows, :]
        beta_st = stack_cols(beta[rows, :], 8 + d * H)
        gcol_st = stack_cols(gc_c, d * H)
        glast_st = stack_cols(gc_c, d * H, rows=0 if reverse else C - 1)
        grow = grow_all[ci, d:d + 1, :]
        gcol_sbs = jnp.broadcast_to(gc_c[:, d * H + H - 1:d * H + H], (C, DN_ROWS))
        for h in range(H - 2, -1, -1):
            gcol_sbs = jnp.where(lane_head <= h, gc_c[:, d * H + h:d * H + h + 1], gcol_sbs)
        u["incl"] = (row_sbs <= col_sbs) if reverse else (row_sbs >= col_sbs)
        u["decay"] = jnp.where(u["incl"], jnp.exp(gcol_sbs - grow), 0.0)
        kb_st = u["k"] * beta_st
        eg = jnp.exp(gcol_st)
        u["kq"] = jnp.concatenate([side_by_side_blocks(kb_st), side_by_side_blocks(u["q"])],
                                  axis=0).astype(BF16)
        u["kbd"] = block_diag_blocks(u["k"].astype(BF16))
        u["rhs"] = jnp.concatenate([v_st * beta_st, kb_st * eg], axis=1).astype(BF16)
        u["qdec"] = u["q"] * eg
        u["kdec"] = (u["k"] * jnp.exp(glast_st - gcol_st)).astype(BF16)
        u["gl"] = jnp.exp(glast_st)

    eye_sbs = (row_sbs == col_sbs).astype(F32)

    def prep_stages(units):
        for u in units:
            fill_operands(u)
        yield
        for n, u in enumerate(units):
            u["kk"] = _dot_nt(u["kq"], u["kbd"])
            if n == len(units) // 2 - 1:
                yield
        yield
        for u in units:
            strict = u["incl"] & (row_sbs != col_sbs)
            a_sbs = jnp.where(strict, u["kk"][0:C] * u["decay"], 0.0)
            u["attn"] = (u["kk"][C:2 * C] * u["decay"]).astype(BF16)
            u["pinv"] = eye_sbs - a_sbs
            u["a_sbs"] = a_sbs.astype(BF16)
        for u in units:
            u["apow"] = _dot(u["a_sbs"], block_diag(u["a_sbs"]))
        yield
        for _ in range(4):
            for n, u in enumerate(units):
                ap = u["apow"].astype(BF16)
                u["both"] = _dot(jnp.concatenate([u["pinv"].astype(BF16), ap], axis=0), block_diag(ap))
                if n == len(units) // 2 - 1:
                    yield
            yield
            for u in units:
                u["pinv"] = u["pinv"] + u["both"][0:C]
                u["apow"] = u["both"][C:2 * C]
        for u in units:
            u["last"] = _dot(u["pinv"].astype(BF16), block_diag(u["apow"].astype(BF16)))
        yield
        for u in units:
            u["uw"] = _dot(block_diag((u["pinv"] + u["last"]).astype(BF16)), u["rhs"])
        yield

    def scan_stages(units):
        for i in range(0, len(units), 2):
            pair = units[i:i + 2]
            for u in pair:
                d = u["d"]
                u["wq"] = []
                for h in range(H):
                    hr = slice(h * C, (h + 1) * C)
                    lhs = jnp.concatenate([u["uw"][hr, 128:256], u["qdec"][hr]], axis=0).astype(BF16)
                    u["wq"].append(_dot(lhs, s_scr[d * H + h].astype(BF16)))
            yield
            for u in pair:
                u["vnew"] = [(u["uw"][h * C:(h + 1) * C, 0:128] - u["wq"][h][0:C]).astype(BF16)
                             for h in range(H)]
                vnew_bd = block_diag_blocks(jnp.concatenate(u["vnew"], axis=0))
                u["o"] = jnp.concatenate([w[C:2 * C] for w in u["wq"]], axis=1) + _dot(u["attn"], vnew_bd)
            yield
            for u in pair:
                d = u["d"]
                for h in range(H):
                    hr = slice(h * C, (h + 1) * C)
                    s_scr[d * H + h] = (s_scr[d * H + h] * u["gl"][h * C:h * C + 1, :]
                                        + _dot_tn(u["kdec"][hr], u["vnew"][h]))
                u["o_ref"][u["rows"], :] = u["o"]
            yield

    def run_together(*stage_generators):
        live = list(stage_generators)
        while live:
            for gen in list(live):
                if next(gen, StopIteration) is StopIteration:
                    live.remove(gen)

    batches = [[dict(step=step, d=d) for step in range(s0, s0 + DN_SCAN_LAG_STEPS) for d in range(2)]
               for s0 in range(0, DN_GROUP, DN_SCAN_LAG_STEPS)]
    run_together(prep_stages(batches[0]))
    for prev, cur in zip(batches[:-1], batches[1:]):
        run_together(prep_stages(cur), scan_stages(prev))
    run_together(scan_stages(batches[-1]))


def _dn_scan(qkvn, pg, a_log, dt_bias, batch, seq_len):
    T = qkvn.shape[0]
    C, H, G = DN_CHUNK, DN_HEADS, DN_GROUP
    R = G * C
    nc = seq_len // C
    ncg = nc // G
    ab = pg[:, 0:4 * H].reshape(batch, nc, C, 2, 2, H)
    ab_row = jnp.transpose(ab, (0, 1, 3, 4, 5, 2)).reshape(batch * nc, 4, H * C)
    ab_row = jnp.pad(ab_row, ((0, 0), (0, 4), (0, 0)))
    pad_lanes = lambda v: jnp.pad(v.reshape(1, 2 * H), ((0, 0), (0, GATE_LANES - 2 * H)))
    row_param = lambda v: jnp.pad(jnp.repeat(v, C, axis=1), ((0, 6), (0, 0)))

    fwd = lambda b, c: (b * ncg + c, 0)
    bwd = lambda b, c: (b * ncg + ncg - 1 - c, 0)
    fwd3 = lambda b, c: (b * ncg + c, 0, 0)
    bwd3 = lambda b, c: (b * ncg + ncg - 1 - c, 0, 0)
    const = lambda shape: pl.BlockSpec(shape, lambda b, c: (0,) * len(shape))
    return pl.pallas_call(
        _dn_scan_kernel,
        grid=(batch, ncg),
        in_specs=[
            pl.BlockSpec((R, 3 * GROUP), fwd), pl.BlockSpec((R, 3 * GROUP), bwd),
            pl.BlockSpec((R, GATE_LANES), fwd), pl.BlockSpec((R, GATE_LANES), bwd),
            pl.BlockSpec((G, 8, H * C), fwd3), pl.BlockSpec((G, 8, H * C), bwd3),
            const((1, GATE_LANES)), const((1, GATE_LANES)), const((8, H * C)), const((8, H * C)),
        ],
        out_specs=[pl.BlockSpec((R, GROUP), fwd), pl.BlockSpec((R, GROUP), bwd)],
        out_shape=[jax.ShapeDtypeStruct((T, GROUP), F32), jax.ShapeDtypeStruct((T, GROUP), F32)],
        scratch_shapes=[pltpu.VMEM((2 * H, DN_HEAD_DIM, DN_HEAD_DIM), F32)],
        compiler_params=_cparams("arbitrary", "arbitrary"),
        name="deltanet_scan",
    )(qkvn, qkvn, pg, pg, ab_row, ab_row, pad_lanes(a_log), pad_lanes(dt_bias),
      row_param(a_log), row_param(dt_bias))


def _dn_post_kernel(of_ref, ob_ref, z_ref, g_ref, y_ref):
    for h in range(DN_HEADS):
        lanes = slice(h * 128, (h + 1) * 128)
        o = of_ref[:, lanes] + ob_ref[:, lanes]
        o = o * lax.rsqrt(jnp.mean(o * o, axis=-1, keepdims=True) + EPS) * g_ref[...]
        y_ref[:, lanes] = (o * _silu(z_ref[:, lanes])).astype(BF16)


def _dn_post(o_f, o_b, p, norm_g, tm=1024):
    T = p.shape[0]
    row = lambda cb: pl.BlockSpec((tm, GROUP), lambda i: (i, cb))
    return pl.pallas_call(
        _dn_post_kernel,
        grid=(T // tm,),
        in_specs=[row(0), row(0), row(9), pl.BlockSpec((1, DN_HEAD_DIM), lambda i: (0, 0))],
        out_specs=row(0),
        out_shape=jax.ShapeDtypeStruct((T, GROUP), BF16),
        compiler_params=_cparams("parallel"),
        name="deltanet_post",
    )(o_f, o_b, p, norm_g)


def _deltanet_mixer(p, pg, conv_w, a_log, dt_bias, norm_g, batch, seq_len):
    qkvn = _dn_prep(p, conv_w, seq_len)
    o_f, o_b = _dn_scan(qkvn, pg, a_log, dt_bias, batch, seq_len)
    return _dn_post(o_f, o_b, p, norm_g[None])


def _outproj_kernel(x_ref, ya_ref, yb_ref, yc_ref, yd_ref, w_ref, g_ref, xn_ref, hn_ref, ycat_scr):
    for gi, y_ref in enumerate((ya_ref, yb_ref, yc_ref, yd_ref)):
        ycat_scr[:, gi * GROUP:(gi + 1) * GROUP] = y_ref[...]
    acc = x_ref[...] + _dot(ycat_scr[...], w_ref[...])
    xn_ref[...] = acc
    ms = jnp.mean(acc * acc, axis=-1, keepdims=True)
    hn_ref[...] = (acc * lax.rsqrt(ms + EPS) * g_ref[...]).astype(BF16)


def _out_proj(x2, ys, w_out, norm2_g, tm=512):
    T = x2.shape[0]
    row = lambda width: pl.BlockSpec((tm, width), lambda i: (i, 0))
    return pl.pallas_call(
        _outproj_kernel,
        grid=(T // tm,),
        in_specs=[row(D_MODEL), row(GROUP), row(GROUP), row(GROUP), row(GROUP),
                  pl.BlockSpec((D_MODEL, D_MODEL), lambda i: (0, 0), pipeline_mode=pl.Buffered(1)),
                  pl.BlockSpec((1, D_MODEL), lambda i: (0, 0))],
        out_specs=[row(D_MODEL), row(D_MODEL)],
        out_shape=[jax.ShapeDtypeStruct((T, D_MODEL), F32),
                   jax.ShapeDtypeStruct((T, D_MODEL), BF16)],
        scratch_shapes=[pltpu.VMEM((tm, D_MODEL), BF16)],
        compiler_params=_cparams("parallel"),
        name="out_proj",
    )(x2, *ys, w_out, norm2_g)


FFN_HALF = FFN_HIDDEN // 2
FFN_COL_CHUNK = 256


def _ffn_hidden_kernel(hn_ref, hp_ref, hx_ref, wg_ref, wu_ref, dw_ref, db_ref, a_ref, hext_scr, g_scr,
                       *, tm, seq_len):
    first, last, _ = _seq_edges(pl.program_id(1), seq_len // tm)
    hext_scr[0:16, :] = jnp.where(first, jnp.zeros_like(hp_ref[...]), hp_ref[...])
    hext_scr[16:16 + tm, :] = hn_ref[...]
    hext_scr[16 + tm:32 + tm, :] = jnp.where(last, jnp.zeros_like(hx_ref[...]), hx_ref[...])
    for c in range(FFN_HALF // FFN_COL_CHUNK):
        cols = slice(c * FFN_COL_CHUNK, (c + 1) * FFN_COL_CHUNK)
        g = g_scr.at[c % 2]
        g[...] = _dot(hext_scr[...], wg_ref[:, cols])
        up = _dot(hn_ref[...], wu_ref[:, cols])
        gate = (g[pl.ds(15, tm), :] * dw_ref[0:1, cols] + g[pl.ds(16, tm), :] * dw_ref[1:2, cols]
                + g[pl.ds(17, tm), :] * dw_ref[2:3, cols] + db_ref[:, cols])
        a_ref[:, cols] = (_silu(gate) * up).astype(BF16)


def _ffn_down_kernel(a_ref, xn_ref, wd_ref, o_ref):
    o_ref[...] = xn_ref[...] + _dot(a_ref[...], wd_ref[...])


def _ffn(hn, xn, w_gate, dw_w, dw_b, w_up, w_down, seq_len, tm_hidden=1024, tm_down=512):
    T = hn.shape[0]
    n_tiles = T // tm_hidden
    r = tm_hidden // 16
    half_cols = lambda rows: pl.BlockSpec((rows, FFN_HALF), lambda h, i: (0, h),
                                          pipeline_mode=pl.Buffered(1))
    act = pl.pallas_call(
        functools.partial(_ffn_hidden_kernel, tm=tm_hidden, seq_len=seq_len),
        grid=(2, n_tiles),
        in_specs=[
            pl.BlockSpec((tm_hidden, D_MODEL), lambda h, i: (i, 0)),
            pl.BlockSpec((16, D_MODEL), lambda h, i: (jnp.maximum(i * r - 1, 0), 0)),
            pl.BlockSpec((16, D_MODEL), lambda h, i: (jnp.minimum((i + 1) * r, n_tiles * r - 1), 0)),
            half_cols(D_MODEL), half_cols(D_MODEL), half_cols(3), half_cols(1),
        ],
        out_specs=pl.BlockSpec((tm_hidden, FFN_HALF), lambda h, i: (i, h)),
        out_shape=jax.ShapeDtypeStruct((T, FFN_HIDDEN), BF16),
        scratch_shapes=[pltpu.VMEM((tm_hidden + 32, D_MODEL), BF16),
                        pltpu.VMEM((2, tm_hidden + 32, FFN_COL_CHUNK), F32)],
        compiler_params=_cparams("arbitrary", "arbitrary"),
        name="ffn_hidden",
    )(hn, hn, hn, w_gate, w_up, dw_w, dw_b)
    return pl.pallas_call(
        _ffn_down_kernel,
        grid=(T // tm_down,),
        in_specs=[
            pl.BlockSpec((tm_down, FFN_HIDDEN), lambda i: (i, 0)),
            pl.BlockSpec((tm_down, D_MODEL), lambda i: (i, 0)),
            pl.BlockSpec((FFN_HIDDEN, D_MODEL), lambda i: (0, 0), pipeline_mode=pl.Buffered(1)),
        ],
        out_specs=pl.BlockSpec((tm_down, D_MODEL), lambda i: (i, 0)),
        out_shape=jax.ShapeDtypeStruct((T, D_MODEL), F32),
        compiler_params=_cparams("parallel"),
        name="ffn_down",
    )(act, xn, w_down)


def _layer(x2, l, batch, seq_len, rel_bias, norm1_g, w_in, w_pool, pool_scale, att_q_g, att_k_g,
           conv_dw_w, conv_dw_b, conv_ln_g, conv_ln_b, conv_pw, dn_conv_w, dn_a_log, dn_dt_bias,
           dn_norm_g, w_out, norm2_g, ffn_w_gate, ffn_dw_w, ffn_dw_b, ffn_w_up, ffn_w_down):
    w_main = _weight_bf16(w_in, l, cols=PROJ_MAIN)
    w_gates = _gate_weight_bf16(w_in, l)
    p, pg = _in_proj(x2, norm1_g[l][None], w_main, w_gates)
    ya = _pool_mixer(p, w_pool[l].astype(BF16), pool_scale[l][None], seq_len)
    yb = _attention_mixer(p, att_q_g[l], att_k_g[l], rel_bias, seq_len)
    yc = _conformer_mixer(p, conv_dw_w[l], conv_dw_b[l][None], conv_ln_g[l][None], conv_ln_b[l][None],
                          conv_pw[l].astype(BF16), seq_len)
    yd = _deltanet_mixer(p, pg, dn_conv_w[l], dn_a_log[l], dn_dt_bias[l], dn_norm_g[l], batch, seq_len)
    xn, hn = _out_proj(x2, (ya, yb, yc, yd), _weight_bf16(w_out, l), norm2_g[l][None])
    return _ffn(hn, xn, _weight_bf16(ffn_w_gate, l), ffn_dw_w[l], ffn_dw_b[l][None],
                _weight_bf16(ffn_w_up, l), _weight_bf16(ffn_w_down, l), seq_len)


def kernel(x, rel_bias, norm1_g, w_in, w_pool, pool_scale, att_q_g, att_k_g, conv_dw_w, conv_dw_b,
           conv_ln_g, conv_ln_b, conv_pw, dn_conv_w, dn_a_log, dn_dt_bias, dn_norm_g, w_out, norm2_g,
           ffn_w_gate, ffn_dw_w, ffn_dw_b, ffn_w_up, ffn_w_down):
    batch, seq_len, _ = x.shape
    x2 = x.reshape(batch * seq_len, D_MODEL)
    for l in range(norm1_g.shape[0]):
        x2 = _layer(x2, l, batch, seq_len, rel_bias, norm1_g, w_in, w_pool, pool_scale, att_q_g,
                    att_k_g, conv_dw_w, conv_dw_b, conv_ln_g, conv_ln_b, conv_pw, dn_conv_w, dn_a_log,
                    dn_dt_bias, dn_norm_g, w_out, norm2_g, ffn_w_gate, ffn_dw_w, ffn_dw_b, ffn_w_up,
                    ffn_w_down)
    return x2.reshape(batch, seq_len, D_MODEL)
```

```python
import functools
import math

import jax
import jax.numpy as jnp
import numpy as np
from jax import lax
from jax.experimental import pallas as pl
from jax.experimental.pallas import tpu as pltpu

F32 = jnp.float32
BF16 = jnp.bfloat16

D_MODEL = 2048
GROUP = 512
POOL_WINDOWS = (2, 4, 8, 16)
ATT_HEAD_DIM = 64
ATT_HEADS = 8
ATT_RADIUS = 64
ATT_DILATIONS = (1, 4, 16)
REL_BUCKETS = 32
REL_MAX_DIST = 1024
CONV_WIDTH = 31
DN_HEAD_DIM = 128
DN_HEADS = 4
DN_CHUNK = 64
FFN_HIDDEN = 5632
EPS = 1e-6
NEG_INF = -1e30

PROJ_MAIN = 10 * GROUP
GATE_LANES = 128

VMEM_LIMIT_BYTES = 56 * 1024 * 1024


def _cparams(*sem):
    return pltpu.CompilerParams(dimension_semantics=sem, vmem_limit_bytes=VMEM_LIMIT_BYTES)


def _sigmoid(x):
    return 1.0 / (1.0 + jnp.exp(-x))


def _silu(x):
    return x * _sigmoid(x)


def _dot(a, b):
    return jnp.dot(a, b, preferred_element_type=F32)


def _dot_nt(a, b):
    return lax.dot_general(a, b, (((1,), (1,)), ((), ())), preferred_element_type=F32)


def _dot_tn(a, b):
    return lax.dot_general(a, b, (((0,), (0,)), ((), ())), preferred_element_type=F32)


def _cast_kernel(w_ref, o_ref):
    o_ref[...] = w_ref[...].astype(BF16)


def _weight_bf16(w_stack, layer, cols=None, tr=512):
    _, rows, width = w_stack.shape
    cols = width if cols is None else cols
    tc = cols // 2 if cols >= 4096 else cols
    return pl.pallas_call(
        _cast_kernel,
        grid=(rows // tr, cols // tc),
        in_specs=[pl.BlockSpec((None, tr, tc), lambda i, j: (layer, i, j))],
        out_specs=pl.BlockSpec((tr, tc), lambda i, j: (i, j)),
        out_shape=jax.ShapeDtypeStruct((rows, cols), BF16),
        compiler_params=_cparams("parallel", "parallel"),
        name="weight_bf16",
    )(w_stack)


def _gate_cols_kernel(w_ref, o_ref, *, valid):
    lane = lax.broadcasted_iota(jnp.int32, o_ref.shape, 1)
    o_ref[...] = jnp.where(lane < valid, w_ref[...], 0.0).astype(BF16)


def _gate_weight_bf16(w_stack, layer):
    _, rows, width = w_stack.shape
    valid = width - PROJ_MAIN
    return pl.pallas_call(
        functools.partial(_gate_cols_kernel, valid=valid),
        grid=(1,),
        in_specs=[pl.BlockSpec((None, rows, GATE_LANES), lambda i: (layer, 0, PROJ_MAIN // GATE_LANES))],
        out_specs=pl.BlockSpec((rows, GATE_LANES), lambda i: (0, 0)),
        out_shape=jax.ShapeDtypeStruct((rows, GATE_LANES), BF16),
        compiler_params=_cparams("arbitrary"),
        name="gate_weight_bf16",
    )(w_stack)


def _inproj_kernel(x_ref, g_ref, w_ref, wg_ref, p_ref, pg_ref):
    x = x_ref[...]
    ms = jnp.mean(x * x, axis=-1, keepdims=True)
    h = (x * lax.rsqrt(ms + EPS) * g_ref[...]).astype(BF16)
    pg_ref[...] = _dot(h, wg_ref[...])
    p_ref[...] = _dot(h, w_ref[...])


def _in_proj(x2, norm_g, w_main, w_gate, tm=512):
    T = x2.shape[0]
    resident = lambda shape: pl.BlockSpec(shape, lambda i: (0, 0), pipeline_mode=pl.Buffered(1))
    return pl.pallas_call(
        _inproj_kernel,
        grid=(T // tm,),
        in_specs=[
            pl.BlockSpec((tm, D_MODEL), lambda i: (i, 0)),
            resident((1, D_MODEL)),
            resident((D_MODEL, PROJ_MAIN)),
            resident((D_MODEL, GATE_LANES)),
        ],
        out_specs=[
            pl.BlockSpec((tm, PROJ_MAIN), lambda i: (i, 0)),
            pl.BlockSpec((tm, GATE_LANES), lambda i: (i, 0)),
        ],
        out_shape=[
            jax.ShapeDtypeStruct((T, PROJ_MAIN), F32),
            jax.ShapeDtypeStruct((T, GATE_LANES), F32),
        ],
        compiler_params=_cparams("parallel"),
        name="in_proj",
    )(x2, norm_g, w_main, w_gate)


def _halo_specs(tm, halo, width, col_block, n_tiles):
    r = tm // halo
    last = n_tiles * r - 1
    prev_spec = pl.BlockSpec((halo, width), lambda i, *_: (jnp.maximum(i * r - 1, 0), col_block))
    next_spec = pl.BlockSpec((halo, width), lambda i, *_: (jnp.minimum((i + 1) * r, last), col_block))
    return prev_spec, next_spec


def _seq_edges(i, tiles_per_seq):
    k = lax.rem(i, tiles_per_seq)
    return k == 0, k == tiles_per_seq - 1, k


def _pool_kernel(u_ref, up_ref, un_ref, w_ref, sc_ref, y_ref, ext_scr, *, tm, seq_len):
    i = pl.program_id(0)
    first, last, k = _seq_edges(i, seq_len // tm)
    ext_scr[0:8, :] = jnp.where(first, jnp.zeros_like(up_ref[...]), up_ref[...])
    ext_scr[8:8 + tm, :] = u_ref[...]
    ext_scr[8 + tm:16 + tm, :] = jnp.where(last, jnp.zeros_like(un_ref[...]), un_ref[...])
    t = k * tm + lax.broadcasted_iota(jnp.int32, (tm, 1), 0)
    for gi, win in enumerate(POOL_WINDOWS):
        half = win // 2
        lanes = slice(gi * 128, (gi + 1) * 128)
        s = ext_scr[pl.ds(8 - half, tm), lanes]
        for kk in range(1, win):
            s = s + ext_scr[pl.ds(8 - half + kk, tm), lanes]
        cnt = (jnp.minimum(t + half, seq_len) - jnp.maximum(t - half, 0)).astype(F32)
        pooled = s / cnt - ext_scr[pl.ds(8, tm), lanes]
        y = _dot(pooled.astype(BF16), w_ref[gi]) * sc_ref[:, lanes]
        y_ref[:, lanes] = y.astype(BF16)


def _pool_mixer(p, w_pool, pool_scale, seq_len, tm=2048):
    T = p.shape[0]
    n_tiles = T // tm
    prev_spec, next_spec = _halo_specs(tm, 8, GROUP, 0, n_tiles)
    return pl.pallas_call(
        functools.partial(_pool_kernel, tm=tm, seq_len=seq_len),
        grid=(n_tiles,),
        in_specs=[
            pl.BlockSpec((tm, GROUP), lambda i: (i, 0)),
            prev_spec,
            next_spec,
            pl.BlockSpec((4, 128, 128), lambda i: (0, 0, 0)),
            pl.BlockSpec((1, GROUP), lambda i: (0, 0)),
        ],
        out_specs=pl.BlockSpec((tm, GROUP), lambda i: (i, 0)),
        out_shape=jax.ShapeDtypeStruct((T, GROUP), BF16),
        scratch_shapes=[pltpu.VMEM((tm + 16, GROUP), F32)],
        compiler_params=_cparams("parallel"),
        name="pool_mixer",
    )(p, p, p, w_pool, pool_scale)


def _conformer_kernel(v_ref, vp_ref, vn_ref, g_ref, gp_ref, gn_ref, dw_ref, db_ref, lg_ref, lb_ref,
                      pw_ref, y_ref, ext_scr, shift_scr, *, tm, seq_len):
    i = pl.program_id(0)
    first, last, _ = _seq_edges(i, seq_len // tm)
    hp = vp_ref[...] * _sigmoid(gp_ref[...])
    hn = vn_ref[...] * _sigmoid(gn_ref[...])
    ext_scr[0:16, :] = jnp.where(first, jnp.zeros_like(hp), hp)
    ext_scr[16:16 + tm, :] = v_ref[...] * _sigmoid(g_ref[...])
    ext_scr[16 + tm:32 + tm, :] = jnp.where(last, jnp.zeros_like(hn), hn)
    base = 16 - CONV_WIDTH // 2
    acc = db_ref[...]
    for b in range(8):
        taps = [(a, 8 * a + b - base) for a in range(5) if 0 <= 8 * a + b - base < CONV_WIDTH]
        rows = tm + 8 * taps[-1][0]
        shift_scr[b, 0:rows, :] = ext_scr[pl.ds(b, rows), :]
        for a, kk in taps:
            acc = acc + shift_scr[b, 8 * a:8 * a + tm, :] * dw_ref[kk:kk + 1, :]
    mu = jnp.mean(acc, axis=-1, keepdims=True)
    xc = acc - mu
    var = jnp.mean(xc * xc, axis=-1, keepdims=True)
    h = _silu(xc * lax.rsqrt(var + EPS) * lg_ref[...] + lb_ref[...])
    y_ref[...] = _dot(h.astype(BF16), pw_ref[...]).astype(BF16)


def _conformer_mixer(p, dw_w, dw_b, ln_g, ln_b, pw, seq_len, tm=1024):
    T = p.shape[0]
    n_tiles = T // tm
    vprev, vnext = _halo_specs(tm, 16, GROUP, 4, n_tiles)
    gprev, gnext = _halo_specs(tm, 16, GROUP, 5, n_tiles)
    const = lambda shape: pl.BlockSpec(shape, lambda i: (0,) * len(shape))
    return pl.pallas_call(
        functools.partial(_conformer_kernel, tm=tm, seq_len=seq_len),
        grid=(n_tiles,),
        in_specs=[
            pl.BlockSpec((tm, GROUP), lambda i: (i, 4)), vprev, vnext,
            pl.BlockSpec((tm, GROUP), lambda i: (i, 5)), gprev, gnext,
            const((CONV_WIDTH, GROUP)), const((1, GROUP)), const((1, GROUP)), const((1, GROUP)),
            const((GROUP, GROUP)),
        ],
        out_specs=pl.BlockSpec((tm, GROUP), lambda i: (i, 0)),
        out_shape=jax.ShapeDtypeStruct((T, GROUP), BF16),
        scratch_shapes=[pltpu.VMEM((tm + 32, GROUP), F32), pltpu.VMEM((8, tm + 32, GROUP), F32)],
        compiler_params=_cparams("parallel"),
        name="conformer_mixer",
    )(p, p, p, p, p, p, dw_w, dw_b, ln_g, ln_b, pw)


ATT_TILE = 2048
LOG2E = math.log2(math.e)
ATT_PIPELINE_DEPTH = 3


def _t5_bucket_table():
    nb = REL_BUCKETS // 2
    max_exact = nb // 2
    i = np.arange(ATT_RADIUS)[:, None]
    j = np.arange(3 * ATT_RADIUS)[None, :]
    off = j - ATT_RADIUS - i
    tables = []
    for dil in ATT_DILATIONS:
        rel = off * dil
        n = np.abs(rel)
        nf = np.maximum(n, 1).astype(np.float32)
        large = max_exact + (np.log(nf / np.float32(max_exact)) / np.float32(math.log(REL_MAX_DIST / max_exact))
                             * np.float32(nb - max_exact)).astype(np.int32)
        large = np.minimum(large, nb - 1)
        bucket = np.where(rel > 0, nb, 0) + np.where(n < max_exact, n, large)
        tables.append(np.where(np.abs(off) <= ATT_RADIUS, bucket, -1))
    return np.stack(tables).astype(np.int32)


def _att_kernel(rb_ref, bkt_ref, qg_ref, kg_ref, mseg_ref, q_ref, kc_ref, kn_ref,
                vp_ref, vc_ref, vn_ref, y_ref, bias_scr, qbuf, kbuf, vbuf, acc_scr, m_scr, l_scr,
                *, seq_len):
    tile = ATT_TILE
    rad = ATT_RADIUS
    hp = pl.program_id(0)
    i = pl.program_id(1)

    def rms(x, g):
        ms = _dot((x * x).astype(BF16), mseg_ref[...])
        return x * lax.rsqrt(ms + EPS) * g

    @pl.when(i == 0)
    def _():
        lane_head0 = lax.broadcasted_iota(jnp.int32, (1, 128), 1) < ATT_HEAD_DIM
        for di in range(len(ATT_DILATIONS)):
            bkt = bkt_ref[di]
            b = jnp.zeros(bkt.shape, F32)
            for bb in range(REL_BUCKETS):
                b = jnp.where(bkt == bb, jnp.where(lane_head0, rb_ref[bb, 2 * hp], rb_ref[bb, 2 * hp + 1]), b)
            bias_scr[di] = jnp.where(bkt < 0, NEG_INF, b * LOG2E)
        kbuf[0:tile, :] = jnp.zeros((tile, 128), F32)
        kbuf[tile:2 * tile, :] = rms(kc_ref[...], kg_ref[...])

    @pl.when(i > 0)
    def _():
        kbuf[0:tile, :] = kbuf[tile:2 * tile, :]
        kbuf[tile:2 * tile, :] = kbuf[2 * tile:3 * tile, :]

    first, last, _ = _seq_edges(i, seq_len // tile)

    qbuf[...] = rms(q_ref[...], qg_ref[...]) * (ATT_HEAD_DIM ** -0.5 * LOG2E)
    kbuf[2 * tile:3 * tile, :] = rms(kn_ref[...], kg_ref[...])
    vbuf[0:tile, :] = vp_ref[...]
    vbuf[tile:2 * tile, :] = vc_ref[...]
    vbuf[2 * tile:3 * tile, :] = vn_ref[...]

    head0 = lax.broadcasted_iota(jnp.int32, (1, 128), 1) < ATT_HEAD_DIM
    key_row = lax.broadcasted_iota(jnp.int32, (3 * rad, 1), 0)
    ones_cols = jnp.ones((3 * rad, 128), BF16)

    def rows(start, size, dil):
        return pl.ds(start, size) if dil == 1 else pl.ds(start, size, stride=dil)

    blocks = [(di, dil, r + rad * dil * m)
              for di, dil in enumerate(ATT_DILATIONS) for r in range(dil) for m in range(tile // (rad * dil))]
    def score_stage(blk):
        di, dil, qstart = blk
        kstart = tile + qstart - rad * dil
        qb = qbuf[rows(qstart, rad, dil), :]
        kb = kbuf[rows(kstart, 3 * rad, dil), :].astype(BF16)
        q2 = jnp.concatenate([jnp.where(head0, qb, 0.0), jnp.where(head0, 0.0, qb)], axis=0)
        return _dot_nt(kb, q2.astype(BF16))

    def softmax_stage(blk, s):
        di, dil, qstart = blk
        kstart = tile + qstart - rad * dil
        s = s + bias_scr[di]
        n_prev = max(0, -(-(tile - kstart) // dil))
        n_upto = min(3 * rad, -(-(2 * tile - kstart) // dil))
        if n_prev > 0:
            s = jnp.where(key_row < jnp.where(first, n_prev, 0), NEG_INF, s)
        if n_upto < 3 * rad:
            s = jnp.where(key_row >= jnp.where(last, n_upto, 3 * rad), NEG_INF, s)
        mx = jnp.max(s, axis=0, keepdims=True)
        return mx, jnp.exp2(s - mx).astype(BF16)

    def value_stage(blk, pe):
        di, dil, qstart = blk
        kstart = tile + qstart - rad * dil
        vb = vbuf[rows(kstart, 3 * rad, dil), :].astype(BF16)
        return _dot_tn(pe, jnp.concatenate([vb, ones_cols], axis=1))

    def store_stage(blk, mx, ov):
        di, dil, qstart = blk
        dst = rows(qstart, rad, dil)
        m_col = jnp.broadcast_to(mx, (2 * rad, 128)).T
        acc_scr[di, dst, :] = jnp.where(head0, ov[0:rad, 0:128], ov[rad:2 * rad, 0:128])
        l_scr[di, dst, :] = jnp.where(head0, ov[0:rad, 128:256], ov[rad:2 * rad, 128:256])
        m_scr[di, dst, :] = jnp.where(head0, m_col[0:rad], m_col[rad:2 * rad])

    n_blk = len(blocks)
    scores = {b: score_stage(blocks[b]) for b in range(min(ATT_PIPELINE_DEPTH, n_blk))}
    pending = None
    for b in range(n_blk):
        mx, pe = softmax_stage(blocks[b], scores.pop(b))
        ov = value_stage(blocks[b], pe)
        if b + ATT_PIPELINE_DEPTH < n_blk:
            scores[b + ATT_PIPELINE_DEPTH] = score_stage(blocks[b + ATT_PIPELINE_DEPTH])
        if pending is not None:
            store_stage(*pending)
        pending = (blocks[b], mx, ov)
    store_stage(*pending)

    m_all = jnp.maximum(jnp.maximum(m_scr[0], m_scr[1]), m_scr[2])
    num = jnp.zeros((tile, 128), F32)
    den = jnp.zeros((tile, 128), F32)
    for di in range(len(ATT_DILATIONS)):
        e = jnp.exp2(m_scr[di] - m_all)
        num = num + acc_scr[di] * e
        den = den + l_scr[di] * e
    y_ref[...] = (num / den).astype(BF16)


def _attention_mixer(p, q_g, k_g, rel_bias, seq_len):
    T = p.shape[0]
    tile = ATT_TILE
    n_tiles = T // tile
    rad = ATT_RADIUS
    seg = np.kron(np.eye(2), np.full((ATT_HEAD_DIM, ATT_HEAD_DIM), 1.0 / ATT_HEAD_DIM))
    mseg = jnp.asarray(seg, BF16)
    bkt_t = np.transpose(_t5_bucket_table(), (0, 2, 1))
    bkt = jnp.asarray(np.concatenate([bkt_t, bkt_t], axis=2))
    qg2 = jnp.tile(q_g, 2)[None]
    kg2 = jnp.tile(k_g, 2)[None]

    def blk(col0, shift):
        return pl.BlockSpec((tile, 128),
                            lambda hp, i: (jnp.clip(i + shift, 0, n_tiles - 1), col0 * 4 + hp))

    const = lambda shape: pl.BlockSpec(shape, lambda hp, i: (0,) * len(shape))
    return pl.pallas_call(
        functools.partial(_att_kernel, seq_len=seq_len),
        grid=(4, n_tiles),
        in_specs=[
            pl.BlockSpec(memory_space=pltpu.SMEM),
            const((3, 3 * rad, 2 * rad)), const((1, 128)), const((1, 128)), const((128, 128)),
            blk(1, 0),
            blk(2, 0), blk(2, 1),
            blk(3, -1), blk(3, 0), blk(3, 1),
        ],
        out_specs=pl.BlockSpec((tile, 128), lambda hp, i: (i, hp)),
        out_shape=jax.ShapeDtypeStruct((T, GROUP), BF16),
        scratch_shapes=[
            pltpu.VMEM((3, 3 * rad, 2 * rad), F32),
            pltpu.VMEM((tile, 128), F32),
            pltpu.VMEM((3 * tile, 128), F32),
            pltpu.VMEM((3 * tile, 128), F32),
            pltpu.VMEM((3, tile, 128), F32),
            pltpu.VMEM((3, tile, 128), F32),
            pltpu.VMEM((3, tile, 128), F32),
        ],
        compiler_params=_cparams("arbitrary", "arbitrary"),
        name="dilated_attention",
    )(rel_bias, bkt, qg2, kg2, mseg, p, p, p, p, p, p)


def _softplus(x):
    return jnp.maximum(x, 0.0) + jnp.log1p(jnp.exp(-jnp.abs(x)))


def _dn_prep_kernel(x_ref, xp_ref, xn_ref, w_ref, o_ref, ext_scr, *, tm, seq_len):
    i = pl.program_id(0)
    first, last, _ = _seq_edges(i, seq_len // tm)
    ext_scr[0:8, :] = jnp.where(first, jnp.zeros_like(xp_ref[...]), xp_ref[...])
    ext_scr[8:8 + tm, :] = x_ref[...]
    ext_scr[8 + tm:16 + tm, :] = jnp.where(last, jnp.zeros_like(xn_ref[...]), xn_ref[...])
    for cb in range(3 * DN_HEADS):
        lanes = slice(cb * 128, (cb + 1) * 128)
        acc = ext_scr[pl.ds(6, tm), lanes] * w_ref[0:1, lanes]
        for kk in range(1, 4):
            acc = acc + ext_scr[pl.ds(6 + kk, tm), lanes] * w_ref[kk:kk + 1, lanes]
        y = _silu(acc)
        if cb < 2 * DN_HEADS:
            y = y * lax.rsqrt(jnp.sum(y * y, axis=-1, keepdims=True) + EPS)
        if cb < DN_HEADS:
            y = y * (DN_HEAD_DIM ** -0.5)
        o_ref[:, lanes] = y


def _dn_prep(p, conv_w, seq_len, tm=512):
    T = p.shape[0]
    n_tiles = T // tm
    width = 3 * GROUP
    prev_spec, next_spec = _halo_specs(tm, 8, width, 2, n_tiles)
    return pl.pallas_call(
        functools.partial(_dn_prep_kernel, tm=tm, seq_len=seq_len),
        grid=(n_tiles,),
        in_specs=[pl.BlockSpec((tm, width), lambda i: (i, 2)), prev_spec, next_spec,
                  pl.BlockSpec((4, width), lambda i: (0, 0))],
        out_specs=pl.BlockSpec((tm, width), lambda i: (i, 0)),
        out_shape=jax.ShapeDtypeStruct((T, width), F32),
        scratch_shapes=[pltpu.VMEM((tm + 16, width), F32)],
        compiler_params=_cparams("parallel"),
        name="deltanet_prep",
    )(p, p, p, conv_w)


DN_GROUP = 16
DN_ROWS = DN_HEADS * DN_CHUNK
DN_SCAN_LAG_STEPS = 4


def _dn_scan_kernel(xf_ref, xb_ref, gf_ref, gb_ref, rf_ref, rb_ref, alc_ref, dtc_ref, alr_ref, dtr_ref,
                    of_ref, ob_ref, s_scr):
    C = DN_CHUNK
    H = DN_HEADS
    R = DN_GROUP * C

    @pl.when(pl.program_id(1) == 0)
    def _():
        s_scr[...] = jnp.zeros_like(s_scr)

    row_in_chunk = lax.rem(lax.broadcasted_iota(jnp.int32, (R, 1), 0), C)
    lane_in_chunk = lax.rem(lax.broadcasted_iota(jnp.int32, (1, DN_ROWS), 1), C)
    rid = lax.broadcasted_iota(jnp.int32, (DN_ROWS, DN_ROWS), 0)
    cid = lax.broadcasted_iota(jnp.int32, (DN_ROWS, DN_ROWS), 1)
    same_head = (rid // C) == (cid // C)
    row_sbs = lax.broadcasted_iota(jnp.int32, (C, DN_ROWS), 0)
    col_sbs = lax.rem(lax.broadcasted_iota(jnp.int32, (C, DN_ROWS), 1), C)
    lane_head = lax.broadcasted_iota(jnp.int32, (1, DN_ROWS), 1) // C

    def block_diag(m):
        return jnp.where(same_head, jnp.concatenate([m] * H, axis=0), jnp.zeros((), m.dtype))

    def side_by_side_blocks(x):
        return jnp.concatenate([x[h * C:(h + 1) * C] for h in range(H)], axis=1)

    def block_diag_blocks(x):
        zero = jnp.zeros((C, 128), x.dtype)
        return jnp.concatenate(
            [jnp.concatenate([x[h * C:(h + 1) * C] if g == h else zero for g in range(H)], axis=1)
             for h in range(H)], axis=0)

    def stack_heads(x, col0):
        return jnp.concatenate([x[:, col0 + h * 128:col0 + (h + 1) * 128] for h in range(H)], axis=0)

    def stack_cols(x, lane0, rows=None):
        parts = []
        for h in range(H):
            c = x[:, lane0 + h:lane0 + h + 1]
            if rows is not None:
                c = c[rows:rows + 1, :]
            parts.append(jnp.broadcast_to(c, (C, 128)))
        return jnp.concatenate(parts, axis=0)

    refs = ((xf_ref, gf_ref, rf_ref, of_ref), (xb_ref, gb_ref, rb_ref, ob_ref))
    gates = []
    for d, (x_ref, gc_ref, gr_ref, o_ref) in enumerate(refs):
        reverse = d == 1
        gcol = gc_ref[...]
        g = -jnp.exp(alc_ref[...]) * _softplus(gcol + dtc_ref[...])
        beta = _sigmoid(gcol)
        gcum = g
        for sh in (1, 2, 4, 8, 16, 32):
            if reverse:
                gcum = gcum + jnp.where(row_in_chunk < C - sh, pltpu.roll(gcum, R - sh, 0), 0.0)
            else:
                gcum = gcum + jnp.where(row_in_chunk >= sh, pltpu.roll(gcum, sh, 0), 0.0)
        grow_all = -jnp.exp(alr_ref[...])[None] * _softplus(gr_ref[...] + dtr_ref[...][None])
        for sh in (1, 2, 4, 8, 16, 32):
            if reverse:
                grow_all = grow_all + jnp.where(lane_in_chunk < C - sh,
                                                pltpu.roll(grow_all, DN_ROWS - sh, 2), 0.0)
            else:
                grow_all = grow_all + jnp.where(lane_in_chunk >= sh, pltpu.roll(grow_all, sh, 2), 0.0)

        gates.append((gcum, beta, grow_all))

    def fill_operands(u):
        d = u["d"]
        x_ref, _, _, o_ref = refs[d]
        reverse = d == 1
        ci = DN_GROUP - 1 - u["step"] if reverse else u["step"]
        gcum, beta, grow_all = gates[d]
        rows = slice(ci * C, (ci + 1) * C)
        x = x_ref[rows, :]
        u.update(rows=rows, o_ref=o_ref)
        u["q"] = stack_heads(x, 0)
        u["k"] = stack_heads(x, GROUP)
        v_st = stack_heads(x, 2 * GROUP)
        gc_c = gcum[rows, :]
        beta_st = stack_cols(beta[rows, :], 8 + d * H)
        gcol_st = stack_cols(gc_c, d * H)
        glast_st = stack_cols(gc_c, d * H, rows=0 if reverse else C - 1)
        grow = grow_all[ci, d:d + 1, :]
        gcol_sbs = jnp.broadcast_to(gc_c[:, d * H + H - 1:d * H + H], (C, DN_ROWS))
        for h in range(H - 2, -1, -1):
            gcol_sbs = jnp.where(lane_head <= h, gc_c[:, d * H + h:d * H + h + 1], gcol_sbs)
        u["incl"] = (row_sbs <= col_sbs) if reverse else (row_sbs >= col_sbs)
        u["decay"] = jnp.where(u["incl"], jnp.exp(gcol_sbs - grow), 0.0)
        kb_st = u["k"] * beta_st
        eg = jnp.exp(gcol_st)
        u["kq"] = jnp.concatenate([side_by_side_blocks(kb_st), side_by_side_blocks(u["q"])],
                                  axis=0).astype(BF16)
        u["kbd"] = block_diag_blocks(u["k"].astype(BF16))
        u["rhs"] = jnp.concatenate([v_st * beta_st, kb_st * eg], axis=1).astype(BF16)
        u["qdec"] = u["q"] * eg
        u["kdec"] = (u["k"] * jnp.exp(glast_st - gcol_st)).astype(BF16)
        u["gl"] = jnp.exp(glast_st)

    eye_sbs = (row_sbs == col_sbs).astype(F32)

    def prep_stages(units):
        for u in units:
            fill_operands(u)
        yield
        for n, u in enumerate(units):
            u["kk"] = _dot_nt(u["kq"], u["kbd"])
            if n == len(units) // 2 - 1:
                yield
        yield
        for u in units:
            strict = u["incl"] & (row_sbs != col_sbs)
            a_sbs = jnp.where(strict, u["kk"][0:C] * u["decay"], 0.0)
            u["attn"] = (u["kk"][C:2 * C] * u["decay"]).astype(BF16)
            u["pinv"] = eye_sbs - a_sbs
            u["a_sbs"] = a_sbs.astype(BF16)
        for u in units:
            u["apow"] = _dot(u["a_sbs"], block_diag(u["a_sbs"]))
        yield
        for _ in range(4):
            for n, u in enumerate(units):
                ap = u["apow"].astype(BF16)
                u["both"] = _dot(jnp.concatenate([u["pinv"].astype(BF16), ap], axis=0), block_diag(ap))
                if n == len(units) // 2 - 1:
                    yield
            yield
            for u in units:
                u["pinv"] = u["pinv"] + u["both"][0:C]
                u["apow"] = u["both"][C:2 * C]
        for u in units:
            u["last"] = _dot(u["pinv"].astype(BF16), block_diag(u["apow"].astype(BF16)))
        yield
        for u in units:
            u["uw"] = _dot(block_diag((u["pinv"] + u["last"]).astype(BF16)), u["rhs"])
        yield

    def scan_stages(units):
        for i in range(0, len(units), 2):
            pair = units[i:i + 2]
            for u in pair:
                d = u["d"]
                u["wq"] = []
                for h in range(H):
                    hr = slice(h * C, (h + 1) * C)
                    lhs = jnp.concatenate([u["uw"][hr, 128:256], u["qdec"][hr]], axis=0).astype(BF16)
                    u["wq"].append(_dot(lhs, s_scr[d * H + h].astype(BF16)))
            yield
            for u in pair:
                u["vnew"] = [(u["uw"][h * C:(h + 1) * C, 0:128] - u["wq"][h][0:C]).astype(BF16)
                             for h in range(H)]
                vnew_bd = block_diag_blocks(jnp.concatenate(u["vnew"], axis=0))
                u["o"] = jnp.concatenate([w[C:2 * C] for w in u["wq"]], axis=1) + _dot(u["attn"], vnew_bd)
            yield
            for u in pair:
                d = u["d"]
                for h in range(H):
                    hr = slice(h * C, (h + 1) * C)
                    s_scr[d * H + h] = (s_scr[d * H + h] * u["gl"][h * C:h * C + 1, :]
                                        + _dot_tn(u["kdec"][hr], u["vnew"][h]))
                u["o_ref"][u["rows"], :] = u["o"]
            yield

    def run_together(*stage_generators):
        live = list(stage_generators)
        while live:
            for gen in list(live):
                if next(gen, StopIteration) is StopIteration:
                    live.remove(gen)

    batches = [[dict(step=step, d=d) for step in range(s0, s0 + DN_SCAN_LAG_STEPS) for d in range(2)]
               for s0 in range(0, DN_GROUP, DN_SCAN_LAG_STEPS)]
    run_together(prep_stages(batches[0]))
    for prev, cur in zip(batches[:-1], batches[1:]):
        run_together(prep_stages(cur), scan_stages(prev))
    run_together(scan_stages(batches[-1]))


def _dn_scan(qkvn, pg, a_log, dt_bias, batch, seq_len):
    T = qkvn.shape[0]
    C, H, G = DN_CHUNK, DN_HEADS, DN_GROUP
    R = G * C
    nc = seq_len // C
    ncg = nc // G
    ab = pg[:, 0:4 * H].reshape(batch, nc, C, 2, 2, H)
    ab_row = jnp.transpose(ab, (0, 1, 3, 4, 5, 2)).reshape(batch * nc, 4, H * C)
    ab_row = jnp.pad(ab_row, ((0, 0), (0, 4), (0, 0)))
    pad_lanes = lambda v: jnp.pad(v.reshape(1, 2 * H), ((0, 0), (0, GATE_LANES - 2 * H)))
    row_param = lambda v: jnp.pad(jnp.repeat(v, C, axis=1), ((0, 6), (0, 0)))

    fwd = lambda b, c: (b * ncg + c, 0)
    bwd = lambda b, c: (b * ncg + ncg - 1 - c, 0)
    fwd3 = lambda b, c: (b * ncg + c, 0, 0)
    bwd3 = lambda b, c: (b * ncg + ncg - 1 - c, 0, 0)
    const = lambda shape: pl.BlockSpec(shape, lambda b, c: (0,) * len(shape))
    return pl.pallas_call(
        _dn_scan_kernel,
        grid=(batch, ncg),
        in_specs=[
            pl.BlockSpec((R, 3 * GROUP), fwd), pl.BlockSpec((R, 3 * GROUP), bwd),
            pl.BlockSpec((R, GATE_LANES), fwd), pl.BlockSpec((R, GATE_LANES), bwd),
            pl.BlockSpec((G, 8, H * C), fwd3), pl.BlockSpec((G, 8, H * C), bwd3),
            const((1, GATE_LANES)), const((1, GATE_LANES)), const((8, H * C)), const((8, H * C)),
        ],
        out_specs=[pl.BlockSpec((R, GROUP), fwd), pl.BlockSpec((R, GROUP), bwd)],
        out_shape=[jax.ShapeDtypeStruct((T, GROUP), F32), jax.ShapeDtypeStruct((T, GROUP), F32)],
        scratch_shapes=[pltpu.VMEM((2 * H, DN_HEAD_DIM, DN_HEAD_DIM), F32)],
        compiler_params=_cparams("arbitrary", "arbitrary"),
        name="deltanet_scan",
    )(qkvn, qkvn, pg, pg, ab_row, ab_row, pad_lanes(a_log), pad_lanes(dt_bias),
      row_param(a_log), row_param(dt_bias))


def _dn_post_kernel(of_ref, ob_ref, z_ref, g_ref, y_ref):
    for h in range(DN_HEADS):
        lanes = slice(h * 128, (h + 1) * 128)
        o = of_ref[:, lanes] + ob_ref[:, lanes]
        o = o * lax.rsqrt(jnp.mean(o * o, axis=-1, keepdims=True) + EPS) * g_ref[...]
        y_ref[:, lanes] = (o * _silu(z_ref[:, lanes])).astype(BF16)


def _dn_post(o_f, o_b, p, norm_g, tm=2048):
    T = p.shape[0]
    row = lambda cb: pl.BlockSpec((tm, GROUP), lambda i: (i, cb))
    return pl.pallas_call(
        _dn_post_kernel,
        grid=(T // tm,),
        in_specs=[row(0), row(0), row(9), pl.BlockSpec((1, DN_HEAD_DIM), lambda i: (0, 0))],
        out_specs=row(0),
        out_shape=jax.ShapeDtypeStruct((T, GROUP), BF16),
        compiler_params=_cparams("parallel"),
        name="deltanet_post",
    )(o_f, o_b, p, norm_g)


def _deltanet_mixer(p, pg, conv_w, a_log, dt_bias, norm_g, batch, seq_len):
    qkvn = _dn_prep(p, conv_w, seq_len)
    o_f, o_b = _dn_scan(qkvn, pg, a_log, dt_bias, batch, seq_len)
    return _dn_post(o_f, o_b, p, norm_g[None])


def _outproj_kernel(x_ref, ya_ref, yb_ref, yc_ref, yd_ref, w_ref, g_ref, xn_ref, hn_ref, ycat_scr):
    for gi, y_ref in enumerate((ya_ref, yb_ref, yc_ref, yd_ref)):
        ycat_scr[:, gi * GROUP:(gi + 1) * GROUP] = y_ref[...]
    acc = x_ref[...] + _dot(ycat_scr[...], w_ref[...])
    xn_ref[...] = acc
    ms = jnp.mean(acc * acc, axis=-1, keepdims=True)
    hn_ref[...] = (acc * lax.rsqrt(ms + EPS) * g_ref[...]).astype(BF16)


def _out_proj(x2, ys, w_out, norm2_g, tm=512):
    T = x2.shape[0]
    row = lambda width: pl.BlockSpec((tm, width), lambda i: (i, 0))
    return pl.pallas_call(
        _outproj_kernel,
        grid=(T // tm,),
        in_specs=[row(D_MODEL), row(GROUP), row(GROUP), row(GROUP), row(GROUP),
                  pl.BlockSpec((D_MODEL, D_MODEL), lambda i: (0, 0), pipeline_mode=pl.Buffered(1)),
                  pl.BlockSpec((1, D_MODEL), lambda i: (0, 0))],
        out_specs=[row(D_MODEL), row(D_MODEL)],
        out_shape=[jax.ShapeDtypeStruct((T, D_MODEL), F32),
                   jax.ShapeDtypeStruct((T, D_MODEL), BF16)],
        scratch_shapes=[pltpu.VMEM((tm, D_MODEL), BF16)],
        compiler_params=_cparams("parallel"),
        name="out_proj",
    )(x2, *ys, w_out, norm2_g)


FFN_HALF = FFN_HIDDEN // 2
FFN_COL_CHUNK = 256


def _ffn_hidden_kernel(hn_ref, hp_ref, hx_ref, wg_ref, wu_ref, dw_ref, db_ref, a_ref, hext_scr, g_scr,
                       *, tm, seq_len):
    first, last, _ = _seq_edges(pl.program_id(1), seq_len // tm)
    hext_scr[0:16, :] = jnp.where(first, jnp.zeros_like(hp_ref[...]), hp_ref[...])
    hext_scr[16:16 + tm, :] = hn_ref[...]
    hext_scr[16 + tm:32 + tm, :] = jnp.where(last, jnp.zeros_like(hx_ref[...]), hx_ref[...])
    for c in range(FFN_HALF // FFN_COL_CHUNK):
        cols = slice(c * FFN_COL_CHUNK, (c + 1) * FFN_COL_CHUNK)
        g = g_scr.at[c % 2]
        g[...] = _dot(hext_scr[...], wg_ref[:, cols])
        up = _dot(hn_ref[...], wu_ref[:, cols])
        gate = (g[pl.ds(15, tm), :] * dw_ref[0:1, cols] + g[pl.ds(16, tm), :] * dw_ref[1:2, cols]
                + g[pl.ds(17, tm), :] * dw_ref[2:3, cols] + db_ref[:, cols])
        a_ref[:, cols] = (_silu(gate) * up).astype(BF16)


def _ffn_down_kernel(a_ref, xn_ref, wd_ref, o_ref):
    o_ref[...] = xn_ref[...] + _dot(a_ref[...], wd_ref[...])


def _ffn(hn, xn, w_gate, dw_w, dw_b, w_up, w_down, seq_len, tm_hidden=1024, tm_down=512):
    T = hn.shape[0]
    n_tiles = T // tm_hidden
    r = tm_hidden // 16
    half_cols = lambda rows: pl.BlockSpec((rows, FFN_HALF), lambda h, i: (0, h),
                                          pipeline_mode=pl.Buffered(1))
    act = pl.pallas_call(
        functools.partial(_ffn_hidden_kernel, tm=tm_hidden, seq_len=seq_len),
        grid=(2, n_tiles),
        in_specs=[
            pl.BlockSpec((tm_hidden, D_MODEL), lambda h, i: (i, 0)),
            pl.BlockSpec((16, D_MODEL), lambda h, i: (jnp.maximum(i * r - 1, 0), 0)),
            pl.BlockSpec((16, D_MODEL), lambda h, i: (jnp.minimum((i + 1) * r, n_tiles * r - 1), 0)),
            half_cols(D_MODEL), half_cols(D_MODEL), half_cols(3), half_cols(1),
        ],
        out_specs=pl.BlockSpec((tm_hidden, FFN_HALF), lambda h, i: (i, h)),
        out_shape=jax.ShapeDtypeStruct((T, FFN_HIDDEN), BF16),
        scratch_shapes=[pltpu.VMEM((tm_hidden + 32, D_MODEL), BF16),
                        pltpu.VMEM((2, tm_hidden + 32, FFN_COL_CHUNK), F32)],
        compiler_params=_cparams("arbitrary", "arbitrary"),
        name="ffn_hidden",
    )(hn, hn, hn, w_gate, w_up, dw_w, dw_b)
    return pl.pallas_call(
        _ffn_down_kernel,
        grid=(T // tm_down,),
        in_specs=[
            pl.BlockSpec((tm_down, FFN_HIDDEN), lambda i: (i, 0)),
            pl.BlockSpec((tm_down, D_MODEL), lambda i: (i, 0)),
            pl.BlockSpec((FFN_HIDDEN, D_MODEL), lambda i: (0, 0), pipeline_mode=pl.Buffered(1)),
        ],
        out_specs=pl.BlockSpec((tm_down, D_MODEL), lambda i: (i, 0)),
        out_shape=jax.ShapeDtypeStruct((T, D_MODEL), F32),
        compiler_params=_cparams("parallel"),
        name="ffn_down",
    )(act, xn, w_down)


def _layer(x2, l, batch, seq_len, rel_bias, norm1_g, w_in, w_pool, pool_scale, att_q_g, att_k_g,
           conv_dw_w, conv_dw_b, conv_ln_g, conv_ln_b, conv_pw, dn_conv_w, dn_a_log, dn_dt_bias,
           dn_norm_g, w_out, norm2_g, ffn_w_gate, ffn_dw_w, ffn_dw_b, ffn_w_up, ffn_w_down):
    w_main = _weight_bf16(w_in, l, cols=PROJ_MAIN)
    w_gates = _gate_weight_bf16(w_in, l)
    p, pg = _in_proj(x2, norm1_g[l][None], w_main, w_gates)
    ya = _pool_mixer(p, w_pool[l].astype(BF16), pool_scale[l][None], seq_len)
    yb = _attention_mixer(p, att_q_g[l], att_k_g[l], rel_bias, seq_len)
    yc = _conformer_mixer(p, conv_dw_w[l], conv_dw_b[l][None], conv_ln_g[l][None], conv_ln_b[l][None],
                          conv_pw[l].astype(BF16), seq_len)
    yd = _deltanet_mixer(p, pg, dn_conv_w[l], dn_a_log[l], dn_dt_bias[l], dn_norm_g[l], batch, seq_len)
    xn, hn = _out_proj(x2, (ya, yb, yc, yd), _weight_bf16(w_out, l), norm2_g[l][None])
    return _ffn(hn, xn, _weight_bf16(ffn_w_gate, l), ffn_dw_w[l], ffn_dw_b[l][None],
                _weight_bf16(ffn_w_up, l), _weight_bf16(ffn_w_down, l), seq_len)


def kernel(x, rel_bias, norm1_g, w_in, w_pool, pool_scale, att_q_g, att_k_g, conv_dw_w, conv_dw_b,
           conv_ln_g, conv_ln_b, conv_pw, dn_conv_w, dn_a_log, dn_dt_bias, dn_norm_g, w_out, norm2_g,
           ffn_w_gate, ffn_dw_w, ffn_dw_b, ffn_w_up, ffn_w_down):
    batch, seq_len, d_model = x.shape
    assert d_model == D_MODEL and w_in.shape[-1] == PROJ_MAIN + 4 * DN_HEADS
    assert seq_len % ATT_TILE == 0 and seq_len % (DN_GROUP * DN_CHUNK) == 0
    x2 = x.reshape(batch * seq_len, D_MODEL)
    for l in range(norm1_g.shape[0]):
        x2 = _layer(x2, l, batch, seq_len, rel_bias, norm1_g, w_in, w_pool, pool_scale, att_q_g,
                    att_k_g, conv_dw_w, conv_dw_b, conv_ln_g, conv_ln_b, conv_pw, dn_conv_w, dn_a_log,
                    dn_dt_bias, dn_norm_g, w_out, norm2_g, ffn_w_gate, ffn_dw_w, ffn_dw_b, ffn_w_up,
                    ffn_w_down)
    return x2.reshape(batch, seq_len, D_MODEL)
```

```python
import functools
import math

import jax
import jax.numpy as jnp
import numpy as np
from jax import lax
from jax.experimental import pallas as pl
from jax.experimental.pallas import tpu as pltpu

F32 = jnp.float32
BF16 = jnp.bfloat16

D_MODEL = 2048
GROUP = 512
POOL_WINDOWS = (2, 4, 8, 16)
ATT_HEAD_DIM = 64
ATT_HEADS = 8
ATT_RADIUS = 64
ATT_DILATIONS = (1, 4, 16)
REL_BUCKETS = 32
REL_MAX_DIST = 1024
CONV_WIDTH = 31
DN_HEAD_DIM = 128
DN_HEADS = 4
DN_CHUNK = 64
FFN_HIDDEN = 5632
EPS = 1e-6
NEG_INF = -1e30

PROJ_MAIN = 10 * GROUP
GATE_LANES = 128

VMEM_LIMIT_BYTES = 56 * 1024 * 1024


def _cparams(*sem):
    return pltpu.CompilerParams(dimension_semantics=sem, vmem_limit_bytes=VMEM_LIMIT_BYTES)


def _sigmoid(x):
    return 1.0 / (1.0 + jnp.exp(-x))


def _silu(x):
    return x * _sigmoid(x)


def _dot(a, b):
    return jnp.dot(a, b, preferred_element_type=F32)


def _dot_nt(a, b):
    return lax.dot_general(a, b, (((1,), (1,)), ((), ())), preferred_element_type=F32)


def _dot_tn(a, b):
    return lax.dot_general(a, b, (((0,), (0,)), ((), ())), preferred_element_type=F32)


def _cast_kernel(w_ref, o_ref):
    o_ref[...] = w_ref[...].astype(BF16)


def _weight_bf16(w_stack, layer, cols=None, tr=512):
    _, rows, width = w_stack.shape
    cols = width if cols is None else cols
    tc = cols // 2 if cols >= 4096 else cols
    return pl.pallas_call(
        _cast_kernel,
        grid=(rows // tr, cols // tc),
        in_specs=[pl.BlockSpec((None, tr, tc), lambda i, j: (layer, i, j))],
        out_specs=pl.BlockSpec((tr, tc), lambda i, j: (i, j)),
        out_shape=jax.ShapeDtypeStruct((rows, cols), BF16),
        compiler_params=_cparams("parallel", "parallel"),
        name="weight_bf16",
    )(w_stack)


def _gate_cols_kernel(w_ref, o_ref, *, valid):
    lane = lax.broadcasted_iota(jnp.int32, o_ref.shape, 1)
    o_ref[...] = jnp.where(lane < valid, w_ref[...], 0.0).astype(BF16)


def _gate_weight_bf16(w_stack, layer):
    _, rows, width = w_stack.shape
    valid = width - PROJ_MAIN
    return pl.pallas_call(
        functools.partial(_gate_cols_kernel, valid=valid),
        grid=(1,),
        in_specs=[pl.BlockSpec((None, rows, GATE_LANES), lambda i: (layer, 0, PROJ_MAIN // GATE_LANES))],
        out_specs=pl.BlockSpec((rows, GATE_LANES), lambda i: (0, 0)),
        out_shape=jax.ShapeDtypeStruct((rows, GATE_LANES), BF16),
        compiler_params=_cparams("arbitrary"),
        name="gate_weight_bf16",
    )(w_stack)


def _inproj_kernel(x_ref, g_ref, w_ref, wg_ref, p_ref, pg_ref):
    x = x_ref[...]
    ms = jnp.mean(x * x, axis=-1, keepdims=True)
    h = (x * lax.rsqrt(ms + EPS) * g_ref[...]).astype(BF16)
    pg_ref[...] = _dot(h, wg_ref[...])
    p_ref[...] = _dot(h, w_ref[...])


def _in_proj(x2, norm_g, w_main, w_gate, tm=512):
    T = x2.shape[0]
    resident = lambda shape: pl.BlockSpec(shape, lambda i: (0, 0), pipeline_mode=pl.Buffered(1))
    return pl.pallas_call(
        _inproj_kernel,
        grid=(T // tm,),
        in_specs=[
            pl.BlockSpec((tm, D_MODEL), lambda i: (i, 0)),
            resident((1, D_MODEL)),
            resident((D_MODEL, PROJ_MAIN)),
            resident((D_MODEL, GATE_LANES)),
        ],
        out_specs=[
            pl.BlockSpec((tm, PROJ_MAIN), lambda i: (i, 0)),
            pl.BlockSpec((tm, GATE_LANES), lambda i: (i, 0)),
        ],
        out_shape=[
            jax.ShapeDtypeStruct((T, PROJ_MAIN), F32),
            jax.ShapeDtypeStruct((T, GATE_LANES), F32),
        ],
        compiler_params=_cparams("parallel"),
        name="in_proj",
    )(x2, norm_g, w_main, w_gate)


def _halo_specs(tm, halo, width, col_block, n_tiles):
    r = tm // halo
    last = n_tiles * r - 1
    prev_spec = pl.BlockSpec((halo, width), lambda i, *_: (jnp.maximum(i * r - 1, 0), col_block))
    next_spec = pl.BlockSpec((halo, width), lambda i, *_: (jnp.minimum((i + 1) * r, last), col_block))
    return prev_spec, next_spec


def _seq_edges(i, tiles_per_seq):
    k = lax.rem(i, tiles_per_seq)
    return k == 0, k == tiles_per_seq - 1, k


def _pool_kernel(u_ref, up_ref, un_ref, w_ref, sc_ref, y_ref, ext_scr, *, tm, seq_len):
    i = pl.program_id(0)
    first, last, k = _seq_edges(i, seq_len // tm)
    ext_scr[0:8, :] = jnp.where(first, jnp.zeros_like(up_ref[...]), up_ref[...])
    ext_scr[8:8 + tm, :] = u_ref[...]
    ext_scr[8 + tm:16 + tm, :] = jnp.where(last, jnp.zeros_like(un_ref[...]), un_ref[...])
    t = k * tm + lax.broadcasted_iota(jnp.int32, (tm, 1), 0)
    for gi, win in enumerate(POOL_WINDOWS):
        half = win // 2
        lanes = slice(gi * 128, (gi + 1) * 128)
        s = ext_scr[pl.ds(8 - half, tm), lanes]
        for kk in range(1, win):
            s = s + ext_scr[pl.ds(8 - half + kk, tm), lanes]
        cnt = (jnp.minimum(t + half, seq_len) - jnp.maximum(t - half, 0)).astype(F32)
        pooled = s / cnt - ext_scr[pl.ds(8, tm), lanes]
        y = _dot(pooled.astype(BF16), w_ref[gi]) * sc_ref[:, lanes]
        y_ref[:, lanes] = y.astype(BF16)


def _pool_mixer(p, w_pool, pool_scale, seq_len, tm=2048):
    T = p.shape[0]
    n_tiles = T // tm
    prev_spec, next_spec = _halo_specs(tm, 8, GROUP, 0, n_tiles)
    return pl.pallas_call(
        functools.partial(_pool_kernel, tm=tm, seq_len=seq_len),
        grid=(n_tiles,),
        in_specs=[
            pl.BlockSpec((tm, GROUP), lambda i: (i, 0)),
            prev_spec,
            next_spec,
            pl.BlockSpec((4, 128, 128), lambda i: (0, 0, 0)),
            pl.BlockSpec((1, GROUP), lambda i: (0, 0)),
        ],
        out_specs=pl.BlockSpec((tm, GROUP), lambda i: (i, 0)),
        out_shape=jax.ShapeDtypeStruct((T, GROUP), BF16),
        scratch_shapes=[pltpu.VMEM((tm + 16, GROUP), F32)],
        compiler_params=_cparams("parallel"),
        name="pool_mixer",
    )(p, p, p, w_pool, pool_scale)


def _conformer_kernel(v_ref, vp_ref, vn_ref, g_ref, gp_ref, gn_ref, dw_ref, db_ref, lg_ref, lb_ref,
                      pw_ref, y_ref, ext_scr, shift_scr, *, tm, seq_len):
    i = pl.program_id(0)
    first, last, _ = _seq_edges(i, seq_len // tm)
    hp = vp_ref[...] * _sigmoid(gp_ref[...])
    hn = vn_ref[...] * _sigmoid(gn_ref[...])
    ext_scr[0:16, :] = jnp.where(first, jnp.zeros_like(hp), hp)
    ext_scr[16:16 + tm, :] = v_ref[...] * _sigmoid(g_ref[...])
    ext_scr[16 + tm:32 + tm, :] = jnp.where(last, jnp.zeros_like(hn), hn)
    base = 16 - CONV_WIDTH // 2
    acc = db_ref[...]
    for b in range(8):
        taps = [(a, 8 * a + b - base) for a in range(5) if 0 <= 8 * a + b - base < CONV_WIDTH]
        rows = tm + 8 * taps[-1][0]
        shift_scr[b, 0:rows, :] = ext_scr[pl.ds(b, rows), :]
        for a, kk in taps:
            acc = acc + shift_scr[b, 8 * a:8 * a + tm, :] * dw_ref[kk:kk + 1, :]
    mu = jnp.mean(acc, axis=-1, keepdims=True)
    xc = acc - mu
    var = jnp.mean(xc * xc, axis=-1, keepdims=True)
    h = _silu(xc * lax.rsqrt(var + EPS) * lg_ref[...] + lb_ref[...])
    y_ref[...] = _dot(h.astype(BF16), pw_ref[...]).astype(BF16)


def _conformer_mixer(p, dw_w, dw_b, ln_g, ln_b, pw, seq_len, tm=1024):
    T = p.shape[0]
    n_tiles = T // tm
    vprev, vnext = _halo_specs(tm, 16, GROUP, 4, n_tiles)
    gprev, gnext = _halo_specs(tm, 16, GROUP, 5, n_tiles)
    const = lambda shape: pl.BlockSpec(shape, lambda i: (0,) * len(shape))
    return pl.pallas_call(
        functools.partial(_conformer_kernel, tm=tm, seq_len=seq_len),
        grid=(n_tiles,),
        in_specs=[
            pl.BlockSpec((tm, GROUP), lambda i: (i, 4)), vprev, vnext,
            pl.BlockSpec((tm, GROUP), lambda i: (i, 5)), gprev, gnext,
            const((CONV_WIDTH, GROUP)), const((1, GROUP)), const((1, GROUP)), const((1, GROUP)),
            const((GROUP, GROUP)),
        ],
        out_specs=pl.BlockSpec((tm, GROUP), lambda i: (i, 0)),
        out_shape=jax.ShapeDtypeStruct((T, GROUP), BF16),
        scratch_shapes=[pltpu.VMEM((tm + 32, GROUP), F32), pltpu.VMEM((8, tm + 32, GROUP), F32)],
        compiler_params=_cparams("parallel"),
        name="conformer_mixer",
    )(p, p, p, p, p, p, dw_w, dw_b, ln_g, ln_b, pw)


ATT_TILE = 2048
LOG2E = math.log2(math.e)
ATT_PIPELINE_DEPTH = 3


def _t5_bucket_table():
    nb = REL_BUCKETS // 2
    max_exact = nb // 2
    i = np.arange(ATT_RADIUS)[:, None]
    j = np.arange(3 * ATT_RADIUS)[None, :]
    off = j - ATT_RADIUS - i
    tables = []
    for dil in ATT_DILATIONS:
        rel = off * dil
        n = np.abs(rel)
        nf = np.maximum(n, 1).astype(np.float32)
        large = max_exact + (np.log(nf / np.float32(max_exact)) / np.float32(math.log(REL_MAX_DIST / max_exact))
                             * np.float32(nb - max_exact)).astype(np.int32)
        large = np.minimum(large, nb - 1)
        bucket = np.where(rel > 0, nb, 0) + np.where(n < max_exact, n, large)
        tables.append(np.where(np.abs(off) <= ATT_RADIUS, bucket, -1))
    return np.stack(tables).astype(np.int32)


def _att_kernel(rb_ref, bkt_ref, qg_ref, kg_ref, mseg_ref, q_ref, kc_ref, kn_ref,
                vp_ref, vc_ref, vn_ref, y_ref, bias_scr, qbuf, kbuf, vbuf, acc_scr, m_scr, l_scr,
                *, seq_len):
    tile = ATT_TILE
    rad = ATT_RADIUS
    hp = pl.program_id(0)
    i = pl.program_id(1)

    def rms(x, g):
        ms = _dot((x * x).astype(BF16), mseg_ref[...])
        return x * lax.rsqrt(ms + EPS) * g

    @pl.when(i == 0)
    def _():
        lane_head0 = lax.broadcasted_iota(jnp.int32, (1, 128), 1) < ATT_HEAD_DIM
        for di in range(len(ATT_DILATIONS)):
            bkt = bkt_ref[di]
            b = jnp.zeros(bkt.shape, F32)
            for bb in range(REL_BUCKETS):
                b = jnp.where(bkt == bb, jnp.where(lane_head0, rb_ref[bb, 2 * hp], rb_ref[bb, 2 * hp + 1]), b)
            bias_scr[di] = jnp.where(bkt < 0, NEG_INF, b * LOG2E)
        kbuf[0:tile, :] = jnp.zeros((tile, 128), F32)
        kbuf[tile:2 * tile, :] = rms(kc_ref[...], kg_ref[...])

    @pl.when(i > 0)
    def _():
        kbuf[0:tile, :] = kbuf[tile:2 * tile, :]
        kbuf[tile:2 * tile, :] = kbuf[2 * tile:3 * tile, :]

    first, last, _ = _seq_edges(i, seq_len // tile)

    qbuf[...] = rms(q_ref[...], qg_ref[...]) * (ATT_HEAD_DIM ** -0.5 * LOG2E)
    kbuf[2 * tile:3 * tile, :] = rms(kn_ref[...], kg_ref[...])
    vbuf[0:tile, :] = vp_ref[...]
    vbuf[tile:2 * tile, :] = vc_ref[...]
    vbuf[2 * tile:3 * tile, :] = vn_ref[...]

    head0 = lax.broadcasted_iota(jnp.int32, (1, 128), 1) < ATT_HEAD_DIM
    key_row = lax.broadcasted_iota(jnp.int32, (3 * rad, 1), 0)
    ones_cols = jnp.ones((3 * rad, 128), BF16)

    def rows(start, size, dil):
        return pl.ds(start, size) if dil == 1 else pl.ds(start, size, stride=dil)

    blocks = [(di, dil, r + rad * dil * m)
              for di, dil in enumerate(ATT_DILATIONS) for r in range(dil) for m in range(tile // (rad * dil))]
    def score_stage(blk):
        di, dil, qstart = blk
        kstart = tile + qstart - rad * dil
        qb = qbuf[rows(qstart, rad, dil), :]
        kb = kbuf[rows(kstart, 3 * rad, dil), :].astype(BF16)
        q2 = jnp.concatenate([jnp.where(head0, qb, 0.0), jnp.where(head0, 0.0, qb)], axis=0)
        return _dot_nt(kb, q2.astype(BF16))

    def softmax_stage(blk, s):
        di, dil, qstart = blk
        kstart = tile + qstart - rad * dil
        s = s + bias_scr[di]
        n_prev = max(0, -(-(tile - kstart) // dil))
        n_upto = min(3 * rad, -(-(2 * tile - kstart) // dil))
        if n_prev > 0:
            s = jnp.where(key_row < jnp.where(first, n_prev, 0), NEG_INF, s)
        if n_upto < 3 * rad:
            s = jnp.where(key_row >= jnp.where(last, n_upto, 3 * rad), NEG_INF, s)
        mx = jnp.max(s, axis=0, keepdims=True)
        return mx, jnp.exp2(s - mx).astype(BF16)

    def value_stage(blk, pe):
        di, dil, qstart = blk
        kstart = tile + qstart - rad * dil
        vb = vbuf[rows(kstart, 3 * rad, dil), :].astype(BF16)
        return _dot_tn(pe, jnp.concatenate([vb, ones_cols], axis=1))

    def store_stage(blk, mx, ov):
        di, dil, qstart = blk
        dst = rows(qstart, rad, dil)
        m_col = jnp.broadcast_to(mx, (2 * rad, 128)).T
        acc_scr[di, dst, :] = jnp.where(head0, ov[0:rad, 0:128], ov[rad:2 * rad, 0:128])
        l_scr[di, dst, :] = jnp.where(head0, ov[0:rad, 128:256], ov[rad:2 * rad, 128:256])
        m_scr[di, dst, :] = jnp.where(head0, m_col[0:rad], m_col[rad:2 * rad])

    n_blk = len(blocks)
    scores = {b: score_stage(blocks[b]) for b in range(min(ATT_PIPELINE_DEPTH, n_blk))}
    pending = None
    for b in range(n_blk):
        mx, pe = softmax_stage(blocks[b], scores.pop(b))
        ov = value_stage(blocks[b], pe)
        if b + ATT_PIPELINE_DEPTH < n_blk:
            scores[b + ATT_PIPELINE_DEPTH] = score_stage(blocks[b + ATT_PIPELINE_DEPTH])
        if pending is not None:
            store_stage(*pending)
        pending = (blocks[b], mx, ov)
    store_stage(*pending)

    m_all = jnp.maximum(jnp.maximum(m_scr[0], m_scr[1]), m_scr[2])
    num = jnp.zeros((tile, 128), F32)
    den = jnp.zeros((tile, 128), F32)
    for di in range(len(ATT_DILATIONS)):
        e = jnp.exp2(m_scr[di] - m_all)
        num = num + acc_scr[di] * e
        den = den + l_scr[di] * e
    y_ref[...] = (num / den).astype(BF16)


def _attention_mixer(p, q_g, k_g, rel_bias, seq_len):
    T = p.shape[0]
    tile = ATT_TILE
    n_tiles = T // tile
    rad = ATT_RADIUS
    seg = np.kron(np.eye(2), np.full((ATT_HEAD_DIM, ATT_HEAD_DIM), 1.0 / ATT_HEAD_DIM))
    mseg = jnp.asarray(seg, BF16)
    bkt_t = np.transpose(_t5_bucket_table(), (0, 2, 1))
    bkt = jnp.asarray(np.concatenate([bkt_t, bkt_t], axis=2))
    qg2 = jnp.tile(q_g, 2)[None]
    kg2 = jnp.tile(k_g, 2)[None]

    def blk(col0, shift):
        return pl.BlockSpec((tile, 128),
                            lambda hp, i: (jnp.clip(i + shift, 0, n_tiles - 1), col0 * 4 + hp))

    const = lambda shape: pl.BlockSpec(shape, lambda hp, i: (0,) * len(shape))
    return pl.pallas_call(
        functools.partial(_att_kernel, seq_len=seq_len),
        grid=(4, n_tiles),
        in_specs=[
            pl.BlockSpec(memory_space=pltpu.SMEM),
            const((3, 3 * rad, 2 * rad)), const((1, 128)), const((1, 128)), const((128, 128)),
            blk(1, 0),
            blk(2, 0), blk(2, 1),
            blk(3, -1), blk(3, 0), blk(3, 1),
        ],
        out_specs=pl.BlockSpec((tile, 128), lambda hp, i: (i, hp)),
        out_shape=jax.ShapeDtypeStruct((T, GROUP), BF16),
        scratch_shapes=[
            pltpu.VMEM((3, 3 * rad, 2 * rad), F32),
            pltpu.VMEM((tile, 128), F32),
            pltpu.VMEM((3 * tile, 128), F32),
            pltpu.VMEM((3 * tile, 128), F32),
            pltpu.VMEM((3, tile, 128), F32),
            pltpu.VMEM((3, tile, 128), F32),
            pltpu.VMEM((3, tile, 128), F32),
        ],
        compiler_params=_cparams("arbitrary", "arbitrary"),
        name="dilated_attention",
    )(rel_bias, bkt, qg2, kg2, mseg, p, p, p, p, p, p)


def _softplus(x):
    return jnp.maximum(x, 0.0) + jnp.log1p(jnp.exp(-jnp.abs(x)))


def _dn_prep_kernel(x_ref, xp_ref, xn_ref, w_ref, o_ref, ext_scr, *, tm, seq_len):
    i = pl.program_id(0)
    first, last, _ = _seq_edges(i, seq_len // tm)
    ext_scr[0:8, :] = jnp.where(first, jnp.zeros_like(xp_ref[...]), xp_ref[...])
    ext_scr[8:8 + tm, :] = x_ref[...]
    ext_scr[8 + tm:16 + tm, :] = jnp.where(last, jnp.zeros_like(xn_ref[...]), xn_ref[...])
    for cb in range(3 * DN_HEADS):
        lanes = slice(cb * 128, (cb + 1) * 128)
        acc = ext_scr[pl.ds(6, tm), lanes] * w_ref[0:1, lanes]
        for kk in range(1, 4):
            acc = acc + ext_scr[pl.ds(6 + kk, tm), lanes] * w_ref[kk:kk + 1, lanes]
        y = _silu(acc)
        if cb < 2 * DN_HEADS:
            y = y * lax.rsqrt(jnp.sum(y * y, axis=-1, keepdims=True) + EPS)
        if cb < DN_HEADS:
            y = y * (DN_HEAD_DIM ** -0.5)
        o_ref[:, lanes] = y


def _dn_prep(p, conv_w, seq_len, tm=512):
    T = p.shape[0]
    n_tiles = T // tm
    width = 3 * GROUP
    prev_spec, next_spec = _halo_specs(tm, 8, width, 2, n_tiles)
    return pl.pallas_call(
        functools.partial(_dn_prep_kernel, tm=tm, seq_len=seq_len),
        grid=(n_tiles,),
        in_specs=[pl.BlockSpec((tm, width), lambda i: (i, 2)), prev_spec, next_spec,
                  pl.BlockSpec((4, width), lambda i: (0, 0))],
        out_specs=pl.BlockSpec((tm, width), lambda i: (i, 0)),
        out_shape=jax.ShapeDtypeStruct((T, width), F32),
        scratch_shapes=[pltpu.VMEM((tm + 16, width), F32)],
        compiler_params=_cparams("parallel"),
        name="deltanet_prep",
    )(p, p, p, conv_w)


DN_GROUP = 16
DN_ROWS = DN_HEADS * DN_CHUNK
DN_SCAN_LAG_STEPS = 4


def _dn_scan_kernel(xf_ref, xb_ref, gf_ref, gb_ref, rf_ref, rb_ref, alc_ref, dtc_ref, alr_ref, dtr_ref,
                    of_ref, ob_ref, s_scr):
    C = DN_CHUNK
    H = DN_HEADS
    R = DN_GROUP * C

    @pl.when(pl.program_id(1) == 0)
    def _():
        s_scr[...] = jnp.zeros_like(s_scr)

    row_in_chunk = lax.rem(lax.broadcasted_iota(jnp.int32, (R, 1), 0), C)
    lane_in_chunk = lax.rem(lax.broadcasted_iota(jnp.int32, (1, DN_ROWS), 1), C)
    rid = lax.broadcasted_iota(jnp.int32, (DN_ROWS, DN_ROWS), 0)
    cid = lax.broadcasted_iota(jnp.int32, (DN_ROWS, DN_ROWS), 1)
    same_head = (rid // C) == (cid // C)
    row_sbs = lax.broadcasted_iota(jnp.int32, (C, DN_ROWS), 0)
    col_sbs = lax.rem(lax.broadcasted_iota(jnp.int32, (C, DN_ROWS), 1), C)
    lane_head = lax.broadcasted_iota(jnp.int32, (1, DN_ROWS), 1) // C

    def block_diag(m):
        return jnp.where(same_head, jnp.concatenate([m] * H, axis=0), jnp.zeros((), m.dtype))

    def side_by_side_blocks(x):
        return jnp.concatenate([x[h * C:(h + 1) * C] for h in range(H)], axis=1)

    def block_diag_blocks(x):
        zero = jnp.zeros((C, 128), x.dtype)
        return jnp.concatenate(
            [jnp.concatenate([x[h * C:(h + 1) * C] if g == h else zero for g in range(H)], axis=1)
             for h in range(H)], axis=0)

    def stack_heads(x, col0):
        return jnp.concatenate([x[:, col0 + h * 128:col0 + (h + 1) * 128] for h in range(H)], axis=0)

    def stack_cols(x, lane0, rows=None):
        parts = []
        for h in range(H):
            c = x[:, lane0 + h:lane0 + h + 1]
            if rows is not None:
                c = c[rows:rows + 1, :]
            parts.append(jnp.broadcast_to(c, (C, 128)))
        return jnp.concatenate(parts, axis=0)

    refs = ((xf_ref, gf_ref, rf_ref, of_ref), (xb_ref, gb_ref, rb_ref, ob_ref))
    gates = []
    for d, (x_ref, gc_ref, gr_ref, o_ref) in enumerate(refs):
        reverse = d == 1
        gcol = gc_ref[...]
        g = -jnp.exp(alc_ref[...]) * _softplus(gcol + dtc_ref[...])
        beta = _sigmoid(gcol)
        gcum = g
        for sh in (1, 2, 4, 8, 16, 32):
            if reverse:
                gcum = gcum + jnp.where(row_in_chunk < C - sh, pltpu.roll(gcum, R - sh, 0), 0.0)
            else:
                gcum = gcum + jnp.where(row_in_chunk >= sh, pltpu.roll(gcum, sh, 0), 0.0)
        grow_all = -jnp.exp(alr_ref[...])[None] * _softplus(gr_ref[...] + dtr_ref[...][None])
        for sh in (1, 2, 4, 8, 16, 32):
            if reverse:
                grow_all = grow_all + jnp.where(lane_in_chunk < C - sh,
                                                pltpu.roll(grow_all, DN_ROWS - sh, 2), 0.0)
            else:
                grow_all = grow_all + jnp.where(lane_in_chunk >= sh, pltpu.roll(grow_all, sh, 2), 0.0)

        gates.append((gcum, beta, grow_all))

    def fill_operands(u):
        d = u["d"]
        x_ref, _, _, o_ref = refs[d]
        reverse = d == 1
        ci = DN_GROUP - 1 - u["step"] if reverse else u["step"]
        gcum, beta, grow_all = gates[d]
        rows = slice(ci * C, (ci + 1) * C)
        x = x_ref[rows, :]
        u.update(rows=rows, o_ref=o_ref)
        u["q"] = stack_heads(x, 0)
        u["k"] = stack_heads(x, GROUP)
        v_st = stack_heads(x, 2 * GROUP)
        gc_c = gcum[rows, :]
        beta_st = stack_cols(beta[rows, :], 8 + d * H)
        gcol_st = stack_cols(gc_c, d * H)
        glast_st = stack_cols(gc_c, d * H, rows=0 if reverse else C - 1)
        grow = grow_all[ci, d:d + 1, :]
        gcol_sbs = jnp.broadcast_to(gc_c[:, d * H + H - 1:d * H + H], (C, DN_ROWS))
        for h in range(H - 2, -1, -1):
            gcol_sbs = jnp.where(lane_head <= h, gc_c[:, d * H + h:d * H + h + 1], gcol_sbs)
        u["incl"] = (row_sbs <= col_sbs) if reverse else (row_sbs >= col_sbs)
        u["decay"] = jnp.where(u["incl"], jnp.exp(gcol_sbs - grow), 0.0)
        kb_st = u["k"] * beta_st
        eg = jnp.exp(gcol_st)
        u["kq"] = jnp.concatenate([side_by_side_blocks(kb_st), side_by_side_blocks(u["q"])],
                                  axis=0).astype(BF16)
        u["kbd"] = block_diag_blocks(u["k"].astype(BF16))
        u["rhs"] = jnp.concatenate([v_st * beta_st, kb_st * eg], axis=1).astype(BF16)
        u["qdec"] = u["q"] * eg
        u["kdec"] = (u["k"] * jnp.exp(glast_st - gcol_st)).astype(BF16)
        u["gl"] = jnp.exp(glast_st)

    eye_sbs = (row_sbs == col_sbs).astype(F32)

    def prep_stages(units):
        for u in units:
            fill_operands(u)
        yield
        for n, u in enumerate(units):
            u["kk"] = _dot_nt(u["kq"], u["kbd"])
            if n == len(units) // 2 - 1:
                yield
        yield
        for u in units:
            strict = u["incl"] & (row_sbs != col_sbs)
            a_sbs = jnp.where(strict, u["kk"][0:C] * u["decay"], 0.0)
            u["attn"] = (u["kk"][C:2 * C] * u["decay"]).astype(BF16)
            u["pinv"] = eye_sbs - a_sbs
            u["a_sbs"] = a_sbs.astype(BF16)
        for u in units:
            u["apow"] = _dot(u["a_sbs"], block_diag(u["a_sbs"]))
        yield
        for _ in range(4):
            for n, u in enumerate(units):
                ap = u["apow"].astype(BF16)
                u["both"] = _dot(jnp.concatenate([u["pinv"].astype(BF16), ap], axis=0), block_diag(ap))
                if n == len(units) // 2 - 1:
                    yield
            yield
            for u in units:
                u["pinv"] = u["pinv"] + u["both"][0:C]
                u["apow"] = u["both"][C:2 * C]
        for u in units:
            u["last"] = _dot(u["pinv"].astype(BF16), block_diag(u["apow"].astype(BF16)))
        yield
        for u in units:
            u["uw"] = _dot(block_diag((u["pinv"] + u["last"]).astype(BF16)), u["rhs"])
        yield

    def scan_stages(units):
        for i in range(0, len(units), 2):
            pair = units[i:i + 2]
            for u in pair:
                d = u["d"]
                u["wq"] = []
                for h in range(H):
                    hr = slice(h * C, (h + 1) * C)
                    lhs = jnp.concatenate([u["uw"][hr, 128:256], u["qdec"][hr]], axis=0).astype(BF16)
                    u["wq"].append(_dot(lhs, s_scr[d * H + h].astype(BF16)))
            yield
            for u in pair:
                u["vnew"] = [(u["uw"][h * C:(h + 1) * C, 0:128] - u["wq"][h][0:C]).astype(BF16)
                             for h in range(H)]
                vnew_bd = block_diag_blocks(jnp.concatenate(u["vnew"], axis=0))
                u["o"] = jnp.concatenate([w[C:2 * C] for w in u["wq"]], axis=1) + _dot(u["attn"], vnew_bd)
            yield
            for u in pair:
                d = u["d"]
                for h in range(H):
                    hr = slice(h * C, (h + 1) * C)
                    s_scr[d * H + h] = (s_scr[d * H + h] * u["gl"][h * C:h * C + 1, :]
                                        + _dot_tn(u["kdec"][hr], u["vnew"][h]))
                u["o_ref"][u["rows"], :] = u["o"]
            yield

    def run_together(*stage_generators):
        live = list(stage_generators)
        while live:
            for gen in list(live):
                if next(gen, StopIteration) is StopIteration:
                    live.remove(gen)

    batches = [[dict(step=step, d=d) for step in range(s0, s0 + DN_SCAN_LAG_STEPS) for d in range(2)]
               for s0 in range(0, DN_GROUP, DN_SCAN_LAG_STEPS)]
    run_together(prep_stages(batches[0]))
    for prev, cur in zip(batches[:-1], batches[1:]):
        run_together(prep_stages(cur), scan_stages(prev))
    run_together(scan_stages(batches[-1]))


def _dn_scan(qkvn, pg, a_log, dt_bias, batch, seq_len):
    T = qkvn.shape[0]
    C, H, G = DN_CHUNK, DN_HEADS, DN_GROUP
    R = G * C
    nc = seq_len // C
    ncg = nc // G
    ab = pg[:, 0:4 * H].reshape(batch, nc, C, 2, 2, H)
    ab_row = jnp.transpose(ab, (0, 1, 3, 4, 5, 2)).reshape(batch * nc, 4, H * C)
    ab_row = jnp.pad(ab_row, ((0, 0), (0, 4), (0, 0)))
    pad_lanes = lambda v: jnp.pad(v.reshape(1, 2 * H), ((0, 0), (0, GATE_LANES - 2 * H)))
    row_param = lambda v: jnp.pad(jnp.repeat(v, C, axis=1), ((0, 6), (0, 0)))

    fwd = lambda b, c: (b * ncg + c, 0)
    bwd = lambda b, c: (b * ncg + ncg - 1 - c, 0)
    fwd3 = lambda b, c: (b * ncg + c, 0, 0)
    bwd3 = lambda b, c: (b * ncg + ncg - 1 - c, 0, 0)
    const = lambda shape: pl.BlockSpec(shape, lambda b, c: (0,) * len(shape))
    return pl.pallas_call(
        _dn_scan_kernel,
        grid=(batch, ncg),
        in_specs=[
            pl.BlockSpec((R, 3 * GROUP), fwd), pl.BlockSpec((R, 3 * GROUP), bwd),
            pl.BlockSpec((R, GATE_LANES), fwd), pl.BlockSpec((R, GATE_LANES), bwd),
            pl.BlockSpec((G, 8, H * C), fwd3), pl.BlockSpec((G, 8, H * C), bwd3),
            const((1, GATE_LANES)), const((1, GATE_LANES)), const((8, H * C)), const((8, H * C)),
        ],
        out_specs=[pl.BlockSpec((R, GROUP), fwd), pl.BlockSpec((R, GROUP), bwd)],
        out_shape=[jax.ShapeDtypeStruct((T, GROUP), F32), jax.ShapeDtypeStruct((T, GROUP), F32)],
        scratch_shapes=[pltpu.VMEM((2 * H, DN_HEAD_DIM, DN_HEAD_DIM), F32)],
        compiler_params=_cparams("arbitrary", "arbitrary"),
        name="deltanet_scan",
    )(qkvn, qkvn, pg, pg, ab_row, ab_row, pad_lanes(a_log), pad_lanes(dt_bias),
      row_param(a_log), row_param(dt_bias))


def _deltanet_mixer(p, pg, conv_w, a_log, dt_bias, batch, seq_len):
    qkvn = _dn_prep(p, conv_w, seq_len)
    return _dn_scan(qkvn, pg, a_log, dt_bias, batch, seq_len)


def _outproj_kernel(x_ref, ya_ref, yb_ref, yc_ref, of_ref, ob_ref, z_ref, dg_ref, w_ref, g_ref,
                    xn_ref, hn_ref, ycat_scr):
    for gi, y_ref in enumerate((ya_ref, yb_ref, yc_ref)):
        ycat_scr[:, gi * GROUP:(gi + 1) * GROUP] = y_ref[...]
    for h in range(DN_HEADS):
        lanes = slice(h * 128, (h + 1) * 128)
        o = of_ref[:, lanes] + ob_ref[:, lanes]
        o = o * lax.rsqrt(jnp.mean(o * o, axis=-1, keepdims=True) + EPS) * dg_ref[...]
        ycat_scr[:, 3 * GROUP + h * 128:3 * GROUP + (h + 1) * 128] = (o * _silu(z_ref[:, lanes])).astype(BF16)
    acc = x_ref[...] + _dot(ycat_scr[...], w_ref[...])
    xn_ref[...] = acc
    ms = jnp.mean(acc * acc, axis=-1, keepdims=True)
    hn_ref[...] = (acc * lax.rsqrt(ms + EPS) * g_ref[...]).astype(BF16)


def _out_proj(x2, ys, o_f, o_b, p, dn_norm_g, w_out, norm2_g, tm=512):
    T = x2.shape[0]
    row = lambda width, cb=0: pl.BlockSpec((tm, width), lambda i: (i, cb))
    return pl.pallas_call(
        _outproj_kernel,
        grid=(T // tm,),
        in_specs=[row(D_MODEL), row(GROUP), row(GROUP), row(GROUP),
                  row(GROUP), row(GROUP), row(GROUP, 9),
                  pl.BlockSpec((1, DN_HEAD_DIM), lambda i: (0, 0)),
                  pl.BlockSpec((D_MODEL, D_MODEL), lambda i: (0, 0), pipeline_mode=pl.Buffered(1)),
                  pl.BlockSpec((1, D_MODEL), lambda i: (0, 0))],
        out_specs=[row(D_MODEL), row(D_MODEL)],
        out_shape=[jax.ShapeDtypeStruct((T, D_MODEL), F32),
                   jax.ShapeDtypeStruct((T, D_MODEL), BF16)],
        scratch_shapes=[pltpu.VMEM((tm, D_MODEL), BF16)],
        compiler_params=_cparams("parallel"),
        name="out_proj",
    )(x2, *ys, o_f, o_b, p, dn_norm_g, w_out, norm2_g)


FFN_HALF = FFN_HIDDEN // 2
FFN_COL_CHUNK = 256


def _ffn_hidden_kernel(hn_ref, hp_ref, hx_ref, wg_ref, wu_ref, dw_ref, db_ref, a_ref, hext_scr, g_scr,
                       *, tm, seq_len):
    first, last, _ = _seq_edges(pl.program_id(1), seq_len // tm)
    hext_scr[0:16, :] = jnp.where(first, jnp.zeros_like(hp_ref[...]), hp_ref[...])
    hext_scr[16:16 + tm, :] = hn_ref[...]
    hext_scr[16 + tm:32 + tm, :] = jnp.where(last, jnp.zeros_like(hx_ref[...]), hx_ref[...])
    for c in range(FFN_HALF // FFN_COL_CHUNK):
        cols = slice(c * FFN_COL_CHUNK, (c + 1) * FFN_COL_CHUNK)
        g = g_scr.at[c % 2]
        g[...] = _dot(hext_scr[...], wg_ref[:, cols])
        up = _dot(hn_ref[...], wu_ref[:, cols])
        gate = (g[pl.ds(15, tm), :] * dw_ref[0:1, cols] + g[pl.ds(16, tm), :] * dw_ref[1:2, cols]
                + g[pl.ds(17, tm), :] * dw_ref[2:3, cols] + db_ref[:, cols])
        a_ref[:, cols] = (_silu(gate) * up).astype(BF16)


def _ffn_down_kernel(a_ref, xn_ref, wd_ref, o_ref):
    o_ref[...] = xn_ref[...] + _dot(a_ref[...], wd_ref[...])


def _ffn(hn, xn, w_gate, dw_w, dw_b, w_up, w_down, seq_len, tm_hidden=1024, tm_down=512):
    T = hn.shape[0]
    n_tiles = T // tm_hidden
    r = tm_hidden // 16
    half_cols = lambda rows: pl.BlockSpec((rows, FFN_HALF), lambda h, i: (0, h),
                                          pipeline_mode=pl.Buffered(1))
    act = pl.pallas_call(
        functools.partial(_ffn_hidden_kernel, tm=tm_hidden, seq_len=seq_len),
        grid=(2, n_tiles),
        in_specs=[
            pl.BlockSpec((tm_hidden, D_MODEL), lambda h, i: (i, 0)),
            pl.BlockSpec((16, D_MODEL), lambda h, i: (jnp.maximum(i * r - 1, 0), 0)),
            pl.BlockSpec((16, D_MODEL), lambda h, i: (jnp.minimum((i + 1) * r, n_tiles * r - 1), 0)),
            half_cols(D_MODEL), half_cols(D_MODEL), half_cols(3), half_cols(1),
        ],
        out_specs=pl.BlockSpec((tm_hidden, FFN_HALF), lambda h, i: (i, h)),
        out_shape=jax.ShapeDtypeStruct((T, FFN_HIDDEN), BF16),
        scratch_shapes=[pltpu.VMEM((tm_hidden + 32, D_MODEL), BF16),
                        pltpu.VMEM((2, tm_hidden + 32, FFN_COL_CHUNK), F32)],
        compiler_params=_cparams("arbitrary", "arbitrary"),
        name="ffn_hidden",
    )(hn, hn, hn, w_gate, w_up, dw_w, dw_b)
    return pl.pallas_call(
        _ffn_down_kernel,
        grid=(T // tm_down,),
        in_specs=[
            pl.BlockSpec((tm_down, FFN_HIDDEN), lambda i: (i, 0)),
            pl.BlockSpec((tm_down, D_MODEL), lambda i: (i, 0)),
            pl.BlockSpec((FFN_HIDDEN, D_MODEL), lambda i: (0, 0), pipeline_mode=pl.Buffered(1)),
        ],
        out_specs=pl.BlockSpec((tm_down, D_MODEL), lambda i: (i, 0)),
        out_shape=jax.ShapeDtypeStruct((T, D_MODEL), F32),
        compiler_params=_cparams("parallel"),
        name="ffn_down",
    )(act, xn, w_down)


def _layer(x2, l, batch, seq_len, rel_bias, norm1_g, w_in, w_pool, pool_scale, att_q_g, att_k_g,
           conv_dw_w, conv_dw_b, conv_ln_g, conv_ln_b, conv_pw, dn_conv_w, dn_a_log, dn_dt_bias,
           dn_norm_g, w_out, norm2_g, ffn_w_gate, ffn_dw_w, ffn_dw_b, ffn_w_up, ffn_w_down):
    w_main = _weight_bf16(w_in, l, cols=PROJ_MAIN)
    w_gates = _gate_weight_bf16(w_in, l)
    p, pg = _in_proj(x2, norm1_g[l][None], w_main, w_gates)
    ya = _pool_mixer(p, w_pool[l].astype(BF16), pool_scale[l][None], seq_len)
    yb = _attention_mixer(p, att_q_g[l], att_k_g[l], rel_bias, seq_len)
    yc = _conformer_mixer(p, conv_dw_w[l], conv_dw_b[l][None], conv_ln_g[l][None], conv_ln_b[l][None],
                          conv_pw[l].astype(BF16), seq_len)
    o_f, o_b = _deltanet_mixer(p, pg, dn_conv_w[l], dn_a_log[l], dn_dt_bias[l], batch, seq_len)
    xn, hn = _out_proj(x2, (ya, yb, yc), o_f, o_b, p, dn_norm_g[l][None], _weight_bf16(w_out, l),
                       norm2_g[l][None])
    return _ffn(hn, xn, _weight_bf16(ffn_w_gate, l), ffn_dw_w[l], ffn_dw_b[l][None],
                _weight_bf16(ffn_w_up, l), _weight_bf16(ffn_w_down, l), seq_len)


def kernel(x, rel_bias, norm1_g, w_in, w_pool, pool_scale, att_q_g, att_k_g, conv_dw_w, conv_dw_b,
           conv_ln_g, conv_ln_b, conv_pw, dn_conv_w, dn_a_log, dn_dt_bias, dn_norm_g, w_out, norm2_g,
           ffn_w_gate, ffn_dw_w, ffn_dw_b, ffn_w_up, ffn_w_down):
    batch, seq_len, d_model = x.shape
    assert d_model == D_MODEL and w_in.shape[-1] == PROJ_MAIN + 4 * DN_HEADS
    assert seq_len % ATT_TILE == 0 and seq_len % (DN_GROUP * DN_CHUNK) == 0
    x2 = x.reshape(batch * seq_len, D_MODEL)
    for l in range(norm1_g.shape[0]):
        x2 = _layer(x2, l, batch, seq_len, rel_bias, norm1_g, w_in, w_pool, pool_scale, att_q_g,
                    att_k_g, conv_dw_w, conv_dw_b, conv_ln_g, conv_ln_b, conv_pw, dn_conv_w, dn_a_log,
                    dn_dt_bias, dn_norm_g, w_out, norm2_g, ffn_w_gate, ffn_dw_w, ffn_dw_b, ffn_w_up,
                    ffn_w_down)
    return x2.reshape(batch, seq_len, D_MODEL)
```
